```python
import math
import jax, jax.numpy as jnp
from jax import lax
import numpy as np

D_MODEL = 1024
BATCH = 8
SEQ = 2048
DEPTH = 4
DEC_BATCH = 128
DEC_SEQ = 1
PAST_LEN = 16384
PAGE_SIZE = 128

GLA_WIDTH = D_MODEL // 2
N_HEADS_GLA = 4
HEAD_V = GLA_WIDTH // N_HEADS_GLA
HEAD_K = HEAD_V // 2
KEY_WIDTH = N_HEADS_GLA * HEAD_K
GATE_RANK = 16
GATE_TEMP = 16.0
GLA_CHUNK = 64
POOL_WIDTH = D_MODEL - GLA_WIDTH
POOL_WINDOWS = (2, 4, 8, 16)
N_POOL_GROUPS = len(POOL_WINDOWS)
POOL_GROUP = POOL_WIDTH // N_POOL_GROUPS
POOL_BUF = max(POOL_WINDOWS) - 1
D_FF = 4 * D_MODEL
IN_WIDTH = 2 * KEY_WIDTH + 2 * GLA_WIDTH + GATE_RANK + POOL_WIDTH
EPS = 1e-6

kernel_name = "hymba_gla_pool_decoder_step"


def rmsnorm(x, g):
    xf = x.astype(jnp.float32)
    y = xf * lax.rsqrt(jnp.mean(xf * xf, axis=-1, keepdims=True) + EPS)
    return (y * g.astype(jnp.float32)).astype(x.dtype)


def gla_chunked(q, k, v, log_a, s0):
    B, T, H, dk = q.shape
    dv = v.shape[-1]
    C = math.gcd(T, GLA_CHUNK)
    N = T // C
    f32 = jnp.float32
    rs = lambda t: t.reshape(B, N, C, H, t.shape[-1]).astype(f32)
    q, k, v, la = rs(q), rs(k), rs(v), rs(log_a)
    b = jnp.cumsum(la, axis=2)
    b_last = b[:, :, -1:]
    qt = q * jnp.exp(b) * (HEAD_K ** -0.5)
    kt = k * jnp.exp(-b)
    ke = k * jnp.exp(b_last - b)
    mask = jnp.tril(jnp.ones((C, C), dtype=bool))
    att = jnp.einsum('bnchk,bnshk->bnhcs', qt, kt)
    att = jnp.where(mask, att, 0.0)
    o_intra = jnp.einsum('bnhcs,bnshv->bnchv', att, v)
    decay = jnp.exp(b_last[:, :, 0])

    def step(S, xs):
        qn, ken, vn, dn = xs
        o = jnp.einsum('bchk,bhkv->bchv', qn, S)
        S = dn[..., None] * S + jnp.einsum('bchk,bchv->bhkv', ken, vn)
        return S, o

    mv = lambda t: jnp.moveaxis(t, 1, 0)
    s_fin, o_inter = lax.scan(step, s0.astype(f32), (mv(qt), mv(ke), mv(v), mv(decay)))
    o = o_intra + jnp.moveaxis(o_inter, 0, 1)
    return o.reshape(B, T, H, dv), s_fin


def pool_mix(u, prefix):
    ext = jnp.concatenate([prefix.astype(u.dtype), u], axis=1)
    B, L, Cw = ext.shape
    P = prefix.shape[1]
    T = u.shape[1]
    ef = ext.astype(jnp.float32)
    cs = jnp.concatenate([jnp.zeros((B, 1, Cw), jnp.float32), jnp.cumsum(ef, axis=1)], axis=1)
    w = jnp.repeat(jnp.array(POOL_WINDOWS, jnp.int32), POOL_GROUP)
    i = jnp.arange(P, L, dtype=jnp.int32)[:, None]
    lo = jnp.maximum(i - w[None, :] + 1, 0)
    s_lo = jnp.take_along_axis(cs, jnp.broadcast_to(lo[None], (B, T, Cw)), axis=1)
    s = cs[:, P + 1:L + 1] - s_lo
    count = (i + 1 - lo).astype(jnp.float32)
    out = s / count - ef[:, P:]
    return out, ext[:, L - POOL_BUF:]


def layer(x, s0, prefix, n1, w_in, w_gate, b_gate, gla_g, pool_w, pool_scale, w_out, n2, w_up, w_down):
    B, T, _ = x.shape
    h = rmsnorm(x, n1)
    z = h @ w_in
    c = np.cumsum([KEY_WIDTH, KEY_WIDTH, GLA_WIDTH, GLA_WIDTH, GATE_RANK])
    q, k, v, g, a_low, u = jnp.split(z, [int(t) for t in c], axis=-1)
    q = q.reshape(B, T, N_HEADS_GLA, HEAD_K)
    k = k.reshape(B, T, N_HEADS_GLA, HEAD_K)
    v = v.reshape(B, T, N_HEADS_GLA, HEAD_V)
    log_a = jax.nn.log_sigmoid((a_low @ w_gate + b_gate).astype(jnp.float32)) / GATE_TEMP
    log_a = log_a.reshape(B, T, N_HEADS_GLA, HEAD_K)
    o, s_new = gla_chunked(q, k, v, log_a, s0)
    o = o * lax.rsqrt(jnp.mean(o * o, axis=-1, keepdims=True) + EPS) * gla_g.astype(jnp.float32)
    o = o.reshape(B, T, GLA_WIDTH) * jax.nn.silu(g.astype(jnp.float32))
    p, buf = pool_mix(u, prefix)
    p = jnp.einsum('btgc,gcd->btgd', p.reshape(B, T, N_POOL_GROUPS, POOL_GROUP),
                   pool_w.astype(jnp.float32)).reshape(B, T, POOL_WIDTH)
    p = p * pool_scale.astype(jnp.float32)
    mix = jnp.concatenate([o, p], axis=-1).astype(x.dtype)
    x = x + mix @ w_out
    h2 = rmsnorm(x, n2)
    x = x + jnp.square(jax.nn.relu(h2 @ w_up)) @ w_down
    return x, s_new, buf


def setup_inputs(seed: int = 0) -> dict:
    key = jax.random.key(seed)
    ks = jax.random.split(key, 16)
    nrm = jax.random.normal
    f = jnp.float32
    return {
        "x_prompt": nrm(ks[0], (BATCH, SEQ, D_MODEL), f),
        "x_sample": nrm(ks[1], (DEC_BATCH, DEC_SEQ, D_MODEL), f),
        "state_gla": 0.5 * nrm(ks[2], (DEPTH, DEC_BATCH, N_HEADS_GLA, HEAD_K, HEAD_V), f),
        "state_pool": nrm(ks[3], (DEPTH, DEC_BATCH, POOL_BUF, POOL_WIDTH), f),
        "norm1_g": 1.0 + 0.02 * nrm(ks[4], (DEPTH, D_MODEL), f),
        "w_in": nrm(ks[5], (DEPTH, D_MODEL, IN_WIDTH), f) * D_MODEL ** -0.5,
        "w_gate": nrm(ks[6], (DEPTH, GATE_RANK, KEY_WIDTH), f) * GATE_RANK ** -0.5,
        "b_gate": 0.1 * nrm(ks[7], (DEPTH, KEY_WIDTH), f),
        "gla_norm_g": 1.0 + 0.02 * nrm(ks[8], (DEPTH, HEAD_V), f),
        "pool_w": nrm(ks[9], (DEPTH, N_POOL_GROUPS, POOL_GROUP, POOL_GROUP), f) * POOL_GROUP ** -0.5,
        "pool_scale": 1.0 + 0.02 * nrm(ks[10], (DEPTH, POOL_WIDTH), f),
        "w_out": nrm(ks[11], (DEPTH, D_MODEL, D_MODEL), f) * D_MODEL ** -0.5,
        "norm2_g": 1.0 + 0.02 * nrm(ks[12], (DEPTH, D_MODEL), f),
        "w_up": nrm(ks[13], (DEPTH, D_MODEL, D_FF), f) * D_MODEL ** -0.5,
        "w_down": nrm(ks[14], (DEPTH, D_FF, D_MODEL), f) * D_FF ** -0.5,
        "final_g": 1.0 + 0.02 * nrm(ks[15], (D_MODEL,), f),
    }


def reference(x_prompt, x_sample, state_gla, state_pool, norm1_g, w_in, w_gate, b_gate,
              gla_norm_g, pool_w, pool_scale, w_out, norm2_g, w_up, w_down, final_g):
    xp, xs = x_prompt, x_sample
    B = xp.shape[0]
    gla_p, pool_p, gla_s, pool_s = [], [], [], []
    for l in range(DEPTH):
        params = (norm1_g[l], w_in[l], w_gate[l], b_gate[l], gla_norm_g[l], pool_w[l],
                  pool_scale[l], w_out[l], norm2_g[l], w_up[l], w_down[l])
        s0 = jnp.zeros((B, N_HEADS_GLA, HEAD_K, HEAD_V), jnp.float32)
        pre0 = jnp.zeros((B, 0, POOL_WIDTH), xp.dtype)
        xp, sp, bp = layer(xp, s0, pre0, *params)
        xs, ss, bs = layer(xs, state_gla[l], state_pool[l], *params)
        gla_p.append(sp.astype(state_gla.dtype))
        pool_p.append(bp.astype(state_pool.dtype))
        gla_s.append(ss.astype(state_gla.dtype))
        pool_s.append(bs.astype(state_pool.dtype))
    y_prompt = rmsnorm(xp, final_g)
    y_sample = rmsnorm(xs, final_g)
    return (y_prompt, y_sample, jnp.stack(gla_p), jnp.stack(pool_p), jnp.stack(gla_s), jnp.stack(pool_s))
```

```python
import functools

import jax
import jax.numpy as jnp
from jax import lax
from jax.experimental import pallas as pl
from jax.experimental.pallas import tpu as pltpu

D_MODEL = 1024
DEPTH = 4
N_HEADS = 4
HEAD_K = 64
HEAD_V = 128
KEY_WIDTH = N_HEADS * HEAD_K
GLA_WIDTH = N_HEADS * HEAD_V
GATE_RANK = 16
GATE_TEMP = 16.0
CHUNK = 64
POOL_WIDTH = 512
POOL_WINDOWS = (2, 4, 8, 16)
POOL_GROUP = 128
POOL_BUF = 15
POOL_HIST = 16
D_FF = 4 * D_MODEL
FF_CHUNK = 1024
EPS = 1e-6
LANES = 128

COL_Q = 0
COL_K = KEY_WIDTH
COL_V = 2 * KEY_WIDTH
COL_G = COL_V + GLA_WIDTH
COL_A = COL_G + GLA_WIDTH
COL_U = COL_A + LANES
IN_PACKED = COL_U + POOL_WIDTH

PROMPT_TILE = 256
CHANNEL_TILE = 512
SAMPLE_STATE_TILE = 16
VMEM_LIMIT = 56 * 1024 * 1024


def _rmsnorm(x, g):
    return x * lax.rsqrt(jnp.mean(x * x, axis=-1, keepdims=True) + EPS) * g


def _log_sigmoid(x):
    return jnp.minimum(x, 0.0) - jnp.log1p(jnp.exp(-jnp.abs(x)))


def _bf16_round(x):
    return x.astype(jnp.bfloat16).astype(jnp.float32)


def _head_lane_mask(rows):
    lane = lax.broadcasted_iota(jnp.int32, (rows, KEY_WIDTH), 1)
    return [(lane >= h * HEAD_K) & (lane < (h + 1) * HEAD_K) for h in range(N_HEADS)]


def _project(x, n1_ref, w_in_ref, w_gate_ref, b_gate_ref):
    h = _rmsnorm(x, n1_ref[...]).astype(jnp.bfloat16)
    z = jnp.dot(h, w_in_ref[...], preferred_element_type=jnp.float32)
    a_low = z[:, COL_A:COL_U].astype(jnp.bfloat16)
    pre = jnp.dot(a_low, w_gate_ref[...], preferred_element_type=jnp.float32) + b_gate_ref[...]
    log_a = _log_sigmoid(pre) / GATE_TEMP
    return z, log_a


def _gla_finish(o, g, gla_g):
    outs = []
    for h in range(N_HEADS):
        sl = slice(h * HEAD_V, (h + 1) * HEAD_V)
        outs.append(_rmsnorm(o[:, sl], gla_g) * (g[:, sl] * jax.nn.sigmoid(g[:, sl])))
    return jnp.concatenate(outs, axis=-1)


def _prompt_mixer_kernel(x_ref, n1_ref, w_in_ref, w_gate_ref, b_gate_ref, gla_g_ref, pool_w_ref,
                         pool_scale_ref, mix_ref, s_fin_ref, hist_out_ref, st_ref, hist_ref):
    tile = x_ref.shape[0]
    t = pl.program_id(1)

    @pl.when(t == 0)
    def _():
        st_ref[...] = jnp.zeros_like(st_ref)
        hist_ref[...] = jnp.zeros_like(hist_ref)

    z, log_a = _project(x_ref[...], n1_ref, w_in_ref, w_gate_ref, b_gate_ref)

    row = lax.broadcasted_iota(jnp.int32, (tile, KEY_WIDTH), 0) % CHUNK
    b = log_a
    shift = 1
    while shift < CHUNK:
        b = b + jnp.where(row >= shift, pltpu.roll(b, shift, axis=0), 0.0)
        shift *= 2

    head_mask = _head_lane_mask(CHUNK)
    head_mask_v = _head_lane_mask(HEAD_V)
    ci = lax.broadcasted_iota(jnp.int32, (CHUNK, CHUNK), 0)
    si = lax.broadcasted_iota(jnp.int32, (CHUNK, CHUNK), 1)
    causal = si <= ci
    scale = HEAD_K ** -0.5

    st = st_ref[...]
    for c in range(tile // CHUNK):
        rows = slice(c * CHUNK, (c + 1) * CHUNK)
        b_c = b[rows]
        b_last = b_c[CHUNK - 1:CHUNK]
        q_c = z[rows, COL_Q:COL_K]
        k_c = z[rows, COL_K:COL_V]
        v_c = z[rows, COL_V:COL_G].astype(jnp.bfloat16)
        qt = q_c * jnp.exp(b_c) * scale
        kt = (k_c * jnp.exp(-b_c)).astype(jnp.bfloat16)
        ke = (k_c * jnp.exp(b_last - b_c)).astype(jnp.bfloat16)
        decay = jnp.exp(b_last)

        q_stack = jnp.concatenate([jnp.where(m, qt, 0.0) for m in head_mask], axis=0).astype(jnp.bfloat16)
        rhs = jnp.concatenate([st.astype(jnp.bfloat16), kt], axis=0)
        r = lax.dot_general(q_stack, rhs, (((1,), (1,)), ((), ())),
                            preferred_element_type=jnp.float32)
        outs = []
        for h in range(N_HEADS):
            hr = slice(h * CHUNK, (h + 1) * CHUNK)
            o_inter = r[hr, 0:HEAD_V]
            att = jnp.where(causal, r[hr, HEAD_V:HEAD_V + CHUNK], 0.0).astype(jnp.bfloat16)
            o_intra = jnp.dot(att, v_c[:, h * HEAD_V:(h + 1) * HEAD_V], preferred_element_type=jnp.float32)
            outs.append(o_intra + o_inter)
        o = jnp.concatenate(outs, axis=-1)

        p = lax.dot_general(v_c, ke, (((0,), (0,)), ((), ())),
                            preferred_element_type=jnp.float32)
        upd = jnp.where(head_mask_v[0], p[0:HEAD_V], 0.0)
        for h in range(1, N_HEADS):
            upd = upd + jnp.where(head_mask_v[h], p[h * HEAD_V:(h + 1) * HEAD_V], 0.0)
        st = decay * st + upd

        g_c = z[rows, COL_G:COL_A]
        mix_ref[rows, 0:GLA_WIDTH] = _gla_finish(o, g_c, gla_g_ref[...]).astype(mix_ref.dtype)
    st_ref[...] = st

    u = z[:, COL_U:IN_PACKED]
    ext = jnp.concatenate([hist_ref[...], u], axis=0)
    pos = t * tile + lax.broadcasted_iota(jnp.int32, (tile, POOL_GROUP), 0)
    for gi, w in enumerate(POOL_WINDOWS):
        lanes = slice(gi * POOL_GROUP, (gi + 1) * POOL_GROUP)
        s = ext[:, lanes]
        shift = 1
        while shift < w:
            s = s + pltpu.roll(s, shift, axis=0)
            shift *= 2
        count = jnp.minimum(pos + 1, w).astype(jnp.float32)
        pooled = s[POOL_HIST:] / count - u[:, lanes]
        pg = jnp.dot(pooled.astype(jnp.bfloat16), pool_w_ref[gi], preferred_element_type=jnp.float32)
        pg = pg * pool_scale_ref[:, lanes]
        mix_ref[:, GLA_WIDTH + gi * POOL_GROUP:GLA_WIDTH + (gi + 1) * POOL_GROUP] = pg.astype(mix_ref.dtype)
    hist_ref[...] = u[tile - POOL_HIST:]

    @pl.when(t == pl.num_programs(1) - 1)
    def _():
        s_fin_ref[...] = st.T
        hist_out_ref[...] = u[tile - POOL_HIST:]


def _prompt_mixer(layer, x, p):
    B, T, _ = x.shape
    tile = PROMPT_TILE
    whole = lambda *shape: pl.BlockSpec((None,) + shape, lambda b, t: (layer,) + (0,) * len(shape))
    return pl.pallas_call(
        _prompt_mixer_kernel,
        grid=(B, T // tile),
        in_specs=[
            pl.BlockSpec((None, tile, D_MODEL), lambda b, t: (b, t, 0)),
            whole(1, D_MODEL),
            whole(D_MODEL, IN_PACKED),
            whole(LANES, KEY_WIDTH),
            whole(1, KEY_WIDTH),
            whole(1, HEAD_V),
            whole(len(POOL_WINDOWS), POOL_GROUP, POOL_GROUP),
            whole(1, POOL_WIDTH),
        ],
        out_specs=[
            pl.BlockSpec((None, tile, D_MODEL), lambda b, t: (b, t, 0)),
            pl.BlockSpec((None, KEY_WIDTH, HEAD_V), lambda b, t: (b, 0, 0)),
            pl.BlockSpec((None, POOL_HIST, POOL_WIDTH), lambda b, t: (b, 0, 0)),
        ],
        out_shape=[
            jax.ShapeDtypeStruct((B, T, D_MODEL), jnp.bfloat16),
            jax.ShapeDtypeStruct((B, KEY_WIDTH, HEAD_V), jnp.float32),
            jax.ShapeDtypeStruct((B, POOL_HIST, POOL_WIDTH), jnp.float32),
        ],
        scratch_shapes=[
            pltpu.VMEM((HEAD_V, KEY_WIDTH), jnp.float32),
            pltpu.VMEM((POOL_HIST, POOL_WIDTH), jnp.float32),
        ],
        compiler_params=pltpu.CompilerParams(
            dimension_semantics=("parallel", "arbitrary"), vmem_limit_bytes=VMEM_LIMIT),
        name=f"prompt_mixer_l{layer}",
    )(x, p["n1"], p["w_in"], p["w_gate"], p["b_gate"], p["gla_g"], p["pool_w"], p["pool_scale"])


def _sample_mixer_kernel(x_ref, n1_ref, w_in_ref, w_gate_ref, b_gate_ref, gla_g_ref, pool_w_ref,
                         pool_scale_ref, s_ref, pool_state_ref,
                         mix_ref, s_new_ref, pool_new_ref,
                         dec_t_ref, ke_t_ref, qt_t_ref, v_ref, g_ref, o_ref):
    i = pl.program_id(0)
    n_seq = x_ref.shape[0]
    seq_tile = s_ref.shape[0]

    @pl.when(i == 0)
    def _():
        z, log_a = _project(x_ref[...], n1_ref, w_in_ref, w_gate_ref, b_gate_ref)
        q = z[:, COL_Q:COL_K]
        k = z[:, COL_K:COL_V]
        v = _bf16_round(z[:, COL_V:COL_G])
        qt = _bf16_round(q * jnp.exp(log_a) * (HEAD_K ** -0.5))
        kt = _bf16_round(k * jnp.exp(-log_a))
        ke = _bf16_round(k * jnp.exp(log_a - log_a))
        prod = qt * kt
        o_intra = []
        for h, m in enumerate(_head_lane_mask(n_seq)):
            att = _bf16_round(jnp.sum(jnp.where(m, prod, 0.0), axis=-1, keepdims=True))
            o_intra.append(att * v[:, h * HEAD_V:(h + 1) * HEAD_V])
        o_ref[...] = jnp.concatenate(o_intra, axis=-1)
        dec_t_ref[...] = jnp.exp(log_a).T
        ke_t_ref[...] = ke.T
        qt_t_ref[...] = qt.T
        v_ref[...] = v
        g_ref[...] = z[:, COL_G:COL_A]

        u = z[:, COL_U:IN_PACKED]
        for gi, w in enumerate(POOL_WINDOWS):
            lanes = slice(gi * POOL_GROUP, (gi + 1) * POOL_GROUP)
            s = u[:, lanes]
            for j in range(POOL_BUF - (w - 1), POOL_BUF):
                s = s + pool_state_ref[:, j * POOL_WIDTH + gi * POOL_GROUP:j * POOL_WIDTH + (gi + 1) * POOL_GROUP]
            pooled = s / float(w) - u[:, lanes]
            pg = jnp.dot(pooled.astype(jnp.bfloat16), pool_w_ref[gi], preferred_element_type=jnp.float32)
            pg = pg * pool_scale_ref[:, lanes]
            mix_ref[:, GLA_WIDTH + gi * POOL_GROUP:GLA_WIDTH + (gi + 1) * POOL_GROUP] = pg.astype(mix_ref.dtype)
        pool_new_ref[:, 0:(POOL_BUF - 1) * POOL_WIDTH] = pool_state_ref[:, POOL_WIDTH:POOL_BUF * POOL_WIDTH]
        pool_new_ref[:, (POOL_BUF - 1) * POOL_WIDTH:POOL_BUF * POOL_WIDTH] = u

    lane = lax.broadcasted_iota(jnp.int32, (KEY_WIDTH, n_seq), 1)

    def one_sequence(j, carry):
        seq = i * seq_tile + j
        pick = lane == seq
        column = lambda ref: jnp.sum(jnp.where(pick, ref[...], 0.0), axis=-1, keepdims=True)
        dec, ke, qt = column(dec_t_ref), column(ke_t_ref), column(qt_t_ref)
        s_old = s_ref[j]
        v_row = v_ref[pl.ds(seq, 1), :]
        v_rows = jnp.concatenate(
            [jnp.broadcast_to(v_row[:, h * HEAD_V:(h + 1) * HEAD_V], (HEAD_K, HEAD_V)) for h in range(N_HEADS)],
            axis=0)
        s_new_ref[j] = dec * s_old + ke * v_rows
        weighted = qt * s_old
        o_inter = jnp.concatenate(
            [jnp.sum(weighted[h * HEAD_K:(h + 1) * HEAD_K], axis=0, keepdims=True) for h in range(N_HEADS)],
            axis=-1)
        o_ref[pl.ds(seq, 1), :] = o_ref[pl.ds(seq, 1), :] + o_inter
        return carry

    lax.fori_loop(0, seq_tile, one_sequence, 0)

    @pl.when(i == pl.num_programs(0) - 1)
    def _():
        mix_ref[:, 0:GLA_WIDTH] = _gla_finish(o_ref[...], g_ref[...], gla_g_ref[...]).astype(mix_ref.dtype)


def _sample_mixer(layer, x, s_state, pool_state, p):
    n_seq = x.shape[0]
    seq_tile = SAMPLE_STATE_TILE
    whole = lambda *shape: pl.BlockSpec((None,) + shape, lambda i: (layer,) + (0,) * len(shape))
    fixed = lambda *shape: pl.BlockSpec(shape, lambda i: (0,) * len(shape))
    f32 = jnp.float32
    return pl.pallas_call(
        _sample_mixer_kernel,
        grid=(n_seq // seq_tile,),
        in_specs=[
            fixed(n_seq, D_MODEL),
            whole(1, D_MODEL),
            whole(D_MODEL, IN_PACKED),
            whole(LANES, KEY_WIDTH),
            whole(1, KEY_WIDTH),
            whole(1, HEAD_V),
            whole(len(POOL_WINDOWS), POOL_GROUP, POOL_GROUP),
            whole(1, POOL_WIDTH),
            pl.BlockSpec((None, seq_tile, KEY_WIDTH, HEAD_V), lambda i: (layer, i, 0, 0)),
            whole(n_seq, POOL_BUF * POOL_WIDTH),
        ],
        out_specs=[
            fixed(n_seq, D_MODEL),
            pl.BlockSpec((seq_tile, KEY_WIDTH, HEAD_V), lambda i: (i, 0, 0)),
            fixed(n_seq, POOL_BUF * POOL_WIDTH),
        ],
        out_shape=[
            jax.ShapeDtypeStruct((n_seq, D_MODEL), jnp.bfloat16),
            jax.ShapeDtypeStruct((n_seq, KEY_WIDTH, HEAD_V), f32),
            jax.ShapeDtypeStruct((n_seq, POOL_BUF * POOL_WIDTH), f32),
        ],
        scratch_shapes=[
            pltpu.VMEM((KEY_WIDTH, n_seq), f32),
            pltpu.VMEM((KEY_WIDTH, n_seq), f32),
            pltpu.VMEM((KEY_WIDTH, n_seq), f32),
            pltpu.VMEM((n_seq, GLA_WIDTH), f32),
            pltpu.VMEM((n_seq, GLA_WIDTH), f32),
            pltpu.VMEM((n_seq, GLA_WIDTH), f32),
        ],
        compiler_params=pltpu.CompilerParams(
            dimension_semantics=("arbitrary",), vmem_limit_bytes=VMEM_LIMIT),
        name=f"sample_mixer_l{layer}",
    )(x, p["n1"], p["w_in"], p["w_gate"], p["b_gate"], p["gla_g"], p["pool_w"], p["pool_scale"],
      s_state, pool_state)


def _channel_kernel(x_ref, mix_ref, w_out_ref, n2_ref, w_up_ref, w_down_ref, final_g_ref, y_ref, *, final):
    x1 = x_ref[...] + jnp.dot(mix_ref[...], w_out_ref[...], preferred_element_type=jnp.float32)
    h2 = _rmsnorm(x1, n2_ref[...]).astype(jnp.bfloat16)
    acc = x1
    for j in range(D_FF // FF_CHUNK):
        cols = slice(j * FF_CHUNK, (j + 1) * FF_CHUNK)
        hid = jnp.dot(h2, w_up_ref[:, cols], preferred_element_type=jnp.float32)
        act = jnp.square(jnp.maximum(hid, 0.0)).astype(jnp.bfloat16)
        acc = acc + jnp.dot(act, w_down_ref[cols, :], preferred_element_type=jnp.float32)
    if final:
        acc = _rmsnorm(acc, final_g_ref[...])
    y_ref[...] = acc


def _channel_mixer(layer, x, mix, p, final_g, tile):
    rows = x.shape[0]
    whole = lambda *shape: pl.BlockSpec((None,) + shape, lambda i: (layer,) + (0,) * len(shape),
                                        pipeline_mode=pl.Buffered(1))
    return pl.pallas_call(
        functools.partial(_channel_kernel, final=layer == DEPTH - 1),
        grid=(rows // tile,),
        in_specs=[
            pl.BlockSpec((tile, D_MODEL), lambda i: (i, 0)),
            pl.BlockSpec((tile, D_MODEL), lambda i: (i, 0)),
            whole(D_MODEL, D_MODEL),
            whole(1, D_MODEL),
            whole(D_MODEL, D_FF),
            whole(D_FF, D_MODEL),
            pl.BlockSpec((1, D_MODEL), lambda i: (0, 0)),
        ],
        out_specs=pl.BlockSpec((tile, D_MODEL), lambda i: (i, 0)),
        out_shape=jax.ShapeDtypeStruct((rows, D_MODEL), jnp.float32),
        compiler_params=pltpu.CompilerParams(
            dimension_semantics=("parallel",), vmem_limit_bytes=VMEM_LIMIT),
        name=f"channel_mixer_l{layer}_r{rows}",
    )(x, mix, p["w_out"], p["n2"], p["w_up"], p["w_down"], final_g)


def kernel(x_prompt, x_sample, state_gla, state_pool, norm1_g, w_in, w_gate, b_gate, gla_norm_g, pool_w,
           pool_scale, w_out, norm2_g, w_up, w_down, final_g):
    B, T, _ = x_prompt.shape
    n_seq = x_sample.shape[0]
    bf16 = jnp.bfloat16

    a_cols = jnp.pad(w_in[:, :, COL_A:COL_A + GATE_RANK], ((0, 0), (0, 0), (0, LANES - GATE_RANK)))
    w_in_p = jnp.concatenate([w_in[:, :, :COL_A], a_cols, w_in[:, :, COL_A + GATE_RANK:]], axis=-1).astype(bf16)
    params = {
        "n1": norm1_g.reshape(DEPTH, 1, D_MODEL),
        "w_in": w_in_p,
        "w_gate": jnp.pad(w_gate, ((0, 0), (0, LANES - GATE_RANK), (0, 0))).astype(bf16),
        "b_gate": b_gate.reshape(DEPTH, 1, KEY_WIDTH),
        "gla_g": gla_norm_g.reshape(DEPTH, 1, HEAD_V),
        "pool_w": pool_w.astype(bf16),
        "pool_scale": pool_scale.reshape(DEPTH, 1, POOL_WIDTH),
        "w_out": w_out.astype(bf16),
        "n2": norm2_g.reshape(DEPTH, 1, D_MODEL),
        "w_up": w_up.astype(bf16),
        "w_down": w_down.astype(bf16),
    }
    final_g2 = final_g.reshape(1, D_MODEL)
    s_state = state_gla.reshape(DEPTH, n_seq, KEY_WIDTH, HEAD_V)
    pool_state = state_pool.reshape(DEPTH, n_seq, POOL_BUF * POOL_WIDTH)

    xp = x_prompt
    xs = x_sample.reshape(n_seq, D_MODEL)
    gla_p, pool_p, gla_s, pool_s = [], [], [], []
    for layer in range(DEPTH):
        mix_p, s_fin, hist = _prompt_mixer(layer, xp, params)
        xp = _channel_mixer(layer, xp.reshape(B * T, D_MODEL), mix_p.reshape(B * T, D_MODEL), params,
                            final_g2, CHANNEL_TILE).reshape(B, T, D_MODEL)
        mix_s, s_new, pool_new = _sample_mixer(layer, xs, s_state, pool_state, params)
        xs = _channel_mixer(layer, xs, mix_s, params, final_g2, n_seq)
        gla_p.append(s_fin.reshape(B, N_HEADS, HEAD_K, HEAD_V))
        pool_p.append(hist[:, POOL_HIST - POOL_BUF:])
        gla_s.append(s_new.reshape(n_seq, N_HEADS, HEAD_K, HEAD_V))
        pool_s.append(pool_new.reshape(n_seq, POOL_BUF, POOL_WIDTH))
    return (xp, xs.reshape(n_seq, 1, D_MODEL), jnp.stack(gla_p), jnp.stack(pool_p),
            jnp.stack(gla_s), jnp.stack(pool_s))
```

```python
import functools

import jax
import jax.numpy as jnp
from jax import lax
from jax.experimental import pallas as pl
from jax.experimental.pallas import tpu as pltpu

D_MODEL = 1024
DEPTH = 4
N_HEADS = 4
HEAD_K = 64
HEAD_V = 128
KEY_WIDTH = N_HEADS * HEAD_K
GLA_WIDTH = N_HEADS * HEAD_V
GATE_RANK = 16
GATE_TEMP = 16.0
CHUNK = 64
POOL_WIDTH = 512
POOL_WINDOWS = (2, 4, 8, 16)
POOL_GROUP = 128
POOL_BUF = 15
POOL_HIST = 16
D_FF = 4 * D_MODEL
FF_CHUNK = 1024
EPS = 1e-6
LANES = 128

COL_Q = 0
COL_K = KEY_WIDTH
COL_V = 2 * KEY_WIDTH
COL_G = COL_V + GLA_WIDTH
COL_A = COL_G + GLA_WIDTH
COL_U = COL_A + LANES
IN_PACKED = COL_U + POOL_WIDTH

PROMPT_TILE = 512
PROMPT_SUB = 128
PROJ_PIECE = 256
CHANNEL_TILE = 512
SAMPLE_STATE_TILE = 16
VMEM_LIMIT = 56 * 1024 * 1024


def _rmsnorm(x, g):
    return x * lax.rsqrt(jnp.mean(x * x, axis=-1, keepdims=True) + EPS) * g


def _log_sigmoid(x):
    return jnp.minimum(x, 0.0) - jnp.log1p(jnp.exp(-jnp.abs(x)))


def _bf16_round(x):
    return x.astype(jnp.bfloat16).astype(jnp.float32)


def _head_lane_mask(rows):
    lane = lax.broadcasted_iota(jnp.int32, (rows, KEY_WIDTH), 1)
    return [(lane >= h * HEAD_K) & (lane < (h + 1) * HEAD_K) for h in range(N_HEADS)]


def _gate_log_decay(a_low, w_gate_ref, b_gate_ref):
    pre = jnp.dot(a_low.astype(jnp.bfloat16), w_gate_ref[...], preferred_element_type=jnp.float32) + b_gate_ref[...]
    return _log_sigmoid(pre) / GATE_TEMP


def _project(x, n1_ref, w_in_ref, w_gate_ref, b_gate_ref):
    h = _rmsnorm(x, n1_ref[...]).astype(jnp.bfloat16)
    z = jnp.dot(h, w_in_ref[...], preferred_element_type=jnp.float32)
    return z, _gate_log_decay(z[:, COL_A:COL_U], w_gate_ref, b_gate_ref)


def _gla_finish(o, g, gla_g):
    outs = []
    for h in range(N_HEADS):
        sl = slice(h * HEAD_V, (h + 1) * HEAD_V)
        outs.append(_rmsnorm(o[:, sl], gla_g) * (g[:, sl] * jax.nn.sigmoid(g[:, sl])))
    return jnp.concatenate(outs, axis=-1)


def _trace_interleaved(a, b):
    i = j = 0
    while i < len(a) or j < len(b):
        if j >= len(b) or (i < len(a) and i * len(b) <= j * len(a)):
            a[i]()
            i += 1
        else:
            b[j]()
            j += 1


def _prompt_project_tasks(load_x, slot, n1_ref, w_in_ref, w_gate_ref, b_gate_ref, z_ref, b_ref):
    v = {}

    def norm():
        v["h"] = _rmsnorm(load_x(), n1_ref[...]).astype(jnp.bfloat16)

    def piece(lo, hi):
        def run():
            z_ref[slot, :, lo:hi] = jnp.dot(v["h"], w_in_ref[:, lo:hi], preferred_element_type=jnp.float32)
        return run

    def gate_code():
        v["a_low"] = jnp.dot(v["h"], w_in_ref[:, COL_A:COL_U], preferred_element_type=jnp.float32)

    def gate_pre():
        v["pre"] = (jnp.dot(v["a_low"].astype(jnp.bfloat16), w_gate_ref[...], preferred_element_type=jnp.float32)
                    + b_gate_ref[...])

    def log_decay():
        v["b"] = _log_sigmoid(v["pre"]) / GATE_TEMP

    def scan(shifts, last):
        def run():
            b = v["b"]
            row = lax.broadcasted_iota(jnp.int32, b.shape, 0) % CHUNK
            for shift in shifts:
                b = b + jnp.where(row >= shift, pltpu.roll(b, shift, axis=0), 0.0)
            v["b"] = b
            if last:
                b_ref[slot] = b
        return run

    half = PROJ_PIECE
    return [
        norm, gate_code,
        piece(COL_Q, COL_K), gate_pre,
        piece(COL_K, COL_V), log_decay,
        piece(COL_V, COL_V + half), scan((1, 2), False),
        piece(COL_V + half, COL_G), scan((4, 8), False),
        piece(COL_G, COL_G + half), scan((16, 32), True),
        piece(COL_G + half, COL_A), piece(COL_U, COL_U + half), piece(COL_U + half, IN_PACKED),
    ]


def _prompt_mix_tasks(slot, z_ref, b_ref, carry, first_pos, gla_g_ref, pool_w_ref, pool_scale_ref, mix_ref, base):
    head_mask = _head_lane_mask(CHUNK)
    head_mask_v = _head_lane_mask(HEAD_V)
    ci = lax.broadcasted_iota(jnp.int32, (CHUNK, CHUNK), 0)
    si = lax.broadcasted_iota(jnp.int32, (CHUNK, CHUNK), 1)
    causal = si <= ci
    scale = HEAD_K ** -0.5
    sub = z_ref.shape[1]

    def chunk_tasks(c):
        rows = slice(c * CHUNK, (c + 1) * CHUNK)
        out_rows = slice(base + c * CHUNK, base + (c + 1) * CHUNK)
        v = {}

        def prepare():
            b_c = b_ref[slot, rows, :]
            b_last = b_c[CHUNK - 1:CHUNK]
            q_c = z_ref[slot, rows, COL_Q:COL_K]
            k_c = z_ref[slot, rows, COL_K:COL_V]
            v["v"] = z_ref[slot, rows, COL_V:COL_G].astype(jnp.bfloat16)
            qt = q_c * jnp.exp(b_c) * scale
            v["kt"] = (k_c * jnp.exp(-b_c)).astype(jnp.bfloat16)
            v["ke"] = (k_c * jnp.exp(b_last - b_c)).astype(jnp.bfloat16)
            v["decay"] = jnp.exp(b_last)
            v["q_stack"] = jnp.concatenate([jnp.where(m, qt, 0.0) for m in head_mask], axis=0).astype(jnp.bfloat16)

        def update_product():
            v["p"] = lax.dot_general(v["v"], v["ke"], (((0,), (0,)), ((), ())),
                                     preferred_element_type=jnp.float32)

        def query_product():
            st = carry["st"]
            rhs = jnp.concatenate([st.astype(jnp.bfloat16), v["kt"]], axis=0)
            v["r"] = lax.dot_general(v["q_stack"], rhs, (((1,), (1,)), ((), ())),
                                     preferred_element_type=jnp.float32)

        def update_state():
            p = v["p"]
            upd = jnp.where(head_mask_v[0], p[0:HEAD_V], 0.0)
            for h in range(1, N_HEADS):
                upd = upd + jnp.where(head_mask_v[h], p[h * HEAD_V:(h + 1) * HEAD_V], 0.0)
            carry["st"] = v["decay"] * carry["st"] + upd

        def intra():
            r = v["r"]
            outs = []
            for h in range(N_HEADS):
                hr = slice(h * CHUNK, (h + 1) * CHUNK)
                att = jnp.where(causal, r[hr, HEAD_V:HEAD_V + CHUNK], 0.0).astype(jnp.bfloat16)
                o_intra = jnp.dot(att, v["v"][:, h * HEAD_V:(h + 1) * HEAD_V], preferred_element_type=jnp.float32)
                outs.append(o_intra + r[hr, 0:HEAD_V])
            v["o"] = jnp.concatenate(outs, axis=-1)

        def finish():
            g_c = z_ref[slot, rows, COL_G:COL_A]
            mix_ref[out_rows, 0:GLA_WIDTH] = _gla_finish(v["o"], g_c, gla_g_ref[...]).astype(mix_ref.dtype)

        return dict(prepare=prepare, update_product=update_product, query_product=query_product,
                    update_state=update_state, intra=intra, finish=finish)

    def pool_tasks(gi, w):
        v = {}
        lanes = slice(gi * POOL_GROUP, (gi + 1) * POOL_GROUP)

        def window_mean():
            u = z_ref[slot, :, COL_U + gi * POOL_GROUP:COL_U + (gi + 1) * POOL_GROUP]
            s = jnp.concatenate([carry["hist"][:, lanes], u], axis=0)
            shift = 1
            while shift < w:
                s = s + pltpu.roll(s, shift, axis=0)
                shift *= 2
            pos = first_pos + lax.broadcasted_iota(jnp.int32, (sub, POOL_GROUP), 0)
            count = jnp.minimum(pos + 1, w).astype(jnp.float32)
            v["pooled"] = (s[POOL_HIST:] / count - u).astype(jnp.bfloat16)

        def group_map():
            pg = jnp.dot(v["pooled"], pool_w_ref[gi], preferred_element_type=jnp.float32)
            pg = pg * pool_scale_ref[:, lanes]
            mix_ref[base:base + sub, GLA_WIDTH + gi * POOL_GROUP:GLA_WIDTH + (gi + 1) * POOL_GROUP] = (
                pg.astype(mix_ref.dtype))

        return window_mean, group_map

    def keep_history():
        carry["hist"] = z_ref[slot, sub - POOL_HIST:sub, COL_U:IN_PACKED]

    assert sub == 2 * CHUNK
    c0, c1 = chunk_tasks(0), chunk_tasks(1)
    pools = [pool_tasks(gi, w) for gi, w in enumerate(POOL_WINDOWS)]
    return [
        c0["prepare"], c1["prepare"], c0["update_product"], c0["query_product"], c1["update_product"],
        c0["update_state"], c1["query_product"], pools[0][0], c0["intra"], c1["update_state"],
        pools[1][0], c1["intra"], pools[0][1], c0["finish"], pools[1][1], pools[2][0], c1["finish"],
        pools[3][0], pools[2][1], pools[3][1], keep_history,
    ]


def _prompt_mixer_kernel(x_ref, x_next_ref, n1_ref, w_in_ref, w_gate_ref, b_gate_ref, gla_g_ref, pool_w_ref,
                         pool_scale_ref, mix_ref, s_fin_ref, hist_out_ref, st_ref, hist_ref, z_ref, b_ref):
    tile = x_ref.shape[0]
    n_sub = tile // PROMPT_SUB
    t = pl.program_id(1)

    def project_tasks(load_x, slot):
        return _prompt_project_tasks(load_x, slot, n1_ref, w_in_ref, w_gate_ref, b_gate_ref, z_ref, b_ref)

    @pl.when(t == 0)
    def _():
        st_ref[...] = jnp.zeros_like(st_ref)
        hist_ref[...] = jnp.zeros_like(hist_ref)
        for task in project_tasks(lambda: x_ref[0:PROMPT_SUB], 0):
            task()

    carry = {"st": st_ref[...], "hist": hist_ref[...]}
    for sb in range(n_sub):
        if sb + 1 < n_sub:
            load_next = lambda sb=sb: x_ref[(sb + 1) * PROMPT_SUB:(sb + 2) * PROMPT_SUB]
        else:
            load_next = lambda: x_next_ref[...]
        _trace_interleaved(
            project_tasks(load_next, (sb + 1) % 2),
            _prompt_mix_tasks(sb % 2, z_ref, b_ref, carry, t * tile + sb * PROMPT_SUB,
                              gla_g_ref, pool_w_ref, pool_scale_ref, mix_ref, sb * PROMPT_SUB))
    st_ref[...] = carry["st"]
    hist_ref[...] = carry["hist"]

    @pl.when(t == pl.num_programs(1) - 1)
    def _():
        s_fin_ref[...] = carry["st"].T
        hist_out_ref[...] = carry["hist"]


def _prompt_mixer(layer, x, p):
    B, T, _ = x.shape
    tile = PROMPT_TILE
    n_tiles = T // tile
    n_sub = tile // PROMPT_SUB
    assert n_sub % 2 == 0
    whole = lambda *shape: pl.BlockSpec((None,) + shape, lambda b, t: (layer,) + (0,) * len(shape))
    return pl.pallas_call(
        _prompt_mixer_kernel,
        grid=(B, n_tiles),
        in_specs=[
            pl.BlockSpec((None, tile, D_MODEL), lambda b, t: (b, t, 0)),
            pl.BlockSpec((None, PROMPT_SUB, D_MODEL), lambda b, t: (b, jnp.minimum(t + 1, n_tiles - 1) * n_sub, 0)),
            whole(1, D_MODEL),
            whole(D_MODEL, IN_PACKED),
            whole(LANES, KEY_WIDTH),
            whole(1, KEY_WIDTH),
            whole(1, HEAD_V),
            whole(len(POOL_WINDOWS), POOL_GROUP, POOL_GROUP),
            whole(1, POOL_WIDTH),
        ],
        out_specs=[
            pl.BlockSpec((None, tile, D_MODEL), lambda b, t: (b, t, 0)),
            pl.BlockSpec((None, KEY_WIDTH, HEAD_V), lambda b, t: (b, 0, 0)),
            pl.BlockSpec((None, POOL_HIST, POOL_WIDTH), lambda b, t: (b, 0, 0)),
        ],
        out_shape=[
            jax.ShapeDtypeStruct((B, T, D_MODEL), jnp.bfloat16),
            jax.ShapeDtypeStruct((B, KEY_WIDTH, HEAD_V), jnp.float32),
            jax.ShapeDtypeStruct((B, POOL_HIST, POOL_WIDTH), jnp.float32),
        ],
        scratch_shapes=[
            pltpu.VMEM((HEAD_V, KEY_WIDTH), jnp.float32),
            pltpu.VMEM((POOL_HIST, POOL_WIDTH), jnp.float32),
            pltpu.VMEM((2, PROMPT_SUB, IN_PACKED), jnp.float32),
            pltpu.VMEM((2, PROMPT_SUB, KEY_WIDTH), jnp.float32),
        ],
        compiler_params=pltpu.CompilerParams(
            dimension_semantics=("parallel", "arbitrary"), vmem_limit_bytes=VMEM_LIMIT),
        name=f"prompt_mixer_l{layer}",
    )(x, x, p["n1"], p["w_in"], p["w_gate"], p["b_gate"], p["gla_g"], p["pool_w"], p["pool_scale"])


def _sample_mixer_kernel(x_ref, n1_ref, w_in_ref, w_gate_ref, b_gate_ref, gla_g_ref, pool_w_ref,
                         pool_scale_ref, s_ref, pool_state_ref,
                         mix_ref, s_new_ref, pool_new_ref,
                         dec_t_ref, ke_t_ref, qt_t_ref, v_ref, g_ref, o_ref):
    i = pl.program_id(0)
    n_seq = x_ref.shape[0]
    seq_tile = s_ref.shape[0]

    @pl.when(i == 0)
    def _():
        z, log_a = _project(x_ref[...], n1_ref, w_in_ref, w_gate_ref, b_gate_ref)
        q = z[:, COL_Q:COL_K]
        k = z[:, COL_K:COL_V]
        v = _bf16_round(z[:, COL_V:COL_G])
        qt = _bf16_round(q * jnp.exp(log_a) * (HEAD_K ** -0.5))
        kt = _bf16_round(k * jnp.exp(-log_a))
        ke = _bf16_round(k * jnp.exp(log_a - log_a))
        prod = qt * kt
        o_intra = []
        for h, m in enumerate(_head_lane_mask(n_seq)):
            att = _bf16_round(jnp.sum(jnp.where(m, prod, 0.0), axis=-1, keepdims=True))
            o_intra.append(att * v[:, h * HEAD_V:(h + 1) * HEAD_V])
        o_ref[...] = jnp.concatenate(o_intra, axis=-1)
        dec_t_ref[...] = jnp.exp(log_a).T
        ke_t_ref[...] = ke.T
        qt_t_ref[...] = qt.T
        v_ref[...] = v
        g_ref[...] = z[:, COL_G:COL_A]

        u = z[:, COL_U:IN_PACKED]
        for gi, w in enumerate(POOL_WINDOWS):
            lanes = slice(gi * POOL_GROUP, (gi + 1) * POOL_GROUP)
            s = u[:, lanes]
            for j in range(POOL_BUF - (w - 1), POOL_BUF):
                s = s + pool_state_ref[:, j * POOL_WIDTH + gi * POOL_GROUP:j * POOL_WIDTH + (gi + 1) * POOL_GROUP]
            pooled = s / float(w) - u[:, lanes]
            pg = jnp.dot(pooled.astype(jnp.bfloat16), pool_w_ref[gi], preferred_element_type=jnp.float32)
            pg = pg * pool_scale_ref[:, lanes]
            mix_ref[:, GLA_WIDTH + gi * POOL_GROUP:GLA_WIDTH + (gi + 1) * POOL_GROUP] = pg.astype(mix_ref.dtype)
        pool_new_ref[:, 0:(POOL_BUF - 1) * POOL_WIDTH] = pool_state_ref[:, POOL_WIDTH:POOL_BUF * POOL_WIDTH]
        pool_new_ref[:, (POOL_BUF - 1) * POOL_WIDTH:POOL_BUF * POOL_WIDTH] = u

    lane = lax.broadcasted_iota(jnp.int32, (KEY_WIDTH, n_seq), 1)

    def one_sequence(j, carry):
        seq = i * seq_tile + j
        pick = lane == seq
        column = lambda ref: jnp.sum(jnp.where(pick, ref[...], 0.0), axis=-1, keepdims=True)
        dec, ke, qt = column(dec_t_ref), column(ke_t_ref), column(qt_t_ref)
        s_old = s_ref[j]
        v_row = v_ref[pl.ds(seq, 1), :]
        v_rows = jnp.concatenate(
            [jnp.broadcast_to(v_row[:, h * HEAD_V:(h + 1) * HEAD_V], (HEAD_K, HEAD_V)) for h in range(N_HEADS)],
            axis=0)
        s_new_ref[j] = dec * s_old + ke * v_rows
        weighted = qt * s_old
        o_inter = jnp.concatenate(
            [jnp.sum(weighted[h * HEAD_K:(h + 1) * HEAD_K], axis=0, keepdims=True) for h in range(N_HEADS)],
            axis=-1)
        o_ref[pl.ds(seq, 1), :] = o_ref[pl.ds(seq, 1), :] + o_inter
        return carry

    lax.fori_loop(0, seq_tile, one_sequence, 0)

    @pl.when(i == pl.num_programs(0) - 1)
    def _():
        mix_ref[:, 0:GLA_WIDTH] = _gla_finish(o_ref[...], g_ref[...], gla_g_ref[...]).astype(mix_ref.dtype)


def _sample_mixer(layer, x, s_state, pool_state, p):
    n_seq = x.shape[0]
    seq_tile = SAMPLE_STATE_TILE
    whole = lambda *shape: pl.BlockSpec((None,) + shape, lambda i: (layer,) + (0,) * len(shape))
    fixed = lambda *shape: pl.BlockSpec(shape, lambda i: (0,) * len(shape))
    f32 = jnp.float32
    return pl.pallas_call(
        _sample_mixer_kernel,
        grid=(n_seq // seq_tile,),
        in_specs=[
            fixed(n_seq, D_MODEL),
            whole(1, D_MODEL),
            whole(D_MODEL, IN_PACKED),
            whole(LANES, KEY_WIDTH),
            whole(1, KEY_WIDTH),
            whole(1, HEAD_V),
            whole(len(POOL_WINDOWS), POOL_GROUP, POOL_GROUP),
            whole(1, POOL_WIDTH),
            pl.BlockSpec((None, seq_tile, KEY_WIDTH, HEAD_V), lambda i: (layer, i, 0, 0)),
            whole(n_seq, POOL_BUF * POOL_WIDTH),
        ],
        out_specs=[
            fixed(n_seq, D_MODEL),
            pl.BlockSpec((seq_tile, KEY_WIDTH, HEAD_V), lambda i: (i, 0, 0)),
            fixed(n_seq, POOL_BUF * POOL_WIDTH),
        ],
        out_shape=[
            jax.ShapeDtypeStruct((n_seq, D_MODEL), jnp.bfloat16),
            jax.ShapeDtypeStruct((n_seq, KEY_WIDTH, HEAD_V), f32),
            jax.ShapeDtypeStruct((n_seq, POOL_BUF * POOL_WIDTH), f32),
        ],
        scratch_shapes=[
            pltpu.VMEM((KEY_WIDTH, n_seq), f32),
            pltpu.VMEM((KEY_WIDTH, n_seq), f32),
            pltpu.VMEM((KEY_WIDTH, n_seq), f32),
            pltpu.VMEM((n_seq, GLA_WIDTH), f32),
            pltpu.VMEM((n_seq, GLA_WIDTH), f32),
            pltpu.VMEM((n_seq, GLA_WIDTH), f32),
        ],
        compiler_params=pltpu.CompilerParams(
            dimension_semantics=("arbitrary",), vmem_limit_bytes=VMEM_LIMIT),
        name=f"sample_mixer_l{layer}",
    )(x, p["n1"], p["w_in"], p["w_gate"], p["b_gate"], p["gla_g"], p["pool_w"], p["pool_scale"],
      s_state, pool_state)


def _channel_kernel(x_ref, mix_ref, w_out_ref, n2_ref, w_up_ref, w_down_ref, final_g_ref, y_ref, *, final):
    x1 = x_ref[...] + jnp.dot(mix_ref[...], w_out_ref[...], preferred_element_type=jnp.float32)
    h2 = _rmsnorm(x1, n2_ref[...]).astype(jnp.bfloat16)
    acc = x1
    for j in range(D_FF // FF_CHUNK):
        cols = slice(j * FF_CHUNK, (j + 1) * FF_CHUNK)
        hid = jnp.dot(h2, w_up_ref[:, cols], preferred_element_type=jnp.float32)
        act = jnp.square(jnp.maximum(hid, 0.0)).astype(jnp.bfloat16)
        acc = acc + jnp.dot(act, w_down_ref[cols, :], preferred_element_type=jnp.float32)
    if final:
        acc = _rmsnorm(acc, final_g_ref[...])
    y_ref[...] = acc


def _channel_mixer(layer, x, mix, p, final_g, tile):
    rows = x.shape[0]
    whole = lambda *shape: pl.BlockSpec((None,) + shape, lambda i: (layer,) + (0,) * len(shape),
                                        pipeline_mode=pl.Buffered(1))
    return pl.pallas_call(
        functools.partial(_channel_kernel, final=layer == DEPTH - 1),
        grid=(rows // tile,),
        in_specs=[
            pl.BlockSpec((tile, D_MODEL), lambda i: (i, 0)),
            pl.BlockSpec((tile, D_MODEL), lambda i: (i, 0)),
            whole(D_MODEL, D_MODEL),
            whole(1, D_MODEL),
            whole(D_MODEL, D_FF),
            whole(D_FF, D_MODEL),
            pl.BlockSpec((1, D_MODEL), lambda i: (0, 0)),
        ],
        out_specs=pl.BlockSpec((tile, D_MODEL), lambda i: (i, 0)),
        out_shape=jax.ShapeDtypeStruct((rows, D_MODEL), jnp.float32),
        compiler_params=pltpu.CompilerParams(
            dimension_semantics=("parallel",), vmem_limit_bytes=VMEM_LIMIT),
        name=f"channel_mixer_l{layer}_r{rows}",
    )(x, mix, p["w_out"], p["n2"], p["w_up"], p["w_down"], final_g)


def kernel(x_prompt, x_sample, state_gla, state_pool, norm1_g, w_in, w_gate, b_gate, gla_norm_g, pool_w,
           pool_scale, w_out, norm2_g, w_up, w_down, final_g):
    B, T, _ = x_prompt.shape
    n_seq = x_sample.shape[0]
    bf16 = jnp.bfloat16

    a_cols = jnp.pad(w_in[:, :, COL_A:COL_A + GATE_RANK], ((0, 0), (0, 0), (0, LANES - GATE_RANK)))
    w_in_p = jnp.concatenate([w_in[:, :, :COL_A], a_cols, w_in[:, :, COL_A + GATE_RANK:]], axis=-1).astype(bf16)
    params = {
        "n1": norm1_g.reshape(DEPTH, 1, D_MODEL),
        "w_in": w_in_p,
        "w_gate": jnp.pad(w_gate, ((0, 0), (0, LANES - GATE_RANK), (0, 0))).astype(bf16),
        "b_gate": b_gate.reshape(DEPTH, 1, KEY_WIDTH),
        "gla_g": gla_norm_g.reshape(DEPTH, 1, HEAD_V),
        "pool_w": pool_w.astype(bf16),
        "pool_scale": pool_scale.reshape(DEPTH, 1, POOL_WIDTH),
        "w_out": w_out.astype(bf16),
        "n2": norm2_g.reshape(DEPTH, 1, D_MODEL),
        "w_up": w_up.astype(bf16),
        "w_down": w_down.astype(bf16),
    }
    final_g2 = final_g.reshape(1, D_MODEL)
    s_state = state_gla.reshape(DEPTH, n_seq, KEY_WIDTH, HEAD_V)
    pool_state = state_pool.reshape(DEPTH, n_seq, POOL_BUF * POOL_WIDTH)

    xp = x_prompt
    xs = x_sample.reshape(n_seq, D_MODEL)
    gla_p, pool_p, gla_s, pool_s = [], [], [], []
    for layer in range(DEPTH):
        mix_p, s_fin, hist = _prompt_mixer(layer, xp, params)
        xp = _channel_mixer(layer, xp.reshape(B * T, D_MODEL), mix_p.reshape(B * T, D_MODEL), params,
                            final_g2, CHANNEL_TILE).reshape(B, T, D_MODEL)
        mix_s, s_new, pool_new = _sample_mixer(layer, xs, s_state, pool_state, params)
        xs = _channel_mixer(layer, xs, mix_s, params, final_g2, n_seq)
        gla_p.append(s_fin.reshape(B, N_HEADS, HEAD_K, HEAD_V))
        pool_p.append(hist[:, POOL_HIST - POOL_BUF:])
        gla_s.append(s_new.reshape(n_seq, N_HEADS, HEAD_K, HEAD_V))
        pool_s.append(pool_new.reshape(n_seq, POOL_BUF, POOL_WIDTH))
    return (xp, xs.reshape(n_seq, 1, D_MODEL), jnp.stack(gla_p), jnp.stack(pool_p),
            jnp.stack(gla_s), jnp.stack(pool_s))
```

```python
import functools

import jax
import jax.numpy as jnp
from jax import lax
from jax.experimental import pallas as pl
from jax.experimental.pallas import tpu as pltpu

D_MODEL = 1024
DEPTH = 4
N_HEADS = 4
HEAD_K = 64
HEAD_V = 128
KEY_WIDTH = N_HEADS * HEAD_K
GLA_WIDTH = N_HEADS * HEAD_V
GATE_RANK = 16
GATE_TEMP = 16.0
CHUNK = 64
POOL_WIDTH = 512
POOL_WINDOWS = (2, 4, 8, 16)
POOL_GROUP = 128
POOL_BUF = 15
POOL_HIST = 16
D_FF = 4 * D_MODEL
FF_CHUNK = 1024
EPS = 1e-6
LANES = 128

COL_Q = 0
COL_K = KEY_WIDTH
COL_V = 2 * KEY_WIDTH
COL_G = COL_V + GLA_WIDTH
COL_U = COL_G + GLA_WIDTH
Z_WIDTH = COL_U + POOL_WIDTH
W_IN_GATE = COL_U
W_IN_U = COL_U + GATE_RANK

PROMPT_TILE = 512
PROMPT_SUB = 128
PROJ_PIECE = 256
CHANNEL_TILE = 512
SAMPLE_STATE_TILE = 16
VMEM_LIMIT = 56 * 1024 * 1024


def _rmsnorm(x, g):
    return x * lax.rsqrt(jnp.mean(x * x, axis=-1, keepdims=True) + EPS) * g


def _log_sigmoid(x):
    return jnp.minimum(x, 0.0) - jnp.log1p(jnp.exp(-jnp.abs(x)))


def _bf16_round(x):
    return x.astype(jnp.bfloat16).astype(jnp.float32)


def _head_lane_mask(rows):
    lane = lax.broadcasted_iota(jnp.int32, (rows, KEY_WIDTH), 1)
    return [(lane >= h * HEAD_K) & (lane < (h + 1) * HEAD_K) for h in range(N_HEADS)]


def _gate_log_decay(a_low, w_gate_ref, b_gate_ref):
    pre = jnp.dot(a_low.astype(jnp.bfloat16), w_gate_ref[...], preferred_element_type=jnp.float32) + b_gate_ref[...]
    return _log_sigmoid(pre) / GATE_TEMP


def _project(x, n1_ref, w_main_ref, w_code_ref, w_u_ref, w_gate_ref, b_gate_ref):
    h = _rmsnorm(x, n1_ref[...]).astype(jnp.bfloat16)
    qkvg = jnp.dot(h, w_main_ref[...], preferred_element_type=jnp.float32)
    a_low = jnp.dot(h, w_code_ref[...], preferred_element_type=jnp.float32)
    u = jnp.dot(h, w_u_ref[...], preferred_element_type=jnp.float32)
    return qkvg, u, _gate_log_decay(a_low, w_gate_ref, b_gate_ref)


def _gla_finish(o, g, gla_g):
    outs = []
    for h in range(N_HEADS):
        sl = slice(h * HEAD_V, (h + 1) * HEAD_V)
        outs.append(_rmsnorm(o[:, sl], gla_g) * (g[:, sl] * jax.nn.sigmoid(g[:, sl])))
    return jnp.concatenate(outs, axis=-1)


def _trace_interleaved(a, b):
    i = j = 0
    while i < len(a) or j < len(b):
        if j >= len(b) or (i < len(a) and i * len(b) <= j * len(a)):
            a[i]()
            i += 1
        else:
            b[j]()
            j += 1


def _prompt_project_tasks(load_x, slot, n1_ref, w_main_ref, w_code_ref, w_u_ref, w_gate_ref, b_gate_ref,
                          z_ref, b_ref):
    v = {}

    def norm():
        v["h"] = _rmsnorm(load_x(), n1_ref[...]).astype(jnp.bfloat16)

    def piece(lo, hi):
        w_ref, first = (w_main_ref, 0) if hi <= COL_U else (w_u_ref, COL_U)

        def run():
            z_ref[slot, :, lo:hi] = jnp.dot(v["h"], w_ref[:, lo - first:hi - first],
                                            preferred_element_type=jnp.float32)
        return run

    def gate_code():
        v["a_low"] = jnp.dot(v["h"], w_code_ref[...], preferred_element_type=jnp.float32)

    def gate_pre():
        v["pre"] = (jnp.dot(v["a_low"].astype(jnp.bfloat16), w_gate_ref[...], preferred_element_type=jnp.float32)
                    + b_gate_ref[...])

    def log_decay():
        v["b"] = _log_sigmoid(v["pre"]) / GATE_TEMP

    def scan(shifts, last):
        def run():
            b = v["b"]
            row = lax.broadcasted_iota(jnp.int32, b.shape, 0) % CHUNK
            for shift in shifts:
                b = b + jnp.where(row >= shift, pltpu.roll(b, shift, axis=0), 0.0)
            v["b"] = b
            if last:
                b_ref[slot] = b
        return run

    half = PROJ_PIECE
    return [
        norm, gate_code,
        piece(COL_Q, COL_K), gate_pre,
        piece(COL_K, COL_V), log_decay,
        piece(COL_V, COL_V + half), scan((1, 2), False),
        piece(COL_V + half, COL_G), scan((4, 8), False),
        piece(COL_G, COL_G + half), scan((16, 32), True),
        piece(COL_G + half, COL_U), piece(COL_U, COL_U + half), piece(COL_U + half, Z_WIDTH),
    ]


def _prompt_mix_tasks(slot, z_ref, b_ref, carry, first_pos, gla_g_ref, pool_w_ref, pool_scale_ref, mix_ref, base):
    head_mask = _head_lane_mask(CHUNK)
    head_mask_v = _head_lane_mask(HEAD_V)
    ci = lax.broadcasted_iota(jnp.int32, (CHUNK, CHUNK), 0)
    si = lax.broadcasted_iota(jnp.int32, (CHUNK, CHUNK), 1)
    causal = si <= ci
    scale = HEAD_K ** -0.5
    sub = z_ref.shape[1]

    def chunk_tasks(c):
        rows = slice(c * CHUNK, (c + 1) * CHUNK)
        out_rows = slice(base + c * CHUNK, base + (c + 1) * CHUNK)
        v = {}

        def prepare():
            b_c = b_ref[slot, rows, :]
            b_last = b_c[CHUNK - 1:CHUNK]
            q_c = z_ref[slot, rows, COL_Q:COL_K]
            k_c = z_ref[slot, rows, COL_K:COL_V]
            v["v"] = z_ref[slot, rows, COL_V:COL_G].astype(jnp.bfloat16)
            qt = q_c * jnp.exp(b_c) * scale
            v["kt"] = (k_c * jnp.exp(-b_c)).astype(jnp.bfloat16)
            v["ke"] = (k_c * jnp.exp(b_last - b_c)).astype(jnp.bfloat16)
            v["decay"] = jnp.exp(b_last)
            v["q_stack"] = jnp.concatenate([jnp.where(m, qt, 0.0) for m in head_mask], axis=0).astype(jnp.bfloat16)

        def update_product():
            v["p"] = lax.dot_general(v["v"], v["ke"], (((0,), (0,)), ((), ())),
                                     preferred_element_type=jnp.float32)

        def query_product():
            st = carry["st"]
            rhs = jnp.concatenate([st.astype(jnp.bfloat16), v["kt"]], axis=0)
            v["r"] = lax.dot_general(v["q_stack"], rhs, (((1,), (1,)), ((), ())),
                                     preferred_element_type=jnp.float32)

        def update_state():
            p = v["p"]
            upd = jnp.where(head_mask_v[0], p[0:HEAD_V], 0.0)
            for h in range(1, N_HEADS):
                upd = upd + jnp.where(head_mask_v[h], p[h * HEAD_V:(h + 1) * HEAD_V], 0.0)
            carry["st"] = v["decay"] * carry["st"] + upd

        def intra():
            r = v["r"]
            outs = []
            for h in range(N_HEADS):
                hr = slice(h * CHUNK, (h + 1) * CHUNK)
                att = jnp.where(causal, r[hr, HEAD_V:HEAD_V + CHUNK], 0.0).astype(jnp.bfloat16)
                o_intra = jnp.dot(att, v["v"][:, h * HEAD_V:(h + 1) * HEAD_V], preferred_element_type=jnp.float32)
                outs.append(o_intra + r[hr, 0:HEAD_V])
            v["o"] = jnp.concatenate(outs, axis=-1)

        def finish():
            g_c = z_ref[slot, rows, COL_G:COL_U]
            mix_ref[out_rows, 0:GLA_WIDTH] = _gla_finish(v["o"], g_c, gla_g_ref[...]).astype(mix_ref.dtype)

        return dict(prepare=prepare, update_product=update_product, query_product=query_product,
                    update_state=update_state, intra=intra, finish=finish)

    def pool_tasks(gi, w):
        v = {}
        lanes = slice(gi * POOL_GROUP, (gi + 1) * POOL_GROUP)

        def window_mean():
            u = z_ref[slot, :, COL_U + gi * POOL_GROUP:COL_U + (gi + 1) * POOL_GROUP]
            s = jnp.concatenate([carry["hist"][:, lanes], u], axis=0)
            shift = 1
            while shift < w:
                s = s + pltpu.roll(s, shift, axis=0)
                shift *= 2
            pos = first_pos + lax.broadcasted_iota(jnp.int32, (sub, POOL_GROUP), 0)
            count = jnp.minimum(pos + 1, w).astype(jnp.float32)
            v["pooled"] = (s[POOL_HIST:] / count - u).astype(jnp.bfloat16)

        def group_map():
            pg = jnp.dot(v["pooled"], pool_w_ref[gi], preferred_element_type=jnp.float32)
            pg = pg * pool_scale_ref[:, lanes]
            mix_ref[base:base + sub, GLA_WIDTH + gi * POOL_GROUP:GLA_WIDTH + (gi + 1) * POOL_GROUP] = (
                pg.astype(mix_ref.dtype))

        return window_mean, group_map

    def keep_history():
        carry["hist"] = z_ref[slot, sub - POOL_HIST:sub, COL_U:Z_WIDTH]

    assert sub == 2 * CHUNK
    c0, c1 = chunk_tasks(0), chunk_tasks(1)
    pools = [pool_tasks(gi, w) for gi, w in enumerate(POOL_WINDOWS)]
    return [
        c0["prepare"], c1["prepare"], c0["update_product"], c0["query_product"], c1["update_product"],
        c0["update_state"], c1["query_product"], pools[0][0], c0["intra"], c1["update_state"],
        pools[1][0], c1["intra"], pools[0][1], c0["finish"], pools[1][1], pools[2][0], c1["finish"],
        pools[3][0], pools[2][1], pools[3][1], keep_history,
    ]


def _prompt_mixer_kernel(*refs, chained):
    (x_ref, x_next_ref, n1_ref, w_main_ref, w_code_ref, w_u_ref, w_gate_ref, b_gate_ref, gla_g_ref, pool_w_ref,
     pool_scale_ref) = refs[:11]
    mix_ref, s_fin_ref, hist_out_ref, st_ref, hist_ref, z_ref, b_ref = refs[-7:]
    tile = x_ref.shape[0]
    n_sub = tile // PROMPT_SUB
    t = pl.program_id(1)

    def project_tasks(load_x, slot):
        return _prompt_project_tasks(load_x, slot, n1_ref, w_main_ref, w_code_ref, w_u_ref, w_gate_ref, b_gate_ref,
                                     z_ref, b_ref)

    @pl.when(t == 0)
    def _():
        st_ref[...] = jnp.zeros_like(st_ref)
        hist_ref[...] = jnp.zeros_like(hist_ref)
        for task in project_tasks(lambda: x_ref[0:PROMPT_SUB], 0):
            task()

    carry = {"st": st_ref[...], "hist": hist_ref[...]}
    for sb in range(n_sub):
        if sb + 1 < n_sub:
            load_next = lambda sb=sb: x_ref[(sb + 1) * PROMPT_SUB:(sb + 2) * PROMPT_SUB]
        else:
            load_next = lambda: x_next_ref[...]
        _trace_interleaved(
            project_tasks(load_next, (sb + 1) % 2),
            _prompt_mix_tasks(sb % 2, z_ref, b_ref, carry, t * tile + sb * PROMPT_SUB,
                              gla_g_ref, pool_w_ref, pool_scale_ref, mix_ref, sb * PROMPT_SUB))
    st_ref[...] = carry["st"]
    hist_ref[...] = carry["hist"]

    @pl.when(t == pl.num_programs(1) - 1)
    def _():
        s_fin_ref[...] = carry["st"].T
        hist_out_ref[...] = carry["hist"]


def _prompt_mixer(layer, x, p, stacked):
    B, T, _ = x.shape
    tile = PROMPT_TILE
    n_tiles = T // tile
    n_sub = tile // PROMPT_SUB
    assert n_sub % 2 == 0
    whole = lambda *shape: pl.BlockSpec((None,) + shape, lambda b, t: (layer,) + (0,) * len(shape))
    chained = stacked is not None
    operands = [x, x, p["n1"], p["w_main"], p["w_code"], p["w_u"], p["w_gate"], p["b_gate"], p["gla_g"], p["pool_w"],
                p["pool_scale"]]
    in_specs = [
        pl.BlockSpec((None, tile, D_MODEL), lambda b, t: (b, t, 0)),
        pl.BlockSpec((None, PROMPT_SUB, D_MODEL), lambda b, t: (b, jnp.minimum(t + 1, n_tiles - 1) * n_sub, 0)),
        whole(1, D_MODEL),
        whole(D_MODEL, COL_U),
        whole(D_MODEL, LANES),
        whole(D_MODEL, POOL_WIDTH),
        whole(LANES, KEY_WIDTH),
        whole(1, KEY_WIDTH),
        whole(1, HEAD_V),
        whole(len(POOL_WINDOWS), POOL_GROUP, POOL_GROUP),
        whole(1, POOL_WIDTH),
    ]
    aliases = {}
    if chained:
        aliases = {len(operands): 1, len(operands) + 1: 2}
        operands += list(stacked)
        in_specs += [pl.BlockSpec(memory_space=pl.ANY)] * 2
    return pl.pallas_call(
        functools.partial(_prompt_mixer_kernel, chained=chained),
        grid=(B, n_tiles),
        in_specs=in_specs,
        out_specs=[
            pl.BlockSpec((None, tile, D_MODEL), lambda b, t: (b, t, 0)),
            pl.BlockSpec((None, None, KEY_WIDTH, HEAD_V), lambda b, t: (layer, b, 0, 0)),
            pl.BlockSpec((None, None, POOL_HIST, POOL_WIDTH), lambda b, t: (layer, b, 0, 0)),
        ],
        out_shape=[
            jax.ShapeDtypeStruct((B, T, D_MODEL), jnp.bfloat16),
            jax.ShapeDtypeStruct((DEPTH, B, KEY_WIDTH, HEAD_V), jnp.float32),
            jax.ShapeDtypeStruct((DEPTH, B, POOL_HIST, POOL_WIDTH), jnp.float32),
        ],
        scratch_shapes=[
            pltpu.VMEM((HEAD_V, KEY_WIDTH), jnp.float32),
            pltpu.VMEM((POOL_HIST, POOL_WIDTH), jnp.float32),
            pltpu.VMEM((2, PROMPT_SUB, Z_WIDTH), jnp.float32),
            pltpu.VMEM((2, PROMPT_SUB, KEY_WIDTH), jnp.float32),
        ],
        input_output_aliases=aliases,
        compiler_params=pltpu.CompilerParams(
            dimension_semantics=("parallel", "arbitrary"), vmem_limit_bytes=VMEM_LIMIT),
        name=f"prompt_mixer_l{layer}",
    )(*operands)


def _sample_mixer_kernel(*refs, chained):
    (x_ref, n1_ref, w_main_ref, w_code_ref, w_u_ref, w_gate_ref, b_gate_ref, gla_g_ref, pool_w_ref,
     pool_scale_ref, s_ref, pool_state_ref) = refs[:12]
    mix_ref, s_new_ref, pool_new_ref, dec_t_ref, ke_t_ref, qt_t_ref, v_ref, g_ref, o_ref = refs[-9:]
    i = pl.program_id(0)
    n_seq = x_ref.shape[0]
    seq_tile = s_ref.shape[0]

    @pl.when(i == 0)
    def _():
        qkvg, u, log_a = _project(x_ref[...], n1_ref, w_main_ref, w_code_ref, w_u_ref, w_gate_ref, b_gate_ref)
        q = qkvg[:, COL_Q:COL_K]
        k = qkvg[:, COL_K:COL_V]
        v = _bf16_round(qkvg[:, COL_V:COL_G])
        qt = _bf16_round(q * jnp.exp(log_a) * (HEAD_K ** -0.5))
        kt = _bf16_round(k * jnp.exp(-log_a))
        ke = _bf16_round(k * jnp.exp(log_a - log_a))
        prod = qt * kt
        o_intra = []
        for h, m in enumerate(_head_lane_mask(n_seq)):
            att = _bf16_round(jnp.sum(jnp.where(m, prod, 0.0), axis=-1, keepdims=True))
            o_intra.append(att * v[:, h * HEAD_V:(h + 1) * HEAD_V])
        o_ref[...] = jnp.concatenate(o_intra, axis=-1)
        dec_t_ref[...] = jnp.exp(log_a).T
        ke_t_ref[...] = ke.T
        qt_t_ref[...] = qt.T
        v_ref[...] = v
        g_ref[...] = qkvg[:, COL_G:COL_U]

        for gi, w in enumerate(POOL_WINDOWS):
            lanes = slice(gi * POOL_GROUP, (gi + 1) * POOL_GROUP)
            s = u[:, lanes]
            for j in range(POOL_BUF - (w - 1), POOL_BUF):
                s = s + pool_state_ref[j, :, lanes]
            pooled = s / float(w) - u[:, lanes]
            pg = jnp.dot(pooled.astype(jnp.bfloat16), pool_w_ref[gi], preferred_element_type=jnp.float32)
            pg = pg * pool_scale_ref[:, lanes]
            mix_ref[:, GLA_WIDTH + gi * POOL_GROUP:GLA_WIDTH + (gi + 1) * POOL_GROUP] = pg.astype(mix_ref.dtype)
        for j in range(POOL_BUF - 1):
            pool_new_ref[j] = pool_state_ref[j + 1]
        pool_new_ref[POOL_BUF - 1] = u

    lane = lax.broadcasted_iota(jnp.int32, (KEY_WIDTH, n_seq), 1)

    def one_sequence(j, carry):
        seq = i * seq_tile + j
        pick = lane == seq
        column = lambda ref: jnp.sum(jnp.where(pick, ref[...], 0.0), axis=-1, keepdims=True)
        dec, ke, qt = column(dec_t_ref), column(ke_t_ref), column(qt_t_ref)
        s_old = s_ref[j]
        v_row = v_ref[pl.ds(seq, 1), :]
        v_rows = jnp.concatenate(
            [jnp.broadcast_to(v_row[:, h * HEAD_V:(h + 1) * HEAD_V], (HEAD_K, HEAD_V)) for h in range(N_HEADS)],
            axis=0)
        s_new_ref[j] = dec * s_old + ke * v_rows
        weighted = qt * s_old
        o_inter = jnp.concatenate(
            [jnp.sum(weighted[h * HEAD_K:(h + 1) * HEAD_K], axis=0, keepdims=True) for h in range(N_HEADS)],
            axis=-1)
        o_ref[pl.ds(seq, 1), :] = o_ref[pl.ds(seq, 1), :] + o_inter
        return carry

    lax.fori_loop(0, seq_tile, one_sequence, 0)

    @pl.when(i == pl.num_programs(0) - 1)
    def _():
        mix_ref[:, 0:GLA_WIDTH] = _gla_finish(o_ref[...], g_ref[...], gla_g_ref[...]).astype(mix_ref.dtype)


def _sample_mixer(layer, x, s_state, pool_state, p, stacked):
    n_seq = x.shape[0]
    seq_tile = SAMPLE_STATE_TILE
    whole = lambda *shape: pl.BlockSpec((None,) + shape, lambda i: (layer,) + (0,) * len(shape))
    fixed = lambda *shape: pl.BlockSpec(shape, lambda i: (0,) * len(shape))
    f32 = jnp.float32
    chained = stacked is not None
    operands = [x, p["n1"], p["w_main"], p["w_code"], p["w_u"], p["w_gate"], p["b_gate"], p["gla_g"], p["pool_w"],
                p["pool_scale"], s_state, pool_state]
    in_specs = [
        fixed(n_seq, D_MODEL),
        whole(1, D_MODEL),
        whole(D_MODEL, COL_U),
        whole(D_MODEL, LANES),
        whole(D_MODEL, POOL_WIDTH),
        whole(LANES, KEY_WIDTH),
        whole(1, KEY_WIDTH),
        whole(1, HEAD_V),
        whole(len(POOL_WINDOWS), POOL_GROUP, POOL_GROUP),
        whole(1, POOL_WIDTH),
        pl.BlockSpec((None, seq_tile, KEY_WIDTH, HEAD_V), lambda i: (layer, i, 0, 0)),
        whole(POOL_BUF, n_seq, POOL_WIDTH),
    ]
    aliases = {}
    if chained:
        aliases = {len(operands): 1, len(operands) + 1: 2}
        operands += list(stacked)
        in_specs += [pl.BlockSpec(memory_space=pl.ANY)] * 2
    return pl.pallas_call(
        functools.partial(_sample_mixer_kernel, chained=chained),
        grid=(n_seq // seq_tile,),
        in_specs=in_specs,
        out_specs=[
            fixed(n_seq, D_MODEL),
            pl.BlockSpec((None, seq_tile, KEY_WIDTH, HEAD_V), lambda i: (layer, i, 0, 0)),
            whole(POOL_BUF, n_seq, POOL_WIDTH),
        ],
        out_shape=[
            jax.ShapeDtypeStruct((n_seq, D_MODEL), jnp.bfloat16),
            jax.ShapeDtypeStruct((DEPTH, n_seq, KEY_WIDTH, HEAD_V), f32),
            jax.ShapeDtypeStruct((DEPTH, POOL_BUF, n_seq, POOL_WIDTH), f32),
        ],
        scratch_shapes=[
            pltpu.VMEM((KEY_WIDTH, n_seq), f32),
            pltpu.VMEM((KEY_WIDTH, n_seq), f32),
            pltpu.VMEM((KEY_WIDTH, n_seq), f32),
            pltpu.VMEM((n_seq, GLA_WIDTH), f32),
            pltpu.VMEM((n_seq, GLA_WIDTH), f32),
            pltpu.VMEM((n_seq, GLA_WIDTH), f32),
        ],
        input_output_aliases=aliases,
        compiler_params=pltpu.CompilerParams(
            dimension_semantics=("arbitrary",), vmem_limit_bytes=VMEM_LIMIT),
        name=f"sample_mixer_l{layer}",
    )(*operands)


def _channel_kernel(x_ref, mix_ref, w_out_ref, n2_ref, w_up_ref, w_down_ref, final_g_ref, y_ref, *, final):
    x1 = x_ref[...] + jnp.dot(mix_ref[...], w_out_ref[...], preferred_element_type=jnp.float32)
    h2 = _rmsnorm(x1, n2_ref[...]).astype(jnp.bfloat16)
    acc = x1
    for j in range(D_FF // FF_CHUNK):
        cols = slice(j * FF_CHUNK, (j + 1) * FF_CHUNK)
        hid = jnp.dot(h2, w_up_ref[:, cols], preferred_element_type=jnp.float32)
        act = jnp.square(jnp.maximum(hid, 0.0)).astype(jnp.bfloat16)
        acc = acc + jnp.dot(act, w_down_ref[cols, :], preferred_element_type=jnp.float32)
    if final:
        acc = _rmsnorm(acc, final_g_ref[...])
    y_ref[...] = acc


def _channel_mixer(layer, x, mix, p, final_g, tile):
    rows = x.shape[0]
    whole = lambda *shape: pl.BlockSpec((None,) + shape, lambda i: (layer,) + (0,) * len(shape),
                                        pipeline_mode=pl.Buffered(1))
    return pl.pallas_call(
        functools.partial(_channel_kernel, final=layer == DEPTH - 1),
        grid=(rows // tile,),
        in_specs=[
            pl.BlockSpec((tile, D_MODEL), lambda i: (i, 0)),
            pl.BlockSpec((tile, D_MODEL), lambda i: (i, 0)),
            whole(D_MODEL, D_MODEL),
            whole(1, D_MODEL),
            whole(D_MODEL, D_FF),
            whole(D_FF, D_MODEL),
            pl.BlockSpec((1, D_MODEL), lambda i: (0, 0)),
        ],
        out_specs=pl.BlockSpec((tile, D_MODEL), lambda i: (i, 0)),
        out_shape=jax.ShapeDtypeStruct((rows, D_MODEL), jnp.float32),
        compiler_params=pltpu.CompilerParams(
            dimension_semantics=("parallel",), vmem_limit_bytes=VMEM_LIMIT),
        name=f"channel_mixer_l{layer}_r{rows}",
    )(x, mix, p["w_out"], p["n2"], p["w_up"], p["w_down"], final_g)


def kernel(x_prompt, x_sample, state_gla, state_pool, norm1_g, w_in, w_gate, b_gate, gla_norm_g, pool_w,
           pool_scale, w_out, norm2_g, w_up, w_down, final_g):
    B, T, _ = x_prompt.shape
    n_seq = x_sample.shape[0]
    bf16 = jnp.bfloat16

    w_code = jnp.pad(w_in[:, :, W_IN_GATE:W_IN_U], ((0, 0), (0, 0), (0, LANES - GATE_RANK)))
    params = {
        "n1": norm1_g.reshape(DEPTH, 1, D_MODEL),
        "w_main": w_in[:, :, :W_IN_GATE].astype(bf16),
        "w_code": w_code.astype(bf16),
        "w_u": w_in[:, :, W_IN_U:].astype(bf16),
        "w_gate": jnp.pad(w_gate, ((0, 0), (0, LANES - GATE_RANK), (0, 0))).astype(bf16),
        "b_gate": b_gate.reshape(DEPTH, 1, KEY_WIDTH),
        "gla_g": gla_norm_g.reshape(DEPTH, 1, HEAD_V),
        "pool_w": pool_w.astype(bf16),
        "pool_scale": pool_scale.reshape(DEPTH, 1, POOL_WIDTH),
        "w_out": w_out.astype(bf16),
        "n2": norm2_g.reshape(DEPTH, 1, D_MODEL),
        "w_up": w_up.astype(bf16),
        "w_down": w_down.astype(bf16),
    }
    final_g2 = final_g.reshape(1, D_MODEL)
    s_state = state_gla.reshape(DEPTH, n_seq, KEY_WIDTH, HEAD_V)
    pool_rows = jnp.swapaxes(state_pool, 1, 2)

    xp = x_prompt
    xs = x_sample.reshape(n_seq, D_MODEL)
    prompt_states = sample_states = None
    for layer in range(DEPTH):
        mix_p, *prompt_states = _prompt_mixer(layer, xp, params, prompt_states)
        xp = _channel_mixer(layer, xp.reshape(B * T, D_MODEL), mix_p.reshape(B * T, D_MODEL), params,
                            final_g2, CHANNEL_TILE).reshape(B, T, D_MODEL)
        mix_s, *sample_states = _sample_mixer(layer, xs, s_state, pool_rows, params, sample_states)
        xs = _channel_mixer(layer, xs, mix_s, params, final_g2, n_seq)
    gla_p, hist_p = prompt_states
    gla_s, pool_s = sample_states
    return (xp, xs.reshape(n_seq, 1, D_MODEL),
            gla_p.reshape(DEPTH, B, N_HEADS, HEAD_K, HEAD_V), hist_p[:, :, POOL_HIST - POOL_BUF:],
            gla_s.reshape(DEPTH, n_seq, N_HEADS, HEAD_K, HEAD_V), jnp.swapaxes(pool_s, 1, 2))
```

```python
import functools

import jax
import jax.numpy as jnp
from jax import lax
from jax.experimental import pallas as pl
from jax.experimental.pallas import tpu as pltpu

D_MODEL = 1024
DEPTH = 4
N_HEADS = 4
HEAD_K = 64
HEAD_V = 128
KEY_WIDTH = N_HEADS * HEAD_K
GLA_WIDTH = N_HEADS * HEAD_V
GATE_RANK = 16
GATE_TEMP = 16.0
CHUNK = 64
POOL_WIDTH = 512
POOL_WINDOWS = (2, 4, 8, 16)
POOL_GROUP = 128
POOL_PAIR = 2 * POOL_GROUP
POOL_BUF = 15
POOL_HIST = 16
D_FF = 4 * D_MODEL
FF_CHUNK = 1024
EPS = 1e-6
LANES = 128
SUBLANES = 8

COL_Q = 0
COL_K = KEY_WIDTH
COL_V = 2 * KEY_WIDTH
COL_G = COL_V + GLA_WIDTH
COL_U = COL_G + GLA_WIDTH
Z_WIDTH = COL_U + POOL_WIDTH
W_IN_GATE = COL_U
W_IN_U = COL_U + GATE_RANK

PROMPT_TILE = 512
PROMPT_SUB = 128
PROJ_PIECE = 256
CHANNEL_TILE = 512
SAMPLE_STATE_TILE = 16
VMEM_LIMIT = 56 * 1024 * 1024


def _rmsnorm(x, g):
    return x * lax.rsqrt(jnp.mean(x * x, axis=-1, keepdims=True) + EPS) * g


def _log_sigmoid(x):
    return jnp.minimum(x, 0.0) - jnp.log1p(jnp.exp(-jnp.abs(x)))


def _bf16_round(x):
    return x.astype(jnp.bfloat16).astype(jnp.float32)


def _head_lane_mask(rows):
    lane = lax.broadcasted_iota(jnp.int32, (rows, KEY_WIDTH), 1)
    return [(lane >= h * HEAD_K) & (lane < (h + 1) * HEAD_K) for h in range(N_HEADS)]


def _gate_log_decay(a_low, w_gate_ref, b_gate_ref):
    pre = jnp.dot(a_low.astype(jnp.bfloat16), w_gate_ref[...], preferred_element_type=jnp.float32) + b_gate_ref[...]
    return _log_sigmoid(pre) / GATE_TEMP


def _project(x, n1_ref, w_main_ref, w_code_ref, w_u_ref, w_gate_ref, b_gate_ref):
    h = _rmsnorm(x, n1_ref[...]).astype(jnp.bfloat16)
    qkvg = jnp.dot(h, w_main_ref[...], preferred_element_type=jnp.float32)
    a_low = jnp.dot(h, w_code_ref[...], preferred_element_type=jnp.float32)
    u = jnp.dot(h, w_u_ref[...], preferred_element_type=jnp.float32)
    return qkvg, u, _gate_log_decay(a_low, w_gate_ref, b_gate_ref)


def _gla_finish(o, g, gla_g):
    outs = []
    for h in range(N_HEADS):
        sl = slice(h * HEAD_V, (h + 1) * HEAD_V)
        outs.append(_rmsnorm(o[:, sl], gla_g) * (g[:, sl] * jax.nn.sigmoid(g[:, sl])))
    return jnp.concatenate(outs, axis=-1)


def _trace_interleaved(a, b):
    i = j = 0
    while i < len(a) or j < len(b):
        if j >= len(b) or (i < len(a) and i * len(b) <= j * len(a)):
            a[i]()
            i += 1
        else:
            b[j]()
            j += 1


def _prompt_project_tasks(load_x, slot, n1_ref, w_main_ref, w_code_ref, w_u_ref, w_gate_ref, b_gate_ref,
                          z_ref, b_ref):
    v = {}

    def norm():
        v["h"] = _rmsnorm(load_x(), n1_ref[...]).astype(jnp.bfloat16)

    def piece(lo, hi):
        w_ref, first = (w_main_ref, 0) if hi <= COL_U else (w_u_ref, COL_U)

        def run():
            z_ref[slot, :, lo:hi] = jnp.dot(v["h"], w_ref[:, lo - first:hi - first],
                                            preferred_element_type=jnp.float32)
        return run

    def gate_code():
        v["a_low"] = jnp.dot(v["h"], w_code_ref[...], preferred_element_type=jnp.float32)

    def gate_pre():
        v["pre"] = (jnp.dot(v["a_low"].astype(jnp.bfloat16), w_gate_ref[...], preferred_element_type=jnp.float32)
                    + b_gate_ref[...])

    def log_decay():
        v["b"] = _log_sigmoid(v["pre"]) / GATE_TEMP

    def scan(shifts, last):
        def run():
            b = v["b"]
            row = lax.broadcasted_iota(jnp.int32, b.shape, 0) % CHUNK
            for shift in shifts:
                b = b + jnp.where(row >= shift, pltpu.roll(b, shift, axis=0), 0.0)
            v["b"] = b
            if last:
                b_ref[slot] = b
        return run

    half = PROJ_PIECE
    return [
        norm, gate_code,
        piece(COL_Q, COL_K), gate_pre,
        piece(COL_K, COL_V), log_decay,
        piece(COL_V, COL_V + half), scan((1, 2), False),
        piece(COL_V + half, COL_G), scan((4, 8), False),
        piece(COL_G, COL_G + half), scan((16, 32), True),
        piece(COL_G + half, COL_U), piece(COL_U, COL_U + half), piece(COL_U + half, Z_WIDTH),
    ]


def _prompt_mix_tasks(slot, z_ref, b_ref, carry, first_pos, gla_g_ref, pool_w_ref, pool_scale_ref, mix_ref, base):
    head_mask = _head_lane_mask(CHUNK)
    head_mask_v = _head_lane_mask(HEAD_V)
    ci = lax.broadcasted_iota(jnp.int32, (CHUNK, CHUNK), 0)
    si = lax.broadcasted_iota(jnp.int32, (CHUNK, CHUNK), 1)
    causal = si <= ci
    scale = HEAD_K ** -0.5
    sub = z_ref.shape[1]

    def chunk_tasks(c):
        rows = slice(c * CHUNK, (c + 1) * CHUNK)
        out_rows = slice(base + c * CHUNK, base + (c + 1) * CHUNK)
        v = {}

        def prepare():
            b_c = b_ref[slot, rows, :]
            b_last = b_c[CHUNK - 1:CHUNK]
            q_c = z_ref[slot, rows, COL_Q:COL_K]
            k_c = z_ref[slot, rows, COL_K:COL_V]
            v["v"] = z_ref[slot, rows, COL_V:COL_G].astype(jnp.bfloat16)
            qt = q_c * jnp.exp(b_c) * scale
            v["kt"] = (k_c * jnp.exp(-b_c)).astype(jnp.bfloat16)
            v["ke"] = (k_c * jnp.exp(b_last - b_c)).astype(jnp.bfloat16)
            v["decay"] = jnp.exp(b_last)
            v["q_stack"] = jnp.concatenate([jnp.where(m, qt, 0.0) for m in head_mask], axis=0).astype(jnp.bfloat16)

        def update_product():
            v["p"] = lax.dot_general(v["v"], v["ke"], (((0,), (0,)), ((), ())),
                                     preferred_element_type=jnp.float32)

        def query_product():
            st = carry["st"]
            rhs = jnp.concatenate([st.astype(jnp.bfloat16), v["kt"]], axis=0)
            v["r"] = lax.dot_general(v["q_stack"], rhs, (((1,), (1,)), ((), ())),
                                     preferred_element_type=jnp.float32)

        def update_state():
            p = v["p"]
            upd = jnp.where(head_mask_v[0], p[0:HEAD_V], 0.0)
            for h in range(1, N_HEADS):
                upd = upd + jnp.where(head_mask_v[h], p[h * HEAD_V:(h + 1) * HEAD_V], 0.0)
            carry["st"] = v["decay"] * carry["st"] + upd

        def intra():
            r = v["r"]
            outs = []
            for h in range(N_HEADS):
                hr = slice(h * CHUNK, (h + 1) * CHUNK)
                att = jnp.where(causal, r[hr, HEAD_V:HEAD_V + CHUNK], 0.0).astype(jnp.bfloat16)
                o_intra = jnp.dot(att, v["v"][:, h * HEAD_V:(h + 1) * HEAD_V], preferred_element_type=jnp.float32)
                outs.append(o_intra + r[hr, 0:HEAD_V])
            v["o"] = jnp.concatenate(outs, axis=-1)

        def finish():
            g_c = z_ref[slot, rows, COL_G:COL_U]
            mix_ref[out_rows, 0:GLA_WIDTH] = _gla_finish(v["o"], g_c, gla_g_ref[...]).astype(mix_ref.dtype)

        return dict(prepare=prepare, update_product=update_product, query_product=query_product,
                    update_state=update_state, intra=intra, finish=finish)

    pooled = {}

    def window_mean_task(gi, w):
        lanes = slice(gi * POOL_GROUP, (gi + 1) * POOL_GROUP)

        def window_mean():
            u = z_ref[slot, :, COL_U + gi * POOL_GROUP:COL_U + (gi + 1) * POOL_GROUP]
            s = jnp.concatenate([carry["hist"][:, lanes], u], axis=0)
            shift = 1
            while shift < w:
                s = s + pltpu.roll(s, shift, axis=0)
                shift *= 2
            pos = first_pos + lax.broadcasted_iota(jnp.int32, (sub, POOL_GROUP), 0)
            count = jnp.minimum(pos + 1, w).astype(jnp.float32)
            pooled[gi] = (s[POOL_HIST:] / count - u).astype(jnp.bfloat16)

        return window_mean

    def group_map_task(pair):
        lanes = slice(pair * POOL_PAIR, (pair + 1) * POOL_PAIR)

        def group_map():
            both = jnp.concatenate([pooled[2 * pair], pooled[2 * pair + 1]], axis=-1)
            pg = jnp.dot(both, pool_w_ref[pair], preferred_element_type=jnp.float32)
            pg = pg * pool_scale_ref[:, lanes]
            mix_ref[base:base + sub, GLA_WIDTH + pair * POOL_PAIR:GLA_WIDTH + (pair + 1) * POOL_PAIR] = (
                pg.astype(mix_ref.dtype))

        return group_map

    def keep_history():
        carry["hist"] = z_ref[slot, sub - POOL_HIST:sub, COL_U:Z_WIDTH]

    assert sub == 2 * CHUNK
    c0, c1 = chunk_tasks(0), chunk_tasks(1)
    means = [window_mean_task(gi, w) for gi, w in enumerate(POOL_WINDOWS)]
    maps = [group_map_task(pair) for pair in range(len(POOL_WINDOWS) // 2)]
    return [
        c0["prepare"], c1["prepare"], c0["update_product"], c0["query_product"], c1["update_product"],
        c0["update_state"], c1["query_product"], means[0], c0["intra"], c1["update_state"],
        means[1], c1["intra"], maps[0], c0["finish"], means[2], c1["finish"], means[3], maps[1], keep_history,
    ]


def _prompt_mixer_kernel(*refs, chained):
    (x_ref, x_next_ref, n1_ref, w_main_ref, w_code_ref, w_u_ref, w_gate_ref, b_gate_ref, gla_g_ref, pool_w_ref,
     pool_scale_ref) = refs[:11]
    mix_ref, s_fin_ref, hist_out_ref, st_ref, hist_ref, z_ref, b_ref = refs[-7:]
    tile = x_ref.shape[0]
    n_sub = tile // PROMPT_SUB
    t = pl.program_id(1)

    def project_tasks(load_x, slot):
        return _prompt_project_tasks(load_x, slot, n1_ref, w_main_ref, w_code_ref, w_u_ref, w_gate_ref, b_gate_ref,
                                     z_ref, b_ref)

    @pl.when(t == 0)
    def _():
        st_ref[...] = jnp.zeros_like(st_ref)
        hist_ref[...] = jnp.zeros_like(hist_ref)
        for task in project_tasks(lambda: x_ref[0:PROMPT_SUB], 0):
            task()

    carry = {"st": st_ref[...], "hist": hist_ref[...]}
    for sb in range(n_sub):
        if sb + 1 < n_sub:
            load_next = lambda sb=sb: x_ref[(sb + 1) * PROMPT_SUB:(sb + 2) * PROMPT_SUB]
        else:
            load_next = lambda: x_next_ref[...]
        _trace_interleaved(
            project_tasks(load_next, (sb + 1) % 2),
            _prompt_mix_tasks(sb % 2, z_ref, b_ref, carry, t * tile + sb * PROMPT_SUB,
                              gla_g_ref, pool_w_ref, pool_scale_ref, mix_ref, sb * PROMPT_SUB))
    st_ref[...] = carry["st"]
    hist_ref[...] = carry["hist"]

    @pl.when(t == pl.num_programs(1) - 1)
    def _():
        s_fin_ref[...] = carry["st"].T
        hist_out_ref[...] = carry["hist"]


def _prompt_mixer(layer, x, p, stacked):
    B, T, _ = x.shape
    tile = PROMPT_TILE
    n_tiles = T // tile
    n_sub = tile // PROMPT_SUB
    assert n_sub % 2 == 0
    whole = lambda *shape: pl.BlockSpec((None,) + shape, lambda b, t: (layer,) + (0,) * len(shape))
    chained = stacked is not None
    operands = [x, x, p["n1"], p["w_main"], p["w_code"], p["w_u"], p["w_gate"], p["b_gate"], p["gla_g"], p["pool_w"],
                p["pool_scale"]]
    in_specs = [
        pl.BlockSpec((None, tile, D_MODEL), lambda b, t: (b, t, 0)),
        pl.BlockSpec((None, PROMPT_SUB, D_MODEL), lambda b, t: (b, jnp.minimum(t + 1, n_tiles - 1) * n_sub, 0)),
        whole(1, D_MODEL),
        whole(D_MODEL, COL_U),
        whole(D_MODEL, LANES),
        whole(D_MODEL, POOL_WIDTH),
        whole(LANES, KEY_WIDTH),
        whole(1, KEY_WIDTH),
        whole(1, HEAD_V),
        whole(len(POOL_WINDOWS) // 2, POOL_PAIR, POOL_PAIR),
        whole(1, POOL_WIDTH),
    ]
    aliases = {}
    if chained:
        aliases = {len(operands): 1, len(operands) + 1: 2}
        operands += list(stacked)
        in_specs += [pl.BlockSpec(memory_space=pl.ANY)] * 2
    return pl.pallas_call(
        functools.partial(_prompt_mixer_kernel, chained=chained),
        grid=(B, n_tiles),
        in_specs=in_specs,
        out_specs=[
            pl.BlockSpec((None, tile, D_MODEL), lambda b, t: (b, t, 0)),
            pl.BlockSpec((None, None, KEY_WIDTH, HEAD_V), lambda b, t: (layer, b, 0, 0)),
            pl.BlockSpec((None, None, POOL_HIST, POOL_WIDTH), lambda b, t: (layer, b, 0, 0)),
        ],
        out_shape=[
            jax.ShapeDtypeStruct((B, T, D_MODEL), jnp.bfloat16),
            jax.ShapeDtypeStruct((DEPTH, B, KEY_WIDTH, HEAD_V), jnp.float32),
            jax.ShapeDtypeStruct((DEPTH, B, POOL_HIST, POOL_WIDTH), jnp.float32),
        ],
        scratch_shapes=[
            pltpu.VMEM((HEAD_V, KEY_WIDTH), jnp.float32),
            pltpu.VMEM((POOL_HIST, POOL_WIDTH), jnp.float32),
            pltpu.VMEM((2, PROMPT_SUB, Z_WIDTH), jnp.float32),
            pltpu.VMEM((2, PROMPT_SUB, KEY_WIDTH), jnp.float32),
        ],
        input_output_aliases=aliases,
        compiler_params=pltpu.CompilerParams(
            dimension_semantics=("parallel", "arbitrary"), vmem_limit_bytes=VMEM_LIMIT),
        name=f"prompt_mixer_l{layer}",
    )(*operands)


def _sample_mixer_kernel(*refs, chained):
    (x_ref, n1_ref, w_main_ref, w_code_ref, w_u_ref, w_gate_ref, b_gate_ref, gla_g_ref, pool_w_ref,
     pool_scale_ref, s_ref, pool_state_ref) = refs[:12]
    mix_ref, s_new_ref, pool_new_ref, dec_ref, ke_ref, qt_ref, v_ref, g_ref, o_ref = refs[-9:]
    i = pl.program_id(0)
    n_seq = x_ref.shape[0]
    seq_tile = s_ref.shape[0]

    @pl.when(i == 0)
    def _():
        qkvg, u, log_a = _project(x_ref[...], n1_ref, w_main_ref, w_code_ref, w_u_ref, w_gate_ref, b_gate_ref)
        q = qkvg[:, COL_Q:COL_K]
        k = qkvg[:, COL_K:COL_V]
        v = _bf16_round(qkvg[:, COL_V:COL_G])
        qt = _bf16_round(q * jnp.exp(log_a) * (HEAD_K ** -0.5))
        kt = _bf16_round(k * jnp.exp(-log_a))
        ke = _bf16_round(k * jnp.exp(log_a - log_a))
        prod = qt * kt
        o_intra = []
        for h, m in enumerate(_head_lane_mask(n_seq)):
            att = _bf16_round(jnp.sum(jnp.where(m, prod, 0.0), axis=-1, keepdims=True))
            o_intra.append(att * v[:, h * HEAD_V:(h + 1) * HEAD_V])
        o_ref[...] = jnp.concatenate(o_intra, axis=-1)
        dec_ref[...] = jnp.exp(log_a)
        ke_ref[...] = ke
        qt_ref[...] = qt
        v_ref[...] = v
        g_ref[...] = qkvg[:, COL_G:COL_U]

        pooled = []
        for gi, w in enumerate(POOL_WINDOWS):
            lanes = slice(gi * POOL_GROUP, (gi + 1) * POOL_GROUP)
            s = u[:, lanes]
            for j in range(POOL_BUF - (w - 1), POOL_BUF):
                s = s + pool_state_ref[j, :, lanes]
            pooled.append((s / float(w) - u[:, lanes]).astype(jnp.bfloat16))
        for pair in range(len(POOL_WINDOWS) // 2):
            lanes = slice(pair * POOL_PAIR, (pair + 1) * POOL_PAIR)
            both = jnp.concatenate(pooled[2 * pair:2 * pair + 2], axis=-1)
            pg = jnp.dot(both, pool_w_ref[pair], preferred_element_type=jnp.float32) * pool_scale_ref[:, lanes]
            mix_ref[:, GLA_WIDTH + pair * POOL_PAIR:GLA_WIDTH + (pair + 1) * POOL_PAIR] = pg.astype(mix_ref.dtype)
        for j in range(POOL_BUF - 1):
            pool_new_ref[j] = pool_state_ref[j + 1]
        pool_new_ref[POOL_BUF - 1] = u

    lane = lax.broadcasted_iota(jnp.int32, (SUBLANES, KEY_WIDTH), 1)
    head = lax.broadcasted_iota(jnp.int32, (SUBLANES, KEY_WIDTH), 0)
    head_rows = jnp.where(lane // HEAD_K == head, 1.0, 0.0)
    for sub in range(seq_tile // SUBLANES):
        local = slice(sub * SUBLANES, (sub + 1) * SUBLANES)
        rows = pl.ds(pl.multiple_of(i * seq_tile + sub * SUBLANES, SUBLANES), SUBLANES)
        s_old = s_ref[local]
        v = v_ref[rows, :]
        v_rows = jnp.concatenate(
            [jnp.broadcast_to(v[:, None, h * HEAD_V:(h + 1) * HEAD_V], (SUBLANES, HEAD_K, HEAD_V))
             for h in range(N_HEADS)], axis=1)
        s_new_ref[local] = dec_ref[rows, :][:, :, None] * s_old + ke_ref[rows, :][:, :, None] * v_rows
        q_rows = (qt_ref[rows, :][:, None, :] * head_rows[None]).astype(jnp.bfloat16)
        o_inter = jnp.einsum("bhk,bkv->bhv", q_rows, s_old.astype(jnp.bfloat16),
                             preferred_element_type=jnp.float32)
        o_ref[rows, :] = o_ref[rows, :] + jnp.concatenate([o_inter[:, h, :] for h in range(N_HEADS)], axis=-1)

    @pl.when(i == pl.num_programs(0) - 1)
    def _():
        mix_ref[:, 0:GLA_WIDTH] = _gla_finish(o_ref[...], g_ref[...], gla_g_ref[...]).astype(mix_ref.dtype)


def _sample_mixer(layer, x, s_state, pool_state, p, stacked):
    n_seq = x.shape[0]
    seq_tile = SAMPLE_STATE_TILE
    whole = lambda *shape: pl.BlockSpec((None,) + shape, lambda i: (layer,) + (0,) * len(shape))
    fixed = lambda *shape: pl.BlockSpec(shape, lambda i: (0,) * len(shape))
    f32 = jnp.float32
    chained = stacked is not None
    operands = [x, p["n1"], p["w_main"], p["w_code"], p["w_u"], p["w_gate"], p["b_gate"], p["gla_g"], p["pool_w"],
                p["pool_scale"], s_state, pool_state]
    in_specs = [
        fixed(n_seq, D_MODEL),
        whole(1, D_MODEL),
        whole(D_MODEL, COL_U),
        whole(D_MODEL, LANES),
        whole(D_MODEL, POOL_WIDTH),
        whole(LANES, KEY_WIDTH),
        whole(1, KEY_WIDTH),
        whole(1, HEAD_V),
        whole(len(POOL_WINDOWS) // 2, POOL_PAIR, POOL_PAIR),
        whole(1, POOL_WIDTH),
        pl.BlockSpec((None, seq_tile, KEY_WIDTH, HEAD_V), lambda i: (layer, i, 0, 0)),
        whole(POOL_BUF, n_seq, POOL_WIDTH),
    ]
    aliases = {}
    if chained:
        aliases = {len(operands): 1, len(operands) + 1: 2}
        operands += list(stacked)
        in_specs += [pl.BlockSpec(memory_space=pl.ANY)] * 2
    return pl.pallas_call(
        functools.partial(_sample_mixer_kernel, chained=chained),
        grid=(n_seq // seq_tile,),
        in_specs=in_specs,
        out_specs=[
            fixed(n_seq, D_MODEL),
            pl.BlockSpec((None, seq_tile, KEY_WIDTH, HEAD_V), lambda i: (layer, i, 0, 0)),
            whole(POOL_BUF, n_seq, POOL_WIDTH),
        ],
        out_shape=[
            jax.ShapeDtypeStruct((n_seq, D_MODEL), jnp.bfloat16),
            jax.ShapeDtypeStruct((DEPTH, n_seq, KEY_WIDTH, HEAD_V), f32),
            jax.ShapeDtypeStruct((DEPTH, POOL_BUF, n_seq, POOL_WIDTH), f32),
        ],
        scratch_shapes=[
            pltpu.VMEM((n_seq, KEY_WIDTH), f32),
            pltpu.VMEM((n_seq, KEY_WIDTH), f32),
            pltpu.VMEM((n_seq, KEY_WIDTH), f32),
            pltpu.VMEM((n_seq, GLA_WIDTH), f32),
            pltpu.VMEM((n_seq, GLA_WIDTH), f32),
            pltpu.VMEM((n_seq, GLA_WIDTH), f32),
        ],
        input_output_aliases=aliases,
        compiler_params=pltpu.CompilerParams(
            dimension_semantics=("arbitrary",), vmem_limit_bytes=VMEM_LIMIT),
        name=f"sample_mixer_l{layer}",
    )(*operands)


def _channel_kernel(x_ref, mix_ref, xs_ref, mix_s_ref, w_out_ref, n2_ref, w_up_ref, w_down_ref, final_g_ref,
                    y_ref, ys_ref, *, final):
    def channel_mix(x, mix):
        x1 = x + jnp.dot(mix, w_out_ref[...], preferred_element_type=jnp.float32)
        h2 = _rmsnorm(x1, n2_ref[...]).astype(jnp.bfloat16)
        acc = x1
        for j in range(D_FF // FF_CHUNK):
            cols = slice(j * FF_CHUNK, (j + 1) * FF_CHUNK)
            hid = jnp.dot(h2, w_up_ref[:, cols], preferred_element_type=jnp.float32)
            act = jnp.square(jnp.maximum(hid, 0.0)).astype(jnp.bfloat16)
            acc = acc + jnp.dot(act, w_down_ref[cols, :], preferred_element_type=jnp.float32)
        if final:
            acc = _rmsnorm(acc, final_g_ref[...])
        return acc

    y_ref[...] = channel_mix(x_ref[...], mix_ref[...])

    @pl.when(pl.program_id(0) == pl.num_programs(0) - 1)
    def _():
        ys_ref[...] = channel_mix(xs_ref[...], mix_s_ref[...])


def _channel_mixer(layer, x, mix, xs, mix_s, p, final_g):
    rows = x.shape[0]
    n_seq = xs.shape[0]
    tile = CHANNEL_TILE
    whole = lambda *shape: pl.BlockSpec((None,) + shape, lambda i: (layer,) + (0,) * len(shape),
                                        pipeline_mode=pl.Buffered(1))
    return pl.pallas_call(
        functools.partial(_channel_kernel, final=layer == DEPTH - 1),
        grid=(rows // tile,),
        in_specs=[
            pl.BlockSpec((tile, D_MODEL), lambda i: (i, 0)),
            pl.BlockSpec((tile, D_MODEL), lambda i: (i, 0)),
            pl.BlockSpec((n_seq, D_MODEL), lambda i: (0, 0)),
            pl.BlockSpec((n_seq, D_MODEL), lambda i: (0, 0)),
            whole(D_MODEL, D_MODEL),
            whole(1, D_MODEL),
            whole(D_MODEL, D_FF),
            whole(D_FF, D_MODEL),
            pl.BlockSpec((1, D_MODEL), lambda i: (0, 0)),
        ],
        out_specs=[
            pl.BlockSpec((tile, D_MODEL), lambda i: (i, 0)),
            pl.BlockSpec((n_seq, D_MODEL), lambda i: (0, 0)),
        ],
        out_shape=[
            jax.ShapeDtypeStruct((rows, D_MODEL), jnp.float32),
            jax.ShapeDtypeStruct((n_seq, D_MODEL), jnp.float32),
        ],
        compiler_params=pltpu.CompilerParams(
            dimension_semantics=("arbitrary",), vmem_limit_bytes=VMEM_LIMIT),
        name=f"channel_mixer_l{layer}",
    )(x, mix, xs, mix_s, p["w_out"], p["n2"], p["w_up"], p["w_down"], final_g)


def kernel(x_prompt, x_sample, state_gla, state_pool, norm1_g, w_in, w_gate, b_gate, gla_norm_g, pool_w,
           pool_scale, w_out, norm2_g, w_up, w_down, final_g):
    B, T, _ = x_prompt.shape
    n_seq = x_sample.shape[0]
    bf16 = jnp.bfloat16

    w_code = jnp.pad(w_in[:, :, W_IN_GATE:W_IN_U], ((0, 0), (0, 0), (0, LANES - GATE_RANK)))
    pw = pool_w.reshape(DEPTH, len(POOL_WINDOWS) // 2, 2, POOL_GROUP, POOL_GROUP)
    zero = jnp.zeros_like(pw[:, :, 0])
    pool_pairs = jnp.concatenate([jnp.concatenate([pw[:, :, 0], zero], axis=-1),
                                  jnp.concatenate([zero, pw[:, :, 1]], axis=-1)], axis=-2)
    params = {
        "n1": norm1_g.reshape(DEPTH, 1, D_MODEL),
        "w_main": w_in[:, :, :W_IN_GATE].astype(bf16),
        "w_code": w_code.astype(bf16),
        "w_u": w_in[:, :, W_IN_U:].astype(bf16),
        "w_gate": jnp.pad(w_gate, ((0, 0), (0, LANES - GATE_RANK), (0, 0))).astype(bf16),
        "b_gate": b_gate.reshape(DEPTH, 1, KEY_WIDTH),
        "gla_g": gla_norm_g.reshape(DEPTH, 1, HEAD_V),
        "pool_w": pool_pairs.astype(bf16),
        "pool_scale": pool_scale.reshape(DEPTH, 1, POOL_WIDTH),
        "w_out": w_out.astype(bf16),
        "n2": norm2_g.reshape(DEPTH, 1, D_MODEL),
        "w_up": w_up.astype(bf16),
        "w_down": w_down.astype(bf16),
    }
    final_g2 = final_g.reshape(1, D_MODEL)
    s_state = state_gla.reshape(DEPTH, n_seq, KEY_WIDTH, HEAD_V)
    pool_rows = jnp.swapaxes(state_pool, 1, 2)

    xp = x_prompt
    xs = x_sample.reshape(n_seq, D_MODEL)
    prompt_states = sample_states = None
    for layer in range(DEPTH):
        mix_p, *prompt_states = _prompt_mixer(layer, xp, params, prompt_states)
        mix_s, *sample_states = _sample_mixer(layer, xs, s_state, pool_rows, params, sample_states)
        xp, xs = _channel_mixer(layer, xp.reshape(B * T, D_MODEL), mix_p.reshape(B * T, D_MODEL), xs, mix_s,
                                params, final_g2)
        xp = xp.reshape(B, T, D_MODEL)
    gla_p, hist_p = prompt_states
    gla_s, pool_s = sample_states
    return (xp, xs.reshape(n_seq, 1, D_MODEL),
            gla_p.reshape(DEPTH, B, N_HEADS, HEAD_K, HEAD_V), hist_p[:, :, POOL_HIST - POOL_BUF:],
            gla_s.reshape(DEPTH, n_seq, N_HEADS, HEAD_K, HEAD_V), jnp.swapaxes(pool_s, 1, 2))
```

```python
import functools

import jax
import jax.numpy as jnp
from jax import lax
from jax.experimental import pallas as pl
from jax.experimental.pallas import tpu as pltpu

D_MODEL = 1024
DEPTH = 4
N_HEADS = 4
HEAD_K = 64
HEAD_V = 128
KEY_WIDTH = N_HEADS * HEAD_K
GLA_WIDTH = N_HEADS * HEAD_V
GATE_RANK = 16
GATE_TEMP = 16.0
CHUNK = 64
POOL_WIDTH = 512
POOL_WINDOWS = (2, 4, 8, 16)
POOL_GROUP = 128
POOL_PAIR = 2 * POOL_GROUP
POOL_BUF = 15
POOL_HIST = 16
D_FF = 4 * D_MODEL
FF_CHUNK = 1024
EPS = 1e-6
LANES = 128
SUBLANES = 8

COL_Q = 0
COL_K = KEY_WIDTH
COL_V = 2 * KEY_WIDTH
COL_G = COL_V + GLA_WIDTH
COL_U = COL_G + GLA_WIDTH
Z_WIDTH = COL_U + POOL_WIDTH
W_IN_GATE = COL_U
W_IN_U = COL_U + GATE_RANK

PROMPT_TILE = 1024
PROMPT_SUB = 128
PROJ_PIECE = 256
CHANNEL_TILE = 512
SAMPLE_STATE_TILE = 16
VMEM_LIMIT = 56 * 1024 * 1024


def _rmsnorm(x, g):
    return x * lax.rsqrt(jnp.mean(x * x, axis=-1, keepdims=True) + EPS) * g


def _log_sigmoid(x):
    return jnp.minimum(x, 0.0) - jnp.log(1.0 + jnp.exp(-jnp.abs(x)))


def _bf16_round(x):
    return x.astype(jnp.bfloat16).astype(jnp.float32)


def _head_lane_mask(rows):
    lane = lax.broadcasted_iota(jnp.int32, (rows, KEY_WIDTH), 1)
    return [(lane >= h * HEAD_K) & (lane < (h + 1) * HEAD_K) for h in range(N_HEADS)]


def _gate_log_decay(a_low, w_gate_ref, b_gate_ref):
    pre = jnp.dot(a_low.astype(jnp.bfloat16), w_gate_ref[...], preferred_element_type=jnp.float32) + b_gate_ref[...]
    return _log_sigmoid(pre) / GATE_TEMP


def _project(x, n1_ref, w_main_ref, w_code_ref, w_u_ref, w_gate_ref, b_gate_ref):
    h = _rmsnorm(x, n1_ref[...]).astype(jnp.bfloat16)
    qkvg = jnp.dot(h, w_main_ref[...], preferred_element_type=jnp.float32)
    a_low = jnp.dot(h, w_code_ref[...], preferred_element_type=jnp.float32)
    u = jnp.dot(h, w_u_ref[...], preferred_element_type=jnp.float32)
    return qkvg, u, _gate_log_decay(a_low, w_gate_ref, b_gate_ref)


def _gla_finish(o, g, gla_g):
    outs = []
    for h in range(N_HEADS):
        sl = slice(h * HEAD_V, (h + 1) * HEAD_V)
        outs.append(_rmsnorm(o[:, sl], gla_g) * (g[:, sl] * jax.nn.sigmoid(g[:, sl])))
    return jnp.concatenate(outs, axis=-1)


def _trace_interleaved(a, b):
    i = j = 0
    while i < len(a) or j < len(b):
        if j >= len(b) or (i < len(a) and i * len(b) <= j * len(a)):
            a[i]()
            i += 1
        else:
            b[j]()
            j += 1


def _prompt_project_tasks(load_x, slot, n1_ref, w_main_ref, w_code_ref, w_u_ref, w_gate_ref, b_gate_ref,
                          z_ref, b_ref):
    v = {}

    def norm():
        v["h"] = _rmsnorm(load_x(), n1_ref[...]).astype(jnp.bfloat16)

    def piece(lo, hi):
        w_ref, first = (w_main_ref, 0) if hi <= COL_U else (w_u_ref, COL_U)

        def run():
            z_ref[slot, :, lo:hi] = jnp.dot(v["h"], w_ref[:, lo - first:hi - first],
                                            preferred_element_type=jnp.float32)
        return run

    def gate_code():
        v["a_low"] = jnp.dot(v["h"], w_code_ref[...], preferred_element_type=jnp.float32)

    def gate_pre():
        v["pre"] = (jnp.dot(v["a_low"].astype(jnp.bfloat16), w_gate_ref[...], preferred_element_type=jnp.float32)
                    + b_gate_ref[...])

    def log_decay():
        v["b"] = _log_sigmoid(v["pre"]) / GATE_TEMP

    def scan(shifts, last):
        def run():
            b = v["b"]
            row = lax.broadcasted_iota(jnp.int32, b.shape, 0) % CHUNK
            for shift in shifts:
                b = b + jnp.where(row >= shift, pltpu.roll(b, shift, axis=0), 0.0)
            v["b"] = b
            if last:
                b_ref[slot] = b
        return run

    half = PROJ_PIECE
    return [
        norm, gate_code,
        piece(COL_Q, COL_K), gate_pre,
        piece(COL_K, COL_V), log_decay,
        piece(COL_V, COL_V + half), scan((1, 2), False),
        piece(COL_V + half, COL_G), scan((4, 8), False),
        piece(COL_G, COL_G + half), scan((16, 32), True),
        piece(COL_G + half, COL_U), piece(COL_U, COL_U + half), piece(COL_U + half, Z_WIDTH),
    ]


def _prompt_mix_tasks(slot, z_ref, b_ref, carry, first_pos, gla_g_ref, pool_w_ref, pool_scale_ref, mix_ref, base):
    head_mask = _head_lane_mask(CHUNK)
    head_mask_v = _head_lane_mask(HEAD_V)
    ci = lax.broadcasted_iota(jnp.int32, (CHUNK, CHUNK), 0)
    si = lax.broadcasted_iota(jnp.int32, (CHUNK, CHUNK), 1)
    causal = si <= ci
    scale = HEAD_K ** -0.5
    sub = z_ref.shape[1]

    def chunk_tasks(c):
        rows = slice(c * CHUNK, (c + 1) * CHUNK)
        out_rows = slice(base + c * CHUNK, base + (c + 1) * CHUNK)
        v = {}

        def prepare():
            b_c = b_ref[slot, rows, :]
            b_last = b_c[CHUNK - 1:CHUNK]
            q_c = z_ref[slot, rows, COL_Q:COL_K]
            k_c = z_ref[slot, rows, COL_K:COL_V]
            v["v"] = z_ref[slot, rows, COL_V:COL_G].astype(jnp.bfloat16)
            qt = q_c * jnp.exp(b_c) * scale
            v["kt"] = (k_c * jnp.exp(-b_c)).astype(jnp.bfloat16)
            v["ke"] = (k_c * jnp.exp(b_last - b_c)).astype(jnp.bfloat16)
            v["decay"] = jnp.exp(b_last)
            v["q_stack"] = jnp.concatenate([jnp.where(m, qt, 0.0) for m in head_mask], axis=0).astype(jnp.bfloat16)

        def update_product():
            v["p"] = lax.dot_general(v["v"], v["ke"], (((0,), (0,)), ((), ())),
                                     preferred_element_type=jnp.float32)

        def query_product():
            st = carry["st"]
            rhs = jnp.concatenate([st.astype(jnp.bfloat16), v["kt"]], axis=0)
            v["r"] = lax.dot_general(v["q_stack"], rhs, (((1,), (1,)), ((), ())),
                                     preferred_element_type=jnp.float32)

        def update_state():
            p = v["p"]
            upd = jnp.where(head_mask_v[0], p[0:HEAD_V], 0.0)
            for h in range(1, N_HEADS):
                upd = upd + jnp.where(head_mask_v[h], p[h * HEAD_V:(h + 1) * HEAD_V], 0.0)
            carry["st"] = v["decay"] * carry["st"] + upd

        def intra():
            r = v["r"]
            outs = []
            for h in range(N_HEADS):
                hr = slice(h * CHUNK, (h + 1) * CHUNK)
                att = jnp.where(causal, r[hr, HEAD_V:HEAD_V + CHUNK], 0.0).astype(jnp.bfloat16)
                o_intra = jnp.dot(att, v["v"][:, h * HEAD_V:(h + 1) * HEAD_V], preferred_element_type=jnp.float32)
                outs.append(o_intra + r[hr, 0:HEAD_V])
            v["o"] = jnp.concatenate(outs, axis=-1)

        def finish():
            g_c = z_ref[slot, rows, COL_G:COL_U]
            mix_ref[out_rows, 0:GLA_WIDTH] = _gla_finish(v["o"], g_c, gla_g_ref[...]).astype(mix_ref.dtype)

        return dict(prepare=prepare, update_product=update_product, query_product=query_product,
                    update_state=update_state, intra=intra, finish=finish)

    pooled = {}

    def window_mean_task(gi, w):
        lanes = slice(gi * POOL_GROUP, (gi + 1) * POOL_GROUP)

        def window_mean():
            u = z_ref[slot, :, COL_U + gi * POOL_GROUP:COL_U + (gi + 1) * POOL_GROUP]
            s = jnp.concatenate([carry["hist"][:, lanes], u], axis=0)
            shift = 1
            while shift < w:
                s = s + pltpu.roll(s, shift, axis=0)
                shift *= 2
            seen = first_pos + 1 + lax.broadcasted_iota(jnp.int32, (sub, POOL_GROUP), 0)
            if "inv_seen" not in pooled:
                pooled["inv_seen"] = 1.0 / seen.astype(jnp.float32)
            inv_count = jnp.where(seen < w, pooled["inv_seen"], 1.0 / w)
            pooled[gi] = (s[POOL_HIST:] * inv_count - u).astype(jnp.bfloat16)

        return window_mean

    def group_map_task(pair):
        lanes = slice(pair * POOL_PAIR, (pair + 1) * POOL_PAIR)

        def group_map():
            both = jnp.concatenate([pooled[2 * pair], pooled[2 * pair + 1]], axis=-1)
            pg = jnp.dot(both, pool_w_ref[pair], preferred_element_type=jnp.float32)
            pg = pg * pool_scale_ref[:, lanes]
            mix_ref[base:base + sub, GLA_WIDTH + pair * POOL_PAIR:GLA_WIDTH + (pair + 1) * POOL_PAIR] = (
                pg.astype(mix_ref.dtype))

        return group_map

    def keep_history():
        carry["hist"] = z_ref[slot, sub - POOL_HIST:sub, COL_U:Z_WIDTH]

    assert sub == 2 * CHUNK
    c0, c1 = chunk_tasks(0), chunk_tasks(1)
    means = [window_mean_task(gi, w) for gi, w in enumerate(POOL_WINDOWS)]
    maps = [group_map_task(pair) for pair in range(len(POOL_WINDOWS) // 2)]
    return [
        c0["prepare"], c1["prepare"], c0["update_product"], c0["query_product"], c1["update_product"],
        c0["update_state"], c1["query_product"], means[0], c0["intra"], c1["update_state"],
        means[1], c1["intra"], maps[0], c0["finish"], means[2], c1["finish"], means[3], maps[1], keep_history,
    ]


def _prompt_mixer_kernel(*refs, chained):
    (x_ref, x_next_ref, n1_ref, w_main_ref, w_code_ref, w_u_ref, w_gate_ref, b_gate_ref, gla_g_ref, pool_w_ref,
     pool_scale_ref) = refs[:11]
    mix_ref, s_fin_ref, hist_out_ref, st_ref, hist_ref, z_ref, b_ref = refs[-7:]
    tile = x_ref.shape[0]
    n_sub = tile // PROMPT_SUB
    t = pl.program_id(1)

    def project_tasks(load_x, slot):
        return _prompt_project_tasks(load_x, slot, n1_ref, w_main_ref, w_code_ref, w_u_ref, w_gate_ref, b_gate_ref,
                                     z_ref, b_ref)

    @pl.when(t == 0)
    def _():
        st_ref[...] = jnp.zeros_like(st_ref)
        hist_ref[...] = jnp.zeros_like(hist_ref)

    @pl.when((t == 0) & (pl.program_id(0) == 0))
    def _():
        for task in project_tasks(lambda: x_ref[0:PROMPT_SUB], 0):
            task()

    carry = {"st": st_ref[...], "hist": hist_ref[...]}
    for sb in range(n_sub):
        if sb + 1 < n_sub:
            load_next = lambda sb=sb: x_ref[(sb + 1) * PROMPT_SUB:(sb + 2) * PROMPT_SUB]
        else:
            load_next = lambda: x_next_ref[...]
        _trace_interleaved(
            project_tasks(load_next, (sb + 1) % 2),
            _prompt_mix_tasks(sb % 2, z_ref, b_ref, carry, t * tile + sb * PROMPT_SUB,
                              gla_g_ref, pool_w_ref, pool_scale_ref, mix_ref, sb * PROMPT_SUB))
    st_ref[...] = carry["st"]
    hist_ref[...] = carry["hist"]

    @pl.when(t == pl.num_programs(1) - 1)
    def _():
        s_fin_ref[...] = carry["st"].T
        hist_out_ref[...] = carry["hist"]


def _prompt_mixer(layer, x, p, stacked):
    B, T, _ = x.shape
    tile = PROMPT_TILE
    n_tiles = T // tile
    n_sub = tile // PROMPT_SUB
    assert n_sub % 2 == 0
    whole = lambda *shape: pl.BlockSpec((None,) + shape, lambda b, t: (layer,) + (0,) * len(shape))
    chained = stacked is not None
    operands = [x, x, p["n1"], p["w_main"], p["w_code"], p["w_u"], p["w_gate"], p["b_gate"], p["gla_g"], p["pool_w"],
                p["pool_scale"]]

    def next_first_sub_block(b, t):
        n = jnp.minimum(b * n_tiles + t + 1, B * n_tiles - 1)
        return (n // n_tiles, (n % n_tiles) * n_sub, 0)

    in_specs = [
        pl.BlockSpec((None, tile, D_MODEL), lambda b, t: (b, t, 0)),
        pl.BlockSpec((None, PROMPT_SUB, D_MODEL), next_first_sub_block),
        whole(1, D_MODEL),
        whole(D_MODEL, COL_U),
        whole(D_MODEL, LANES),
        whole(D_MODEL, POOL_WIDTH),
        whole(LANES, KEY_WIDTH),
        whole(1, KEY_WIDTH),
        whole(1, HEAD_V),
        whole(len(POOL_WINDOWS) // 2, POOL_PAIR, POOL_PAIR),
        whole(1, POOL_WIDTH),
    ]
    aliases = {}
    if chained:
        aliases = {len(operands): 1, len(operands) + 1: 2}
        operands += list(stacked)
        in_specs += [pl.BlockSpec(memory_space=pl.ANY)] * 2
    return pl.pallas_call(
        functools.partial(_prompt_mixer_kernel, chained=chained),
        grid=(B, n_tiles),
        in_specs=in_specs,
        out_specs=[
            pl.BlockSpec((None, tile, D_MODEL), lambda b, t: (b, t, 0)),
            pl.BlockSpec((None, None, KEY_WIDTH, HEAD_V), lambda b, t: (layer, b, 0, 0)),
            pl.BlockSpec((None, None, POOL_HIST, POOL_WIDTH), lambda b, t: (layer, b, 0, 0)),
        ],
        out_shape=[
            jax.ShapeDtypeStruct((B, T, D_MODEL), jnp.bfloat16),
            jax.ShapeDtypeStruct((DEPTH, B, KEY_WIDTH, HEAD_V), jnp.float32),
            jax.ShapeDtypeStruct((DEPTH, B, POOL_HIST, POOL_WIDTH), jnp.float32),
        ],
        scratch_shapes=[
            pltpu.VMEM((HEAD_V, KEY_WIDTH), jnp.float32),
            pltpu.VMEM((POOL_HIST, POOL_WIDTH), jnp.float32),
            pltpu.VMEM((2, PROMPT_SUB, Z_WIDTH), jnp.float32),
            pltpu.VMEM((2, PROMPT_SUB, KEY_WIDTH), jnp.float32),
        ],
        input_output_aliases=aliases,
        compiler_params=pltpu.CompilerParams(
            dimension_semantics=("arbitrary", "arbitrary"), vmem_limit_bytes=VMEM_LIMIT),
        name=f"prompt_mixer_l{layer}",
    )(*operands)


def _sample_mixer_kernel(*refs, chained):
    (x_ref, n1_ref, w_main_ref, w_code_ref, w_u_ref, w_gate_ref, b_gate_ref, gla_g_ref, pool_w_ref,
     pool_scale_ref, s_ref, pool_state_ref) = refs[:12]
    mix_ref, s_new_ref, pool_new_ref, dec_ref, ke_ref, qt_ref, v_ref, g_ref, o_ref = refs[-9:]
    i = pl.program_id(0)
    n_seq = x_ref.shape[0]
    seq_tile = s_ref.shape[0]

    @pl.when(i == 0)
    def _():
        qkvg, u, log_a = _project(x_ref[...], n1_ref, w_main_ref, w_code_ref, w_u_ref, w_gate_ref, b_gate_ref)
        q = qkvg[:, COL_Q:COL_K]
        k = qkvg[:, COL_K:COL_V]
        v = _bf16_round(qkvg[:, COL_V:COL_G])
        qt = _bf16_round(q * jnp.exp(log_a) * (HEAD_K ** -0.5))
        kt = _bf16_round(k * jnp.exp(-log_a))
        ke = _bf16_round(k * jnp.exp(log_a - log_a))
        prod = qt * kt
        o_intra = []
        for h, m in enumerate(_head_lane_mask(n_seq)):
            att = _bf16_round(jnp.sum(jnp.where(m, prod, 0.0), axis=-1, keepdims=True))
            o_intra.append(att * v[:, h * HEAD_V:(h + 1) * HEAD_V])
        o_ref[...] = jnp.concatenate(o_intra, axis=-1)
        dec_ref[...] = jnp.exp(log_a)
        ke_ref[...] = ke
        qt_ref[...] = qt
        v_ref[...] = v
        g_ref[...] = qkvg[:, COL_G:COL_U]

        pooled = []
        for gi, w in enumerate(POOL_WINDOWS):
            lanes = slice(gi * POOL_GROUP, (gi + 1) * POOL_GROUP)
            s = u[:, lanes]
            for j in range(POOL_BUF - (w - 1), POOL_BUF):
                s = s + pool_state_ref[j, :, lanes]
            pooled.append((s / float(w) - u[:, lanes]).astype(jnp.bfloat16))
        for pair in range(len(POOL_WINDOWS) // 2):
            lanes = slice(pair * POOL_PAIR, (pair + 1) * POOL_PAIR)
            both = jnp.concatenate(pooled[2 * pair:2 * pair + 2], axis=-1)
            pg = jnp.dot(both, pool_w_ref[pair], preferred_element_type=jnp.float32) * pool_scale_ref[:, lanes]
            mix_ref[:, GLA_WIDTH + pair * POOL_PAIR:GLA_WIDTH + (pair + 1) * POOL_PAIR] = pg.astype(mix_ref.dtype)
        for j in range(POOL_BUF - 1):
            pool_new_ref[j] = pool_state_ref[j + 1]
        pool_new_ref[POOL_BUF - 1] = u

    lane = lax.broadcasted_iota(jnp.int32, (SUBLANES, KEY_WIDTH), 1)
    head = lax.broadcasted_iota(jnp.int32, (SUBLANES, KEY_WIDTH), 0)
    head_rows = jnp.where(lane // HEAD_K == head, 1.0, 0.0)
    for sub in range(seq_tile // SUBLANES):
        local = slice(sub * SUBLANES, (sub + 1) * SUBLANES)
        rows = pl.ds(pl.multiple_of(i * seq_tile + sub * SUBLANES, SUBLANES), SUBLANES)
        s_old = s_ref[local]
        v = v_ref[rows, :]
        v_rows = jnp.concatenate(
            [jnp.broadcast_to(v[:, None, h * HEAD_V:(h + 1) * HEAD_V], (SUBLANES, HEAD_K, HEAD_V))
             for h in range(N_HEADS)], axis=1)
        s_new_ref[local] = dec_ref[rows, :][:, :, None] * s_old + ke_ref[rows, :][:, :, None] * v_rows
        q_rows = (qt_ref[rows, :][:, None, :] * head_rows[None]).astype(jnp.bfloat16)
        o_inter = jnp.einsum("bhk,bkv->bhv", q_rows, s_old.astype(jnp.bfloat16),
                             preferred_element_type=jnp.float32)
        o_ref[rows, :] = o_ref[rows, :] + jnp.concatenate([o_inter[:, h, :] for h in range(N_HEADS)], axis=-1)

    @pl.when(i == pl.num_programs(0) - 1)
    def _():
        mix_ref[:, 0:GLA_WIDTH] = _gla_finish(o_ref[...], g_ref[...], gla_g_ref[...]).astype(mix_ref.dtype)


def _sample_mixer(layer, x, s_state, pool_state, p, stacked):
    n_seq = x.shape[0]
    seq_tile = SAMPLE_STATE_TILE
    whole = lambda *shape: pl.BlockSpec((None,) + shape, lambda i: (layer,) + (0,) * len(shape))
    fixed = lambda *shape: pl.BlockSpec(shape, lambda i: (0,) * len(shape))
    f32 = jnp.float32
    chained = stacked is not None
    operands = [x, p["n1"], p["w_main"], p["w_code"], p["w_u"], p["w_gate"], p["b_gate"], p["gla_g"], p["pool_w"],
                p["pool_scale"], s_state, pool_state]
    in_specs = [
        fixed(n_seq, D_MODEL),
        whole(1, D_MODEL),
        whole(D_MODEL, COL_U),
        whole(D_MODEL, LANES),
        whole(D_MODEL, POOL_WIDTH),
        whole(LANES, KEY_WIDTH),
        whole(1, KEY_WIDTH),
        whole(1, HEAD_V),
        whole(len(POOL_WINDOWS) // 2, POOL_PAIR, POOL_PAIR),
        whole(1, POOL_WIDTH),
        pl.BlockSpec((None, seq_tile, KEY_WIDTH, HEAD_V), lambda i: (layer, i, 0, 0)),
        whole(POOL_BUF, n_seq, POOL_WIDTH),
    ]
    aliases = {}
    if chained:
        aliases = {len(operands): 1, len(operands) + 1: 2}
        operands += list(stacked)
        in_specs += [pl.BlockSpec(memory_space=pl.ANY)] * 2
    return pl.pallas_call(
        functools.partial(_sample_mixer_kernel, chained=chained),
        grid=(n_seq // seq_tile,),
        in_specs=in_specs,
        out_specs=[
            fixed(n_seq, D_MODEL),
            pl.BlockSpec((None, seq_tile, KEY_WIDTH, HEAD_V), lambda i: (layer, i, 0, 0)),
            whole(POOL_BUF, n_seq, POOL_WIDTH),
        ],
        out_shape=[
            jax.ShapeDtypeStruct((n_seq, D_MODEL), jnp.bfloat16),
            jax.ShapeDtypeStruct((DEPTH, n_seq, KEY_WIDTH, HEAD_V), f32),
            jax.ShapeDtypeStruct((DEPTH, POOL_BUF, n_seq, POOL_WIDTH), f32),
        ],
        scratch_shapes=[
            pltpu.VMEM((n_seq, KEY_WIDTH), f32),
            pltpu.VMEM((n_seq, KEY_WIDTH), f32),
            pltpu.VMEM((n_seq, KEY_WIDTH), f32),
            pltpu.VMEM((n_seq, GLA_WIDTH), f32),
            pltpu.VMEM((n_seq, GLA_WIDTH), f32),
            pltpu.VMEM((n_seq, GLA_WIDTH), f32),
        ],
        input_output_aliases=aliases,
        compiler_params=pltpu.CompilerParams(
            dimension_semantics=("arbitrary",), vmem_limit_bytes=VMEM_LIMIT),
        name=f"sample_mixer_l{layer}",
    )(*operands)


def _channel_kernel(x_ref, mix_ref, xs_ref, mix_s_ref, w_out_ref, n2_ref, w_up_ref, w_down_ref, final_g_ref,
                    y_ref, ys_ref, *, final):
    def channel_mix(x, mix):
        x1 = x + jnp.dot(mix, w_out_ref[...], preferred_element_type=jnp.float32)
        h2 = _rmsnorm(x1, n2_ref[...]).astype(jnp.bfloat16)
        acc = x1
        for j in range(D_FF // FF_CHUNK):
            cols = slice(j * FF_CHUNK, (j + 1) * FF_CHUNK)
            hid = jnp.dot(h2, w_up_ref[:, cols], preferred_element_type=jnp.float32)
            act = jnp.square(jnp.maximum(hid, 0.0)).astype(jnp.bfloat16)
            acc = acc + jnp.dot(act, w_down_ref[cols, :], preferred_element_type=jnp.float32)
        if final:
            acc = _rmsnorm(acc, final_g_ref[...])
        return acc

    y_ref[...] = channel_mix(x_ref[...], mix_ref[...])

    @pl.when(pl.program_id(0) == pl.num_programs(0) - 1)
    def _():
        ys_ref[...] = channel_mix(xs_ref[...], mix_s_ref[...])


def _channel_mixer(layer, x, mix, xs, mix_s, p, final_g):
    rows = x.shape[0]
    n_seq = xs.shape[0]
    tile = CHANNEL_TILE
    whole = lambda *shape: pl.BlockSpec((None,) + shape, lambda i: (layer,) + (0,) * len(shape),
                                        pipeline_mode=pl.Buffered(1))
    return pl.pallas_call(
        functools.partial(_channel_kernel, final=layer == DEPTH - 1),
        grid=(rows // tile,),
        in_specs=[
            pl.BlockSpec((tile, D_MODEL), lambda i: (i, 0)),
            pl.BlockSpec((tile, D_MODEL), lambda i: (i, 0)),
            pl.BlockSpec((n_seq, D_MODEL), lambda i: (0, 0)),
            pl.BlockSpec((n_seq, D_MODEL), lambda i: (0, 0)),
            whole(D_MODEL, D_MODEL),
            whole(1, D_MODEL),
            whole(D_MODEL, D_FF),
            whole(D_FF, D_MODEL),
            pl.BlockSpec((1, D_MODEL), lambda i: (0, 0)),
        ],
        out_specs=[
            pl.BlockSpec((tile, D_MODEL), lambda i: (i, 0)),
            pl.BlockSpec((n_seq, D_MODEL), lambda i: (0, 0)),
        ],
        out_shape=[
            jax.ShapeDtypeStruct((rows, D_MODEL), jnp.float32),
            jax.ShapeDtypeStruct((n_seq, D_MODEL), jnp.float32),
        ],
        compiler_params=pltpu.CompilerParams(
            dimension_semantics=("arbitrary",), vmem_limit_bytes=VMEM_LIMIT),
        name=f"channel_mixer_l{layer}",
    )(x, mix, xs, mix_s, p["w_out"], p["n2"], p["w_up"], p["w_down"], final_g)


def kernel(x_prompt, x_sample, state_gla, state_pool, norm1_g, w_in, w_gate, b_gate, gla_norm_g, pool_w,
           pool_scale, w_out, norm2_g, w_up, w_down, final_g):
    B, T, _ = x_prompt.shape
    n_seq = x_sample.shape[0]
    bf16 = jnp.bfloat16

    w_code = jnp.pad(w_in[:, :, W_IN_GATE:W_IN_U], ((0, 0), (0, 0), (0, LANES - GATE_RANK)))
    pw = pool_w.reshape(DEPTH, len(POOL_WINDOWS) // 2, 2, POOL_GROUP, POOL_GROUP)
    zero = jnp.zeros_like(pw[:, :, 0])
    pool_pairs = jnp.concatenate([jnp.concatenate([pw[:, :, 0], zero], axis=-1),
                                  jnp.concatenate([zero, pw[:, :, 1]], axis=-1)], axis=-2)
    params = {
        "n1": norm1_g.reshape(DEPTH, 1, D_MODEL),
        "w_main": w_in[:, :, :W_IN_GATE].astype(bf16),
        "w_code": w_code.astype(bf16),
        "w_u": w_in[:, :, W_IN_U:].astype(bf16),
        "w_gate": jnp.pad(w_gate, ((0, 0), (0, LANES - GATE_RANK), (0, 0))).astype(bf16),
        "b_gate": b_gate.reshape(DEPTH, 1, KEY_WIDTH),
        "gla_g": gla_norm_g.reshape(DEPTH, 1, HEAD_V),
        "pool_w": pool_pairs.astype(bf16),
        "pool_scale": pool_scale.reshape(DEPTH, 1, POOL_WIDTH),
        "w_out": w_out.astype(bf16),
        "n2": norm2_g.reshape(DEPTH, 1, D_MODEL),
        "w_up": w_up.astype(bf16),
        "w_down": w_down.astype(bf16),
    }
    final_g2 = final_g.reshape(1, D_MODEL)
    s_state = state_gla.reshape(DEPTH, n_seq, KEY_WIDTH, HEAD_V)
    pool_rows = jnp.swapaxes(state_pool, 1, 2)

    xp = x_prompt
    xs = x_sample.reshape(n_seq, D_MODEL)
    prompt_states = sample_states = None
    for layer in range(DEPTH):
        mix_p, *prompt_states = _prompt_mixer(layer, xp, params, prompt_states)
        mix_s, *sample_states = _sample_mixer(layer, xs, s_state, pool_rows, params, sample_states)
        xp, xs = _channel_mixer(layer, xp.reshape(B * T, D_MODEL), mix_p.reshape(B * T, D_MODEL), xs, mix_s,
                                params, final_g2)
        xp = xp.reshape(B, T, D_MODEL)
    gla_p, hist_p = prompt_states
    gla_s, pool_s = sample_states
    return (xp, xs.reshape(n_seq, 1, D_MODEL),
            gla_p.reshape(DEPTH, B, N_HEADS, HEAD_K, HEAD_V), hist_p[:, :, POOL_HIST - POOL_BUF:],
            gla_s.reshape(DEPTH, n_seq, N_HEADS, HEAD_K, HEAD_V), jnp.swapaxes(pool_s, 1, 2))
```

```python
import functools

import jax
import jax.numpy as jnp
from jax import lax
from jax.experimental import pallas as pl
from jax.experimental.pallas import tpu as pltpu

D_MODEL = 1024
DEPTH = 4
N_HEADS = 4
HEAD_K = 64
HEAD_V = 128
KEY_WIDTH = N_HEADS * HEAD_K
GLA_WIDTH = N_HEADS * HEAD_V
GATE_RANK = 16
GATE_TEMP = 16.0
CHUNK = 64
POOL_WIDTH = 512
POOL_WINDOWS = (2, 4, 8, 16)
POOL_GROUP = 128
POOL_PAIR = 2 * POOL_GROUP
POOL_BUF = 15
POOL_HIST = 16
D_FF = 4 * D_MODEL
FF_CHUNK = 1024
EPS = 1e-6
LANES = 128
SUBLANES = 8

COL_Q = 0
COL_K = KEY_WIDTH
COL_V = 2 * KEY_WIDTH
COL_G = COL_V + GLA_WIDTH
COL_U = COL_G + GLA_WIDTH
Z_WIDTH = COL_U + POOL_WIDTH
W_IN_GATE = COL_U
W_IN_U = COL_U + GATE_RANK

PROMPT_TILE = 1024
PROMPT_SUB = 128
PROJ_PIECE = 256
CHANNEL_TILE = 512
SAMPLE_STATE_TILE = 16
VMEM_LIMIT = 56 * 1024 * 1024


def _rmsnorm(x, g):
    return x * lax.rsqrt(jnp.mean(x * x, axis=-1, keepdims=True) + EPS) * g


def _log_sigmoid(x):
    return jnp.minimum(x, 0.0) - jnp.log(1.0 + jnp.exp(-jnp.abs(x)))


def _bf16_round(x):
    return x.astype(jnp.bfloat16).astype(jnp.float32)


def _head_lane_mask(rows):
    lane = lax.broadcasted_iota(jnp.int32, (rows, KEY_WIDTH), 1)
    return [(lane >= h * HEAD_K) & (lane < (h + 1) * HEAD_K) for h in range(N_HEADS)]


def _gate_log_decay(a_low, w_gate_ref, b_gate_ref):
    pre = jnp.dot(a_low.astype(jnp.bfloat16), w_gate_ref[...], preferred_element_type=jnp.float32) + b_gate_ref[...]
    return _log_sigmoid(pre) / GATE_TEMP


def _project(x, n1_ref, w_main_ref, w_code_ref, w_u_ref, w_gate_ref, b_gate_ref):
    h = _rmsnorm(x, n1_ref[...]).astype(jnp.bfloat16)
    qkvg = jnp.dot(h, w_main_ref[...], preferred_element_type=jnp.float32)
    a_low = jnp.dot(h, w_code_ref[...], preferred_element_type=jnp.float32)
    u = jnp.dot(h, w_u_ref[...], preferred_element_type=jnp.float32)
    return qkvg, u, _gate_log_decay(a_low, w_gate_ref, b_gate_ref)


def _gla_finish(o, g, gla_g):
    outs = []
    for h in range(N_HEADS):
        sl = slice(h * HEAD_V, (h + 1) * HEAD_V)
        outs.append(_rmsnorm(o[:, sl], gla_g) * (g[:, sl] * jax.nn.sigmoid(g[:, sl])))
    return jnp.concatenate(outs, axis=-1)


def _trace_interleaved(a, b):
    i = j = 0
    while i < len(a) or j < len(b):
        if j >= len(b) or (i < len(a) and i * len(b) <= j * len(a)):
            a[i]()
            i += 1
        else:
            b[j]()
            j += 1


def _prompt_project_tasks(load_x, slot, n1_ref, w_main_ref, w_code_ref, w_u_ref, w_gate_ref, b_gate_ref,
                          z_ref, b_ref):
    v = {}

    def norm():
        v["h"] = _rmsnorm(load_x(), n1_ref[...]).astype(jnp.bfloat16)

    def piece(lo, hi):
        w_ref, first = (w_main_ref, 0) if hi <= COL_U else (w_u_ref, COL_U)

        def run():
            z_ref[slot, :, lo:hi] = jnp.dot(v["h"], w_ref[:, lo - first:hi - first],
                                            preferred_element_type=jnp.float32)
        return run

    def gate_code():
        v["a_low"] = jnp.dot(v["h"], w_code_ref[...], preferred_element_type=jnp.float32)

    def gate_pre():
        v["pre"] = (jnp.dot(v["a_low"].astype(jnp.bfloat16), w_gate_ref[...], preferred_element_type=jnp.float32)
                    + b_gate_ref[...])

    def log_decay():
        v["b"] = _log_sigmoid(v["pre"]) / GATE_TEMP

    def scan(shifts, last):
        def run():
            b = v["b"]
            row = lax.broadcasted_iota(jnp.int32, b.shape, 0) % CHUNK
            for shift in shifts:
                b = b + jnp.where(row >= shift, pltpu.roll(b, shift, axis=0), 0.0)
            v["b"] = b
            if last:
                b_ref[slot] = b
        return run

    half = PROJ_PIECE
    return [
        norm, gate_code,
        piece(COL_Q, COL_K), gate_pre,
        piece(COL_K, COL_V), log_decay,
        piece(COL_V, COL_V + half), scan((1, 2), False),
        piece(COL_V + half, COL_G), scan((4, 8), False),
        piece(COL_G, COL_G + half), scan((16, 32), True),
        piece(COL_G + half, COL_U), piece(COL_U, COL_U + half), piece(COL_U + half, Z_WIDTH),
    ]


def _prompt_mix_tasks(slot, z_ref, b_ref, carry, first_pos, gla_g_ref, pool_w_ref, pool_scale_ref, mix_ref, base):
    head_mask = _head_lane_mask(CHUNK)
    head_mask_v = _head_lane_mask(HEAD_V)
    ci = lax.broadcasted_iota(jnp.int32, (CHUNK, CHUNK), 0)
    si = lax.broadcasted_iota(jnp.int32, (CHUNK, CHUNK), 1)
    causal = si <= ci
    scale = HEAD_K ** -0.5
    sub = z_ref.shape[1]

    def chunk_tasks(c):
        rows = slice(c * CHUNK, (c + 1) * CHUNK)
        out_rows = slice(base + c * CHUNK, base + (c + 1) * CHUNK)
        v = {}

        def prepare():
            b_c = b_ref[slot, rows, :]
            b_last = b_c[CHUNK - 1:CHUNK]
            q_c = z_ref[slot, rows, COL_Q:COL_K]
            k_c = z_ref[slot, rows, COL_K:COL_V]
            v["v"] = z_ref[slot, rows, COL_V:COL_G].astype(jnp.bfloat16)
            qt = q_c * jnp.exp(b_c) * scale
            v["kt"] = (k_c * jnp.exp(-b_c)).astype(jnp.bfloat16)
            v["ke"] = (k_c * jnp.exp(b_last - b_c)).astype(jnp.bfloat16)
            v["decay"] = jnp.exp(b_last)
            v["q_stack"] = jnp.concatenate([jnp.where(m, qt, 0.0) for m in head_mask], axis=0).astype(jnp.bfloat16)

        def update_product():
            v["p"] = lax.dot_general(v["v"], v["ke"], (((0,), (0,)), ((), ())),
                                     preferred_element_type=jnp.float32)

        def query_product():
            st = carry["st"]
            rhs = jnp.concatenate([st.astype(jnp.bfloat16), v["kt"]], axis=0)
            v["r"] = lax.dot_general(v["q_stack"], rhs, (((1,), (1,)), ((), ())),
                                     preferred_element_type=jnp.float32)

        def update_state():
            p = v["p"]
            upd = jnp.where(head_mask_v[0], p[0:HEAD_V], 0.0)
            for h in range(1, N_HEADS):
                upd = upd + jnp.where(head_mask_v[h], p[h * HEAD_V:(h + 1) * HEAD_V], 0.0)
            carry["st"] = v["decay"] * carry["st"] + upd

        def intra():
            r = v["r"]
            outs = []
            for h in range(N_HEADS):
                hr = slice(h * CHUNK, (h + 1) * CHUNK)
                att = jnp.where(causal, r[hr, HEAD_V:HEAD_V + CHUNK], 0.0).astype(jnp.bfloat16)
                o_intra = jnp.dot(att, v["v"][:, h * HEAD_V:(h + 1) * HEAD_V], preferred_element_type=jnp.float32)
                outs.append(o_intra + r[hr, 0:HEAD_V])
            v["o"] = jnp.concatenate(outs, axis=-1)

        def finish():
            g_c = z_ref[slot, rows, COL_G:COL_U]
            mix_ref[out_rows, 0:GLA_WIDTH] = _gla_finish(v["o"], g_c, gla_g_ref[...]).astype(mix_ref.dtype)

        return dict(prepare=prepare, update_product=update_product, query_product=query_product,
                    update_state=update_state, intra=intra, finish=finish)

    pooled = {}

    def window_mean_task(gi, w):
        lanes = slice(gi * POOL_GROUP, (gi + 1) * POOL_GROUP)

        def window_mean():
            u = z_ref[slot, :, COL_U + gi * POOL_GROUP:COL_U + (gi + 1) * POOL_GROUP]
            s = jnp.concatenate([carry["hist"][:, lanes], u], axis=0)
            shift = 1
            while shift < w:
                s = s + pltpu.roll(s, shift, axis=0)
                shift *= 2
            seen = first_pos + 1 + lax.broadcasted_iota(jnp.int32, (sub, POOL_GROUP), 0)
            if "inv_seen" not in pooled:
                pooled["inv_seen"] = 1.0 / seen.astype(jnp.float32)
            inv_count = jnp.where(seen < w, pooled["inv_seen"], 1.0 / w)
            pooled[gi] = (s[POOL_HIST:] * inv_count - u).astype(jnp.bfloat16)

        return window_mean

    def group_map_task(pair):
        lanes = slice(pair * POOL_PAIR, (pair + 1) * POOL_PAIR)

        def group_map():
            both = jnp.concatenate([pooled[2 * pair], pooled[2 * pair + 1]], axis=-1)
            pg = jnp.dot(both, pool_w_ref[pair], preferred_element_type=jnp.float32)
            pg = pg * pool_scale_ref[:, lanes]
            mix_ref[base:base + sub, GLA_WIDTH + pair * POOL_PAIR:GLA_WIDTH + (pair + 1) * POOL_PAIR] = (
                pg.astype(mix_ref.dtype))

        return group_map

    def keep_history():
        carry["hist"] = z_ref[slot, sub - POOL_HIST:sub, COL_U:Z_WIDTH]

    assert sub == 2 * CHUNK
    c0, c1 = chunk_tasks(0), chunk_tasks(1)
    means = [window_mean_task(gi, w) for gi, w in enumerate(POOL_WINDOWS)]
    maps = [group_map_task(pair) for pair in range(len(POOL_WINDOWS) // 2)]
    return [
        c0["prepare"], c1["prepare"], c0["update_product"], c0["query_product"], c1["update_product"],
        c0["update_state"], c1["query_product"], means[0], c0["intra"], c1["update_state"],
        means[1], c1["intra"], maps[0], c0["finish"], means[2], c1["finish"], means[3], maps[1], keep_history,
    ]


def _prompt_mixer_kernel(*refs, chained):
    (x_ref, x_next_ref, n1_ref, w_main_ref, w_code_ref, w_u_ref, w_gate_ref, b_gate_ref, gla_g_ref, pool_w_ref,
     pool_scale_ref) = refs[:11]
    mix_ref, s_fin_ref, hist_out_ref, st_ref, hist_ref, z_ref, b_ref = refs[-7:]
    tile = x_ref.shape[0]
    n_sub = tile // PROMPT_SUB
    t = pl.program_id(1)

    def project_tasks(load_x, slot):
        return _prompt_project_tasks(load_x, slot, n1_ref, w_main_ref, w_code_ref, w_u_ref, w_gate_ref, b_gate_ref,
                                     z_ref, b_ref)

    @pl.when(t == 0)
    def _():
        st_ref[...] = jnp.zeros_like(st_ref)
        hist_ref[...] = jnp.zeros_like(hist_ref)

    @pl.when((t == 0) & (pl.program_id(0) == 0))
    def _():
        for task in project_tasks(lambda: x_ref[0:PROMPT_SUB], 0):
            task()

    carry = {"st": st_ref[...], "hist": hist_ref[...]}
    for sb in range(n_sub):
        if sb + 1 < n_sub:
            load_next = lambda sb=sb: x_ref[(sb + 1) * PROMPT_SUB:(sb + 2) * PROMPT_SUB]
        else:
            load_next = lambda: x_next_ref[...]
        _trace_interleaved(
            project_tasks(load_next, (sb + 1) % 2),
            _prompt_mix_tasks(sb % 2, z_ref, b_ref, carry, t * tile + sb * PROMPT_SUB,
                              gla_g_ref, pool_w_ref, pool_scale_ref, mix_ref, sb * PROMPT_SUB))
    st_ref[...] = carry["st"]
    hist_ref[...] = carry["hist"]

    @pl.when(t == pl.num_programs(1) - 1)
    def _():
        s_fin_ref[...] = carry["st"].T
        hist_out_ref[...] = carry["hist"]


def _prompt_mixer(layer, x, p, stacked):
    B, T, _ = x.shape
    tile = PROMPT_TILE
    n_tiles = T // tile
    n_sub = tile // PROMPT_SUB
    assert n_sub % 2 == 0
    whole = lambda *shape: pl.BlockSpec((None,) + shape, lambda b, t: (layer,) + (0,) * len(shape))
    chained = stacked is not None
    operands = [x, x, p["n1"], p["w_main"], p["w_code"], p["w_u"], p["w_gate"], p["b_gate"], p["gla_g"], p["pool_w"],
                p["pool_scale"]]

    def next_first_sub_block(b, t):
        n = jnp.minimum(b * n_tiles + t + 1, B * n_tiles - 1)
        return (n // n_tiles, (n % n_tiles) * n_sub, 0)

    in_specs = [
        pl.BlockSpec((None, tile, D_MODEL), lambda b, t: (b, t, 0)),
        pl.BlockSpec((None, PROMPT_SUB, D_MODEL), next_first_sub_block),
        whole(1, D_MODEL),
        whole(D_MODEL, COL_U),
        whole(D_MODEL, LANES),
        whole(D_MODEL, POOL_WIDTH),
        whole(LANES, KEY_WIDTH),
        whole(1, KEY_WIDTH),
        whole(1, HEAD_V),
        whole(len(POOL_WINDOWS) // 2, POOL_PAIR, POOL_PAIR),
        whole(1, POOL_WIDTH),
    ]
    aliases = {}
    if chained:
        aliases = {len(operands): 1, len(operands) + 1: 2}
        operands += list(stacked)
        in_specs += [pl.BlockSpec(memory_space=pl.ANY)] * 2
    return pl.pallas_call(
        functools.partial(_prompt_mixer_kernel, chained=chained),
        grid=(B, n_tiles),
        in_specs=in_specs,
        out_specs=[
            pl.BlockSpec((None, tile, D_MODEL), lambda b, t: (b, t, 0)),
            pl.BlockSpec((None, None, KEY_WIDTH, HEAD_V), lambda b, t: (layer, b, 0, 0)),
            pl.BlockSpec((None, None, POOL_HIST, POOL_WIDTH), lambda b, t: (layer, b, 0, 0)),
        ],
        out_shape=[
            jax.ShapeDtypeStruct((B, T, D_MODEL), jnp.bfloat16),
            jax.ShapeDtypeStruct((DEPTH, B, KEY_WIDTH, HEAD_V), jnp.float32),
            jax.ShapeDtypeStruct((DEPTH, B, POOL_HIST, POOL_WIDTH), jnp.float32),
        ],
        scratch_shapes=[
            pltpu.VMEM((HEAD_V, KEY_WIDTH), jnp.float32),
            pltpu.VMEM((POOL_HIST, POOL_WIDTH), jnp.float32),
            pltpu.VMEM((2, PROMPT_SUB, Z_WIDTH), jnp.float32),
            pltpu.VMEM((2, PROMPT_SUB, KEY_WIDTH), jnp.float32),
        ],
        input_output_aliases=aliases,
        compiler_params=pltpu.CompilerParams(
            dimension_semantics=("arbitrary", "arbitrary"), vmem_limit_bytes=VMEM_LIMIT),
        name=f"prompt_mixer_l{layer}",
    )(*operands)


def _sample_mixer_kernel(*refs, chained):
    (x_ref, n1_ref, w_main_ref, w_code_ref, w_u_ref, w_gate_ref, b_gate_ref, gla_g_ref, pool_w_ref,
     pool_scale_ref, s_ref, pool_state_ref) = refs[:12]
    mix_ref, s_new_ref, pool_new_ref, dec_ref, ke_ref, qt_ref, v_ref, g_ref, o_ref = refs[-9:]
    i = pl.program_id(0)
    n_seq = x_ref.shape[0]
    seq_tile = s_ref.shape[0]

    @pl.when(i == 0)
    def _():
        qkvg, u, log_a = _project(x_ref[...], n1_ref, w_main_ref, w_code_ref, w_u_ref, w_gate_ref, b_gate_ref)
        q = qkvg[:, COL_Q:COL_K]
        k = qkvg[:, COL_K:COL_V]
        v = _bf16_round(qkvg[:, COL_V:COL_G])
        qt = _bf16_round(q * jnp.exp(log_a) * (HEAD_K ** -0.5))
        kt = _bf16_round(k * jnp.exp(-log_a))
        ke = _bf16_round(k * jnp.exp(log_a - log_a))
        prod = qt * kt
        o_intra = []
        for h, m in enumerate(_head_lane_mask(n_seq)):
            att = _bf16_round(jnp.sum(jnp.where(m, prod, 0.0), axis=-1, keepdims=True))
            o_intra.append(att * v[:, h * HEAD_V:(h + 1) * HEAD_V])
        o_ref[...] = jnp.concatenate(o_intra, axis=-1)
        dec_ref[...] = jnp.exp(log_a)
        ke_ref[...] = ke
        qt_ref[...] = qt
        v_ref[...] = v
        g_ref[...] = qkvg[:, COL_G:COL_U]

        pooled = []
        for gi, w in enumerate(POOL_WINDOWS):
            lanes = slice(gi * POOL_GROUP, (gi + 1) * POOL_GROUP)
            s = u[:, lanes]
            for j in range(POOL_BUF - (w - 1), POOL_BUF):
                s = s + pool_state_ref[j, :, lanes]
            pooled.append((s / float(w) - u[:, lanes]).astype(jnp.bfloat16))
        for pair in range(len(POOL_WINDOWS) // 2):
            lanes = slice(pair * POOL_PAIR, (pair + 1) * POOL_PAIR)
            both = jnp.concatenate(pooled[2 * pair:2 * pair + 2], axis=-1)
            pg = jnp.dot(both, pool_w_ref[pair], preferred_element_type=jnp.float32) * pool_scale_ref[:, lanes]
            mix_ref[:, GLA_WIDTH + pair * POOL_PAIR:GLA_WIDTH + (pair + 1) * POOL_PAIR] = pg.astype(mix_ref.dtype)
        for j in range(POOL_BUF - 1):
            pool_new_ref[j] = pool_state_ref[j + 1]
        pool_new_ref[POOL_BUF - 1] = u

    lane = lax.broadcasted_iota(jnp.int32, (SUBLANES, KEY_WIDTH), 1)
    head = lax.broadcasted_iota(jnp.int32, (SUBLANES, KEY_WIDTH), 0)
    head_rows = jnp.where(lane // HEAD_K == head, 1.0, 0.0)
    for sub in range(seq_tile // SUBLANES):
        local = slice(sub * SUBLANES, (sub + 1) * SUBLANES)
        rows = pl.ds(pl.multiple_of(i * seq_tile + sub * SUBLANES, SUBLANES), SUBLANES)
        s_old = s_ref[local]
        v = v_ref[rows, :]
        v_rows = jnp.concatenate(
            [jnp.broadcast_to(v[:, None, h * HEAD_V:(h + 1) * HEAD_V], (SUBLANES, HEAD_K, HEAD_V))
             for h in range(N_HEADS)], axis=1)
        s_new_ref[local] = dec_ref[rows, :][:, :, None] * s_old + ke_ref[rows, :][:, :, None] * v_rows
        q_rows = (qt_ref[rows, :][:, None, :] * head_rows[None]).astype(jnp.bfloat16)
        o_inter = jnp.einsum("bhk,bkv->bhv", q_rows, s_old.astype(jnp.bfloat16),
                             preferred_element_type=jnp.float32)
        o_ref[rows, :] = o_ref[rows, :] + jnp.concatenate([o_inter[:, h, :] for h in range(N_HEADS)], axis=-1)

    @pl.when(i == pl.num_programs(0) - 1)
    def _():
        mix_ref[:, 0:GLA_WIDTH] = _gla_finish(o_ref[...], g_ref[...], gla_g_ref[...]).astype(mix_ref.dtype)


def _sample_mixer(layer, x, s_state, pool_state, p, stacked):
    n_seq = x.shape[0]
    seq_tile = SAMPLE_STATE_TILE
    whole = lambda *shape: pl.BlockSpec((None,) + shape, lambda i: (layer,) + (0,) * len(shape))
    fixed = lambda *shape: pl.BlockSpec(shape, lambda i: (0,) * len(shape))
    f32 = jnp.float32
    chained = stacked is not None
    operands = [x, p["n1"], p["w_main"], p["w_code"], p["w_u"], p["w_gate"], p["b_gate"], p["gla_g"], p["pool_w"],
                p["pool_scale"], s_state, pool_state]
    in_specs = [
        fixed(n_seq, D_MODEL),
        whole(1, D_MODEL),
        whole(D_MODEL, COL_U),
        whole(D_MODEL, LANES),
        whole(D_MODEL, POOL_WIDTH),
        whole(LANES, KEY_WIDTH),
        whole(1, KEY_WIDTH),
        whole(1, HEAD_V),
        whole(len(POOL_WINDOWS) // 2, POOL_PAIR, POOL_PAIR),
        whole(1, POOL_WIDTH),
        pl.BlockSpec((None, seq_tile, KEY_WIDTH, HEAD_V), lambda i: (layer, i, 0, 0)),
        whole(POOL_BUF, n_seq, POOL_WIDTH),
    ]
    aliases = {}
    if chained:
        aliases = {len(operands): 1, len(operands) + 1: 2}
        operands += list(stacked)
        in_specs += [pl.BlockSpec(memory_space=pl.ANY)] * 2
    return pl.pallas_call(
        functools.partial(_sample_mixer_kernel, chained=chained),
        grid=(n_seq // seq_tile,),
        in_specs=in_specs,
        out_specs=[
            fixed(n_seq, D_MODEL),
            pl.BlockSpec((None, seq_tile, KEY_WIDTH, HEAD_V), lambda i: (layer, i, 0, 0)),
            whole(POOL_BUF, n_seq, POOL_WIDTH),
        ],
        out_shape=[
            jax.ShapeDtypeStruct((n_seq, D_MODEL), jnp.bfloat16),
            jax.ShapeDtypeStruct((DEPTH, n_seq, KEY_WIDTH, HEAD_V), f32),
            jax.ShapeDtypeStruct((DEPTH, POOL_BUF, n_seq, POOL_WIDTH), f32),
        ],
        scratch_shapes=[
            pltpu.VMEM((n_seq, KEY_WIDTH), f32),
            pltpu.VMEM((n_seq, KEY_WIDTH), f32),
            pltpu.VMEM((n_seq, KEY_WIDTH), f32),
            pltpu.VMEM((n_seq, GLA_WIDTH), f32),
            pltpu.VMEM((n_seq, GLA_WIDTH), f32),
            pltpu.VMEM((n_seq, GLA_WIDTH), f32),
        ],
        input_output_aliases=aliases,
        compiler_params=pltpu.CompilerParams(
            dimension_semantics=("arbitrary",), vmem_limit_bytes=VMEM_LIMIT),
        name=f"sample_mixer_l{layer}",
    )(*operands)


N_FF_CHUNKS = D_FF // FF_CHUNK
N_WEIGHT_CHUNKS = 1 + 2 * N_FF_CHUNKS
assert FF_CHUNK == D_MODEL


def _channel_kernel(x_ref, mix_ref, xs_ref, mix_s_ref, n2_ref, final_g_ref, w_out_hbm, w_up_hbm, w_down_hbm,
                    y_ref, ys_ref, w_out_ref, w_up_ref, w_down_ref, stage_ref, sem, *, layer, final):
    step = pl.program_id(0)

    def chunk(k):
        if k == 0:
            return w_out_hbm.at[layer], w_out_ref
        j, down = divmod(k - 1, 2)
        span = pl.ds(j * FF_CHUNK, FF_CHUNK)
        if down:
            return w_down_hbm.at[layer, span, :], w_down_ref.at[span, :]
        return w_up_hbm.at[layer, :, span], w_up_ref.at[:, span]

    def chunk_copy(k):
        return pltpu.make_async_copy(chunk(k)[0], stage_ref.at[k % 2], sem.at[k % 2])

    def fetch(k):
        chunk_copy(k).wait()
        chunk(k)[1][...] = stage_ref[k % 2].astype(jnp.bfloat16)
        if k + 2 < N_WEIGHT_CHUNKS:
            chunk_copy(k + 2).start()

    def channel_mix(x, mix, fetch_weights):
        if fetch_weights:
            fetch(0)
        x1 = x + jnp.dot(mix, w_out_ref[...], preferred_element_type=jnp.float32)
        h2 = _rmsnorm(x1, n2_ref[...]).astype(jnp.bfloat16)
        acc = x1
        for j in range(N_FF_CHUNKS):
            cols = slice(j * FF_CHUNK, (j + 1) * FF_CHUNK)
            if fetch_weights:
                fetch(1 + 2 * j)
                fetch(2 + 2 * j)
            hid = jnp.dot(h2, w_up_ref[:, cols], preferred_element_type=jnp.float32)
            act = jnp.square(jnp.maximum(hid, 0.0)).astype(jnp.bfloat16)
            acc = acc + jnp.dot(act, w_down_ref[cols, :], preferred_element_type=jnp.float32)
        if final:
            acc = _rmsnorm(acc, final_g_ref[...])
        return acc

    @pl.when(step == 0)
    def _():
        chunk_copy(0).start()
        chunk_copy(1).start()
        y_ref[...] = channel_mix(x_ref[...], mix_ref[...], True)

    @pl.when(step > 0)
    def _():
        y_ref[...] = channel_mix(x_ref[...], mix_ref[...], False)

    @pl.when(step == pl.num_programs(0) - 1)
    def _():
        ys_ref[...] = channel_mix(xs_ref[...], mix_s_ref[...], False)


def _channel_mixer(layer, x, mix, xs, mix_s, p, final_g):
    rows = x.shape[0]
    n_seq = xs.shape[0]
    tile = CHANNEL_TILE
    assert rows // tile > 1
    bf16 = jnp.bfloat16
    return pl.pallas_call(
        functools.partial(_channel_kernel, layer=layer, final=layer == DEPTH - 1),
        grid=(rows // tile,),
        in_specs=[
            pl.BlockSpec((tile, D_MODEL), lambda i: (i, 0)),
            pl.BlockSpec((tile, D_MODEL), lambda i: (i, 0)),
            pl.BlockSpec((n_seq, D_MODEL), lambda i: (0, 0)),
            pl.BlockSpec((n_seq, D_MODEL), lambda i: (0, 0)),
            pl.BlockSpec((None, 1, D_MODEL), lambda i: (layer, 0, 0)),
            pl.BlockSpec((1, D_MODEL), lambda i: (0, 0)),
            pl.BlockSpec(memory_space=pl.ANY),
            pl.BlockSpec(memory_space=pl.ANY),
            pl.BlockSpec(memory_space=pl.ANY),
        ],
        out_specs=[
            pl.BlockSpec((tile, D_MODEL), lambda i: (i, 0)),
            pl.BlockSpec((n_seq, D_MODEL), lambda i: (0, 0)),
        ],
        out_shape=[
            jax.ShapeDtypeStruct((rows, D_MODEL), jnp.float32),
            jax.ShapeDtypeStruct((n_seq, D_MODEL), jnp.float32),
        ],
        scratch_shapes=[
            pltpu.VMEM((D_MODEL, D_MODEL), bf16),
            pltpu.VMEM((D_MODEL, D_FF), bf16),
            pltpu.VMEM((D_FF, D_MODEL), bf16),
            pltpu.VMEM((2, D_MODEL, D_MODEL), jnp.float32),
            pltpu.SemaphoreType.DMA((2,)),
        ],
        compiler_params=pltpu.CompilerParams(
            dimension_semantics=("arbitrary",), vmem_limit_bytes=VMEM_LIMIT),
        name=f"channel_mixer_l{layer}",
    )(x, mix, xs, mix_s, p["n2"], final_g, p["w_out"], p["w_up"], p["w_down"])


def kernel(x_prompt, x_sample, state_gla, state_pool, norm1_g, w_in, w_gate, b_gate, gla_norm_g, pool_w,
           pool_scale, w_out, norm2_g, w_up, w_down, final_g):
    B, T, _ = x_prompt.shape
    n_seq = x_sample.shape[0]
    bf16 = jnp.bfloat16

    w_code = jnp.pad(w_in[:, :, W_IN_GATE:W_IN_U], ((0, 0), (0, 0), (0, LANES - GATE_RANK)))
    pw = pool_w.reshape(DEPTH, len(POOL_WINDOWS) // 2, 2, POOL_GROUP, POOL_GROUP)
    zero = jnp.zeros_like(pw[:, :, 0])
    pool_pairs = jnp.concatenate([jnp.concatenate([pw[:, :, 0], zero], axis=-1),
                                  jnp.concatenate([zero, pw[:, :, 1]], axis=-1)], axis=-2)
    params = {
        "n1": norm1_g.reshape(DEPTH, 1, D_MODEL),
        "w_main": w_in[:, :, :W_IN_GATE].astype(bf16),
        "w_code": w_code.astype(bf16),
        "w_u": w_in[:, :, W_IN_U:].astype(bf16),
        "w_gate": jnp.pad(w_gate, ((0, 0), (0, LANES - GATE_RANK), (0, 0))).astype(bf16),
        "b_gate": b_gate.reshape(DEPTH, 1, KEY_WIDTH),
        "gla_g": gla_norm_g.reshape(DEPTH, 1, HEAD_V),
        "pool_w": pool_pairs.astype(bf16),
        "pool_scale": pool_scale.reshape(DEPTH, 1, POOL_WIDTH),
        "w_out": w_out,
        "n2": norm2_g.reshape(DEPTH, 1, D_MODEL),
        "w_up": w_up,
        "w_down": w_down,
    }
    final_g2 = final_g.reshape(1, D_MODEL)
    s_state = state_gla.reshape(DEPTH, n_seq, KEY_WIDTH, HEAD_V)
    pool_rows = jnp.swapaxes(state_pool, 1, 2)

    xp = x_prompt
    xs = x_sample.reshape(n_seq, D_MODEL)
    prompt_states = sample_states = None
    for layer in range(DEPTH):
        mix_p, *prompt_states = _prompt_mixer(layer, xp, params, prompt_states)
        mix_s, *sample_states = _sample_mixer(layer, xs, s_state, pool_rows, params, sample_states)
        xp, xs = _channel_mixer(layer, xp.reshape(B * T, D_MODEL), mix_p.reshape(B * T, D_MODEL), xs, mix_s,
                                params, final_g2)
        xp = xp.reshape(B, T, D_MODEL)
    gla_p, hist_p = prompt_states
    gla_s, pool_s = sample_states
    return (xp, xs.reshape(n_seq, 1, D_MODEL),
            gla_p.reshape(DEPTH, B, N_HEADS, HEAD_K, HEAD_V), hist_p[:, :, POOL_HIST - POOL_BUF:],
            gla_s.reshape(DEPTH, n_seq, N_HEADS, HEAD_K, HEAD_V), jnp.swapaxes(pool_s, 1, 2))
```

```python
import functools

import jax
import jax.numpy as jnp
from jax import lax
from jax.experimental import pallas as pl
from jax.experimental.pallas import tpu as pltpu

D_MODEL = 1024
DEPTH = 4
N_HEADS = 4
HEAD_K = 64
HEAD_V = 128
KEY_WIDTH = N_HEADS * HEAD_K
GLA_WIDTH = N_HEADS * HEAD_V
GATE_RANK = 16
GATE_TEMP = 16.0
CHUNK = 64
POOL_WIDTH = 512
POOL_WINDOWS = (2, 4, 8, 16)
POOL_GROUP = 128
POOL_PAIR = 2 * POOL_GROUP
POOL_BUF = 15
POOL_HIST = 16
D_FF = 4 * D_MODEL
FF_CHUNK = 1024
EPS = 1e-6
LANES = 128
SUBLANES = 8

COL_Q = 0
COL_K = KEY_WIDTH
COL_V = 2 * KEY_WIDTH
COL_G = COL_V + GLA_WIDTH
COL_U = COL_G + GLA_WIDTH
Z_WIDTH = COL_U + POOL_WIDTH
W_IN_GATE = COL_U
W_IN_U = COL_U + GATE_RANK

PROMPT_TILE = 1024
PROMPT_SUB = 128
PROJ_PIECE = 256
CHANNEL_TILE = 512
SAMPLE_STATE_TILE = 32
VMEM_LIMIT = 56 * 1024 * 1024


def _rmsnorm(x, g):
    return x * lax.rsqrt(jnp.mean(x * x, axis=-1, keepdims=True) + EPS) * g


def _log_sigmoid(x):
    return jnp.minimum(x, 0.0) - jnp.log(1.0 + jnp.exp(-jnp.abs(x)))


def _bf16_round(x):
    return x.astype(jnp.bfloat16).astype(jnp.float32)


def _head_lane_mask(rows):
    lane = lax.broadcasted_iota(jnp.int32, (rows, KEY_WIDTH), 1)
    return [(lane >= h * HEAD_K) & (lane < (h + 1) * HEAD_K) for h in range(N_HEADS)]


def _gate_log_decay(a_low, w_gate_ref, b_gate_ref):
    pre = jnp.dot(a_low.astype(jnp.bfloat16), w_gate_ref[...], preferred_element_type=jnp.float32) + b_gate_ref[...]
    return _log_sigmoid(pre) / GATE_TEMP


def _project(x, n1_ref, w_main_ref, w_code_ref, w_u_ref, w_gate_ref, b_gate_ref):
    h = _rmsnorm(x, n1_ref[...]).astype(jnp.bfloat16)
    qkvg = jnp.dot(h, w_main_ref[...], preferred_element_type=jnp.float32)
    a_low = jnp.dot(h, w_code_ref[...], preferred_element_type=jnp.float32)
    u = jnp.dot(h, w_u_ref[...], preferred_element_type=jnp.float32)
    return qkvg, u, _gate_log_decay(a_low, w_gate_ref, b_gate_ref)


def _gla_finish(o, g, gla_g):
    outs = []
    for h in range(N_HEADS):
        sl = slice(h * HEAD_V, (h + 1) * HEAD_V)
        outs.append(_rmsnorm(o[:, sl], gla_g) * (g[:, sl] * jax.nn.sigmoid(g[:, sl])))
    return jnp.concatenate(outs, axis=-1)


def _trace_interleaved(a, b):
    i = j = 0
    while i < len(a) or j < len(b):
        if j >= len(b) or (i < len(a) and i * len(b) <= j * len(a)):
            a[i]()
            i += 1
        else:
            b[j]()
            j += 1


def _prompt_project_tasks(load_x, slot, n1_ref, w_main_ref, w_code_ref, w_u_ref, w_gate_ref, b_gate_ref,
                          z_ref, b_ref):
    v = {}

    def norm():
        v["h"] = _rmsnorm(load_x(), n1_ref[...]).astype(jnp.bfloat16)

    def piece(lo, hi):
        w_ref, first = (w_main_ref, 0) if hi <= COL_U else (w_u_ref, COL_U)

        def run():
            z_ref[slot, :, lo:hi] = jnp.dot(v["h"], w_ref[:, lo - first:hi - first],
                                            preferred_element_type=jnp.float32)
        return run

    def gate_code():
        v["a_low"] = jnp.dot(v["h"], w_code_ref[...], preferred_element_type=jnp.float32)

    def gate_pre():
        v["pre"] = (jnp.dot(v["a_low"].astype(jnp.bfloat16), w_gate_ref[...], preferred_element_type=jnp.float32)
                    + b_gate_ref[...])

    def log_decay():
        v["b"] = _log_sigmoid(v["pre"]) / GATE_TEMP

    def scan(shifts, last):
        def run():
            b = v["b"]
            row = lax.broadcasted_iota(jnp.int32, b.shape, 0) % CHUNK
            for shift in shifts:
                b = b + jnp.where(row >= shift, pltpu.roll(b, shift, axis=0), 0.0)
            v["b"] = b
            if last:
                b_ref[slot] = b
        return run

    half = PROJ_PIECE
    return [
        norm, gate_code,
        piece(COL_Q, COL_K), gate_pre,
        piece(COL_K, COL_V), log_decay,
        piece(COL_V, COL_V + half), scan((1, 2), False),
        piece(COL_V + half, COL_G), scan((4, 8), False),
        piece(COL_G, COL_G + half), scan((16, 32), True),
        piece(COL_G + half, COL_U), piece(COL_U, COL_U + half), piece(COL_U + half, Z_WIDTH),
    ]


def _prompt_mix_tasks(slot, z_ref, b_ref, carry, first_pos, gla_g_ref, pool_w_ref, pool_scale_ref, mix_ref, base):
    head_mask = _head_lane_mask(CHUNK)
    head_mask_v = _head_lane_mask(HEAD_V)
    ci = lax.broadcasted_iota(jnp.int32, (CHUNK, CHUNK), 0)
    si = lax.broadcasted_iota(jnp.int32, (CHUNK, CHUNK), 1)
    causal = si <= ci
    scale = HEAD_K ** -0.5
    sub = z_ref.shape[1]

    def chunk_tasks(c):
        rows = slice(c * CHUNK, (c + 1) * CHUNK)
        out_rows = slice(base + c * CHUNK, base + (c + 1) * CHUNK)
        v = {}

        def prepare():
            b_c = b_ref[slot, rows, :]
            b_last = b_c[CHUNK - 1:CHUNK]
            q_c = z_ref[slot, rows, COL_Q:COL_K]
            k_c = z_ref[slot, rows, COL_K:COL_V]
            v["v"] = z_ref[slot, rows, COL_V:COL_G].astype(jnp.bfloat16)
            qt = q_c * jnp.exp(b_c) * scale
            v["kt"] = (k_c * jnp.exp(-b_c)).astype(jnp.bfloat16)
            v["ke"] = (k_c * jnp.exp(b_last - b_c)).astype(jnp.bfloat16)
            v["decay"] = jnp.exp(b_last)
            v["q_stack"] = jnp.concatenate([jnp.where(m, qt, 0.0) for m in head_mask], axis=0).astype(jnp.bfloat16)

        def update_product():
            v["p"] = lax.dot_general(v["v"], v["ke"], (((0,), (0,)), ((), ())),
                                     preferred_element_type=jnp.float32)

        def query_product():
            st = carry["st"]
            rhs = jnp.concatenate([st.astype(jnp.bfloat16), v["kt"]], axis=0)
            v["r"] = lax.dot_general(v["q_stack"], rhs, (((1,), (1,)), ((), ())),
                                     preferred_element_type=jnp.float32)

        def update_state():
            p = v["p"]
            upd = jnp.where(head_mask_v[0], p[0:HEAD_V], 0.0)
            for h in range(1, N_HEADS):
                upd = upd + jnp.where(head_mask_v[h], p[h * HEAD_V:(h + 1) * HEAD_V], 0.0)
            carry["st"] = v["decay"] * carry["st"] + upd

        def intra():
            r = v["r"]
            outs = []
            for h in range(N_HEADS):
                hr = slice(h * CHUNK, (h + 1) * CHUNK)
                att = jnp.where(causal, r[hr, HEAD_V:HEAD_V + CHUNK], 0.0).astype(jnp.bfloat16)
                o_intra = jnp.dot(att, v["v"][:, h * HEAD_V:(h + 1) * HEAD_V], preferred_element_type=jnp.float32)
                outs.append(o_intra + r[hr, 0:HEAD_V])
            v["o"] = jnp.concatenate(outs, axis=-1)

        def finish():
            g_c = z_ref[slot, rows, COL_G:COL_U]
            mix_ref[out_rows, 0:GLA_WIDTH] = _gla_finish(v["o"], g_c, gla_g_ref[...]).astype(mix_ref.dtype)

        return dict(prepare=prepare, update_product=update_product, query_product=query_product,
                    update_state=update_state, intra=intra, finish=finish)

    pooled = {}

    def window_mean_task(gi, w):
        lanes = slice(gi * POOL_GROUP, (gi + 1) * POOL_GROUP)

        def window_mean():
            u = z_ref[slot, :, COL_U + gi * POOL_GROUP:COL_U + (gi + 1) * POOL_GROUP]
            s = jnp.concatenate([carry["hist"][:, lanes], u], axis=0)
            shift = 1
            while shift < w:
                s = s + pltpu.roll(s, shift, axis=0)
                shift *= 2
            seen = first_pos + 1 + lax.broadcasted_iota(jnp.int32, (sub, POOL_GROUP), 0)
            if "inv_seen" not in pooled:
                pooled["inv_seen"] = 1.0 / seen.astype(jnp.float32)
            inv_count = jnp.where(seen < w, pooled["inv_seen"], 1.0 / w)
            pooled[gi] = (s[POOL_HIST:] * inv_count - u).astype(jnp.bfloat16)

        return window_mean

    def group_map_task(pair):
        lanes = slice(pair * POOL_PAIR, (pair + 1) * POOL_PAIR)

        def group_map():
            both = jnp.concatenate([pooled[2 * pair], pooled[2 * pair + 1]], axis=-1)
            pg = jnp.dot(both, pool_w_ref[pair], preferred_element_type=jnp.float32)
            pg = pg * pool_scale_ref[:, lanes]
            mix_ref[base:base + sub, GLA_WIDTH + pair * POOL_PAIR:GLA_WIDTH + (pair + 1) * POOL_PAIR] = (
                pg.astype(mix_ref.dtype))

        return group_map

    def keep_history():
        carry["hist"] = z_ref[slot, sub - POOL_HIST:sub, COL_U:Z_WIDTH]

    assert sub == 2 * CHUNK
    c0, c1 = chunk_tasks(0), chunk_tasks(1)
    means = [window_mean_task(gi, w) for gi, w in enumerate(POOL_WINDOWS)]
    maps = [group_map_task(pair) for pair in range(len(POOL_WINDOWS) // 2)]
    return [
        c0["prepare"], c1["prepare"], c0["update_product"], c0["query_product"], c1["update_product"],
        c0["update_state"], c1["query_product"], means[0], c0["intra"], c1["update_state"],
        means[1], c1["intra"], maps[0], c0["finish"], means[2], c1["finish"], means[3], maps[1], keep_history,
    ]


def _prompt_mixer_kernel(*refs, chained):
    (x_ref, x_next_ref, n1_ref, w_main_ref, w_code_ref, w_u_ref, w_gate_ref, b_gate_ref, gla_g_ref, pool_w_ref,
     pool_scale_ref) = refs[:11]
    mix_ref, s_fin_ref, hist_out_ref, st_ref, hist_ref, z_ref, b_ref = refs[-7:]
    tile = x_ref.shape[0]
    n_sub = tile // PROMPT_SUB
    t = pl.program_id(1)

    def project_tasks(load_x, slot):
        return _prompt_project_tasks(load_x, slot, n1_ref, w_main_ref, w_code_ref, w_u_ref, w_gate_ref, b_gate_ref,
                                     z_ref, b_ref)

    @pl.when(t == 0)
    def _():
        st_ref[...] = jnp.zeros_like(st_ref)
        hist_ref[...] = jnp.zeros_like(hist_ref)

    @pl.when((t == 0) & (pl.program_id(0) == 0))
    def _():
        for task in project_tasks(lambda: x_ref[0:PROMPT_SUB], 0):
            task()

    carry = {"st": st_ref[...], "hist": hist_ref[...]}
    for sb in range(n_sub):
        if sb + 1 < n_sub:
            load_next = lambda sb=sb: x_ref[(sb + 1) * PROMPT_SUB:(sb + 2) * PROMPT_SUB]
        else:
            load_next = lambda: x_next_ref[...]
        _trace_interleaved(
            project_tasks(load_next, (sb + 1) % 2),
            _prompt_mix_tasks(sb % 2, z_ref, b_ref, carry, t * tile + sb * PROMPT_SUB,
                              gla_g_ref, pool_w_ref, pool_scale_ref, mix_ref, sb * PROMPT_SUB))
    st_ref[...] = carry["st"]
    hist_ref[...] = carry["hist"]

    @pl.when(t == pl.num_programs(1) - 1)
    def _():
        s_fin_ref[...] = carry["st"].T
        hist_out_ref[...] = carry["hist"]


def _prompt_mixer(layer, x, p, stacked):
    B, T, _ = x.shape
    tile = PROMPT_TILE
    n_tiles = T // tile
    n_sub = tile // PROMPT_SUB
    assert n_sub % 2 == 0
    whole = lambda *shape: pl.BlockSpec((None,) + shape, lambda b, t: (layer,) + (0,) * len(shape))
    chained = stacked is not None
    operands = [x, x, p["n1"], p["w_main"], p["w_code"], p["w_u"], p["w_gate"], p["b_gate"], p["gla_g"], p["pool_w"],
                p["pool_scale"]]

    def next_first_sub_block(b, t):
        n = jnp.minimum(b * n_tiles + t + 1, B * n_tiles - 1)
        return (n // n_tiles, (n % n_tiles) * n_sub, 0)

    in_specs = [
        pl.BlockSpec((None, tile, D_MODEL), lambda b, t: (b, t, 0)),
        pl.BlockSpec((None, PROMPT_SUB, D_MODEL), next_first_sub_block),
        whole(1, D_MODEL),
        whole(D_MODEL, COL_U),
        whole(D_MODEL, LANES),
        whole(D_MODEL, POOL_WIDTH),
        whole(LANES, KEY_WIDTH),
        whole(1, KEY_WIDTH),
        whole(1, HEAD_V),
        whole(len(POOL_WINDOWS) // 2, POOL_PAIR, POOL_PAIR),
        whole(1, POOL_WIDTH),
    ]
    aliases = {}
    if chained:
        aliases = {len(operands): 1, len(operands) + 1: 2}
        operands += list(stacked)
        in_specs += [pl.BlockSpec(memory_space=pl.ANY)] * 2
    return pl.pallas_call(
        functools.partial(_prompt_mixer_kernel, chained=chained),
        grid=(B, n_tiles),
        in_specs=in_specs,
        out_specs=[
            pl.BlockSpec((None, tile, D_MODEL), lambda b, t: (b, t, 0)),
            pl.BlockSpec((None, None, KEY_WIDTH, HEAD_V), lambda b, t: (layer, b, 0, 0)),
            pl.BlockSpec((None, None, POOL_HIST, POOL_WIDTH), lambda b, t: (layer, b, 0, 0)),
        ],
        out_shape=[
            jax.ShapeDtypeStruct((B, T, D_MODEL), jnp.bfloat16),
            jax.ShapeDtypeStruct((DEPTH, B, KEY_WIDTH, HEAD_V), jnp.float32),
            jax.ShapeDtypeStruct((DEPTH, B, POOL_HIST, POOL_WIDTH), jnp.float32),
        ],
        scratch_shapes=[
            pltpu.VMEM((HEAD_V, KEY_WIDTH), jnp.float32),
            pltpu.VMEM((POOL_HIST, POOL_WIDTH), jnp.float32),
            pltpu.VMEM((2, PROMPT_SUB, Z_WIDTH), jnp.float32),
            pltpu.VMEM((2, PROMPT_SUB, KEY_WIDTH), jnp.float32),
        ],
        input_output_aliases=aliases,
        compiler_params=pltpu.CompilerParams(
            dimension_semantics=("arbitrary", "arbitrary"), vmem_limit_bytes=VMEM_LIMIT),
        name=f"prompt_mixer_l{layer}",
    )(*operands)


def _sample_mixer_kernel(*refs, chained):
    (x_ref, n1_ref, w_main_ref, w_code_ref, w_u_ref, w_gate_ref, b_gate_ref, gla_g_ref, pool_w_ref,
     pool_scale_ref, s_ref, pool_state_ref) = refs[:12]
    mix_ref, s_new_ref, pool_new_ref, dec_ref, ke_ref, qt_ref, v_ref, g_ref, o_ref = refs[-9:]
    i = pl.program_id(0)
    n_seq = x_ref.shape[0]
    seq_tile = s_ref.shape[0]

    @pl.when(i == 0)
    def _():
        qkvg, u, log_a = _project(x_ref[...], n1_ref, w_main_ref, w_code_ref, w_u_ref, w_gate_ref, b_gate_ref)
        q = qkvg[:, COL_Q:COL_K]
        k = qkvg[:, COL_K:COL_V]
        v = _bf16_round(qkvg[:, COL_V:COL_G])
        qt = _bf16_round(q * jnp.exp(log_a) * (HEAD_K ** -0.5))
        kt = _bf16_round(k * jnp.exp(-log_a))
        ke = _bf16_round(k * jnp.exp(log_a - log_a))
        prod = qt * kt
        o_intra = []
        for h, m in enumerate(_head_lane_mask(n_seq)):
            att = _bf16_round(jnp.sum(jnp.where(m, prod, 0.0), axis=-1, keepdims=True))
            o_intra.append(att * v[:, h * HEAD_V:(h + 1) * HEAD_V])
        o_ref[...] = jnp.concatenate(o_intra, axis=-1)
        dec_ref[...] = jnp.exp(log_a)
        ke_ref[...] = ke
        qt_ref[...] = qt
        v_ref[...] = v
        g_ref[...] = qkvg[:, COL_G:COL_U]

        pooled = []
        for gi, w in enumerate(POOL_WINDOWS):
            lanes = slice(gi * POOL_GROUP, (gi + 1) * POOL_GROUP)
            s = u[:, lanes]
            for j in range(POOL_BUF - (w - 1), POOL_BUF):
                s = s + pool_state_ref[j, :, lanes]
            pooled.append((s / float(w) - u[:, lanes]).astype(jnp.bfloat16))
        for pair in range(len(POOL_WINDOWS) // 2):
            lanes = slice(pair * POOL_PAIR, (pair + 1) * POOL_PAIR)
            both = jnp.concatenate(pooled[2 * pair:2 * pair + 2], axis=-1)
            pg = jnp.dot(both, pool_w_ref[pair], preferred_element_type=jnp.float32) * pool_scale_ref[:, lanes]
            mix_ref[:, GLA_WIDTH + pair * POOL_PAIR:GLA_WIDTH + (pair + 1) * POOL_PAIR] = pg.astype(mix_ref.dtype)
        for j in range(POOL_BUF - 1):
            pool_new_ref[j] = pool_state_ref[j + 1]
        pool_new_ref[POOL_BUF - 1] = u

    lane = lax.broadcasted_iota(jnp.int32, (SUBLANES, KEY_WIDTH), 1)
    head = lax.broadcasted_iota(jnp.int32, (SUBLANES, KEY_WIDTH), 0)
    head_rows = jnp.where(lane // HEAD_K == head, 1.0, 0.0)
    for sub in range(seq_tile // SUBLANES):
        local = slice(sub * SUBLANES, (sub + 1) * SUBLANES)
        rows = pl.ds(pl.multiple_of(i * seq_tile + sub * SUBLANES, SUBLANES), SUBLANES)
        s_old = s_ref[local]
        v = v_ref[rows, :]
        v_rows = jnp.concatenate(
            [jnp.broadcast_to(v[:, None, h * HEAD_V:(h + 1) * HEAD_V], (SUBLANES, HEAD_K, HEAD_V))
             for h in range(N_HEADS)], axis=1)
        s_new_ref[local] = dec_ref[rows, :][:, :, None] * s_old + ke_ref[rows, :][:, :, None] * v_rows
        q_rows = (qt_ref[rows, :][:, None, :] * head_rows[None]).astype(jnp.bfloat16)
        o_inter = jnp.einsum("bhk,bkv->bhv", q_rows, s_old.astype(jnp.bfloat16),
                             preferred_element_type=jnp.float32)
        o_ref[rows, :] = o_ref[rows, :] + jnp.concatenate([o_inter[:, h, :] for h in range(N_HEADS)], axis=-1)

    @pl.when(i == pl.num_programs(0) - 1)
    def _():
        mix_ref[:, 0:GLA_WIDTH] = _gla_finish(o_ref[...], g_ref[...], gla_g_ref[...]).astype(mix_ref.dtype)


def _sample_mixer(layer, x, s_state, pool_state, p, stacked):
    n_seq = x.shape[0]
    seq_tile = SAMPLE_STATE_TILE
    whole = lambda *shape: pl.BlockSpec((None,) + shape, lambda i: (layer,) + (0,) * len(shape))
    fixed = lambda *shape: pl.BlockSpec(shape, lambda i: (0,) * len(shape))
    f32 = jnp.float32
    chained = stacked is not None
    operands = [x, p["n1"], p["w_main"], p["w_code"], p["w_u"], p["w_gate"], p["b_gate"], p["gla_g"], p["pool_w"],
                p["pool_scale"], s_state, pool_state]
    in_specs = [
        fixed(n_seq, D_MODEL),
        whole(1, D_MODEL),
        whole(D_MODEL, COL_U),
        whole(D_MODEL, LANES),
        whole(D_MODEL, POOL_WIDTH),
        whole(LANES, KEY_WIDTH),
        whole(1, KEY_WIDTH),
        whole(1, HEAD_V),
        whole(len(POOL_WINDOWS) // 2, POOL_PAIR, POOL_PAIR),
        whole(1, POOL_WIDTH),
        pl.BlockSpec((None, seq_tile, KEY_WIDTH, HEAD_V), lambda i: (layer, i, 0, 0)),
        whole(POOL_BUF, n_seq, POOL_WIDTH),
    ]
    aliases = {}
    if chained:
        aliases = {len(operands): 1, len(operands) + 1: 2}
        operands += list(stacked)
        in_specs += [pl.BlockSpec(memory_space=pl.ANY)] * 2
    return pl.pallas_call(
        functools.partial(_sample_mixer_kernel, chained=chained),
        grid=(n_seq // seq_tile,),
        in_specs=in_specs,
        out_specs=[
            fixed(n_seq, D_MODEL),
            pl.BlockSpec((None, seq_tile, KEY_WIDTH, HEAD_V), lambda i: (layer, i, 0, 0)),
            whole(POOL_BUF, n_seq, POOL_WIDTH),
        ],
        out_shape=[
            jax.ShapeDtypeStruct((n_seq, D_MODEL), jnp.bfloat16),
            jax.ShapeDtypeStruct((DEPTH, n_seq, KEY_WIDTH, HEAD_V), f32),
            jax.ShapeDtypeStruct((DEPTH, POOL_BUF, n_seq, POOL_WIDTH), f32),
        ],
        scratch_shapes=[
            pltpu.VMEM((n_seq, KEY_WIDTH), f32),
            pltpu.VMEM((n_seq, KEY_WIDTH), f32),
            pltpu.VMEM((n_seq, KEY_WIDTH), f32),
            pltpu.VMEM((n_seq, GLA_WIDTH), f32),
            pltpu.VMEM((n_seq, GLA_WIDTH), f32),
            pltpu.VMEM((n_seq, GLA_WIDTH), f32),
        ],
        input_output_aliases=aliases,
        compiler_params=pltpu.CompilerParams(
            dimension_semantics=("arbitrary",), vmem_limit_bytes=VMEM_LIMIT),
        name=f"sample_mixer_l{layer}",
    )(*operands)


N_FF_CHUNKS = D_FF // FF_CHUNK
N_WEIGHT_CHUNKS = 1 + 2 * N_FF_CHUNKS
assert FF_CHUNK == D_MODEL


def _channel_kernel(x_ref, mix_ref, xs_ref, mix_s_ref, n2_ref, final_g_ref, w_out_hbm, w_up_hbm, w_down_hbm,
                    y_ref, ys_ref, w_out_ref, w_up_ref, w_down_ref, stage_ref, sem, *, layer, final):
    step = pl.program_id(0)

    def chunk(k):
        if k == 0:
            return w_out_hbm.at[layer], w_out_ref
        j, down = divmod(k - 1, 2)
        span = pl.ds(j * FF_CHUNK, FF_CHUNK)
        if down:
            return w_down_hbm.at[layer, span, :], w_down_ref.at[span, :]
        return w_up_hbm.at[layer, :, span], w_up_ref.at[:, span]

    def chunk_copy(k):
        return pltpu.make_async_copy(chunk(k)[0], stage_ref.at[k % 2], sem.at[k % 2])

    def fetch(k):
        chunk_copy(k).wait()
        chunk(k)[1][...] = stage_ref[k % 2].astype(jnp.bfloat16)
        if k + 2 < N_WEIGHT_CHUNKS:
            chunk_copy(k + 2).start()

    def channel_mix(x, mix, fetch_weights):
        def before_matmul(m):
            if fetch_weights and m + 1 < N_WEIGHT_CHUNKS:
                fetch(m + 1)

        if fetch_weights:
            fetch(0)
        before_matmul(0)
        x1 = x + jnp.dot(mix, w_out_ref[...], preferred_element_type=jnp.float32)
        h2 = _rmsnorm(x1, n2_ref[...]).astype(jnp.bfloat16)
        acc = x1
        for j in range(N_FF_CHUNKS):
            cols = slice(j * FF_CHUNK, (j + 1) * FF_CHUNK)
            before_matmul(1 + 2 * j)
            hid = jnp.dot(h2, w_up_ref[:, cols], preferred_element_type=jnp.float32)
            act = jnp.square(jnp.maximum(hid, 0.0)).astype(jnp.bfloat16)
            before_matmul(2 + 2 * j)
            acc = acc + jnp.dot(act, w_down_ref[cols, :], preferred_element_type=jnp.float32)
        if final:
            acc = _rmsnorm(acc, final_g_ref[...])
        return acc

    @pl.when(step == 0)
    def _():
        chunk_copy(0).start()
        chunk_copy(1).start()
        y_ref[...] = channel_mix(x_ref[...], mix_ref[...], True)

    @pl.when(step > 0)
    def _():
        y_ref[...] = channel_mix(x_ref[...], mix_ref[...], False)

    @pl.when(step == pl.num_programs(0) - 1)
    def _():
        ys_ref[...] = channel_mix(xs_ref[...], mix_s_ref[...], False)


def _channel_mixer(layer, x, mix, xs, mix_s, p, final_g):
    rows = x.shape[0]
    n_seq = xs.shape[0]
    tile = CHANNEL_TILE
    assert rows // tile > 1
    bf16 = jnp.bfloat16
    return pl.pallas_call(
        functools.partial(_channel_kernel, layer=layer, final=layer == DEPTH - 1),
        grid=(rows // tile,),
        in_specs=[
            pl.BlockSpec((tile, D_MODEL), lambda i: (i, 0)),
            pl.BlockSpec((tile, D_MODEL), lambda i: (i, 0)),
            pl.BlockSpec((n_seq, D_MODEL), lambda i: (0, 0)),
            pl.BlockSpec((n_seq, D_MODEL), lambda i: (0, 0)),
            pl.BlockSpec((None, 1, D_MODEL), lambda i: (layer, 0, 0)),
            pl.BlockSpec((1, D_MODEL), lambda i: (0, 0)),
            pl.BlockSpec(memory_space=pl.ANY),
            pl.BlockSpec(memory_space=pl.ANY),
            pl.BlockSpec(memory_space=pl.ANY),
        ],
        out_specs=[
            pl.BlockSpec((tile, D_MODEL), lambda i: (i, 0)),
            pl.BlockSpec((n_seq, D_MODEL), lambda i: (0, 0)),
        ],
        out_shape=[
            jax.ShapeDtypeStruct((rows, D_MODEL), jnp.float32),
            jax.ShapeDtypeStruct((n_seq, D_MODEL), jnp.float32),
        ],
        scratch_shapes=[
            pltpu.VMEM((D_MODEL, D_MODEL), bf16),
            pltpu.VMEM((D_MODEL, D_FF), bf16),
            pltpu.VMEM((D_FF, D_MODEL), bf16),
            pltpu.VMEM((2, D_MODEL, D_MODEL), jnp.float32),
            pltpu.SemaphoreType.DMA((2,)),
        ],
        compiler_params=pltpu.CompilerParams(
            dimension_semantics=("arbitrary",), vmem_limit_bytes=VMEM_LIMIT),
        name=f"channel_mixer_l{layer}",
    )(x, mix, xs, mix_s, p["n2"], final_g, p["w_out"], p["w_up"], p["w_down"])


def kernel(x_prompt, x_sample, state_gla, state_pool, norm1_g, w_in, w_gate, b_gate, gla_norm_g, pool_w,
           pool_scale, w_out, norm2_g, w_up, w_down, final_g):
    B, T, _ = x_prompt.shape
    n_seq = x_sample.shape[0]
    bf16 = jnp.bfloat16

    w_code = jnp.pad(w_in[:, :, W_IN_GATE:W_IN_U], ((0, 0), (0, 0), (0, LANES - GATE_RANK)))
    pw = pool_w.reshape(DEPTH, len(POOL_WINDOWS) // 2, 2, POOL_GROUP, POOL_GROUP)
    zero = jnp.zeros_like(pw[:, :, 0])
    pool_pairs = jnp.concatenate([jnp.concatenate([pw[:, :, 0], zero], axis=-1),
                                  jnp.concatenate([zero, pw[:, :, 1]], axis=-1)], axis=-2)
    params = {
        "n1": norm1_g.reshape(DEPTH, 1, D_MODEL),
        "w_main": w_in[:, :, :W_IN_GATE].astype(bf16),
        "w_code": w_code.astype(bf16),
        "w_u": w_in[:, :, W_IN_U:].astype(bf16),
        "w_gate": jnp.pad(w_gate, ((0, 0), (0, LANES - GATE_RANK), (0, 0))).astype(bf16),
        "b_gate": b_gate.reshape(DEPTH, 1, KEY_WIDTH),
        "gla_g": gla_norm_g.reshape(DEPTH, 1, HEAD_V),
        "pool_w": pool_pairs.astype(bf16),
        "pool_scale": pool_scale.reshape(DEPTH, 1, POOL_WIDTH),
        "w_out": w_out,
        "n2": norm2_g.reshape(DEPTH, 1, D_MODEL),
        "w_up": w_up,
        "w_down": w_down,
    }
    final_g2 = final_g.reshape(1, D_MODEL)
    s_state = state_gla.reshape(DEPTH, n_seq, KEY_WIDTH, HEAD_V)
    pool_rows = jnp.swapaxes(state_pool, 1, 2)

    xp = x_prompt
    xs = x_sample.reshape(n_seq, D_MODEL)
    prompt_states = sample_states = None
    for layer in range(DEPTH):
        mix_p, *prompt_states = _prompt_mixer(layer, xp, params, prompt_states)
        mix_s, *sample_states = _sample_mixer(layer, xs, s_state, pool_rows, params, sample_states)
        xp, xs = _channel_mixer(layer, xp.reshape(B * T, D_MODEL), mix_p.reshape(B * T, D_MODEL), xs, mix_s,
                                params, final_g2)
        xp = xp.reshape(B, T, D_MODEL)
    gla_p, hist_p = prompt_states
    gla_s, pool_s = sample_states
    return (xp, xs.reshape(n_seq, 1, D_MODEL),
            gla_p.reshape(DEPTH, B, N_HEADS, HEAD_K, HEAD_V), hist_p[:, :, POOL_HIST - POOL_BUF:],
            gla_s.reshape(DEPTH, n_seq, N_HEADS, HEAD_K, HEAD_V), jnp.swapaxes(pool_s, 1, 2))
```

```python
import functools

import jax
import jax.numpy as jnp
from jax import lax
from jax.experimental import pallas as pl
from jax.experimental.pallas import tpu as pltpu

D_MODEL = 1024
DEPTH = 4
N_HEADS = 4
HEAD_K = 64
HEAD_V = 128
KEY_WIDTH = N_HEADS * HEAD_K
GLA_WIDTH = N_HEADS * HEAD_V
GATE_RANK = 16
GATE_TEMP = 16.0
CHUNK = 64
POOL_WIDTH = 512
POOL_WINDOWS = (2, 4, 8, 16)
POOL_GROUP = 128
POOL_PAIR = 2 * POOL_GROUP
POOL_BUF = 15
POOL_HIST = 16
D_FF = 4 * D_MODEL
FF_CHUNK = 1024
EPS = 1e-6
LANES = 128
SUBLANES = 8

COL_Q = 0
COL_K = KEY_WIDTH
COL_V = 2 * KEY_WIDTH
COL_G = COL_V + GLA_WIDTH
COL_U = COL_G + GLA_WIDTH
Z_WIDTH = COL_U + POOL_WIDTH
W_IN_GATE = COL_U
W_IN_U = COL_U + GATE_RANK

PROMPT_TILE = 1024
PROMPT_SUB = 128
PROJ_PIECE = 256
CHANNEL_TILE = 1024
CHANNEL_ROWS = 512
SAMPLE_STATE_TILE = 32
VMEM_LIMIT = 56 * 1024 * 1024
CHANNEL_VMEM_LIMIT = 60 * 1024 * 1024


def _rmsnorm(x, g):
    return x * lax.rsqrt(jnp.mean(x * x, axis=-1, keepdims=True) + EPS) * g


def _log_sigmoid(x):
    return jnp.minimum(x, 0.0) - jnp.log(1.0 + jnp.exp(-jnp.abs(x)))


def _bf16_round(x):
    return x.astype(jnp.bfloat16).astype(jnp.float32)


def _head_lane_mask(rows):
    lane = lax.broadcasted_iota(jnp.int32, (rows, KEY_WIDTH), 1)
    return [(lane >= h * HEAD_K) & (lane < (h + 1) * HEAD_K) for h in range(N_HEADS)]


def _gate_log_decay(a_low, w_gate_ref, b_gate_ref):
    pre = jnp.dot(a_low.astype(jnp.bfloat16), w_gate_ref[...], preferred_element_type=jnp.float32) + b_gate_ref[...]
    return _log_sigmoid(pre) / GATE_TEMP


def _project(x, n1_ref, w_main_ref, w_code_ref, w_u_ref, w_gate_ref, b_gate_ref):
    h = _rmsnorm(x, n1_ref[...]).astype(jnp.bfloat16)
    qkvg = jnp.dot(h, w_main_ref[...], preferred_element_type=jnp.float32)
    a_low = jnp.dot(h, w_code_ref[...], preferred_element_type=jnp.float32)
    u = jnp.dot(h, w_u_ref[...], preferred_element_type=jnp.float32)
    return qkvg, u, _gate_log_decay(a_low, w_gate_ref, b_gate_ref)


def _gla_finish(o, g, gla_g):
    outs = []
    for h in range(N_HEADS):
        sl = slice(h * HEAD_V, (h + 1) * HEAD_V)
        outs.append(_rmsnorm(o[:, sl], gla_g) * (g[:, sl] * jax.nn.sigmoid(g[:, sl])))
    return jnp.concatenate(outs, axis=-1)


def _trace_interleaved(a, b):
    i = j = 0
    while i < len(a) or j < len(b):
        if j >= len(b) or (i < len(a) and i * len(b) <= j * len(a)):
            a[i]()
            i += 1
        else:
            b[j]()
            j += 1


def _prompt_project_tasks(load_x, slot, n1_ref, w_main_ref, w_code_ref, w_u_ref, w_gate_ref, b_gate_ref,
                          z_ref, b_ref):
    v = {}

    def norm():
        v["h"] = _rmsnorm(load_x(), n1_ref[...]).astype(jnp.bfloat16)

    def piece(lo, hi):
        w_ref, first = (w_main_ref, 0) if hi <= COL_U else (w_u_ref, COL_U)

        def run():
            z_ref[slot, :, lo:hi] = jnp.dot(v["h"], w_ref[:, lo - first:hi - first],
                                            preferred_element_type=jnp.float32)
        return run

    def gate_code():
        v["a_low"] = jnp.dot(v["h"], w_code_ref[...], preferred_element_type=jnp.float32)

    def gate_pre():
        v["pre"] = (jnp.dot(v["a_low"].astype(jnp.bfloat16), w_gate_ref[...], preferred_element_type=jnp.float32)
                    + b_gate_ref[...])

    def log_decay():
        v["b"] = _log_sigmoid(v["pre"]) / GATE_TEMP

    def scan(shifts, last):
        def run():
            b = v["b"]
            row = lax.broadcasted_iota(jnp.int32, b.shape, 0) % CHUNK
            for shift in shifts:
                b = b + jnp.where(row >= shift, pltpu.roll(b, shift, axis=0), 0.0)
            v["b"] = b
            if last:
                b_ref[slot] = b
        return run

    half = PROJ_PIECE
    return [
        norm, gate_code,
        piece(COL_Q, COL_K), gate_pre,
        piece(COL_K, COL_V), log_decay,
        piece(COL_V, COL_V + half), scan((1, 2), False),
        piece(COL_V + half, COL_G), scan((4, 8), False),
        piece(COL_G, COL_G + half), scan((16, 32), True),
        piece(COL_G + half, COL_U), piece(COL_U, COL_U + half), piece(COL_U + half, Z_WIDTH),
    ]


def _prompt_mix_tasks(slot, z_ref, b_ref, carry, first_pos, gla_g_ref, pool_w_ref, pool_scale_ref, mix_ref, base):
    head_mask = _head_lane_mask(CHUNK)
    head_mask_v = _head_lane_mask(HEAD_V)
    ci = lax.broadcasted_iota(jnp.int32, (CHUNK, CHUNK), 0)
    si = lax.broadcasted_iota(jnp.int32, (CHUNK, CHUNK), 1)
    causal = si <= ci
    scale = HEAD_K ** -0.5
    sub = z_ref.shape[1]

    def chunk_tasks(c):
        rows = slice(c * CHUNK, (c + 1) * CHUNK)
        out_rows = slice(base + c * CHUNK, base + (c + 1) * CHUNK)
        v = {}

        def prepare():
            b_c = b_ref[slot, rows, :]
            b_last = b_c[CHUNK - 1:CHUNK]
            q_c = z_ref[slot, rows, COL_Q:COL_K]
            k_c = z_ref[slot, rows, COL_K:COL_V]
            v["v"] = z_ref[slot, rows, COL_V:COL_G].astype(jnp.bfloat16)
            qt = q_c * jnp.exp(b_c) * scale
            v["kt"] = (k_c * jnp.exp(-b_c)).astype(jnp.bfloat16)
            v["ke"] = (k_c * jnp.exp(b_last - b_c)).astype(jnp.bfloat16)
            v["decay"] = jnp.exp(b_last)
            v["q_stack"] = jnp.concatenate([jnp.where(m, qt, 0.0) for m in head_mask], axis=0).astype(jnp.bfloat16)

        def update_product():
            v["p"] = lax.dot_general(v["v"], v["ke"], (((0,), (0,)), ((), ())),
                                     preferred_element_type=jnp.float32)

        def query_product():
            st = carry["st"]
            rhs = jnp.concatenate([st.astype(jnp.bfloat16), v["kt"]], axis=0)
            v["r"] = lax.dot_general(v["q_stack"], rhs, (((1,), (1,)), ((), ())),
                                     preferred_element_type=jnp.float32)

        def update_state():
            p = v["p"]
            upd = jnp.where(head_mask_v[0], p[0:HEAD_V], 0.0)
            for h in range(1, N_HEADS):
                upd = upd + jnp.where(head_mask_v[h], p[h * HEAD_V:(h + 1) * HEAD_V], 0.0)
            carry["st"] = v["decay"] * carry["st"] + upd

        def intra():
            r = v["r"]
            outs = []
            for h in range(N_HEADS):
                hr = slice(h * CHUNK, (h + 1) * CHUNK)
                att = jnp.where(causal, r[hr, HEAD_V:HEAD_V + CHUNK], 0.0).astype(jnp.bfloat16)
                o_intra = jnp.dot(att, v["v"][:, h * HEAD_V:(h + 1) * HEAD_V], preferred_element_type=jnp.float32)
                outs.append(o_intra + r[hr, 0:HEAD_V])
            v["o"] = jnp.concatenate(outs, axis=-1)

        def finish():
            g_c = z_ref[slot, rows, COL_G:COL_U]
            mix_ref[out_rows, 0:GLA_WIDTH] = _gla_finish(v["o"], g_c, gla_g_ref[...]).astype(mix_ref.dtype)

        return dict(prepare=prepare, update_product=update_product, query_product=query_product,
                    update_state=update_state, intra=intra, finish=finish)

    pooled = {}

    def window_mean_task(gi, w):
        lanes = slice(gi * POOL_GROUP, (gi + 1) * POOL_GROUP)

        def window_mean():
            u = z_ref[slot, :, COL_U + gi * POOL_GROUP:COL_U + (gi + 1) * POOL_GROUP]
            s = jnp.concatenate([carry["hist"][:, lanes], u], axis=0)
            shift = 1
            while shift < w:
                s = s + pltpu.roll(s, shift, axis=0)
                shift *= 2
            seen = first_pos + 1 + lax.broadcasted_iota(jnp.int32, (sub, POOL_GROUP), 0)
            if "inv_seen" not in pooled:
                pooled["inv_seen"] = 1.0 / seen.astype(jnp.float32)
            inv_count = jnp.where(seen < w, pooled["inv_seen"], 1.0 / w)
            pooled[gi] = (s[POOL_HIST:] * inv_count - u).astype(jnp.bfloat16)

        return window_mean

    def group_map_task(pair):
        lanes = slice(pair * POOL_PAIR, (pair + 1) * POOL_PAIR)

        def group_map():
            both = jnp.concatenate([pooled[2 * pair], pooled[2 * pair + 1]], axis=-1)
            pg = jnp.dot(both, pool_w_ref[pair], preferred_element_type=jnp.float32)
            pg = pg * pool_scale_ref[:, lanes]
            mix_ref[base:base + sub, GLA_WIDTH + pair * POOL_PAIR:GLA_WIDTH + (pair + 1) * POOL_PAIR] = (
                pg.astype(mix_ref.dtype))

        return group_map

    def keep_history():
        carry["hist"] = z_ref[slot, sub - POOL_HIST:sub, COL_U:Z_WIDTH]

    assert sub == 2 * CHUNK
    c0, c1 = chunk_tasks(0), chunk_tasks(1)
    means = [window_mean_task(gi, w) for gi, w in enumerate(POOL_WINDOWS)]
    maps = [group_map_task(pair) for pair in range(len(POOL_WINDOWS) // 2)]
    return [
        c0["prepare"], c1["prepare"], c0["update_product"], c0["query_product"], c1["update_product"],
        c0["update_state"], c1["query_product"], means[0], c0["intra"], c1["update_state"],
        means[1], c1["intra"], maps[0], c0["finish"], means[2], c1["finish"], means[3], maps[1], keep_history,
    ]


def _prompt_mixer_kernel(*refs, chained):
    (x_ref, x_next_ref, n1_ref, w_main_ref, w_code_ref, w_u_ref, w_gate_ref, b_gate_ref, gla_g_ref, pool_w_ref,
     pool_scale_ref) = refs[:11]
    mix_ref, s_fin_ref, hist_out_ref, st_ref, hist_ref, z_ref, b_ref = refs[-7:]
    tile = x_ref.shape[0]
    n_sub = tile // PROMPT_SUB
    t = pl.program_id(1)

    def project_tasks(load_x, slot):
        return _prompt_project_tasks(load_x, slot, n1_ref, w_main_ref, w_code_ref, w_u_ref, w_gate_ref, b_gate_ref,
                                     z_ref, b_ref)

    @pl.when(t == 0)
    def _():
        st_ref[...] = jnp.zeros_like(st_ref)
        hist_ref[...] = jnp.zeros_like(hist_ref)

    @pl.when((t == 0) & (pl.program_id(0) == 0))
    def _():
        for task in project_tasks(lambda: x_ref[0:PROMPT_SUB], 0):
            task()

    carry = {"st": st_ref[...], "hist": hist_ref[...]}
    for sb in range(n_sub):
        if sb + 1 < n_sub:
            load_next = lambda sb=sb: x_ref[(sb + 1) * PROMPT_SUB:(sb + 2) * PROMPT_SUB]
        else:
            load_next = lambda: x_next_ref[...]
        _trace_interleaved(
            project_tasks(load_next, (sb + 1) % 2),
            _prompt_mix_tasks(sb % 2, z_ref, b_ref, carry, t * tile + sb * PROMPT_SUB,
                              gla_g_ref, pool_w_ref, pool_scale_ref, mix_ref, sb * PROMPT_SUB))
    st_ref[...] = carry["st"]
    hist_ref[...] = carry["hist"]

    @pl.when(t == pl.num_programs(1) - 1)
    def _():
        s_fin_ref[...] = carry["st"].T
        hist_out_ref[...] = carry["hist"]


def _prompt_mixer(layer, x, p, stacked):
    B, T, _ = x.shape
    tile = PROMPT_TILE
    n_tiles = T // tile
    n_sub = tile // PROMPT_SUB
    assert n_sub % 2 == 0
    whole = lambda *shape: pl.BlockSpec((None,) + shape, lambda b, t: (layer,) + (0,) * len(shape))
    chained = stacked is not None
    operands = [x, x, p["n1"], p["w_main"], p["w_code"], p["w_u"], p["w_gate"], p["b_gate"], p["gla_g"], p["pool_w"],
                p["pool_scale"]]

    def next_first_sub_block(b, t):
        n = jnp.minimum(b * n_tiles + t + 1, B * n_tiles - 1)
        return (n // n_tiles, (n % n_tiles) * n_sub, 0)

    in_specs = [
        pl.BlockSpec((None, tile, D_MODEL), lambda b, t: (b, t, 0)),
        pl.BlockSpec((None, PROMPT_SUB, D_MODEL), next_first_sub_block),
        whole(1, D_MODEL),
        whole(D_MODEL, COL_U),
        whole(D_MODEL, LANES),
        whole(D_MODEL, POOL_WIDTH),
        whole(LANES, KEY_WIDTH),
        whole(1, KEY_WIDTH),
        whole(1, HEAD_V),
        whole(len(POOL_WINDOWS) // 2, POOL_PAIR, POOL_PAIR),
        whole(1, POOL_WIDTH),
    ]
    aliases = {}
    if chained:
        aliases = {len(operands): 1, len(operands) + 1: 2}
        operands += list(stacked)
        in_specs += [pl.BlockSpec(memory_space=pl.ANY)] * 2
    return pl.pallas_call(
        functools.partial(_prompt_mixer_kernel, chained=chained),
        grid=(B, n_tiles),
        in_specs=in_specs,
        out_specs=[
            pl.BlockSpec((None, tile, D_MODEL), lambda b, t: (b, t, 0)),
            pl.BlockSpec((None, None, KEY_WIDTH, HEAD_V), lambda b, t: (layer, b, 0, 0)),
            pl.BlockSpec((None, None, POOL_HIST, POOL_WIDTH), lambda b, t: (layer, b, 0, 0)),
        ],
        out_shape=[
            jax.ShapeDtypeStruct((B, T, D_MODEL), jnp.bfloat16),
            jax.ShapeDtypeStruct((DEPTH, B, KEY_WIDTH, HEAD_V), jnp.float32),
            jax.ShapeDtypeStruct((DEPTH, B, POOL_HIST, POOL_WIDTH), jnp.float32),
        ],
        scratch_shapes=[
            pltpu.VMEM((HEAD_V, KEY_WIDTH), jnp.float32),
            pltpu.VMEM((POOL_HIST, POOL_WIDTH), jnp.float32),
            pltpu.VMEM((2, PROMPT_SUB, Z_WIDTH), jnp.float32),
            pltpu.VMEM((2, PROMPT_SUB, KEY_WIDTH), jnp.float32),
        ],
        input_output_aliases=aliases,
        compiler_params=pltpu.CompilerParams(
            dimension_semantics=("arbitrary", "arbitrary"), vmem_limit_bytes=VMEM_LIMIT),
        name=f"prompt_mixer_l{layer}",
    )(*operands)


def _sample_mixer_kernel(*refs, chained):
    (x_ref, n1_ref, w_main_ref, w_code_ref, w_u_ref, w_gate_ref, b_gate_ref, gla_g_ref, pool_w_ref,
     pool_scale_ref, s_ref, pool_state_ref) = refs[:12]
    mix_ref, s_new_ref, pool_new_ref, dec_ref, ke_ref, qt_ref, v_ref, g_ref, o_ref = refs[-9:]
    i = pl.program_id(0)
    n_seq = x_ref.shape[0]
    seq_tile = s_ref.shape[0]

    @pl.when(i == 0)
    def _():
        qkvg, u, log_a = _project(x_ref[...], n1_ref, w_main_ref, w_code_ref, w_u_ref, w_gate_ref, b_gate_ref)
        q = qkvg[:, COL_Q:COL_K]
        k = qkvg[:, COL_K:COL_V]
        v = _bf16_round(qkvg[:, COL_V:COL_G])
        qt = _bf16_round(q * jnp.exp(log_a) * (HEAD_K ** -0.5))
        kt = _bf16_round(k * jnp.exp(-log_a))
        ke = _bf16_round(k * jnp.exp(log_a - log_a))
        prod = qt * kt
        o_intra = []
        for h, m in enumerate(_head_lane_mask(n_seq)):
            att = _bf16_round(jnp.sum(jnp.where(m, prod, 0.0), axis=-1, keepdims=True))
            o_intra.append(att * v[:, h * HEAD_V:(h + 1) * HEAD_V])
        o_ref[...] = jnp.concatenate(o_intra, axis=-1)
        dec_ref[...] = jnp.exp(log_a)
        ke_ref[...] = ke
        qt_ref[...] = qt
        v_ref[...] = v
        g_ref[...] = qkvg[:, COL_G:COL_U]

        pooled = []
        for gi, w in enumerate(POOL_WINDOWS):
            lanes = slice(gi * POOL_GROUP, (gi + 1) * POOL_GROUP)
            s = u[:, lanes]
            for j in range(POOL_BUF - (w - 1), POOL_BUF):
                s = s + pool_state_ref[j, :, lanes]
            pooled.append((s / float(w) - u[:, lanes]).astype(jnp.bfloat16))
        for pair in range(len(POOL_WINDOWS) // 2):
            lanes = slice(pair * POOL_PAIR, (pair + 1) * POOL_PAIR)
            both = jnp.concatenate(pooled[2 * pair:2 * pair + 2], axis=-1)
            pg = jnp.dot(both, pool_w_ref[pair], preferred_element_type=jnp.float32) * pool_scale_ref[:, lanes]
            mix_ref[:, GLA_WIDTH + pair * POOL_PAIR:GLA_WIDTH + (pair + 1) * POOL_PAIR] = pg.astype(mix_ref.dtype)
        for j in range(POOL_BUF - 1):
            pool_new_ref[j] = pool_state_ref[j + 1]
        pool_new_ref[POOL_BUF - 1] = u

    lane = lax.broadcasted_iota(jnp.int32, (SUBLANES, KEY_WIDTH), 1)
    head = lax.broadcasted_iota(jnp.int32, (SUBLANES, KEY_WIDTH), 0)
    head_rows = jnp.where(lane // HEAD_K == head, 1.0, 0.0)
    for sub in range(seq_tile // SUBLANES):
        local = slice(sub * SUBLANES, (sub + 1) * SUBLANES)
        rows = pl.ds(pl.multiple_of(i * seq_tile + sub * SUBLANES, SUBLANES), SUBLANES)
        s_old = s_ref[local]
        v = v_ref[rows, :]
        v_rows = jnp.concatenate(
            [jnp.broadcast_to(v[:, None, h * HEAD_V:(h + 1) * HEAD_V], (SUBLANES, HEAD_K, HEAD_V))
             for h in range(N_HEADS)], axis=1)
        s_new_ref[local] = dec_ref[rows, :][:, :, None] * s_old + ke_ref[rows, :][:, :, None] * v_rows
        q_rows = (qt_ref[rows, :][:, None, :] * head_rows[None]).astype(jnp.bfloat16)
        o_inter = jnp.einsum("bhk,bkv->bhv", q_rows, s_old.astype(jnp.bfloat16),
                             preferred_element_type=jnp.float32)
        o_ref[rows, :] = o_ref[rows, :] + jnp.concatenate([o_inter[:, h, :] for h in range(N_HEADS)], axis=-1)

    @pl.when(i == pl.num_programs(0) - 1)
    def _():
        mix_ref[:, 0:GLA_WIDTH] = _gla_finish(o_ref[...], g_ref[...], gla_g_ref[...]).astype(mix_ref.dtype)


def _sample_mixer(layer, x, s_state, pool_state, p, stacked):
    n_seq = x.shape[0]
    seq_tile = SAMPLE_STATE_TILE
    whole = lambda *shape: pl.BlockSpec((None,) + shape, lambda i: (layer,) + (0,) * len(shape))
    fixed = lambda *shape: pl.BlockSpec(shape, lambda i: (0,) * len(shape))
    f32 = jnp.float32
    chained = stacked is not None
    operands = [x, p["n1"], p["w_main"], p["w_code"], p["w_u"], p["w_gate"], p["b_gate"], p["gla_g"], p["pool_w"],
                p["pool_scale"], s_state, pool_state]
    in_specs = [
        fixed(n_seq, D_MODEL),
        whole(1, D_MODEL),
        whole(D_MODEL, COL_U),
        whole(D_MODEL, LANES),
        whole(D_MODEL, POOL_WIDTH),
        whole(LANES, KEY_WIDTH),
        whole(1, KEY_WIDTH),
        whole(1, HEAD_V),
        whole(len(POOL_WINDOWS) // 2, POOL_PAIR, POOL_PAIR),
        whole(1, POOL_WIDTH),
        pl.BlockSpec((None, seq_tile, KEY_WIDTH, HEAD_V), lambda i: (layer, i, 0, 0)),
        whole(POOL_BUF, n_seq, POOL_WIDTH),
    ]
    aliases = {}
    if chained:
        aliases = {len(operands): 1, len(operands) + 1: 2}
        operands += list(stacked)
        in_specs += [pl.BlockSpec(memory_space=pl.ANY)] * 2
    return pl.pallas_call(
        functools.partial(_sample_mixer_kernel, chained=chained),
        grid=(n_seq // seq_tile,),
        in_specs=in_specs,
        out_specs=[
            fixed(n_seq, D_MODEL),
            pl.BlockSpec((None, seq_tile, KEY_WIDTH, HEAD_V), lambda i: (layer, i, 0, 0)),
            whole(POOL_BUF, n_seq, POOL_WIDTH),
        ],
        out_shape=[
            jax.ShapeDtypeStruct((n_seq, D_MODEL), jnp.bfloat16),
            jax.ShapeDtypeStruct((DEPTH, n_seq, KEY_WIDTH, HEAD_V), f32),
            jax.ShapeDtypeStruct((DEPTH, POOL_BUF, n_seq, POOL_WIDTH), f32),
        ],
        scratch_shapes=[
            pltpu.VMEM((n_seq, KEY_WIDTH), f32),
            pltpu.VMEM((n_seq, KEY_WIDTH), f32),
            pltpu.VMEM((n_seq, KEY_WIDTH), f32),
            pltpu.VMEM((n_seq, GLA_WIDTH), f32),
            pltpu.VMEM((n_seq, GLA_WIDTH), f32),
            pltpu.VMEM((n_seq, GLA_WIDTH), f32),
        ],
        input_output_aliases=aliases,
        compiler_params=pltpu.CompilerParams(
            dimension_semantics=("arbitrary",), vmem_limit_bytes=VMEM_LIMIT),
        name=f"sample_mixer_l{layer}",
    )(*operands)


N_FF_CHUNKS = D_FF // FF_CHUNK
N_WEIGHT_CHUNKS = 1 + 2 * N_FF_CHUNKS
assert FF_CHUNK == D_MODEL


def _channel_kernel(x_ref, mix_ref, xs_ref, mix_s_ref, n2_ref, final_g_ref, w_out_hbm, w_up_hbm, w_down_hbm,
                    y_ref, ys_ref, w_out_ref, w_up_ref, w_down_ref, stage_ref, sem, *, layer, final):
    step = pl.program_id(0)

    def chunk(k):
        if k == 0:
            return w_out_hbm.at[layer], w_out_ref
        j, down = divmod(k - 1, 2)
        span = pl.ds(j * FF_CHUNK, FF_CHUNK)
        if down:
            return w_down_hbm.at[layer, span, :], w_down_ref.at[span, :]
        return w_up_hbm.at[layer, :, span], w_up_ref.at[:, span]

    def chunk_copy(k):
        return pltpu.make_async_copy(chunk(k)[0], stage_ref.at[k % 2], sem.at[k % 2])

    def fetch(k):
        chunk_copy(k).wait()
        chunk(k)[1][...] = stage_ref[k % 2].astype(jnp.bfloat16)
        if k + 2 < N_WEIGHT_CHUNKS:
            chunk_copy(k + 2).start()

    def channel_mix(x, mix, fetch_weights):
        def before_matmul(m):
            if fetch_weights and m + 1 < N_WEIGHT_CHUNKS:
                fetch(m + 1)

        if fetch_weights:
            fetch(0)
        before_matmul(0)
        x1 = x + jnp.dot(mix, w_out_ref[...], preferred_element_type=jnp.float32)
        h2 = _rmsnorm(x1, n2_ref[...]).astype(jnp.bfloat16)
        acc = x1
        for j in range(N_FF_CHUNKS):
            cols = slice(j * FF_CHUNK, (j + 1) * FF_CHUNK)
            before_matmul(1 + 2 * j)
            hid = jnp.dot(h2, w_up_ref[:, cols], preferred_element_type=jnp.float32)
            act = jnp.square(jnp.maximum(hid, 0.0)).astype(jnp.bfloat16)
            before_matmul(2 + 2 * j)
            acc = acc + jnp.dot(act, w_down_ref[cols, :], preferred_element_type=jnp.float32)
        if final:
            acc = _rmsnorm(acc, final_g_ref[...])
        return acc

    def rows(part):
        return slice(part * CHANNEL_ROWS, (part + 1) * CHANNEL_ROWS)

    def mix_parts(first_fetches):
        for part in range(x_ref.shape[0] // CHANNEL_ROWS):
            y_ref[rows(part), :] = channel_mix(x_ref[rows(part), :], mix_ref[rows(part), :],
                                               first_fetches and part == 0)

    @pl.when(step == 0)
    def _():
        chunk_copy(0).start()
        chunk_copy(1).start()
        mix_parts(True)

    @pl.when(step > 0)
    def _():
        mix_parts(False)

    @pl.when(step == pl.num_programs(0) - 1)
    def _():
        ys_ref[...] = channel_mix(xs_ref[...], mix_s_ref[...], False)


def _channel_mixer(layer, x, mix, xs, mix_s, p, final_g):
    rows = x.shape[0]
    n_seq = xs.shape[0]
    tile = CHANNEL_TILE
    assert rows // tile > 1
    bf16 = jnp.bfloat16
    return pl.pallas_call(
        functools.partial(_channel_kernel, layer=layer, final=layer == DEPTH - 1),
        grid=(rows // tile,),
        in_specs=[
            pl.BlockSpec((tile, D_MODEL), lambda i: (i, 0)),
            pl.BlockSpec((tile, D_MODEL), lambda i: (i, 0)),
            pl.BlockSpec((n_seq, D_MODEL), lambda i: (0, 0)),
            pl.BlockSpec((n_seq, D_MODEL), lambda i: (0, 0)),
            pl.BlockSpec((None, 1, D_MODEL), lambda i: (layer, 0, 0)),
            pl.BlockSpec((1, D_MODEL), lambda i: (0, 0)),
            pl.BlockSpec(memory_space=pl.ANY),
            pl.BlockSpec(memory_space=pl.ANY),
            pl.BlockSpec(memory_space=pl.ANY),
        ],
        out_specs=[
            pl.BlockSpec((tile, D_MODEL), lambda i: (i, 0)),
            pl.BlockSpec((n_seq, D_MODEL), lambda i: (0, 0)),
        ],
        out_shape=[
            jax.ShapeDtypeStruct((rows, D_MODEL), jnp.float32),
            jax.ShapeDtypeStruct((n_seq, D_MODEL), jnp.float32),
        ],
        scratch_shapes=[
            pltpu.VMEM((D_MODEL, D_MODEL), bf16),
            pltpu.VMEM((D_MODEL, D_FF), bf16),
            pltpu.VMEM((D_FF, D_MODEL), bf16),
            pltpu.VMEM((2, D_MODEL, D_MODEL), jnp.float32),
            pltpu.SemaphoreType.DMA((2,)),
        ],
        compiler_params=pltpu.CompilerParams(
            dimension_semantics=("arbitrary",), vmem_limit_bytes=CHANNEL_VMEM_LIMIT),
        name=f"channel_mixer_l{layer}",
    )(x, mix, xs, mix_s, p["n2"], final_g, p["w_out"], p["w_up"], p["w_down"])


def kernel(x_prompt, x_sample, state_gla, state_pool, norm1_g, w_in, w_gate, b_gate, gla_norm_g, pool_w,
           pool_scale, w_out, norm2_g, w_up, w_down, final_g):
    B, T, _ = x_prompt.shape
    n_seq = x_sample.shape[0]
    bf16 = jnp.bfloat16

    w_code = jnp.pad(w_in[:, :, W_IN_GATE:W_IN_U], ((0, 0), (0, 0), (0, LANES - GATE_RANK)))
    pw = pool_w.reshape(DEPTH, len(POOL_WINDOWS) // 2, 2, POOL_GROUP, POOL_GROUP)
    zero = jnp.zeros_like(pw[:, :, 0])
    pool_pairs = jnp.concatenate([jnp.concatenate([pw[:, :, 0], zero], axis=-1),
                                  jnp.concatenate([zero, pw[:, :, 1]], axis=-1)], axis=-2)
    params = {
        "n1": norm1_g.reshape(DEPTH, 1, D_MODEL),
        "w_main": w_in[:, :, :W_IN_GATE].astype(bf16),
        "w_code": w_code.astype(bf16),
        "w_u": w_in[:, :, W_IN_U:].astype(bf16),
        "w_gate": jnp.pad(w_gate, ((0, 0), (0, LANES - GATE_RANK), (0, 0))).astype(bf16),
        "b_gate": b_gate.reshape(DEPTH, 1, KEY_WIDTH),
        "gla_g": gla_norm_g.reshape(DEPTH, 1, HEAD_V),
        "pool_w": pool_pairs.astype(bf16),
        "pool_scale": pool_scale.reshape(DEPTH, 1, POOL_WIDTH),
        "w_out": w_out,
        "n2": norm2_g.reshape(DEPTH, 1, D_MODEL),
        "w_up": w_up,
        "w_down": w_down,
    }
    final_g2 = final_g.reshape(1, D_MODEL)
    s_state = state_gla.reshape(DEPTH, n_seq, KEY_WIDTH, HEAD_V)
    pool_rows = jnp.swapaxes(state_pool, 1, 2)

    xp = x_prompt
    xs = x_sample.reshape(n_seq, D_MODEL)
    prompt_states = sample_states = None
    for layer in range(DEPTH):
        mix_p, *prompt_states = _prompt_mixer(layer, xp, params, prompt_states)
        mix_s, *sample_states = _sample_mixer(layer, xs, s_state, pool_rows, params, sample_states)
        xp, xs = _channel_mixer(layer, xp.reshape(B * T, D_MODEL), mix_p.reshape(B * T, D_MODEL), xs, mix_s,
                                params, final_g2)
        xp = xp.reshape(B, T, D_MODEL)
    gla_p, hist_p = prompt_states
    gla_s, pool_s = sample_states
    return (xp, xs.reshape(n_seq, 1, D_MODEL),
            gla_p.reshape(DEPTH, B, N_HEADS, HEAD_K, HEAD_V), hist_p[:, :, POOL_HIST - POOL_BUF:],
            gla_s.reshape(DEPTH, n_seq, N_HEADS, HEAD_K, HEAD_V), jnp.swapaxes(pool_s, 1, 2))
```

```python
import functools

import jax
import jax.numpy as jnp
from jax import lax
from jax.experimental import pallas as pl
from jax.experimental.pallas import tpu as pltpu

D_MODEL = 1024
DEPTH = 4
N_HEADS = 4
HEAD_K = 64
HEAD_V = 128
KEY_WIDTH = N_HEADS * HEAD_K
GLA_WIDTH = N_HEADS * HEAD_V
GATE_RANK = 16
GATE_TEMP = 16.0
CHUNK = 64
POOL_WIDTH = 512
POOL_WINDOWS = (2, 4, 8, 16)
POOL_GROUP = 128
POOL_PAIR = 2 * POOL_GROUP
POOL_BUF = 15
POOL_HIST = 16
D_FF = 4 * D_MODEL
FF_CHUNK = 1024
EPS = 1e-6
LANES = 128
SUBLANES = 8

COL_Q = 0
COL_K = KEY_WIDTH
COL_V = 2 * KEY_WIDTH
COL_G = COL_V + GLA_WIDTH
COL_U = COL_G + GLA_WIDTH
Z_WIDTH = COL_U + POOL_WIDTH
W_IN_GATE = COL_U
W_IN_U = COL_U + GATE_RANK

PROMPT_TILE = 1024
PROMPT_SUB = 128
PROJ_PIECE = 256
CHANNEL_TILE = 512
SAMPLE_STATE_TILE = 32
VMEM_LIMIT = 56 * 1024 * 1024


def _rmsnorm(x, g):
    return x * lax.rsqrt(jnp.mean(x * x, axis=-1, keepdims=True) + EPS) * g


def _log_sigmoid(x):
    return jnp.minimum(x, 0.0) - jnp.log(1.0 + jnp.exp(-jnp.abs(x)))


def _bf16_round(x):
    return x.astype(jnp.bfloat16).astype(jnp.float32)


def _head_lane_mask(rows):
    lane = lax.broadcasted_iota(jnp.int32, (rows, KEY_WIDTH), 1)
    return [(lane >= h * HEAD_K) & (lane < (h + 1) * HEAD_K) for h in range(N_HEADS)]


def _gate_log_decay(a_low, w_gate_ref, b_gate_ref):
    pre = jnp.dot(a_low.astype(jnp.bfloat16), w_gate_ref[...], preferred_element_type=jnp.float32) + b_gate_ref[...]
    return _log_sigmoid(pre) / GATE_TEMP


def _project(x, n1_ref, w_main_ref, w_code_ref, w_u_ref, w_gate_ref, b_gate_ref):
    h = _rmsnorm(x, n1_ref[...]).astype(jnp.bfloat16)
    qkvg = jnp.dot(h, w_main_ref[...], preferred_element_type=jnp.float32)
    a_low = jnp.dot(h, w_code_ref[...], preferred_element_type=jnp.float32)
    u = jnp.dot(h, w_u_ref[...], preferred_element_type=jnp.float32)
    return qkvg, u, _gate_log_decay(a_low, w_gate_ref, b_gate_ref)


def _gla_finish(o, g, gla_g):
    outs = []
    for h in range(N_HEADS):
        sl = slice(h * HEAD_V, (h + 1) * HEAD_V)
        outs.append(_rmsnorm(o[:, sl], gla_g) * (g[:, sl] * jax.nn.sigmoid(g[:, sl])))
    return jnp.concatenate(outs, axis=-1)


def _trace_interleaved(a, b):
    i = j = 0
    while i < len(a) or j < len(b):
        if j >= len(b) or (i < len(a) and i * len(b) <= j * len(a)):
            a[i]()
            i += 1
        else:
            b[j]()
            j += 1


def _prompt_project_tasks(load_x, slot, n1_ref, w_main_ref, w_code_ref, w_u_ref, w_gate_ref, b_gate_ref,
                          z_ref, b_ref):
    v = {}

    def norm():
        v["h"] = _rmsnorm(load_x(), n1_ref[...]).astype(jnp.bfloat16)

    def piece(lo, hi):
        w_ref, first = (w_main_ref, 0) if hi <= COL_U else (w_u_ref, COL_U)

        def run():
            z_ref[slot, :, lo:hi] = jnp.dot(v["h"], w_ref[:, lo - first:hi - first],
                                            preferred_element_type=jnp.float32)
        return run

    def gate_code():
        v["a_low"] = jnp.dot(v["h"], w_code_ref[...], preferred_element_type=jnp.float32)

    def gate_pre():
        v["pre"] = (jnp.dot(v["a_low"].astype(jnp.bfloat16), w_gate_ref[...], preferred_element_type=jnp.float32)
                    + b_gate_ref[...])

    def log_decay():
        v["b"] = _log_sigmoid(v["pre"]) / GATE_TEMP

    def scan(shifts, last):
        def run():
            b = v["b"]
            row = lax.broadcasted_iota(jnp.int32, b.shape, 0) % CHUNK
            for shift in shifts:
                b = b + jnp.where(row >= shift, pltpu.roll(b, shift, axis=0), 0.0)
            v["b"] = b
            if last:
                b_ref[slot] = b
        return run

    half = PROJ_PIECE
    return [
        norm, gate_code,
        piece(COL_Q, COL_K), gate_pre,
        piece(COL_K, COL_V), log_decay,
        piece(COL_V, COL_V + half), scan((1, 2), False),
        piece(COL_V + half, COL_G), scan((4, 8), False),
        piece(COL_G, COL_G + half), scan((16, 32), True),
        piece(COL_G + half, COL_U), piece(COL_U, COL_U + half), piece(COL_U + half, Z_WIDTH),
    ]


def _prompt_mix_tasks(slot, z_ref, b_ref, carry, first_pos, gla_g_ref, pool_w_ref, pool_scale_ref, mix_ref, base):
    head_mask = _head_lane_mask(CHUNK)
    head_mask_v = _head_lane_mask(HEAD_V)
    ci = lax.broadcasted_iota(jnp.int32, (CHUNK, CHUNK), 0)
    si = lax.broadcasted_iota(jnp.int32, (CHUNK, CHUNK), 1)
    causal = si <= ci
    scale = HEAD_K ** -0.5
    sub = z_ref.shape[1]

    def chunk_tasks(c):
        rows = slice(c * CHUNK, (c + 1) * CHUNK)
        out_rows = slice(base + c * CHUNK, base + (c + 1) * CHUNK)
        v = {}

        def prepare():
            b_c = b_ref[slot, rows, :]
            b_last = b_c[CHUNK - 1:CHUNK]
            q_c = z_ref[slot, rows, COL_Q:COL_K]
            k_c = z_ref[slot, rows, COL_K:COL_V]
            v["v"] = z_ref[slot, rows, COL_V:COL_G].astype(jnp.bfloat16)
            qt = q_c * jnp.exp(b_c) * scale
            v["kt"] = (k_c * jnp.exp(-b_c)).astype(jnp.bfloat16)
            v["ke"] = (k_c * jnp.exp(b_last - b_c)).astype(jnp.bfloat16)
            v["decay"] = jnp.exp(b_last)
            v["q_stack"] = jnp.concatenate([jnp.where(m, qt, 0.0) for m in head_mask], axis=0).astype(jnp.bfloat16)

        def update_product():
            v["p"] = lax.dot_general(v["v"], v["ke"], (((0,), (0,)), ((), ())),
                                     preferred_element_type=jnp.float32)

        def query_product():
            st = carry["st"]
            rhs = jnp.concatenate([st.astype(jnp.bfloat16), v["kt"]], axis=0)
            v["r"] = lax.dot_general(v["q_stack"], rhs, (((1,), (1,)), ((), ())),
                                     preferred_element_type=jnp.float32)

        def update_state():
            p = v["p"]
            upd = jnp.where(head_mask_v[0], p[0:HEAD_V], 0.0)
            for h in range(1, N_HEADS):
                upd = upd + jnp.where(head_mask_v[h], p[h * HEAD_V:(h + 1) * HEAD_V], 0.0)
            carry["st"] = v["decay"] * carry["st"] + upd

        def intra():
            r = v["r"]
            outs = []
            for h in range(N_HEADS):
                hr = slice(h * CHUNK, (h + 1) * CHUNK)
                att = jnp.where(causal, r[hr, HEAD_V:HEAD_V + CHUNK], 0.0).astype(jnp.bfloat16)
                o_intra = jnp.dot(att, v["v"][:, h * HEAD_V:(h + 1) * HEAD_V], preferred_element_type=jnp.float32)
                outs.append(o_intra + r[hr, 0:HEAD_V])
            v["o"] = jnp.concatenate(outs, axis=-1)

        def finish():
            g_c = z_ref[slot, rows, COL_G:COL_U]
            mix_ref[out_rows, 0:GLA_WIDTH] = _gla_finish(v["o"], g_c, gla_g_ref[...]).astype(mix_ref.dtype)

        return dict(prepare=prepare, update_product=update_product, query_product=query_product,
                    update_state=update_state, intra=intra, finish=finish)

    pooled = {}

    def window_mean_task(gi, w):
        lanes = slice(gi * POOL_GROUP, (gi + 1) * POOL_GROUP)

        def window_mean():
            u = z_ref[slot, :, COL_U + gi * POOL_GROUP:COL_U + (gi + 1) * POOL_GROUP]
            s = jnp.concatenate([carry["hist"][:, lanes], u], axis=0)
            shift = 1
            while shift < w:
                s = s + pltpu.roll(s, shift, axis=0)
                shift *= 2
            seen = first_pos + 1 + lax.broadcasted_iota(jnp.int32, (sub, POOL_GROUP), 0)
            if "inv_seen" not in pooled:
                pooled["inv_seen"] = 1.0 / seen.astype(jnp.float32)
            inv_count = jnp.where(seen < w, pooled["inv_seen"], 1.0 / w)
            pooled[gi] = (s[POOL_HIST:] * inv_count - u).astype(jnp.bfloat16)

        return window_mean

    def group_map_task(pair):
        lanes = slice(pair * POOL_PAIR, (pair + 1) * POOL_PAIR)

        def group_map():
            both = jnp.concatenate([pooled[2 * pair], pooled[2 * pair + 1]], axis=-1)
            pg = jnp.dot(both, pool_w_ref[pair], preferred_element_type=jnp.float32)
            pg = pg * pool_scale_ref[:, lanes]
            mix_ref[base:base + sub, GLA_WIDTH + pair * POOL_PAIR:GLA_WIDTH + (pair + 1) * POOL_PAIR] = (
                pg.astype(mix_ref.dtype))

        return group_map

    def keep_history():
        carry["hist"] = z_ref[slot, sub - POOL_HIST:sub, COL_U:Z_WIDTH]

    assert sub == 2 * CHUNK
    c0, c1 = chunk_tasks(0), chunk_tasks(1)
    means = [window_mean_task(gi, w) for gi, w in enumerate(POOL_WINDOWS)]
    maps = [group_map_task(pair) for pair in range(len(POOL_WINDOWS) // 2)]
    return [
        c0["prepare"], c1["prepare"], c0["update_product"], c0["query_product"], c1["update_product"],
        c0["update_state"], c1["query_product"], means[0], c0["intra"], c1["update_state"],
        means[1], c1["intra"], maps[0], c0["finish"], means[2], c1["finish"], means[3], maps[1], keep_history,
    ]


def _prompt_mixer_kernel(x_ref, x_next_ref, n1_ref, w_main_ref, w_code_ref, w_u_ref, w_gate_ref, b_gate_ref, gla_g_ref,
                         pool_w_ref, pool_scale_ref, mix_ref, s_fin_ref, hist_out_ref, st_ref, hist_ref, z_ref, b_ref):
    tile = x_ref.shape[0]
    n_sub = tile // PROMPT_SUB
    t = pl.program_id(1)

    def project_tasks(load_x, slot):
        return _prompt_project_tasks(load_x, slot, n1_ref, w_main_ref, w_code_ref, w_u_ref, w_gate_ref, b_gate_ref,
                                     z_ref, b_ref)

    @pl.when(t == 0)
    def _():
        st_ref[...] = jnp.zeros_like(st_ref)
        hist_ref[...] = jnp.zeros_like(hist_ref)

    @pl.when((t == 0) & (pl.program_id(0) == 0))
    def _():
        for task in project_tasks(lambda: x_ref[0:PROMPT_SUB], 0):
            task()

    carry = {"st": st_ref[...], "hist": hist_ref[...]}
    for sb in range(n_sub):
        if sb + 1 < n_sub:
            load_next = lambda sb=sb: x_ref[(sb + 1) * PROMPT_SUB:(sb + 2) * PROMPT_SUB]
        else:
            load_next = lambda: x_next_ref[...]
        _trace_interleaved(
            project_tasks(load_next, (sb + 1) % 2),
            _prompt_mix_tasks(sb % 2, z_ref, b_ref, carry, t * tile + sb * PROMPT_SUB,
                              gla_g_ref, pool_w_ref, pool_scale_ref, mix_ref, sb * PROMPT_SUB))
    st_ref[...] = carry["st"]
    hist_ref[...] = carry["hist"]

    @pl.when(t == pl.num_programs(1) - 1)
    def _():
        s_fin_ref[...] = carry["st"].T
        hist_out_ref[...] = carry["hist"]


def _prompt_mixer(layer, x, p):
    B, T, _ = x.shape
    tile = PROMPT_TILE
    n_tiles = T // tile
    n_sub = tile // PROMPT_SUB
    assert n_sub % 2 == 0
    whole = lambda *shape: pl.BlockSpec((None,) + shape, lambda b, t: (layer,) + (0,) * len(shape))
    operands = [x, x, p["n1"], p["w_main"], p["w_code"], p["w_u"], p["w_gate"], p["b_gate"], p["gla_g"], p["pool_w"],
                p["pool_scale"]]

    def next_first_sub_block(b, t):
        n = jnp.minimum(b * n_tiles + t + 1, B * n_tiles - 1)
        return (n // n_tiles, (n % n_tiles) * n_sub, 0)

    in_specs = [
        pl.BlockSpec((None, tile, D_MODEL), lambda b, t: (b, t, 0)),
        pl.BlockSpec((None, PROMPT_SUB, D_MODEL), next_first_sub_block),
        whole(1, D_MODEL),
        whole(D_MODEL, COL_U),
        whole(D_MODEL, LANES),
        whole(D_MODEL, POOL_WIDTH),
        whole(LANES, KEY_WIDTH),
        whole(1, KEY_WIDTH),
        whole(1, HEAD_V),
        whole(len(POOL_WINDOWS) // 2, POOL_PAIR, POOL_PAIR),
        whole(1, POOL_WIDTH),
    ]
    return pl.pallas_call(
        _prompt_mixer_kernel,
        grid=(B, n_tiles),
        in_specs=in_specs,
        out_specs=[
            pl.BlockSpec((None, tile, D_MODEL), lambda b, t: (b, t, 0)),
            pl.BlockSpec((None, KEY_WIDTH, HEAD_V), lambda b, t: (b, 0, 0)),
            pl.BlockSpec((None, POOL_HIST, POOL_WIDTH), lambda b, t: (b, 0, 0)),
        ],
        out_shape=[
            jax.ShapeDtypeStruct((B, T, D_MODEL), jnp.bfloat16),
            jax.ShapeDtypeStruct((B, KEY_WIDTH, HEAD_V), jnp.float32),
            jax.ShapeDtypeStruct((B, POOL_HIST, POOL_WIDTH), jnp.float32),
        ],
        scratch_shapes=[
            pltpu.VMEM((HEAD_V, KEY_WIDTH), jnp.float32),
            pltpu.VMEM((POOL_HIST, POOL_WIDTH), jnp.float32),
            pltpu.VMEM((2, PROMPT_SUB, Z_WIDTH), jnp.float32),
            pltpu.VMEM((2, PROMPT_SUB, KEY_WIDTH), jnp.float32),
        ],
        compiler_params=pltpu.CompilerParams(
            dimension_semantics=("arbitrary", "arbitrary"), vmem_limit_bytes=VMEM_LIMIT),
        name=f"prompt_mixer_l{layer}",
    )(*operands)


def _sample_mixer_kernel(x_ref, n1_ref, w_main_ref, w_code_ref, w_u_ref, w_gate_ref, b_gate_ref, gla_g_ref, pool_w_ref,
                         pool_scale_ref, s_ref, pool_state_ref, mix_ref, s_new_ref, pool_new_ref,
                         dec_ref, ke_ref, qt_ref, v_ref, g_ref, o_ref):
    i = pl.program_id(0)
    n_seq = x_ref.shape[0]
    seq_tile = s_ref.shape[0]

    @pl.when(i == 0)
    def _():
        qkvg, u, log_a = _project(x_ref[...], n1_ref, w_main_ref, w_code_ref, w_u_ref, w_gate_ref, b_gate_ref)
        q = qkvg[:, COL_Q:COL_K]
        k = qkvg[:, COL_K:COL_V]
        v = _bf16_round(qkvg[:, COL_V:COL_G])
        qt = _bf16_round(q * jnp.exp(log_a) * (HEAD_K ** -0.5))
        kt = _bf16_round(k * jnp.exp(-log_a))
        ke = _bf16_round(k * jnp.exp(log_a - log_a))
        prod = qt * kt
        o_intra = []
        for h, m in enumerate(_head_lane_mask(n_seq)):
            att = _bf16_round(jnp.sum(jnp.where(m, prod, 0.0), axis=-1, keepdims=True))
            o_intra.append(att * v[:, h * HEAD_V:(h + 1) * HEAD_V])
        o_ref[...] = jnp.concatenate(o_intra, axis=-1)
        dec_ref[...] = jnp.exp(log_a)
        ke_ref[...] = ke
        qt_ref[...] = qt
        v_ref[...] = v
        g_ref[...] = qkvg[:, COL_G:COL_U]

        pooled = []
        for gi, w in enumerate(POOL_WINDOWS):
            lanes = slice(gi * POOL_GROUP, (gi + 1) * POOL_GROUP)
            s = u[:, lanes]
            for j in range(POOL_BUF - (w - 1), POOL_BUF):
                s = s + pool_state_ref[j, :, lanes]
            pooled.append((s / float(w) - u[:, lanes]).astype(jnp.bfloat16))
        for pair in range(len(POOL_WINDOWS) // 2):
            lanes = slice(pair * POOL_PAIR, (pair + 1) * POOL_PAIR)
            both = jnp.concatenate(pooled[2 * pair:2 * pair + 2], axis=-1)
            pg = jnp.dot(both, pool_w_ref[pair], preferred_element_type=jnp.float32) * pool_scale_ref[:, lanes]
            mix_ref[:, GLA_WIDTH + pair * POOL_PAIR:GLA_WIDTH + (pair + 1) * POOL_PAIR] = pg.astype(mix_ref.dtype)
        for j in range(POOL_BUF - 1):
            pool_new_ref[j] = pool_state_ref[j + 1]
        pool_new_ref[POOL_BUF - 1] = u

    lane = lax.broadcasted_iota(jnp.int32, (SUBLANES, KEY_WIDTH), 1)
    head = lax.broadcasted_iota(jnp.int32, (SUBLANES, KEY_WIDTH), 0)
    head_rows = jnp.where(lane // HEAD_K == head, 1.0, 0.0)
    for sub in range(seq_tile // SUBLANES):
        local = slice(sub * SUBLANES, (sub + 1) * SUBLANES)
        rows = pl.ds(pl.multiple_of(i * seq_tile + sub * SUBLANES, SUBLANES), SUBLANES)
        s_old = s_ref[local]
        v = v_ref[rows, :]
        v_rows = jnp.concatenate(
            [jnp.broadcast_to(v[:, None, h * HEAD_V:(h + 1) * HEAD_V], (SUBLANES, HEAD_K, HEAD_V))
             for h in range(N_HEADS)], axis=1)
        s_new_ref[local] = dec_ref[rows, :][:, :, None] * s_old + ke_ref[rows, :][:, :, None] * v_rows
        q_rows = (qt_ref[rows, :][:, None, :] * head_rows[None]).astype(jnp.bfloat16)
        o_inter = jnp.einsum("bhk,bkv->bhv", q_rows, s_old.astype(jnp.bfloat16),
                             preferred_element_type=jnp.float32)
        o_ref[rows, :] = o_ref[rows, :] + jnp.concatenate([o_inter[:, h, :] for h in range(N_HEADS)], axis=-1)

    @pl.when(i == pl.num_programs(0) - 1)
    def _():
        mix_ref[:, 0:GLA_WIDTH] = _gla_finish(o_ref[...], g_ref[...], gla_g_ref[...]).astype(mix_ref.dtype)


def _sample_mixer(layer, x, s_state, pool_state, p):
    n_seq = x.shape[0]
    seq_tile = SAMPLE_STATE_TILE
    whole = lambda *shape: pl.BlockSpec((None,) + shape, lambda i: (layer,) + (0,) * len(shape))
    fixed = lambda *shape: pl.BlockSpec(shape, lambda i: (0,) * len(shape))
    f32 = jnp.float32
    operands = [x, p["n1"], p["w_main"], p["w_code"], p["w_u"], p["w_gate"], p["b_gate"], p["gla_g"], p["pool_w"],
                p["pool_scale"], s_state, pool_state]
    in_specs = [
        fixed(n_seq, D_MODEL),
        whole(1, D_MODEL),
        whole(D_MODEL, COL_U),
        whole(D_MODEL, LANES),
        whole(D_MODEL, POOL_WIDTH),
        whole(LANES, KEY_WIDTH),
        whole(1, KEY_WIDTH),
        whole(1, HEAD_V),
        whole(len(POOL_WINDOWS) // 2, POOL_PAIR, POOL_PAIR),
        whole(1, POOL_WIDTH),
        pl.BlockSpec((None, seq_tile, KEY_WIDTH, HEAD_V), lambda i: (layer, i, 0, 0)),
        whole(POOL_BUF, n_seq, POOL_WIDTH),
    ]
    return pl.pallas_call(
        _sample_mixer_kernel,
        grid=(n_seq // seq_tile,),
        in_specs=in_specs,
        out_specs=[
            fixed(n_seq, D_MODEL),
            pl.BlockSpec((seq_tile, KEY_WIDTH, HEAD_V), lambda i: (i, 0, 0)),
            fixed(POOL_BUF, n_seq, POOL_WIDTH),
        ],
        out_shape=[
            jax.ShapeDtypeStruct((n_seq, D_MODEL), jnp.bfloat16),
            jax.ShapeDtypeStruct((n_seq, KEY_WIDTH, HEAD_V), f32),
            jax.ShapeDtypeStruct((POOL_BUF, n_seq, POOL_WIDTH), f32),
        ],
        scratch_shapes=[
            pltpu.VMEM((n_seq, KEY_WIDTH), f32),
            pltpu.VMEM((n_seq, KEY_WIDTH), f32),
            pltpu.VMEM((n_seq, KEY_WIDTH), f32),
            pltpu.VMEM((n_seq, GLA_WIDTH), f32),
            pltpu.VMEM((n_seq, GLA_WIDTH), f32),
            pltpu.VMEM((n_seq, GLA_WIDTH), f32),
        ],
        compiler_params=pltpu.CompilerParams(
            dimension_semantics=("arbitrary",), vmem_limit_bytes=VMEM_LIMIT),
        name=f"sample_mixer_l{layer}",
    )(*operands)


N_FF_CHUNKS = D_FF // FF_CHUNK
N_STATE_KINDS = 4
N_WEIGHT_CHUNKS = 1 + 2 * N_FF_CHUNKS
assert FF_CHUNK == D_MODEL


def _channel_kernel(*refs, layer, final):
    n_states = DEPTH * N_STATE_KINDS if final else 0
    n_stacked = N_STATE_KINDS if final else 0
    x_ref, mix_ref, xs_ref, mix_s_ref, n2_ref, final_g_ref, w_out_hbm, w_up_hbm, w_down_hbm = refs[:9]
    refs = refs[9:]
    layer_states, refs = refs[:n_states], refs[n_states:]
    (y_ref, ys_ref), refs = refs[:2], refs[2:]
    stacked, refs = refs[:n_stacked], refs[n_stacked:]
    w_out_ref, w_up_ref, w_down_ref, stage_ref, sem = refs[:5]
    step = pl.program_id(0)

    if final:
        stack_sem = refs[5]

        def stack_copy(l, kind):
            i = l * N_STATE_KINDS + kind
            return pltpu.make_async_copy(layer_states[i], stacked[kind].at[l], stack_sem.at[i])

        def for_all_stack_copies(action):
            for l in range(DEPTH):
                for kind in range(N_STATE_KINDS):
                    action(stack_copy(l, kind))

        @pl.when(step == 0)
        def _():
            for_all_stack_copies(lambda copy: copy.start())

        @pl.when(step == pl.num_programs(0) - 1)
        def _():
            for_all_stack_copies(lambda copy: copy.wait())

    def chunk(k):
        if k == 0:
            return w_out_hbm.at[layer], w_out_ref
        j, down = divmod(k - 1, 2)
        span = pl.ds(j * FF_CHUNK, FF_CHUNK)
        if down:
            return w_down_hbm.at[layer, span, :], w_down_ref.at[span, :]
        return w_up_hbm.at[layer, :, span], w_up_ref.at[:, span]

    def chunk_copy(k):
        return pltpu.make_async_copy(chunk(k)[0], stage_ref.at[k % 2], sem.at[k % 2])

    def fetch(k):
        chunk_copy(k).wait()
        chunk(k)[1][...] = stage_ref[k % 2].astype(jnp.bfloat16)
        if k + 2 < N_WEIGHT_CHUNKS:
            chunk_copy(k + 2).start()

    def channel_mix(x, mix, fetch_weights):
        def before_matmul(m):
            if fetch_weights and m + 1 < N_WEIGHT_CHUNKS:
                fetch(m + 1)

        if fetch_weights:
            fetch(0)
        before_matmul(0)
        x1 = x + jnp.dot(mix, w_out_ref[...], preferred_element_type=jnp.float32)
        h2 = _rmsnorm(x1, n2_ref[...]).astype(jnp.bfloat16)
        acc = x1
        for j in range(N_FF_CHUNKS):
            cols = slice(j * FF_CHUNK, (j + 1) * FF_CHUNK)
            before_matmul(1 + 2 * j)
            hid = jnp.dot(h2, w_up_ref[:, cols], preferred_element_type=jnp.float32)
            act = jnp.square(jnp.maximum(hid, 0.0)).astype(jnp.bfloat16)
            before_matmul(2 + 2 * j)
            acc = acc + jnp.dot(act, w_down_ref[cols, :], preferred_element_type=jnp.float32)
        if final:
            acc = _rmsnorm(acc, final_g_ref[...])
        return acc

    @pl.when(step == 0)
    def _():
        chunk_copy(0).start()
        chunk_copy(1).start()
        y_ref[...] = channel_mix(x_ref[...], mix_ref[...], True)

    @pl.when(step > 0)
    def _():
        y_ref[...] = channel_mix(x_ref[...], mix_ref[...], False)

    @pl.when(step == pl.num_programs(0) - 1)
    def _():
        ys_ref[...] = channel_mix(xs_ref[...], mix_s_ref[...], False)


def _channel_mixer(layer, x, mix, xs, mix_s, p, final_g, layer_states=None):
    rows = x.shape[0]
    n_seq = xs.shape[0]
    tile = CHANNEL_TILE
    assert rows // tile > 1
    final = layer == DEPTH - 1
    assert final == (layer_states is not None)
    bf16 = jnp.bfloat16
    anywhere = pl.BlockSpec(memory_space=pl.ANY)
    operands = [x, mix, xs, mix_s, p["n2"], final_g, p["w_out"], p["w_up"], p["w_down"]]
    in_specs = [
        pl.BlockSpec((tile, D_MODEL), lambda i: (i, 0)),
        pl.BlockSpec((tile, D_MODEL), lambda i: (i, 0)),
        pl.BlockSpec((n_seq, D_MODEL), lambda i: (0, 0)),
        pl.BlockSpec((n_seq, D_MODEL), lambda i: (0, 0)),
        pl.BlockSpec((None, 1, D_MODEL), lambda i: (layer, 0, 0)),
        pl.BlockSpec((1, D_MODEL), lambda i: (0, 0)),
        anywhere, anywhere, anywhere,
    ]
    out_specs = [
        pl.BlockSpec((tile, D_MODEL), lambda i: (i, 0)),
        pl.BlockSpec((n_seq, D_MODEL), lambda i: (0, 0)),
    ]
    out_shape = [
        jax.ShapeDtypeStruct((rows, D_MODEL), jnp.float32),
        jax.ShapeDtypeStruct((n_seq, D_MODEL), jnp.float32),
    ]
    scratch_shapes = [
        pltpu.VMEM((D_MODEL, D_MODEL), bf16),
        pltpu.VMEM((D_MODEL, D_FF), bf16),
        pltpu.VMEM((D_FF, D_MODEL), bf16),
        pltpu.VMEM((2, D_MODEL, D_MODEL), jnp.float32),
        pltpu.SemaphoreType.DMA((2,)),
    ]
    if final:
        assert len(layer_states) == DEPTH * N_STATE_KINDS
        operands += list(layer_states)
        in_specs += [anywhere] * len(layer_states)
        out_specs += [anywhere] * N_STATE_KINDS
        out_shape += [jax.ShapeDtypeStruct((DEPTH,) + a.shape, a.dtype) for a in layer_states[:N_STATE_KINDS]]
        scratch_shapes.append(pltpu.SemaphoreType.DMA((DEPTH * N_STATE_KINDS,)))
    return pl.pallas_call(
        functools.partial(_channel_kernel, layer=layer, final=final),
        grid=(rows // tile,),
        in_specs=in_specs,
        out_specs=out_specs,
        out_shape=out_shape,
        scratch_shapes=scratch_shapes,
        compiler_params=pltpu.CompilerParams(
            dimension_semantics=("arbitrary",), vmem_limit_bytes=VMEM_LIMIT),
        name=f"channel_mixer_l{layer}",
    )(*operands)


def kernel(x_prompt, x_sample, state_gla, state_pool, norm1_g, w_in, w_gate, b_gate, gla_norm_g, pool_w,
           pool_scale, w_out, norm2_g, w_up, w_down, final_g):
    B, T, _ = x_prompt.shape
    n_seq = x_sample.shape[0]
    bf16 = jnp.bfloat16

    w_code = jnp.pad(w_in[:, :, W_IN_GATE:W_IN_U], ((0, 0), (0, 0), (0, LANES - GATE_RANK)))
    pw = pool_w.reshape(DEPTH, len(POOL_WINDOWS) // 2, 2, POOL_GROUP, POOL_GROUP)
    zero = jnp.zeros_like(pw[:, :, 0])
    pool_pairs = jnp.concatenate([jnp.concatenate([pw[:, :, 0], zero], axis=-1),
                                  jnp.concatenate([zero, pw[:, :, 1]], axis=-1)], axis=-2)
    params = {
        "n1": norm1_g.reshape(DEPTH, 1, D_MODEL),
        "w_main": w_in[:, :, :W_IN_GATE].astype(bf16),
        "w_code": w_code.astype(bf16),
        "w_u": w_in[:, :, W_IN_U:].astype(bf16),
        "w_gate": jnp.pad(w_gate, ((0, 0), (0, LANES - GATE_RANK), (0, 0))).astype(bf16),
        "b_gate": b_gate.reshape(DEPTH, 1, KEY_WIDTH),
        "gla_g": gla_norm_g.reshape(DEPTH, 1, HEAD_V),
        "pool_w": pool_pairs.astype(bf16),
        "pool_scale": pool_scale.reshape(DEPTH, 1, POOL_WIDTH),
        "w_out": w_out,
        "n2": norm2_g.reshape(DEPTH, 1, D_MODEL),
        "w_up": w_up,
        "w_down": w_down,
    }
    final_g2 = final_g.reshape(1, D_MODEL)
    s_state = state_gla.reshape(DEPTH, n_seq, KEY_WIDTH, HEAD_V)
    pool_rows = jnp.swapaxes(state_pool, 1, 2)

    xp = x_prompt
    xs = x_sample.reshape(n_seq, D_MODEL)
    layer_states = []
    for layer in range(DEPTH):
        mix_p, gla_p, hist_p = _prompt_mixer(layer, xp, params)
        mix_s, gla_s, pool_s = _sample_mixer(layer, xs, s_state, pool_rows, params)
        layer_states += [gla_p, hist_p, gla_s, pool_s]
        xp, xs, *stacked = _channel_mixer(layer, xp.reshape(B * T, D_MODEL), mix_p.reshape(B * T, D_MODEL), xs, mix_s,
                                          params, final_g2, layer_states if layer == DEPTH - 1 else None)
        xp = xp.reshape(B, T, D_MODEL)
    gla_p, hist_p, gla_s, pool_s = stacked
    return (xp, xs.reshape(n_seq, 1, D_MODEL),
            gla_p.reshape(DEPTH, B, N_HEADS, HEAD_K, HEAD_V), hist_p[:, :, POOL_HIST - POOL_BUF:],
            gla_s.reshape(DEPTH, n_seq, N_HEADS, HEAD_K, HEAD_V), jnp.swapaxes(pool_s, 1, 2))
```

```python
import functools

import jax
import jax.numpy as jnp
from jax import lax
from jax.experimental import pallas as pl
from jax.experimental.pallas import tpu as pltpu

D_MODEL = 1024
DEPTH = 4
N_HEADS = 4
HEAD_K = 64
HEAD_V = 128
KEY_WIDTH = N_HEADS * HEAD_K
GLA_WIDTH = N_HEADS * HEAD_V
GATE_RANK = 16
GATE_TEMP = 16.0
CHUNK = 64
POOL_WIDTH = 512
POOL_WINDOWS = (2, 4, 8, 16)
POOL_GROUP = 128
POOL_PAIR = 2 * POOL_GROUP
POOL_BUF = 15
POOL_HIST = 16
D_FF = 4 * D_MODEL
FF_CHUNK = 1024
EPS = 1e-6
LANES = 128
SUBLANES = 8

COL_Q = 0
COL_K = KEY_WIDTH
COL_V = 2 * KEY_WIDTH
COL_G = COL_V + GLA_WIDTH
COL_U = COL_G + GLA_WIDTH
Z_WIDTH = COL_U + POOL_WIDTH
W_IN_GATE = COL_U
W_IN_U = COL_U + GATE_RANK

PROMPT_TILE = 1024
PROMPT_SUB = 128
PROJ_PIECE = 256
CHANNEL_TILE = 512
SAMPLE_STATE_TILE = 32
VMEM_LIMIT = 56 * 1024 * 1024


def _rmsnorm(x, g):
    return x * lax.rsqrt(jnp.mean(x * x, axis=-1, keepdims=True) + EPS) * g


def _log_sigmoid(x):
    return jnp.minimum(x, 0.0) - jnp.log(1.0 + jnp.exp(-jnp.abs(x)))


def _bf16_round(x):
    return x.astype(jnp.bfloat16).astype(jnp.float32)


def _head_lane_mask(rows):
    lane = lax.broadcasted_iota(jnp.int32, (rows, KEY_WIDTH), 1)
    return [(lane >= h * HEAD_K) & (lane < (h + 1) * HEAD_K) for h in range(N_HEADS)]


def _gate_log_decay(a_low, w_gate_ref, b_gate_ref):
    pre = jnp.dot(a_low.astype(jnp.bfloat16), w_gate_ref[...], preferred_element_type=jnp.float32) + b_gate_ref[...]
    return _log_sigmoid(pre) / GATE_TEMP


def _project(x, n1_ref, w_main_ref, w_code_ref, w_u_ref, w_gate_ref, b_gate_ref):
    h = _rmsnorm(x, n1_ref[...]).astype(jnp.bfloat16)
    qkvg = jnp.dot(h, w_main_ref[...], preferred_element_type=jnp.float32)
    a_low = jnp.dot(h, w_code_ref[...], preferred_element_type=jnp.float32)
    u = jnp.dot(h, w_u_ref[...], preferred_element_type=jnp.float32)
    return qkvg, u, _gate_log_decay(a_low, w_gate_ref, b_gate_ref)


def _gla_finish(o, g, gla_g):
    outs = []
    for h in range(N_HEADS):
        sl = slice(h * HEAD_V, (h + 1) * HEAD_V)
        outs.append(_rmsnorm(o[:, sl], gla_g) * (g[:, sl] * jax.nn.sigmoid(g[:, sl])))
    return jnp.concatenate(outs, axis=-1)


def _trace_interleaved(a, b):
    i = j = 0
    while i < len(a) or j < len(b):
        if j >= len(b) or (i < len(a) and i * len(b) <= j * len(a)):
            a[i]()
            i += 1
        else:
            b[j]()
            j += 1


def _prompt_project_tasks(load_x, slot, n1_ref, w_main_ref, w_code_ref, w_u_ref, w_gate_ref, b_gate_ref,
                          z_ref, b_ref):
    v = {}

    def norm():
        v["h"] = _rmsnorm(load_x(), n1_ref[...]).astype(jnp.bfloat16)

    def piece(lo, hi):
        w_ref, first = (w_main_ref, 0) if hi <= COL_U else (w_u_ref, COL_U)

        def run():
            z_ref[slot, :, lo:hi] = jnp.dot(v["h"], w_ref[:, lo - first:hi - first],
                                            preferred_element_type=jnp.float32)
        return run

    def gate_code():
        v["a_low"] = jnp.dot(v["h"], w_code_ref[...], preferred_element_type=jnp.float32)

    def gate_pre():
        v["pre"] = (jnp.dot(v["a_low"].astype(jnp.bfloat16), w_gate_ref[...], preferred_element_type=jnp.float32)
                    + b_gate_ref[...])

    def log_decay():
        v["b"] = _log_sigmoid(v["pre"]) / GATE_TEMP

    def scan(shifts, last):
        def run():
            b = v["b"]
            row = lax.broadcasted_iota(jnp.int32, b.shape, 0) % CHUNK
            for shift in shifts:
                b = b + jnp.where(row >= shift, pltpu.roll(b, shift, axis=0), 0.0)
            v["b"] = b
            if last:
                b_ref[slot] = b
        return run

    half = PROJ_PIECE
    return [
        norm, gate_code,
        piece(COL_Q, COL_K), gate_pre,
        piece(COL_K, COL_V), log_decay,
        piece(COL_V, COL_V + half), scan((1, 2), False),
        piece(COL_V + half, COL_G), scan((4, 8), False),
        piece(COL_G, COL_G + half), scan((16, 32), True),
        piece(COL_G + half, COL_U), piece(COL_U, COL_U + half), piece(COL_U + half, Z_WIDTH),
    ]


def _prompt_mix_tasks(slot, z_ref, b_ref, carry, first_pos, gla_g_ref, pool_w_ref, pool_scale_ref, mix_ref, base):
    head_mask = _head_lane_mask(CHUNK)
    scale = HEAD_K ** -0.5
    sub = z_ref.shape[1]
    n_chunks = sub // CHUNK
    shared = {}

    def transpose_keys():
        b = b_ref[slot]
        k = z_ref[slot, :, COL_K:COL_V]
        last = [b[(c + 1) * CHUNK - 1:(c + 1) * CHUNK] for c in range(n_chunks)]
        b_last = jnp.concatenate([jnp.broadcast_to(row, (CHUNK, KEY_WIDTH)) for row in last], axis=0)
        shared["kt_t"] = (k * jnp.exp(-b)).T.astype(jnp.bfloat16)
        shared["ke_t"] = (k * jnp.exp(b_last - b)).T.astype(jnp.bfloat16)
        shared["v"] = z_ref[slot, :, COL_V:COL_G].astype(jnp.bfloat16)
        decay = jnp.exp(jnp.concatenate(last + [jnp.zeros((SUBLANES - n_chunks, KEY_WIDTH), jnp.float32)], axis=0))
        shared["decay"] = jnp.broadcast_to(decay[:, :, None], (SUBLANES, KEY_WIDTH, HEAD_V))

    def chunk_tasks(c):
        rows = slice(c * CHUNK, (c + 1) * CHUNK)
        out_rows = slice(base + c * CHUNK, base + (c + 1) * CHUNK)
        token = lax.broadcasted_iota(jnp.int32, (CHUNK, sub), 1) - c * CHUNK
        causal = (token >= 0) & (token <= lax.broadcasted_iota(jnp.int32, (CHUNK, sub), 0))
        in_chunk = (lax.broadcasted_iota(jnp.int32, (KEY_WIDTH, sub), 1) // CHUNK) == c
        v = {}

        def prepare():
            qt = z_ref[slot, rows, COL_Q:COL_K] * jnp.exp(b_ref[slot, rows, :]) * scale
            v["q_stack"] = jnp.concatenate([jnp.where(m, qt, 0.0) for m in head_mask], axis=0).astype(jnp.bfloat16)

        def update_product():
            ke_t = jnp.where(in_chunk, shared["ke_t"], jnp.zeros_like(shared["ke_t"]))
            v["upd"] = jnp.concatenate(
                [jnp.dot(ke_t[h * HEAD_K:(h + 1) * HEAD_K], shared["v"][:, h * HEAD_V:(h + 1) * HEAD_V],
                         preferred_element_type=jnp.float32) for h in range(N_HEADS)], axis=0)

        def query_product():
            rhs = jnp.concatenate([carry["st"].astype(jnp.bfloat16), shared["kt_t"]], axis=1)
            v["r"] = jnp.dot(v["q_stack"], rhs, preferred_element_type=jnp.float32)

        def update_state():
            carry["st"] = shared["decay"][c] * carry["st"] + v["upd"]

        def intra():
            r = v["r"]
            outs = []
            for h in range(N_HEADS):
                hr = slice(h * CHUNK, (h + 1) * CHUNK)
                att = jnp.where(causal, r[hr, HEAD_V:HEAD_V + sub], 0.0).astype(jnp.bfloat16)
                o_intra = jnp.dot(att, shared["v"][:, h * HEAD_V:(h + 1) * HEAD_V], preferred_element_type=jnp.float32)
                outs.append(o_intra + r[hr, 0:HEAD_V])
            v["o"] = jnp.concatenate(outs, axis=-1)

        def finish():
            g_c = z_ref[slot, rows, COL_G:COL_U]
            mix_ref[out_rows, 0:GLA_WIDTH] = _gla_finish(v["o"], g_c, gla_g_ref[...]).astype(mix_ref.dtype)

        return dict(prepare=prepare, update_product=update_product, query_product=query_product,
                    update_state=update_state, intra=intra, finish=finish)

    pooled = {}

    def window_mean_task(gi, w):
        lanes = slice(gi * POOL_GROUP, (gi + 1) * POOL_GROUP)

        def window_mean():
            u = z_ref[slot, :, COL_U + gi * POOL_GROUP:COL_U + (gi + 1) * POOL_GROUP]
            s = jnp.concatenate([carry["hist"][:, lanes], u], axis=0)
            shift = 1
            while shift < w:
                s = s + pltpu.roll(s, shift, axis=0)
                shift *= 2
            seen = first_pos + 1 + lax.broadcasted_iota(jnp.int32, (sub, POOL_GROUP), 0)
            if "inv_seen" not in pooled:
                pooled["inv_seen"] = 1.0 / seen.astype(jnp.float32)
            inv_count = jnp.where(seen < w, pooled["inv_seen"], 1.0 / w)
            pooled[gi] = (s[POOL_HIST:] * inv_count - u).astype(jnp.bfloat16)

        return window_mean

    def group_map_task(pair):
        lanes = slice(pair * POOL_PAIR, (pair + 1) * POOL_PAIR)

        def group_map():
            both = jnp.concatenate([pooled[2 * pair], pooled[2 * pair + 1]], axis=-1)
            pg = jnp.dot(both, pool_w_ref[pair], preferred_element_type=jnp.float32)
            pg = pg * pool_scale_ref[:, lanes]
            mix_ref[base:base + sub, GLA_WIDTH + pair * POOL_PAIR:GLA_WIDTH + (pair + 1) * POOL_PAIR] = (
                pg.astype(mix_ref.dtype))

        return group_map

    def keep_history():
        carry["hist"] = z_ref[slot, sub - POOL_HIST:sub, COL_U:Z_WIDTH]

    assert n_chunks == 2
    c0, c1 = chunk_tasks(0), chunk_tasks(1)
    means = [window_mean_task(gi, w) for gi, w in enumerate(POOL_WINDOWS)]
    maps = [group_map_task(pair) for pair in range(len(POOL_WINDOWS) // 2)]
    return [
        transpose_keys, c0["prepare"], c1["prepare"], c0["update_product"], c0["query_product"], c1["update_product"],
        c0["update_state"], c1["query_product"], means[0], c0["intra"], c1["update_state"],
        means[1], c1["intra"], maps[0], c0["finish"], means[2], c1["finish"], means[3], maps[1], keep_history,
    ]


def _prompt_mixer_kernel(*refs, chained):
    (x_ref, x_next_ref, n1_ref, w_main_ref, w_code_ref, w_u_ref, w_gate_ref, b_gate_ref, gla_g_ref, pool_w_ref,
     pool_scale_ref) = refs[:11]
    mix_ref, s_fin_ref, hist_out_ref, st_ref, hist_ref, z_ref, b_ref = refs[-7:]
    tile = x_ref.shape[0]
    n_sub = tile // PROMPT_SUB
    t = pl.program_id(1)

    def project_tasks(load_x, slot):
        return _prompt_project_tasks(load_x, slot, n1_ref, w_main_ref, w_code_ref, w_u_ref, w_gate_ref, b_gate_ref,
                                     z_ref, b_ref)

    @pl.when(t == 0)
    def _():
        st_ref[...] = jnp.zeros_like(st_ref)
        hist_ref[...] = jnp.zeros_like(hist_ref)

    @pl.when((t == 0) & (pl.program_id(0) == 0))
    def _():
        for task in project_tasks(lambda: x_ref[0:PROMPT_SUB], 0):
            task()

    carry = {"st": st_ref[...], "hist": hist_ref[...]}
    for sb in range(n_sub):
        if sb + 1 < n_sub:
            load_next = lambda sb=sb: x_ref[(sb + 1) * PROMPT_SUB:(sb + 2) * PROMPT_SUB]
        else:
            load_next = lambda: x_next_ref[...]
        _trace_interleaved(
            project_tasks(load_next, (sb + 1) % 2),
            _prompt_mix_tasks(sb % 2, z_ref, b_ref, carry, t * tile + sb * PROMPT_SUB,
                              gla_g_ref, pool_w_ref, pool_scale_ref, mix_ref, sb * PROMPT_SUB))
    st_ref[...] = carry["st"]
    hist_ref[...] = carry["hist"]

    @pl.when(t == pl.num_programs(1) - 1)
    def _():
        s_fin_ref[...] = carry["st"]
        hist_out_ref[...] = carry["hist"]


def _prompt_mixer(layer, x, p, stacked):
    B, T, _ = x.shape
    tile = PROMPT_TILE
    n_tiles = T // tile
    n_sub = tile // PROMPT_SUB
    assert n_sub % 2 == 0
    whole = lambda *shape: pl.BlockSpec((None,) + shape, lambda b, t: (layer,) + (0,) * len(shape))
    chained = stacked is not None
    operands = [x, x, p["n1"], p["w_main"], p["w_code"], p["w_u"], p["w_gate"], p["b_gate"], p["gla_g"], p["pool_w"],
                p["pool_scale"]]

    def next_first_sub_block(b, t):
        n = jnp.minimum(b * n_tiles + t + 1, B * n_tiles - 1)
        return (n // n_tiles, (n % n_tiles) * n_sub, 0)

    in_specs = [
        pl.BlockSpec((None, tile, D_MODEL), lambda b, t: (b, t, 0)),
        pl.BlockSpec((None, PROMPT_SUB, D_MODEL), next_first_sub_block),
        whole(1, D_MODEL),
        whole(D_MODEL, COL_U),
        whole(D_MODEL, LANES),
        whole(D_MODEL, POOL_WIDTH),
        whole(LANES, KEY_WIDTH),
        whole(1, KEY_WIDTH),
        whole(1, HEAD_V),
        whole(len(POOL_WINDOWS) // 2, POOL_PAIR, POOL_PAIR),
        whole(1, POOL_WIDTH),
    ]
    aliases = {}
    if chained:
        aliases = {len(operands): 1, len(operands) + 1: 2}
        operands += list(stacked)
        in_specs += [pl.BlockSpec(memory_space=pl.ANY)] * 2
    return pl.pallas_call(
        functools.partial(_prompt_mixer_kernel, chained=chained),
        grid=(B, n_tiles),
        in_specs=in_specs,
        out_specs=[
            pl.BlockSpec((None, tile, D_MODEL), lambda b, t: (b, t, 0)),
            pl.BlockSpec((None, None, KEY_WIDTH, HEAD_V), lambda b, t: (layer, b, 0, 0)),
            pl.BlockSpec((None, None, POOL_HIST, POOL_WIDTH), lambda b, t: (layer, b, 0, 0)),
        ],
        out_shape=[
            jax.ShapeDtypeStruct((B, T, D_MODEL), jnp.bfloat16),
            jax.ShapeDtypeStruct((DEPTH, B, KEY_WIDTH, HEAD_V), jnp.float32),
            jax.ShapeDtypeStruct((DEPTH, B, POOL_HIST, POOL_WIDTH), jnp.float32),
        ],
        scratch_shapes=[
            pltpu.VMEM((KEY_WIDTH, HEAD_V), jnp.float32),
            pltpu.VMEM((POOL_HIST, POOL_WIDTH), jnp.float32),
            pltpu.VMEM((2, PROMPT_SUB, Z_WIDTH), jnp.float32),
            pltpu.VMEM((2, PROMPT_SUB, KEY_WIDTH), jnp.float32),
        ],
        input_output_aliases=aliases,
        compiler_params=pltpu.CompilerParams(
            dimension_semantics=("arbitrary", "arbitrary"), vmem_limit_bytes=VMEM_LIMIT),
        name=f"prompt_mixer_l{layer}",
    )(*operands)


def _sample_mixer_kernel(*refs, chained):
    (x_ref, n1_ref, w_main_ref, w_code_ref, w_u_ref, w_gate_ref, b_gate_ref, gla_g_ref, pool_w_ref,
     pool_scale_ref, s_ref, pool_state_ref) = refs[:12]
    mix_ref, s_new_ref, pool_new_ref, dec_ref, ke_ref, qt_ref, v_ref, g_ref, o_ref = refs[-9:]
    i = pl.program_id(0)
    n_seq = x_ref.shape[0]
    seq_tile = s_ref.shape[0]

    @pl.when(i == 0)
    def _():
        qkvg, u, log_a = _project(x_ref[...], n1_ref, w_main_ref, w_code_ref, w_u_ref, w_gate_ref, b_gate_ref)
        q = qkvg[:, COL_Q:COL_K]
        k = qkvg[:, COL_K:COL_V]
        v = _bf16_round(qkvg[:, COL_V:COL_G])
        qt = _bf16_round(q * jnp.exp(log_a) * (HEAD_K ** -0.5))
        kt = _bf16_round(k * jnp.exp(-log_a))
        ke = _bf16_round(k * jnp.exp(log_a - log_a))
        prod = qt * kt
        o_intra = []
        for h, m in enumerate(_head_lane_mask(n_seq)):
            att = _bf16_round(jnp.sum(jnp.where(m, prod, 0.0), axis=-1, keepdims=True))
            o_intra.append(att * v[:, h * HEAD_V:(h + 1) * HEAD_V])
        o_ref[...] = jnp.concatenate(o_intra, axis=-1)
        dec_ref[...] = jnp.exp(log_a)
        ke_ref[...] = ke
        qt_ref[...] = qt
        v_ref[...] = v
        g_ref[...] = qkvg[:, COL_G:COL_U]

        pooled = []
        for gi, w in enumerate(POOL_WINDOWS):
            lanes = slice(gi * POOL_GROUP, (gi + 1) * POOL_GROUP)
            s = u[:, lanes]
            for j in range(POOL_BUF - (w - 1), POOL_BUF):
                s = s + pool_state_ref[j, :, lanes]
            pooled.append((s / float(w) - u[:, lanes]).astype(jnp.bfloat16))
        for pair in range(len(POOL_WINDOWS) // 2):
            lanes = slice(pair * POOL_PAIR, (pair + 1) * POOL_PAIR)
            both = jnp.concatenate(pooled[2 * pair:2 * pair + 2], axis=-1)
            pg = jnp.dot(both, pool_w_ref[pair], preferred_element_type=jnp.float32) * pool_scale_ref[:, lanes]
            mix_ref[:, GLA_WIDTH + pair * POOL_PAIR:GLA_WIDTH + (pair + 1) * POOL_PAIR] = pg.astype(mix_ref.dtype)
        for j in range(POOL_BUF - 1):
            pool_new_ref[j] = pool_state_ref[j + 1]
        pool_new_ref[POOL_BUF - 1] = u

    lane = lax.broadcasted_iota(jnp.int32, (SUBLANES, KEY_WIDTH), 1)
    head = lax.broadcasted_iota(jnp.int32, (SUBLANES, KEY_WIDTH), 0)
    own_head = lane // HEAD_K == head
    for sub in range(seq_tile // SUBLANES):
        local = slice(sub * SUBLANES, (sub + 1) * SUBLANES)
        rows = pl.ds(pl.multiple_of(i * seq_tile + sub * SUBLANES, SUBLANES), SUBLANES)
        s_old = s_ref[local]
        v = v_ref[rows, :]
        v_rows = jnp.concatenate(
            [jnp.broadcast_to(v[:, None, h * HEAD_V:(h + 1) * HEAD_V], (SUBLANES, HEAD_K, HEAD_V))
             for h in range(N_HEADS)], axis=1)
        s_new_ref[local] = dec_ref[rows, :][:, :, None] * s_old + ke_ref[rows, :][:, :, None] * v_rows
        q_rows = jnp.where(own_head[None], qt_ref[rows, :][:, None, :], 0.0).astype(jnp.bfloat16)
        o_inter = jnp.einsum("bhk,bkv->bhv", q_rows, s_old.astype(jnp.bfloat16),
                             preferred_element_type=jnp.float32)
        o_ref[rows, :] = o_ref[rows, :] + jnp.concatenate([o_inter[:, h, :] for h in range(N_HEADS)], axis=-1)

    @pl.when(i == pl.num_programs(0) - 1)
    def _():
        mix_ref[:, 0:GLA_WIDTH] = _gla_finish(o_ref[...], g_ref[...], gla_g_ref[...]).astype(mix_ref.dtype)


def _sample_mixer(layer, x, s_state, pool_state, p, stacked):
    n_seq = x.shape[0]
    seq_tile = SAMPLE_STATE_TILE
    whole = lambda *shape: pl.BlockSpec((None,) + shape, lambda i: (layer,) + (0,) * len(shape))
    fixed = lambda *shape: pl.BlockSpec(shape, lambda i: (0,) * len(shape))
    f32 = jnp.float32
    chained = stacked is not None
    operands = [x, p["n1"], p["w_main"], p["w_code"], p["w_u"], p["w_gate"], p["b_gate"], p["gla_g"], p["pool_w"],
                p["pool_scale"], s_state, pool_state]
    in_specs = [
        fixed(n_seq, D_MODEL),
        whole(1, D_MODEL),
        whole(D_MODEL, COL_U),
        whole(D_MODEL, LANES),
        whole(D_MODEL, POOL_WIDTH),
        whole(LANES, KEY_WIDTH),
        whole(1, KEY_WIDTH),
        whole(1, HEAD_V),
        whole(len(POOL_WINDOWS) // 2, POOL_PAIR, POOL_PAIR),
        whole(1, POOL_WIDTH),
        pl.BlockSpec((None, seq_tile, KEY_WIDTH, HEAD_V), lambda i: (layer, i, 0, 0)),
        whole(POOL_BUF, n_seq, POOL_WIDTH),
    ]
    aliases = {}
    if chained:
        aliases = {len(operands): 1, len(operands) + 1: 2}
        operands += list(stacked)
        in_specs += [pl.BlockSpec(memory_space=pl.ANY)] * 2
    return pl.pallas_call(
        functools.partial(_sample_mixer_kernel, chained=chained),
        grid=(n_seq // seq_tile,),
        in_specs=in_specs,
        out_specs=[
            fixed(n_seq, D_MODEL),
            pl.BlockSpec((None, seq_tile, KEY_WIDTH, HEAD_V), lambda i: (layer, i, 0, 0)),
            whole(POOL_BUF, n_seq, POOL_WIDTH),
        ],
        out_shape=[
            jax.ShapeDtypeStruct((n_seq, D_MODEL), jnp.bfloat16),
            jax.ShapeDtypeStruct((DEPTH, n_seq, KEY_WIDTH, HEAD_V), f32),
            jax.ShapeDtypeStruct((DEPTH, POOL_BUF, n_seq, POOL_WIDTH), f32),
        ],
        scratch_shapes=[
            pltpu.VMEM((n_seq, KEY_WIDTH), f32),
            pltpu.VMEM((n_seq, KEY_WIDTH), f32),
            pltpu.VMEM((n_seq, KEY_WIDTH), f32),
            pltpu.VMEM((n_seq, GLA_WIDTH), f32),
            pltpu.VMEM((n_seq, GLA_WIDTH), f32),
            pltpu.VMEM((n_seq, GLA_WIDTH), f32),
        ],
        input_output_aliases=aliases,
        compiler_params=pltpu.CompilerParams(
            dimension_semantics=("arbitrary",), vmem_limit_bytes=VMEM_LIMIT),
        name=f"sample_mixer_l{layer}",
    )(*operands)


N_FF_CHUNKS = D_FF // FF_CHUNK
N_WEIGHT_CHUNKS = 1 + 2 * N_FF_CHUNKS
assert FF_CHUNK == D_MODEL


def _channel_kernel(x_ref, mix_ref, xs_ref, mix_s_ref, n2_ref, final_g_ref, w_out_hbm, w_up_hbm, w_down_hbm,
                    y_ref, ys_ref, w_out_ref, w_up_ref, w_down_ref, stage_ref, sem, *, layer, final):
    step = pl.program_id(0)

    def chunk(k):
        if k == 0:
            return w_out_hbm.at[layer], w_out_ref
        j, down = divmod(k - 1, 2)
        span = pl.ds(j * FF_CHUNK, FF_CHUNK)
        if down:
            return w_down_hbm.at[layer, span, :], w_down_ref.at[span, :]
        return w_up_hbm.at[layer, :, span], w_up_ref.at[:, span]

    def chunk_copy(k):
        return pltpu.make_async_copy(chunk(k)[0], stage_ref.at[k % 2], sem.at[k % 2])

    def fetch(k):
        chunk_copy(k).wait()
        chunk(k)[1][...] = stage_ref[k % 2].astype(jnp.bfloat16)
        if k + 2 < N_WEIGHT_CHUNKS:
            chunk_copy(k + 2).start()

    def channel_mix(x, mix, fetch_weights):
        def before_matmul(m):
            if fetch_weights and m + 1 < N_WEIGHT_CHUNKS:
                fetch(m + 1)

        if fetch_weights:
            fetch(0)
        before_matmul(0)
        x1 = x + jnp.dot(mix, w_out_ref[...], preferred_element_type=jnp.float32)
        h2 = _rmsnorm(x1, n2_ref[...]).astype(jnp.bfloat16)
        acc = x1
        for j in range(N_FF_CHUNKS):
            cols = slice(j * FF_CHUNK, (j + 1) * FF_CHUNK)
            before_matmul(1 + 2 * j)
            hid = jnp.dot(h2, w_up_ref[:, cols], preferred_element_type=jnp.float32)
            act = jnp.square(jnp.maximum(hid, 0.0)).astype(jnp.bfloat16)
            before_matmul(2 + 2 * j)
            acc = acc + jnp.dot(act, w_down_ref[cols, :], preferred_element_type=jnp.float32)
        if final:
            acc = _rmsnorm(acc, final_g_ref[...])
        return acc

    @pl.when(step == 0)
    def _():
        chunk_copy(0).start()
        chunk_copy(1).start()
        y_ref[...] = channel_mix(x_ref[...], mix_ref[...], True)

    @pl.when(step > 0)
    def _():
        y_ref[...] = channel_mix(x_ref[...], mix_ref[...], False)

    @pl.when(step == pl.num_programs(0) - 1)
    def _():
        ys_ref[...] = channel_mix(xs_ref[...], mix_s_ref[...], False)


def _channel_mixer(layer, x, mix, xs, mix_s, p, final_g):
    rows = x.shape[0]
    n_seq = xs.shape[0]
    tile = CHANNEL_TILE
    assert rows // tile > 1
    bf16 = jnp.bfloat16
    return pl.pallas_call(
        functools.partial(_channel_kernel, layer=layer, final=layer == DEPTH - 1),
        grid=(rows // tile,),
        in_specs=[
            pl.BlockSpec((tile, D_MODEL), lambda i: (i, 0)),
            pl.BlockSpec((tile, D_MODEL), lambda i: (i, 0)),
            pl.BlockSpec((n_seq, D_MODEL), lambda i: (0, 0)),
            pl.BlockSpec((n_seq, D_MODEL), lambda i: (0, 0)),
            pl.BlockSpec((None, 1, D_MODEL), lambda i: (layer, 0, 0)),
            pl.BlockSpec((1, D_MODEL), lambda i: (0, 0)),
            pl.BlockSpec(memory_space=pl.ANY),
            pl.BlockSpec(memory_space=pl.ANY),
            pl.BlockSpec(memory_space=pl.ANY),
        ],
        out_specs=[
            pl.BlockSpec((tile, D_MODEL), lambda i: (i, 0)),
            pl.BlockSpec((n_seq, D_MODEL), lambda i: (0, 0)),
        ],
        out_shape=[
            jax.ShapeDtypeStruct((rows, D_MODEL), jnp.float32),
            jax.ShapeDtypeStruct((n_seq, D_MODEL), jnp.float32),
        ],
        scratch_shapes=[
            pltpu.VMEM((D_MODEL, D_MODEL), bf16),
            pltpu.VMEM((D_MODEL, D_FF), bf16),
            pltpu.VMEM((D_FF, D_MODEL), bf16),
            pltpu.VMEM((2, D_MODEL, D_MODEL), jnp.float32),
            pltpu.SemaphoreType.DMA((2,)),
        ],
        compiler_params=pltpu.CompilerParams(
            dimension_semantics=("arbitrary",), vmem_limit_bytes=VMEM_LIMIT),
        name=f"channel_mixer_l{layer}",
    )(x, mix, xs, mix_s, p["n2"], final_g, p["w_out"], p["w_up"], p["w_down"])


def kernel(x_prompt, x_sample, state_gla, state_pool, norm1_g, w_in, w_gate, b_gate, gla_norm_g, pool_w,
           pool_scale, w_out, norm2_g, w_up, w_down, final_g):
    B, T, _ = x_prompt.shape
    n_seq = x_sample.shape[0]
    bf16 = jnp.bfloat16

    w_code = jnp.pad(w_in[:, :, W_IN_GATE:W_IN_U], ((0, 0), (0, 0), (0, LANES - GATE_RANK)))
    pw = pool_w.reshape(DEPTH, len(POOL_WINDOWS) // 2, 2, POOL_GROUP, POOL_GROUP)
    zero = jnp.zeros_like(pw[:, :, 0])
    pool_pairs = jnp.concatenate([jnp.concatenate([pw[:, :, 0], zero], axis=-1),
                                  jnp.concatenate([zero, pw[:, :, 1]], axis=-1)], axis=-2)
    params = {
        "n1": norm1_g.reshape(DEPTH, 1, D_MODEL),
        "w_main": w_in[:, :, :W_IN_GATE].astype(bf16),
        "w_code": w_code.astype(bf16),
        "w_u": w_in[:, :, W_IN_U:].astype(bf16),
        "w_gate": jnp.pad(w_gate, ((0, 0), (0, LANES - GATE_RANK), (0, 0))).astype(bf16),
        "b_gate": b_gate.reshape(DEPTH, 1, KEY_WIDTH),
        "gla_g": gla_norm_g.reshape(DEPTH, 1, HEAD_V),
        "pool_w": pool_pairs.astype(bf16),
        "pool_scale": pool_scale.reshape(DEPTH, 1, POOL_WIDTH),
        "w_out": w_out,
        "n2": norm2_g.reshape(DEPTH, 1, D_MODEL),
        "w_up": w_up,
        "w_down": w_down,
    }
    final_g2 = final_g.reshape(1, D_MODEL)
    s_state = state_gla.reshape(DEPTH, n_seq, KEY_WIDTH, HEAD_V)
    pool_rows = jnp.swapaxes(state_pool, 1, 2)

    xp = x_prompt
    xs = x_sample.reshape(n_seq, D_MODEL)
    prompt_states = sample_states = None
    for layer in range(DEPTH):
        mix_p, *prompt_states = _prompt_mixer(layer, xp, params, prompt_states)
        mix_s, *sample_states = _sample_mixer(layer, xs, s_state, pool_rows, params, sample_states)
        xp, xs = _channel_mixer(layer, xp.reshape(B * T, D_MODEL), mix_p.reshape(B * T, D_MODEL), xs, mix_s,
                                params, final_g2)
        xp = xp.reshape(B, T, D_MODEL)
    gla_p, hist_p = prompt_states
    gla_s, pool_s = sample_states
    return (xp, xs.reshape(n_seq, 1, D_MODEL),
            gla_p.reshape(DEPTH, B, N_HEADS, HEAD_K, HEAD_V), hist_p[:, :, POOL_HIST - POOL_BUF:],
            gla_s.reshape(DEPTH, n_seq, N_HEADS, HEAD_K, HEAD_V), jnp.swapaxes(pool_s, 1, 2))
```

```python
import functools

import jax
import jax.numpy as jnp
from jax import lax
from jax.experimental import pallas as pl
from jax.experimental.pallas import tpu as pltpu

D_MODEL = 1024
DEPTH = 4
N_HEADS = 4
HEAD_K = 64
HEAD_V = 128
KEY_WIDTH = N_HEADS * HEAD_K
GLA_WIDTH = N_HEADS * HEAD_V
GATE_RANK = 16
GATE_TEMP = 16.0
CHUNK = 64
POOL_WIDTH = 512
POOL_WINDOWS = (2, 4, 8, 16)
POOL_GROUP = 128
POOL_PAIR = 2 * POOL_GROUP
POOL_BUF = 15
POOL_HIST = 16
D_FF = 4 * D_MODEL
FF_CHUNK = 1024
EPS = 1e-6
LANES = 128
SUBLANES = 8

COL_Q = 0
COL_K = KEY_WIDTH
COL_V = 2 * KEY_WIDTH
COL_G = COL_V + GLA_WIDTH
COL_U = COL_G + GLA_WIDTH
Z_WIDTH = COL_U + POOL_WIDTH
W_IN_GATE = COL_U
W_IN_U = COL_U + GATE_RANK

PROMPT_TILE = 1024
PROMPT_SUB = 128
PROJ_PIECE = 256
CHANNEL_TILE = 512
VMEM_LIMIT = 56 * 1024 * 1024


def _rmsnorm(x, g):
    return x * lax.rsqrt(jnp.mean(x * x, axis=-1, keepdims=True) + EPS) * g


def _log_sigmoid(x):
    return jnp.minimum(x, 0.0) - jnp.log(1.0 + jnp.exp(-jnp.abs(x)))


def _bf16_round(x):
    return x.astype(jnp.bfloat16).astype(jnp.float32)


def _head_lane_mask(rows):
    lane = lax.broadcasted_iota(jnp.int32, (rows, KEY_WIDTH), 1)
    return [(lane >= h * HEAD_K) & (lane < (h + 1) * HEAD_K) for h in range(N_HEADS)]


def _gate_log_decay(a_low, w_gate_ref, b_gate_ref):
    pre = jnp.dot(a_low.astype(jnp.bfloat16), w_gate_ref[...], preferred_element_type=jnp.float32) + b_gate_ref[...]
    return _log_sigmoid(pre) / GATE_TEMP


def _project(x, n1_ref, w_main_ref, w_code_ref, w_u_ref, w_gate_ref, b_gate_ref):
    h = _rmsnorm(x, n1_ref[...]).astype(jnp.bfloat16)
    qkvg = jnp.dot(h, w_main_ref[...], preferred_element_type=jnp.float32)
    a_low = jnp.dot(h, w_code_ref[...], preferred_element_type=jnp.float32)
    u = jnp.dot(h, w_u_ref[...], preferred_element_type=jnp.float32)
    return qkvg, u, _gate_log_decay(a_low, w_gate_ref, b_gate_ref)


def _gla_finish(o, g, gla_g):
    outs = []
    for h in range(N_HEADS):
        sl = slice(h * HEAD_V, (h + 1) * HEAD_V)
        outs.append(_rmsnorm(o[:, sl], gla_g) * (g[:, sl] * jax.nn.sigmoid(g[:, sl])))
    return jnp.concatenate(outs, axis=-1)


def _trace_interleaved(a, b):
    i = j = 0
    while i < len(a) or j < len(b):
        if j >= len(b) or (i < len(a) and i * len(b) <= j * len(a)):
            a[i]()
            i += 1
        else:
            b[j]()
            j += 1


def _prompt_project_tasks(load_x, slot, n1_ref, w_main_ref, w_code_ref, w_u_ref, w_gate_ref, b_gate_ref,
                          z_ref, b_ref):
    v = {}

    def norm():
        v["h"] = _rmsnorm(load_x(), n1_ref[...]).astype(jnp.bfloat16)

    def piece(lo, hi):
        w_ref, first = (w_main_ref, 0) if hi <= COL_U else (w_u_ref, COL_U)

        def run():
            z_ref[slot, :, lo:hi] = jnp.dot(v["h"], w_ref[:, lo - first:hi - first],
                                            preferred_element_type=jnp.float32)
        return run

    def gate_code():
        v["a_low"] = jnp.dot(v["h"], w_code_ref[...], preferred_element_type=jnp.float32)

    def gate_pre():
        v["pre"] = (jnp.dot(v["a_low"].astype(jnp.bfloat16), w_gate_ref[...], preferred_element_type=jnp.float32)
                    + b_gate_ref[...])

    def log_decay():
        v["b"] = _log_sigmoid(v["pre"]) / GATE_TEMP

    def scan(shifts, last):
        def run():
            b = v["b"]
            row = lax.broadcasted_iota(jnp.int32, b.shape, 0) % CHUNK
            for shift in shifts:
                b = b + jnp.where(row >= shift, pltpu.roll(b, shift, axis=0), 0.0)
            v["b"] = b
            if last:
                b_ref[slot] = b
        return run

    half = PROJ_PIECE
    return [
        norm, gate_code,
        piece(COL_Q, COL_K), gate_pre,
        piece(COL_K, COL_V), log_decay,
        piece(COL_V, COL_V + half), scan((1, 2), False),
        piece(COL_V + half, COL_G), scan((4, 8), False),
        piece(COL_G, COL_G + half), scan((16, 32), True),
        piece(COL_G + half, COL_U), piece(COL_U, COL_U + half), piece(COL_U + half, Z_WIDTH),
    ]


def _prompt_mix_tasks(slot, z_ref, b_ref, carry, first_pos, gla_g_ref, pool_w_ref, pool_scale_ref, mix_ref, base):
    head_mask = _head_lane_mask(CHUNK)
    scale = HEAD_K ** -0.5
    sub = z_ref.shape[1]
    n_chunks = sub // CHUNK
    shared = {}

    def transpose_keys():
        b = b_ref[slot]
        k = z_ref[slot, :, COL_K:COL_V]
        last = [b[(c + 1) * CHUNK - 1:(c + 1) * CHUNK] for c in range(n_chunks)]
        b_last = jnp.concatenate([jnp.broadcast_to(row, (CHUNK, KEY_WIDTH)) for row in last], axis=0)
        shared["kt_t"] = (k * jnp.exp(-b)).T.astype(jnp.bfloat16)
        shared["ke_t"] = (k * jnp.exp(b_last - b)).T.astype(jnp.bfloat16)
        shared["v"] = z_ref[slot, :, COL_V:COL_G].astype(jnp.bfloat16)
        decay = jnp.exp(jnp.concatenate(last + [jnp.zeros((SUBLANES - n_chunks, KEY_WIDTH), jnp.float32)], axis=0))
        shared["decay"] = jnp.broadcast_to(decay[:, :, None], (SUBLANES, KEY_WIDTH, HEAD_V))

    def chunk_tasks(c):
        rows = slice(c * CHUNK, (c + 1) * CHUNK)
        out_rows = slice(base + c * CHUNK, base + (c + 1) * CHUNK)
        token = lax.broadcasted_iota(jnp.int32, (CHUNK, sub), 1) - c * CHUNK
        causal = (token >= 0) & (token <= lax.broadcasted_iota(jnp.int32, (CHUNK, sub), 0))
        in_chunk = (lax.broadcasted_iota(jnp.int32, (KEY_WIDTH, sub), 1) // CHUNK) == c
        v = {}

        def prepare():
            qt = z_ref[slot, rows, COL_Q:COL_K] * jnp.exp(b_ref[slot, rows, :]) * scale
            v["q_stack"] = jnp.concatenate([jnp.where(m, qt, 0.0) for m in head_mask], axis=0).astype(jnp.bfloat16)

        def update_product():
            ke_t = jnp.where(in_chunk, shared["ke_t"], jnp.zeros_like(shared["ke_t"]))
            v["upd"] = jnp.concatenate(
                [jnp.dot(ke_t[h * HEAD_K:(h + 1) * HEAD_K], shared["v"][:, h * HEAD_V:(h + 1) * HEAD_V],
                         preferred_element_type=jnp.float32) for h in range(N_HEADS)], axis=0)

        def query_product():
            rhs = jnp.concatenate([carry["st"].astype(jnp.bfloat16), shared["kt_t"]], axis=1)
            v["r"] = jnp.dot(v["q_stack"], rhs, preferred_element_type=jnp.float32)

        def update_state():
            carry["st"] = shared["decay"][c] * carry["st"] + v["upd"]

        def intra():
            r = v["r"]
            outs = []
            for h in range(N_HEADS):
                hr = slice(h * CHUNK, (h + 1) * CHUNK)
                att = jnp.where(causal, r[hr, HEAD_V:HEAD_V + sub], 0.0).astype(jnp.bfloat16)
                o_intra = jnp.dot(att, shared["v"][:, h * HEAD_V:(h + 1) * HEAD_V], preferred_element_type=jnp.float32)
                outs.append(o_intra + r[hr, 0:HEAD_V])
            v["o"] = jnp.concatenate(outs, axis=-1)

        def finish():
            g_c = z_ref[slot, rows, COL_G:COL_U]
            mix_ref[out_rows, 0:GLA_WIDTH] = _gla_finish(v["o"], g_c, gla_g_ref[...]).astype(mix_ref.dtype)

        return dict(prepare=prepare, update_product=update_product, query_product=query_product,
                    update_state=update_state, intra=intra, finish=finish)

    pooled = {}

    def window_mean_task(gi, w):
        lanes = slice(gi * POOL_GROUP, (gi + 1) * POOL_GROUP)

        def window_mean():
            u = z_ref[slot, :, COL_U + gi * POOL_GROUP:COL_U + (gi + 1) * POOL_GROUP]
            s = jnp.concatenate([carry["hist"][:, lanes], u], axis=0)
            shift = 1
            while shift < w:
                s = s + pltpu.roll(s, shift, axis=0)
                shift *= 2
            seen = first_pos + 1 + lax.broadcasted_iota(jnp.int32, (sub, POOL_GROUP), 0)
            if "inv_seen" not in pooled:
                pooled["inv_seen"] = 1.0 / seen.astype(jnp.float32)
            inv_count = jnp.where(seen < w, pooled["inv_seen"], 1.0 / w)
            pooled[gi] = (s[POOL_HIST:] * inv_count - u).astype(jnp.bfloat16)

        return window_mean

    def group_map_task(pair):
        lanes = slice(pair * POOL_PAIR, (pair + 1) * POOL_PAIR)

        def group_map():
            both = jnp.concatenate([pooled[2 * pair], pooled[2 * pair + 1]], axis=-1)
            pg = jnp.dot(both, pool_w_ref[pair], preferred_element_type=jnp.float32)
            pg = pg * pool_scale_ref[:, lanes]
            mix_ref[base:base + sub, GLA_WIDTH + pair * POOL_PAIR:GLA_WIDTH + (pair + 1) * POOL_PAIR] = (
                pg.astype(mix_ref.dtype))

        return group_map

    def keep_history():
        carry["hist"] = z_ref[slot, sub - POOL_HIST:sub, COL_U:Z_WIDTH]

    assert n_chunks == 2
    c0, c1 = chunk_tasks(0), chunk_tasks(1)
    means = [window_mean_task(gi, w) for gi, w in enumerate(POOL_WINDOWS)]
    maps = [group_map_task(pair) for pair in range(len(POOL_WINDOWS) // 2)]
    return [
        transpose_keys, c0["prepare"], c1["prepare"], c0["update_product"], c0["query_product"], c1["update_product"],
        c0["update_state"], c1["query_product"], means[0], c0["intra"], c1["update_state"],
        means[1], c1["intra"], maps[0], c0["finish"], means[2], c1["finish"], means[3], maps[1], keep_history,
    ]


def _sample_project(xs_ref, n1_ref, w_main_ref, w_code_ref, w_u_ref, w_gate_ref, b_gate_ref, pool_w_ref, pool_scale_ref,
                    pool_state_ref, mix_s_ref, pool_new_ref, dec_ref, ke_ref, qt_ref, v_ref, g_ref, o_ref):
    n_seq = xs_ref.shape[0]
    qkvg, u, log_a = _project(xs_ref[...], n1_ref, w_main_ref, w_code_ref, w_u_ref, w_gate_ref, b_gate_ref)
    q = qkvg[:, COL_Q:COL_K]
    k = qkvg[:, COL_K:COL_V]
    v = _bf16_round(qkvg[:, COL_V:COL_G])
    qt = _bf16_round(q * jnp.exp(log_a) * (HEAD_K ** -0.5))
    kt = _bf16_round(k * jnp.exp(-log_a))
    ke = _bf16_round(k * jnp.exp(log_a - log_a))
    prod = qt * kt
    o_intra = []
    for h, m in enumerate(_head_lane_mask(n_seq)):
        att = _bf16_round(jnp.sum(jnp.where(m, prod, 0.0), axis=-1, keepdims=True))
        o_intra.append(att * v[:, h * HEAD_V:(h + 1) * HEAD_V])
    o_ref[...] = jnp.concatenate(o_intra, axis=-1)
    dec_ref[...] = jnp.exp(log_a)
    ke_ref[...] = ke
    qt_ref[...] = qt
    v_ref[...] = v
    g_ref[...] = qkvg[:, COL_G:COL_U]

    pooled = []
    for gi, w in enumerate(POOL_WINDOWS):
        lanes = slice(gi * POOL_GROUP, (gi + 1) * POOL_GROUP)
        s = u[:, lanes]
        for j in range(POOL_BUF - (w - 1), POOL_BUF):
            s = s + pool_state_ref[j, :, lanes]
        pooled.append((s / float(w) - u[:, lanes]).astype(jnp.bfloat16))
    for pair in range(len(POOL_WINDOWS) // 2):
        lanes = slice(pair * POOL_PAIR, (pair + 1) * POOL_PAIR)
        both = jnp.concatenate(pooled[2 * pair:2 * pair + 2], axis=-1)
        pg = jnp.dot(both, pool_w_ref[pair], preferred_element_type=jnp.float32) * pool_scale_ref[:, lanes]
        mix_s_ref[:, GLA_WIDTH + pair * POOL_PAIR:GLA_WIDTH + (pair + 1) * POOL_PAIR] = pg.astype(mix_s_ref.dtype)
    for j in range(POOL_BUF - 1):
        pool_new_ref[j] = pool_state_ref[j + 1]
    pool_new_ref[POOL_BUF - 1] = u


def _sample_state_tasks(first_seq, s_ref, s_new_ref, dec_ref, ke_ref, qt_ref, v_ref, o_ref):
    rows = pl.ds(pl.multiple_of(first_seq, SUBLANES), SUBLANES)

    def update_state():
        v = v_ref[rows, :]
        v_rows = jnp.concatenate(
            [jnp.broadcast_to(v[:, None, h * HEAD_V:(h + 1) * HEAD_V], (SUBLANES, HEAD_K, HEAD_V))
             for h in range(N_HEADS)], axis=1)
        s_new_ref[...] = dec_ref[rows, :][:, :, None] * s_ref[...] + ke_ref[rows, :][:, :, None] * v_rows

    def query_state():
        lane = lax.broadcasted_iota(jnp.int32, (SUBLANES, KEY_WIDTH), 1)
        head = lax.broadcasted_iota(jnp.int32, (SUBLANES, KEY_WIDTH), 0)
        own_head = lane // HEAD_K == head
        q_rows = jnp.where(own_head[None], qt_ref[rows, :][:, None, :], 0.0).astype(jnp.bfloat16)
        o_inter = jnp.einsum("bhk,bkv->bhv", q_rows, s_ref[...].astype(jnp.bfloat16),
                             preferred_element_type=jnp.float32)
        o_ref[rows, :] = o_ref[rows, :] + jnp.concatenate([o_inter[:, h, :] for h in range(N_HEADS)], axis=-1)

    return [update_state, query_state]


def _mixer_kernel(*refs, chained):
    (x_ref, x_next_ref, n1_ref, w_main_ref, w_code_ref, w_u_ref, w_gate_ref, b_gate_ref, gla_g_ref, pool_w_ref,
     pool_scale_ref, xs_ref, s_ref, pool_state_ref) = refs[:14]
    (mix_ref, s_fin_ref, hist_out_ref, mix_s_ref, s_new_ref, pool_new_ref,
     st_ref, hist_ref, z_ref, b_ref, dec_ref, ke_ref, qt_ref, v_ref, g_ref, o_ref) = refs[-16:]
    tile = x_ref.shape[0]
    n_sub = tile // PROMPT_SUB
    t = pl.program_id(1)
    step = pl.program_id(0) * pl.num_programs(1) + t
    last_step = pl.num_programs(0) * pl.num_programs(1) - 1

    def project_tasks(load_x, slot):
        return _prompt_project_tasks(load_x, slot, n1_ref, w_main_ref, w_code_ref, w_u_ref, w_gate_ref, b_gate_ref,
                                     z_ref, b_ref)

    @pl.when(t == 0)
    def _():
        st_ref[...] = jnp.zeros_like(st_ref)
        hist_ref[...] = jnp.zeros_like(hist_ref)

    @pl.when(step == 0)
    def _():
        for task in project_tasks(lambda: x_ref[0:PROMPT_SUB], 0):
            task()
        _sample_project(xs_ref, n1_ref, w_main_ref, w_code_ref, w_u_ref, w_gate_ref, b_gate_ref, pool_w_ref,
                        pool_scale_ref, pool_state_ref, mix_s_ref, pool_new_ref, dec_ref, ke_ref, qt_ref, v_ref, g_ref,
                        o_ref)

    sample_tasks = _sample_state_tasks(step * SUBLANES, s_ref, s_new_ref, dec_ref, ke_ref, qt_ref, v_ref, o_ref)
    carry = {"st": st_ref[...], "hist": hist_ref[...]}
    for sb in range(n_sub):
        if sb + 1 < n_sub:
            load_next = lambda sb=sb: x_ref[(sb + 1) * PROMPT_SUB:(sb + 2) * PROMPT_SUB]
        else:
            load_next = lambda: x_next_ref[...]
        mix_tasks = _prompt_mix_tasks(sb % 2, z_ref, b_ref, carry, t * tile + sb * PROMPT_SUB,
                                      gla_g_ref, pool_w_ref, pool_scale_ref, mix_ref, sb * PROMPT_SUB)
        if sample_tasks and sb >= 1:
            mix_tasks.insert(len(mix_tasks) // 2, sample_tasks.pop(0))
        _trace_interleaved(project_tasks(load_next, (sb + 1) % 2), mix_tasks)
    st_ref[...] = carry["st"]
    hist_ref[...] = carry["hist"]

    @pl.when(t == pl.num_programs(1) - 1)
    def _():
        s_fin_ref[...] = carry["st"]
        hist_out_ref[...] = carry["hist"]

    @pl.when(step == last_step)
    def _():
        mix_s_ref[:, 0:GLA_WIDTH] = _gla_finish(o_ref[...], g_ref[...], gla_g_ref[...]).astype(mix_s_ref.dtype)


def _mixer(layer, x, xs, s_state, pool_state, p, stacked):
    B, T, _ = x.shape
    n_seq = xs.shape[0]
    tile = PROMPT_TILE
    n_tiles = T // tile
    n_sub = tile // PROMPT_SUB
    assert n_sub % 2 == 0
    assert n_sub >= 3 and B * n_tiles * SUBLANES == n_seq
    f32 = jnp.float32
    whole = lambda *shape: pl.BlockSpec((None,) + shape, lambda b, t: (layer,) + (0,) * len(shape))
    fixed = lambda *shape: pl.BlockSpec(shape, lambda b, t: (0,) * len(shape))
    chained = stacked is not None
    operands = [x, x, p["n1"], p["w_main"], p["w_code"], p["w_u"], p["w_gate"], p["b_gate"], p["gla_g"], p["pool_w"],
                p["pool_scale"], xs, s_state, pool_state]

    def next_first_sub_block(b, t):
        n = jnp.minimum(b * n_tiles + t + 1, B * n_tiles - 1)
        return (n // n_tiles, (n % n_tiles) * n_sub, 0)

    sample_state_block = pl.BlockSpec((None, SUBLANES, KEY_WIDTH, HEAD_V), lambda b, t: (layer, b * n_tiles + t, 0, 0))
    in_specs = [
        pl.BlockSpec((None, tile, D_MODEL), lambda b, t: (b, t, 0)),
        pl.BlockSpec((None, PROMPT_SUB, D_MODEL), next_first_sub_block),
        whole(1, D_MODEL),
        whole(D_MODEL, COL_U),
        whole(D_MODEL, LANES),
        whole(D_MODEL, POOL_WIDTH),
        whole(LANES, KEY_WIDTH),
        whole(1, KEY_WIDTH),
        whole(1, HEAD_V),
        whole(len(POOL_WINDOWS) // 2, POOL_PAIR, POOL_PAIR),
        whole(1, POOL_WIDTH),
        fixed(n_seq, D_MODEL),
        sample_state_block,
        whole(POOL_BUF, n_seq, POOL_WIDTH),
    ]
    aliases = {}
    if chained:
        aliases = {len(operands): 1, len(operands) + 1: 2, len(operands) + 2: 4, len(operands) + 3: 5}
        operands += list(stacked)
        in_specs += [pl.BlockSpec(memory_space=pl.ANY)] * 4
    return pl.pallas_call(
        functools.partial(_mixer_kernel, chained=chained),
        grid=(B, n_tiles),
        in_specs=in_specs,
        out_specs=[
            pl.BlockSpec((None, tile, D_MODEL), lambda b, t: (b, t, 0)),
            pl.BlockSpec((None, None, KEY_WIDTH, HEAD_V), lambda b, t: (layer, b, 0, 0)),
            pl.BlockSpec((None, None, POOL_HIST, POOL_WIDTH), lambda b, t: (layer, b, 0, 0)),
            fixed(n_seq, D_MODEL),
            sample_state_block,
            whole(POOL_BUF, n_seq, POOL_WIDTH),
        ],
        out_shape=[
            jax.ShapeDtypeStruct((B, T, D_MODEL), jnp.bfloat16),
            jax.ShapeDtypeStruct((DEPTH, B, KEY_WIDTH, HEAD_V), f32),
            jax.ShapeDtypeStruct((DEPTH, B, POOL_HIST, POOL_WIDTH), f32),
            jax.ShapeDtypeStruct((n_seq, D_MODEL), jnp.bfloat16),
            jax.ShapeDtypeStruct((DEPTH, n_seq, KEY_WIDTH, HEAD_V), f32),
            jax.ShapeDtypeStruct((DEPTH, POOL_BUF, n_seq, POOL_WIDTH), f32),
        ],
        scratch_shapes=[
            pltpu.VMEM((KEY_WIDTH, HEAD_V), f32),
            pltpu.VMEM((POOL_HIST, POOL_WIDTH), f32),
            pltpu.VMEM((2, PROMPT_SUB, Z_WIDTH), f32),
            pltpu.VMEM((2, PROMPT_SUB, KEY_WIDTH), f32),
            pltpu.VMEM((n_seq, KEY_WIDTH), f32),
            pltpu.VMEM((n_seq, KEY_WIDTH), f32),
            pltpu.VMEM((n_seq, KEY_WIDTH), f32),
            pltpu.VMEM((n_seq, GLA_WIDTH), f32),
            pltpu.VMEM((n_seq, GLA_WIDTH), f32),
            pltpu.VMEM((n_seq, GLA_WIDTH), f32),
        ],
        input_output_aliases=aliases,
        compiler_params=pltpu.CompilerParams(
            dimension_semantics=("arbitrary", "arbitrary"), vmem_limit_bytes=VMEM_LIMIT),
        name=f"mixer_l{layer}",
    )(*operands)


N_FF_CHUNKS = D_FF // FF_CHUNK
N_WEIGHT_CHUNKS = 1 + 2 * N_FF_CHUNKS
assert FF_CHUNK == D_MODEL


def _channel_kernel(x_ref, mix_ref, xs_ref, mix_s_ref, n2_ref, final_g_ref, w_out_hbm, w_up_hbm, w_down_hbm,
                    y_ref, ys_ref, w_out_ref, w_up_ref, w_down_ref, stage_ref, sem, *, layer, final):
    step = pl.program_id(0)

    def chunk(k):
        if k == 0:
            return w_out_hbm.at[layer], w_out_ref
        j, down = divmod(k - 1, 2)
        span = pl.ds(j * FF_CHUNK, FF_CHUNK)
        if down:
            return w_down_hbm.at[layer, span, :], w_down_ref.at[span, :]
        return w_up_hbm.at[layer, :, span], w_up_ref.at[:, span]

    def chunk_copy(k):
        return pltpu.make_async_copy(chunk(k)[0], stage_ref.at[k % 2], sem.at[k % 2])

    def fetch(k):
        chunk_copy(k).wait()
        chunk(k)[1][...] = stage_ref[k % 2].astype(jnp.bfloat16)
        if k + 2 < N_WEIGHT_CHUNKS:
            chunk_copy(k + 2).start()

    def channel_mix(x, mix, fetch_weights):
        def before_matmul(m):
            if fetch_weights and m + 1 < N_WEIGHT_CHUNKS:
                fetch(m + 1)

        if fetch_weights:
            fetch(0)
        before_matmul(0)
        x1 = x + jnp.dot(mix, w_out_ref[...], preferred_element_type=jnp.float32)
        h2 = _rmsnorm(x1, n2_ref[...]).astype(jnp.bfloat16)
        acc = x1
        for j in range(N_FF_CHUNKS):
            cols = slice(j * FF_CHUNK, (j + 1) * FF_CHUNK)
            before_matmul(1 + 2 * j)
            hid = jnp.dot(h2, w_up_ref[:, cols], preferred_element_type=jnp.float32)
            act = jnp.square(jnp.maximum(hid, 0.0)).astype(jnp.bfloat16)
            before_matmul(2 + 2 * j)
            acc = acc + jnp.dot(act, w_down_ref[cols, :], preferred_element_type=jnp.float32)
        if final:
            acc = _rmsnorm(acc, final_g_ref[...])
        return acc

    @pl.when(step == 0)
    def _():
        chunk_copy(0).start()
        chunk_copy(1).start()
        y_ref[...] = channel_mix(x_ref[...], mix_ref[...], True)

    @pl.when(step > 0)
    def _():
        y_ref[...] = channel_mix(x_ref[...], mix_ref[...], False)

    @pl.when(step == pl.num_programs(0) - 1)
    def _():
        ys_ref[...] = channel_mix(xs_ref[...], mix_s_ref[...], False)


def _channel_mixer(layer, x, mix, xs, mix_s, p, final_g):
    rows = x.shape[0]
    n_seq = xs.shape[0]
    tile = CHANNEL_TILE
    assert rows // tile > 1
    bf16 = jnp.bfloat16
    return pl.pallas_call(
        functools.partial(_channel_kernel, layer=layer, final=layer == DEPTH - 1),
        grid=(rows // tile,),
        in_specs=[
            pl.BlockSpec((tile, D_MODEL), lambda i: (i, 0)),
            pl.BlockSpec((tile, D_MODEL), lambda i: (i, 0)),
            pl.BlockSpec((n_seq, D_MODEL), lambda i: (0, 0)),
            pl.BlockSpec((n_seq, D_MODEL), lambda i: (0, 0)),
            pl.BlockSpec((None, 1, D_MODEL), lambda i: (layer, 0, 0)),
            pl.BlockSpec((1, D_MODEL), lambda i: (0, 0)),
            pl.BlockSpec(memory_space=pl.ANY),
            pl.BlockSpec(memory_space=pl.ANY),
            pl.BlockSpec(memory_space=pl.ANY),
        ],
        out_specs=[
            pl.BlockSpec((tile, D_MODEL), lambda i: (i, 0)),
            pl.BlockSpec((n_seq, D_MODEL), lambda i: (0, 0)),
        ],
        out_shape=[
            jax.ShapeDtypeStruct((rows, D_MODEL), jnp.float32),
            jax.ShapeDtypeStruct((n_seq, D_MODEL), jnp.float32),
        ],
        scratch_shapes=[
            pltpu.VMEM((D_MODEL, D_MODEL), bf16),
            pltpu.VMEM((D_MODEL, D_FF), bf16),
            pltpu.VMEM((D_FF, D_MODEL), bf16),
            pltpu.VMEM((2, D_MODEL, D_MODEL), jnp.float32),
            pltpu.SemaphoreType.DMA((2,)),
        ],
        compiler_params=pltpu.CompilerParams(
            dimension_semantics=("arbitrary",), vmem_limit_bytes=VMEM_LIMIT),
        name=f"channel_mixer_l{layer}",
    )(x, mix, xs, mix_s, p["n2"], final_g, p["w_out"], p["w_up"], p["w_down"])


def kernel(x_prompt, x_sample, state_gla, state_pool, norm1_g, w_in, w_gate, b_gate, gla_norm_g, pool_w,
           pool_scale, w_out, norm2_g, w_up, w_down, final_g):
    B, T, _ = x_prompt.shape
    n_seq = x_sample.shape[0]
    bf16 = jnp.bfloat16

    w_code = jnp.pad(w_in[:, :, W_IN_GATE:W_IN_U], ((0, 0), (0, 0), (0, LANES - GATE_RANK)))
    pw = pool_w.reshape(DEPTH, len(POOL_WINDOWS) // 2, 2, POOL_GROUP, POOL_GROUP)
    zero = jnp.zeros_like(pw[:, :, 0])
    pool_pairs = jnp.concatenate([jnp.concatenate([pw[:, :, 0], zero], axis=-1),
                                  jnp.concatenate([zero, pw[:, :, 1]], axis=-1)], axis=-2)
    params = {
        "n1": norm1_g.reshape(DEPTH, 1, D_MODEL),
        "w_main": w_in[:, :, :W_IN_GATE].astype(bf16),
        "w_code": w_code.astype(bf16),
        "w_u": w_in[:, :, W_IN_U:].astype(bf16),
        "w_gate": jnp.pad(w_gate, ((0, 0), (0, LANES - GATE_RANK), (0, 0))).astype(bf16),
        "b_gate": b_gate.reshape(DEPTH, 1, KEY_WIDTH),
        "gla_g": gla_norm_g.reshape(DEPTH, 1, HEAD_V),
        "pool_w": pool_pairs.astype(bf16),
        "pool_scale": pool_scale.reshape(DEPTH, 1, POOL_WIDTH),
        "w_out": w_out,
        "n2": norm2_g.reshape(DEPTH, 1, D_MODEL),
        "w_up": w_up,
        "w_down": w_down,
    }
    final_g2 = final_g.reshape(1, D_MODEL)
    s_state = state_gla.reshape(DEPTH, n_seq, KEY_WIDTH, HEAD_V)
    pool_rows = jnp.swapaxes(state_pool, 1, 2)

    xp = x_prompt
    xs = x_sample.reshape(n_seq, D_MODEL)
    states = None
    for layer in range(DEPTH):
        mix_p, gla_p, hist_p, mix_s, gla_s, pool_s = _mixer(layer, xp, xs, s_state, pool_rows, params, states)
        states = (gla_p, hist_p, gla_s, pool_s)
        xp, xs = _channel_mixer(layer, xp.reshape(B * T, D_MODEL), mix_p.reshape(B * T, D_MODEL), xs, mix_s,
                                params, final_g2)
        xp = xp.reshape(B, T, D_MODEL)
    return (xp, xs.reshape(n_seq, 1, D_MODEL),
            gla_p.reshape(DEPTH, B, N_HEADS, HEAD_K, HEAD_V), hist_p[:, :, POOL_HIST - POOL_BUF:],
            gla_s.reshape(DEPTH, n_seq, N_HEADS, HEAD_K, HEAD_V), jnp.swapaxes(pool_s, 1, 2))
```

```python
import functools

import jax
import jax.numpy as jnp
from jax import lax
from jax.experimental import pallas as pl
from jax.experimental.pallas import tpu as pltpu

D_MODEL = 1024
DEPTH = 4
N_HEADS = 4
HEAD_K = 64
HEAD_V = 128
KEY_WIDTH = N_HEADS * HEAD_K
GLA_WIDTH = N_HEADS * HEAD_V
GATE_RANK = 16
GATE_TEMP = 16.0
CHUNK = 64
POOL_WIDTH = 512
POOL_WINDOWS = (2, 4, 8, 16)
POOL_GROUP = 128
POOL_PAIR = 2 * POOL_GROUP
POOL_BUF = 15
POOL_HIST = 16
D_FF = 4 * D_MODEL
FF_CHUNK = 1024
EPS = 1e-6
LANES = 128
SUBLANES = 8

COL_Q = 0
COL_K = KEY_WIDTH
COL_V = 2 * KEY_WIDTH
COL_G = COL_V + GLA_WIDTH
COL_U = COL_G + GLA_WIDTH
Z_WIDTH = COL_U + POOL_WIDTH
W_IN_GATE = COL_U
W_IN_U = COL_U + GATE_RANK

PROMPT_TILE = 1024
PROMPT_SUB = 256
GLA_PAIR = 2 * CHUNK
PROJ_PIECE = 256
CHANNEL_TILE = 512
VMEM_LIMIT = 56 * 1024 * 1024


def _rmsnorm(x, g):
    return x * lax.rsqrt(jnp.mean(x * x, axis=-1, keepdims=True) + EPS) * g


def _log_sigmoid(x):
    return jnp.minimum(x, 0.0) - jnp.log(1.0 + jnp.exp(-jnp.abs(x)))


def _bf16_round(x):
    return x.astype(jnp.bfloat16).astype(jnp.float32)


def _head_lane_mask(rows):
    lane = lax.broadcasted_iota(jnp.int32, (rows, KEY_WIDTH), 1)
    return [(lane >= h * HEAD_K) & (lane < (h + 1) * HEAD_K) for h in range(N_HEADS)]


def _gate_log_decay(a_low, w_gate_ref, b_gate_ref):
    pre = jnp.dot(a_low.astype(jnp.bfloat16), w_gate_ref[...], preferred_element_type=jnp.float32) + b_gate_ref[...]
    return _log_sigmoid(pre) / GATE_TEMP


def _project(x, n1_ref, w_main_ref, w_code_ref, w_u_ref, w_gate_ref, b_gate_ref):
    h = _rmsnorm(x, n1_ref[...]).astype(jnp.bfloat16)
    qkvg = jnp.dot(h, w_main_ref[...], preferred_element_type=jnp.float32)
    a_low = jnp.dot(h, w_code_ref[...], preferred_element_type=jnp.float32)
    u = jnp.dot(h, w_u_ref[...], preferred_element_type=jnp.float32)
    return qkvg, u, _gate_log_decay(a_low, w_gate_ref, b_gate_ref)


def _gla_finish(o, g, gla_g):
    outs = []
    for h in range(N_HEADS):
        sl = slice(h * HEAD_V, (h + 1) * HEAD_V)
        outs.append(_rmsnorm(o[:, sl], gla_g) * (g[:, sl] * jax.nn.sigmoid(g[:, sl])))
    return jnp.concatenate(outs, axis=-1)


def _trace_interleaved(a, b):
    i = j = 0
    while i < len(a) or j < len(b):
        if j >= len(b) or (i < len(a) and i * len(b) <= j * len(a)):
            a[i]()
            i += 1
        else:
            b[j]()
            j += 1


def _prompt_project_tasks(load_x, slot, n1_ref, w_main_ref, w_code_ref, w_u_ref, w_gate_ref, b_gate_ref,
                          z_ref, b_ref):
    v = {}

    def norm():
        v["h"] = _rmsnorm(load_x(), n1_ref[...]).astype(jnp.bfloat16)

    def piece(lo, hi):
        w_ref, first = (w_main_ref, 0) if hi <= COL_U else (w_u_ref, COL_U)

        def run():
            z_ref[slot, :, lo:hi] = jnp.dot(v["h"], w_ref[:, lo - first:hi - first],
                                            preferred_element_type=jnp.float32)
        return run

    def gate_code():
        v["a_low"] = jnp.dot(v["h"], w_code_ref[...], preferred_element_type=jnp.float32)

    def gate_pre():
        v["pre"] = (jnp.dot(v["a_low"].astype(jnp.bfloat16), w_gate_ref[...], preferred_element_type=jnp.float32)
                    + b_gate_ref[...])

    def log_decay():
        v["b"] = _log_sigmoid(v["pre"]) / GATE_TEMP

    def scan(shifts, last):
        def run():
            b = v["b"]
            row = lax.broadcasted_iota(jnp.int32, b.shape, 0) % CHUNK
            for shift in shifts:
                b = b + jnp.where(row >= shift, pltpu.roll(b, shift, axis=0), 0.0)
            v["b"] = b
            if last:
                b_ref[slot] = b
        return run

    half = PROJ_PIECE
    return [
        norm, gate_code,
        piece(COL_Q, COL_K), gate_pre,
        piece(COL_K, COL_V), log_decay,
        piece(COL_V, COL_V + half), scan((1, 2), False),
        piece(COL_V + half, COL_G), scan((4, 8), False),
        piece(COL_G, COL_G + half), scan((16, 32), True),
        piece(COL_G + half, COL_U), piece(COL_U, COL_U + half), piece(COL_U + half, Z_WIDTH),
    ]


def _prompt_mix_tasks(slot, row0, z_ref, b_ref, carry, first_pos, gla_g_ref, pool_w_ref, pool_scale_ref, mix_ref,
                      base):
    head_mask = _head_lane_mask(CHUNK)
    scale = HEAD_K ** -0.5
    sub = GLA_PAIR
    n_chunks = sub // CHUNK
    part = slice(row0, row0 + sub)
    shared = {}

    def transpose_keys():
        b = b_ref[slot, part, :]
        k = z_ref[slot, part, COL_K:COL_V]
        last = [b[(c + 1) * CHUNK - 1:(c + 1) * CHUNK] for c in range(n_chunks)]
        b_last = jnp.concatenate([jnp.broadcast_to(row, (CHUNK, KEY_WIDTH)) for row in last], axis=0)
        shared["kt_t"] = (k * jnp.exp(-b)).T.astype(jnp.bfloat16)
        shared["ke_t"] = (k * jnp.exp(b_last - b)).T.astype(jnp.bfloat16)
        shared["v"] = z_ref[slot, part, COL_V:COL_G].astype(jnp.bfloat16)
        decay = jnp.exp(jnp.concatenate(last + [jnp.zeros((SUBLANES - n_chunks, KEY_WIDTH), jnp.float32)], axis=0))
        shared["decay"] = jnp.broadcast_to(decay[:, :, None], (SUBLANES, KEY_WIDTH, HEAD_V))

    def chunk_tasks(c):
        rows = slice(row0 + c * CHUNK, row0 + (c + 1) * CHUNK)
        out_rows = slice(base + c * CHUNK, base + (c + 1) * CHUNK)
        token = lax.broadcasted_iota(jnp.int32, (CHUNK, sub), 1) - c * CHUNK
        causal = (token >= 0) & (token <= lax.broadcasted_iota(jnp.int32, (CHUNK, sub), 0))
        in_chunk = (lax.broadcasted_iota(jnp.int32, (KEY_WIDTH, sub), 1) // CHUNK) == c
        v = {}

        def prepare():
            qt = z_ref[slot, rows, COL_Q:COL_K] * jnp.exp(b_ref[slot, rows, :]) * scale
            v["q_stack"] = jnp.concatenate([jnp.where(m, qt, 0.0) for m in head_mask], axis=0).astype(jnp.bfloat16)

        def update_product():
            ke_t = jnp.where(in_chunk, shared["ke_t"], jnp.zeros_like(shared["ke_t"]))
            v["upd"] = jnp.concatenate(
                [jnp.dot(ke_t[h * HEAD_K:(h + 1) * HEAD_K], shared["v"][:, h * HEAD_V:(h + 1) * HEAD_V],
                         preferred_element_type=jnp.float32) for h in range(N_HEADS)], axis=0)

        def query_product():
            rhs = jnp.concatenate([carry["st"].astype(jnp.bfloat16), shared["kt_t"]], axis=1)
            v["r"] = jnp.dot(v["q_stack"], rhs, preferred_element_type=jnp.float32)

        def update_state():
            carry["st"] = shared["decay"][c] * carry["st"] + v["upd"]

        def intra():
            r = v["r"]
            outs = []
            for h in range(N_HEADS):
                hr = slice(h * CHUNK, (h + 1) * CHUNK)
                att = jnp.where(causal, r[hr, HEAD_V:HEAD_V + sub], 0.0).astype(jnp.bfloat16)
                o_intra = jnp.dot(att, shared["v"][:, h * HEAD_V:(h + 1) * HEAD_V], preferred_element_type=jnp.float32)
                outs.append(o_intra + r[hr, 0:HEAD_V])
            v["o"] = jnp.concatenate(outs, axis=-1)

        def finish():
            g_c = z_ref[slot, rows, COL_G:COL_U]
            mix_ref[out_rows, 0:GLA_WIDTH] = _gla_finish(v["o"], g_c, gla_g_ref[...]).astype(mix_ref.dtype)

        return dict(prepare=prepare, update_product=update_product, query_product=query_product,
                    update_state=update_state, intra=intra, finish=finish)

    pooled = {}

    def window_mean_task(gi, w):
        lanes = slice(gi * POOL_GROUP, (gi + 1) * POOL_GROUP)

        def window_mean():
            u = z_ref[slot, part, COL_U + gi * POOL_GROUP:COL_U + (gi + 1) * POOL_GROUP]
            s = jnp.concatenate([carry["hist"][:, lanes], u], axis=0)
            shift = 1
            while shift < w:
                s = s + pltpu.roll(s, shift, axis=0)
                shift *= 2
            seen = first_pos + 1 + lax.broadcasted_iota(jnp.int32, (sub, POOL_GROUP), 0)
            if "inv_seen" not in pooled:
                pooled["inv_seen"] = 1.0 / seen.astype(jnp.float32)
            inv_count = jnp.where(seen < w, pooled["inv_seen"], 1.0 / w)
            pooled[gi] = (s[POOL_HIST:] * inv_count - u).astype(jnp.bfloat16)

        return window_mean

    def group_map_task(pair):
        lanes = slice(pair * POOL_PAIR, (pair + 1) * POOL_PAIR)

        def group_map():
            both = jnp.concatenate([pooled[2 * pair], pooled[2 * pair + 1]], axis=-1)
            pg = jnp.dot(both, pool_w_ref[pair], preferred_element_type=jnp.float32)
            pg = pg * pool_scale_ref[:, lanes]
            mix_ref[base:base + sub, GLA_WIDTH + pair * POOL_PAIR:GLA_WIDTH + (pair + 1) * POOL_PAIR] = (
                pg.astype(mix_ref.dtype))

        return group_map

    def keep_history():
        carry["hist"] = z_ref[slot, row0 + sub - POOL_HIST:row0 + sub, COL_U:Z_WIDTH]

    assert n_chunks == 2
    c0, c1 = chunk_tasks(0), chunk_tasks(1)
    means = [window_mean_task(gi, w) for gi, w in enumerate(POOL_WINDOWS)]
    maps = [group_map_task(pair) for pair in range(len(POOL_WINDOWS) // 2)]
    return [
        transpose_keys, c0["prepare"], c1["prepare"], c0["update_product"], c0["query_product"], c1["update_product"],
        c0["update_state"], c1["query_product"], means[0], c0["intra"], c1["update_state"],
        means[1], c1["intra"], maps[0], c0["finish"], means[2], c1["finish"], means[3], maps[1], keep_history,
    ]


def _sample_project(xs_ref, n1_ref, w_main_ref, w_code_ref, w_u_ref, w_gate_ref, b_gate_ref, pool_w_ref, pool_scale_ref,
                    pool_state_ref, mix_s_ref, pool_new_ref, dec_ref, ke_ref, qt_ref, v_ref, g_ref, o_ref):
    n_seq = xs_ref.shape[0]
    qkvg, u, log_a = _project(xs_ref[...], n1_ref, w_main_ref, w_code_ref, w_u_ref, w_gate_ref, b_gate_ref)
    q = qkvg[:, COL_Q:COL_K]
    k = qkvg[:, COL_K:COL_V]
    v = _bf16_round(qkvg[:, COL_V:COL_G])
    qt = _bf16_round(q * jnp.exp(log_a) * (HEAD_K ** -0.5))
    kt = _bf16_round(k * jnp.exp(-log_a))
    ke = _bf16_round(k * jnp.exp(log_a - log_a))
    prod = qt * kt
    o_intra = []
    for h, m in enumerate(_head_lane_mask(n_seq)):
        att = _bf16_round(jnp.sum(jnp.where(m, prod, 0.0), axis=-1, keepdims=True))
        o_intra.append(att * v[:, h * HEAD_V:(h + 1) * HEAD_V])
    o_ref[...] = jnp.concatenate(o_intra, axis=-1)
    dec_ref[...] = jnp.exp(log_a)
    ke_ref[...] = ke
    qt_ref[...] = qt
    v_ref[...] = v
    g_ref[...] = qkvg[:, COL_G:COL_U]

    pooled = []
    for gi, w in enumerate(POOL_WINDOWS):
        lanes = slice(gi * POOL_GROUP, (gi + 1) * POOL_GROUP)
        s = u[:, lanes]
        for j in range(POOL_BUF - (w - 1), POOL_BUF):
            s = s + pool_state_ref[j, :, lanes]
        pooled.append((s / float(w) - u[:, lanes]).astype(jnp.bfloat16))
    for pair in range(len(POOL_WINDOWS) // 2):
        lanes = slice(pair * POOL_PAIR, (pair + 1) * POOL_PAIR)
        both = jnp.concatenate(pooled[2 * pair:2 * pair + 2], axis=-1)
        pg = jnp.dot(both, pool_w_ref[pair], preferred_element_type=jnp.float32) * pool_scale_ref[:, lanes]
        mix_s_ref[:, GLA_WIDTH + pair * POOL_PAIR:GLA_WIDTH + (pair + 1) * POOL_PAIR] = pg.astype(mix_s_ref.dtype)
    for j in range(POOL_BUF - 1):
        pool_new_ref[j] = pool_state_ref[j + 1]
    pool_new_ref[POOL_BUF - 1] = u


def _sample_state_tasks(first_seq, s_ref, s_new_ref, dec_ref, ke_ref, qt_ref, v_ref, o_ref):
    rows = pl.ds(pl.multiple_of(first_seq, SUBLANES), SUBLANES)

    def update_state():
        v = v_ref[rows, :]
        v_rows = jnp.concatenate(
            [jnp.broadcast_to(v[:, None, h * HEAD_V:(h + 1) * HEAD_V], (SUBLANES, HEAD_K, HEAD_V))
             for h in range(N_HEADS)], axis=1)
        s_new_ref[...] = dec_ref[rows, :][:, :, None] * s_ref[...] + ke_ref[rows, :][:, :, None] * v_rows

    def query_state():
        lane = lax.broadcasted_iota(jnp.int32, (SUBLANES, KEY_WIDTH), 1)
        head = lax.broadcasted_iota(jnp.int32, (SUBLANES, KEY_WIDTH), 0)
        own_head = lane // HEAD_K == head
        q_rows = jnp.where(own_head[None], qt_ref[rows, :][:, None, :], 0.0).astype(jnp.bfloat16)
        o_inter = jnp.einsum("bhk,bkv->bhv", q_rows, s_ref[...].astype(jnp.bfloat16),
                             preferred_element_type=jnp.float32)
        o_ref[rows, :] = o_ref[rows, :] + jnp.concatenate([o_inter[:, h, :] for h in range(N_HEADS)], axis=-1)

    return [update_state, query_state]


def _mixer_kernel(*refs, chained):
    (x_ref, x_next_ref, n1_ref, w_main_ref, w_code_ref, w_u_ref, w_gate_ref, b_gate_ref, gla_g_ref, pool_w_ref,
     pool_scale_ref, xs_ref, s_ref, pool_state_ref) = refs[:14]
    (mix_ref, s_fin_ref, hist_out_ref, mix_s_ref, s_new_ref, pool_new_ref,
     st_ref, hist_ref, z_ref, b_ref, dec_ref, ke_ref, qt_ref, v_ref, g_ref, o_ref) = refs[-16:]
    tile = x_ref.shape[0]
    n_sub = tile // PROMPT_SUB
    t = pl.program_id(1)
    step = pl.program_id(0) * pl.num_programs(1) + t
    last_step = pl.num_programs(0) * pl.num_programs(1) - 1

    def project_tasks(load_x, slot):
        return _prompt_project_tasks(load_x, slot, n1_ref, w_main_ref, w_code_ref, w_u_ref, w_gate_ref, b_gate_ref,
                                     z_ref, b_ref)

    @pl.when(t == 0)
    def _():
        st_ref[...] = jnp.zeros_like(st_ref)
        hist_ref[...] = jnp.zeros_like(hist_ref)

    @pl.when(step == 0)
    def _():
        for task in project_tasks(lambda: x_ref[0:PROMPT_SUB], 0):
            task()
        _sample_project(xs_ref, n1_ref, w_main_ref, w_code_ref, w_u_ref, w_gate_ref, b_gate_ref, pool_w_ref,
                        pool_scale_ref, pool_state_ref, mix_s_ref, pool_new_ref, dec_ref, ke_ref, qt_ref, v_ref, g_ref,
                        o_ref)

    sample_tasks = _sample_state_tasks(step * SUBLANES, s_ref, s_new_ref, dec_ref, ke_ref, qt_ref, v_ref, o_ref)
    carry = {"st": st_ref[...], "hist": hist_ref[...]}
    for sb in range(n_sub):
        if sb + 1 < n_sub:
            load_next = lambda sb=sb: x_ref[(sb + 1) * PROMPT_SUB:(sb + 2) * PROMPT_SUB]
        else:
            load_next = lambda: x_next_ref[...]
        mix_tasks = []
        for row0 in range(0, PROMPT_SUB, GLA_PAIR):
            mix_tasks += _prompt_mix_tasks(sb % 2, row0, z_ref, b_ref, carry, t * tile + sb * PROMPT_SUB + row0,
                                           gla_g_ref, pool_w_ref, pool_scale_ref, mix_ref, sb * PROMPT_SUB + row0)
        if sample_tasks and sb >= 1:
            mix_tasks.insert(len(mix_tasks) // 2, sample_tasks.pop(0))
        _trace_interleaved(project_tasks(load_next, (sb + 1) % 2), mix_tasks)
    st_ref[...] = carry["st"]
    hist_ref[...] = carry["hist"]

    @pl.when(t == pl.num_programs(1) - 1)
    def _():
        s_fin_ref[...] = carry["st"]
        hist_out_ref[...] = carry["hist"]

    @pl.when(step == last_step)
    def _():
        mix_s_ref[:, 0:GLA_WIDTH] = _gla_finish(o_ref[...], g_ref[...], gla_g_ref[...]).astype(mix_s_ref.dtype)


def _mixer(layer, x, xs, s_state, pool_state, p, stacked):
    B, T, _ = x.shape
    n_seq = xs.shape[0]
    tile = PROMPT_TILE
    n_tiles = T // tile
    n_sub = tile // PROMPT_SUB
    assert n_sub % 2 == 0
    assert n_sub >= 3 and B * n_tiles * SUBLANES == n_seq
    assert PROMPT_SUB % GLA_PAIR == 0
    f32 = jnp.float32
    whole = lambda *shape: pl.BlockSpec((None,) + shape, lambda b, t: (layer,) + (0,) * len(shape))
    fixed = lambda *shape: pl.BlockSpec(shape, lambda b, t: (0,) * len(shape))
    chained = stacked is not None
    operands = [x, x, p["n1"], p["w_main"], p["w_code"], p["w_u"], p["w_gate"], p["b_gate"], p["gla_g"], p["pool_w"],
                p["pool_scale"], xs, s_state, pool_state]

    def next_first_sub_block(b, t):
        n = jnp.minimum(b * n_tiles + t + 1, B * n_tiles - 1)
        return (n // n_tiles, (n % n_tiles) * n_sub, 0)

    sample_state_block = pl.BlockSpec((None, SUBLANES, KEY_WIDTH, HEAD_V), lambda b, t: (layer, b * n_tiles + t, 0, 0))
    in_specs = [
        pl.BlockSpec((None, tile, D_MODEL), lambda b, t: (b, t, 0)),
        pl.BlockSpec((None, PROMPT_SUB, D_MODEL), next_first_sub_block),
        whole(1, D_MODEL),
        whole(D_MODEL, COL_U),
        whole(D_MODEL, LANES),
        whole(D_MODEL, POOL_WIDTH),
        whole(LANES, KEY_WIDTH),
        whole(1, KEY_WIDTH),
        whole(1, HEAD_V),
        whole(len(POOL_WINDOWS) // 2, POOL_PAIR, POOL_PAIR),
        whole(1, POOL_WIDTH),
        fixed(n_seq, D_MODEL),
        sample_state_block,
        whole(POOL_BUF, n_seq, POOL_WIDTH),
    ]
    aliases = {}
    if chained:
        aliases = {len(operands): 1, len(operands) + 1: 2, len(operands) + 2: 4, len(operands) + 3: 5}
        operands += list(stacked)
        in_specs += [pl.BlockSpec(memory_space=pl.ANY)] * 4
    return pl.pallas_call(
        functools.partial(_mixer_kernel, chained=chained),
        grid=(B, n_tiles),
        in_specs=in_specs,
        out_specs=[
            pl.BlockSpec((None, tile, D_MODEL), lambda b, t: (b, t, 0)),
            pl.BlockSpec((None, None, KEY_WIDTH, HEAD_V), lambda b, t: (layer, b, 0, 0)),
            pl.BlockSpec((None, None, POOL_HIST, POOL_WIDTH), lambda b, t: (layer, b, 0, 0)),
            fixed(n_seq, D_MODEL),
            sample_state_block,
            whole(POOL_BUF, n_seq, POOL_WIDTH),
        ],
        out_shape=[
            jax.ShapeDtypeStruct((B, T, D_MODEL), jnp.bfloat16),
            jax.ShapeDtypeStruct((DEPTH, B, KEY_WIDTH, HEAD_V), f32),
            jax.ShapeDtypeStruct((DEPTH, B, POOL_HIST, POOL_WIDTH), f32),
            jax.ShapeDtypeStruct((n_seq, D_MODEL), jnp.bfloat16),
            jax.ShapeDtypeStruct((DEPTH, n_seq, KEY_WIDTH, HEAD_V), f32),
            jax.ShapeDtypeStruct((DEPTH, POOL_BUF, n_seq, POOL_WIDTH), f32),
        ],
        scratch_shapes=[
            pltpu.VMEM((KEY_WIDTH, HEAD_V), f32),
            pltpu.VMEM((POOL_HIST, POOL_WIDTH), f32),
            pltpu.VMEM((2, PROMPT_SUB, Z_WIDTH), f32),
            pltpu.VMEM((2, PROMPT_SUB, KEY_WIDTH), f32),
            pltpu.VMEM((n_seq, KEY_WIDTH), f32),
            pltpu.VMEM((n_seq, KEY_WIDTH), f32),
            pltpu.VMEM((n_seq, KEY_WIDTH), f32),
            pltpu.VMEM((n_seq, GLA_WIDTH), f32),
            pltpu.VMEM((n_seq, GLA_WIDTH), f32),
            pltpu.VMEM((n_seq, GLA_WIDTH), f32),
        ],
        input_output_aliases=aliases,
        compiler_params=pltpu.CompilerParams(
            dimension_semantics=("arbitrary", "arbitrary"), vmem_limit_bytes=VMEM_LIMIT),
        name=f"mixer_l{layer}",
    )(*operands)


N_FF_CHUNKS = D_FF // FF_CHUNK
N_WEIGHT_CHUNKS = 1 + 2 * N_FF_CHUNKS
STAGE_SLOTS = 3
assert FF_CHUNK == D_MODEL


def _channel_kernel(x_ref, mix_ref, xs_ref, mix_s_ref, n2_ref, final_g_ref, w_out_hbm, w_up_hbm, w_down_hbm,
                    y_ref, ys_ref, w_out_ref, w_up_ref, w_down_ref, stage_ref, sem, *, layer, final):
    step = pl.program_id(0)

    def chunk(k):
        if k == 0:
            return w_out_hbm.at[layer], w_out_ref
        j, down = divmod(k - 1, 2)
        span = pl.ds(j * FF_CHUNK, FF_CHUNK)
        if down:
            return w_down_hbm.at[layer, span, :], w_down_ref.at[span, :]
        return w_up_hbm.at[layer, :, span], w_up_ref.at[:, span]

    def chunk_copy(k):
        return pltpu.make_async_copy(chunk(k)[0], stage_ref.at[k % STAGE_SLOTS], sem.at[k % STAGE_SLOTS])

    def fetch(k):
        chunk_copy(k).wait()
        chunk(k)[1][...] = stage_ref[k % STAGE_SLOTS].astype(jnp.bfloat16)
        if k + STAGE_SLOTS < N_WEIGHT_CHUNKS:
            chunk_copy(k + STAGE_SLOTS).start()

    def channel_mix(x, mix, fetch_weights):
        def before_matmul(m):
            if fetch_weights and m + 1 < N_WEIGHT_CHUNKS:
                fetch(m + 1)

        if fetch_weights:
            fetch(0)
        before_matmul(0)
        x1 = x + jnp.dot(mix, w_out_ref[...], preferred_element_type=jnp.float32)
        h2 = _rmsnorm(x1, n2_ref[...]).astype(jnp.bfloat16)
        acc = x1
        for j in range(N_FF_CHUNKS):
            cols = slice(j * FF_CHUNK, (j + 1) * FF_CHUNK)
            before_matmul(1 + 2 * j)
            hid = jnp.dot(h2, w_up_ref[:, cols], preferred_element_type=jnp.float32)
            act = jnp.square(jnp.maximum(hid, 0.0)).astype(jnp.bfloat16)
            before_matmul(2 + 2 * j)
            acc = acc + jnp.dot(act, w_down_ref[cols, :], preferred_element_type=jnp.float32)
        if final:
            acc = _rmsnorm(acc, final_g_ref[...])
        return acc

    @pl.when(step == 0)
    def _():
        for k in range(STAGE_SLOTS):
            chunk_copy(k).start()
        y_ref[...] = channel_mix(x_ref[...], mix_ref[...], True)

    @pl.when(step > 0)
    def _():
        y_ref[...] = channel_mix(x_ref[...], mix_ref[...], False)

    @pl.when(step == pl.num_programs(0) - 1)
    def _():
        ys_ref[...] = channel_mix(xs_ref[...], mix_s_ref[...], False)


def _channel_mixer(layer, x, mix, xs, mix_s, p, final_g):
    rows = x.shape[0]
    n_seq = xs.shape[0]
    tile = CHANNEL_TILE
    assert rows // tile > 1
    bf16 = jnp.bfloat16
    return pl.pallas_call(
        functools.partial(_channel_kernel, layer=layer, final=layer == DEPTH - 1),
        grid=(rows // tile,),
        in_specs=[
            pl.BlockSpec((tile, D_MODEL), lambda i: (i, 0)),
            pl.BlockSpec((tile, D_MODEL), lambda i: (i, 0)),
            pl.BlockSpec((n_seq, D_MODEL), lambda i: (0, 0)),
            pl.BlockSpec((n_seq, D_MODEL), lambda i: (0, 0)),
            pl.BlockSpec((None, 1, D_MODEL), lambda i: (layer, 0, 0)),
            pl.BlockSpec((1, D_MODEL), lambda i: (0, 0)),
            pl.BlockSpec(memory_space=pl.ANY),
            pl.BlockSpec(memory_space=pl.ANY),
            pl.BlockSpec(memory_space=pl.ANY),
        ],
        out_specs=[
            pl.BlockSpec((tile, D_MODEL), lambda i: (i, 0)),
            pl.BlockSpec((n_seq, D_MODEL), lambda i: (0, 0)),
        ],
        out_shape=[
            jax.ShapeDtypeStruct((rows, D_MODEL), jnp.float32),
            jax.ShapeDtypeStruct((n_seq, D_MODEL), jnp.float32),
        ],
        scratch_shapes=[
            pltpu.VMEM((D_MODEL, D_MODEL), bf16),
            pltpu.VMEM((D_MODEL, D_FF), bf16),
            pltpu.VMEM((D_FF, D_MODEL), bf16),
            pltpu.VMEM((STAGE_SLOTS, D_MODEL, D_MODEL), jnp.float32),
            pltpu.SemaphoreType.DMA((STAGE_SLOTS,)),
        ],
        compiler_params=pltpu.CompilerParams(
            dimension_semantics=("arbitrary",), vmem_limit_bytes=VMEM_LIMIT),
        name=f"channel_mixer_l{layer}",
    )(x, mix, xs, mix_s, p["n2"], final_g, p["w_out"], p["w_up"], p["w_down"])


def kernel(x_prompt, x_sample, state_gla, state_pool, norm1_g, w_in, w_gate, b_gate, gla_norm_g, pool_w,
           pool_scale, w_out, norm2_g, w_up, w_down, final_g):
    B, T, _ = x_prompt.shape
    n_seq = x_sample.shape[0]
    bf16 = jnp.bfloat16

    w_code = jnp.pad(w_in[:, :, W_IN_GATE:W_IN_U], ((0, 0), (0, 0), (0, LANES - GATE_RANK)))
    pw = pool_w.reshape(DEPTH, len(POOL_WINDOWS) // 2, 2, POOL_GROUP, POOL_GROUP)
    zero = jnp.zeros_like(pw[:, :, 0])
    pool_pairs = jnp.concatenate([jnp.concatenate([pw[:, :, 0], zero], axis=-1),
                                  jnp.concatenate([zero, pw[:, :, 1]], axis=-1)], axis=-2)
    params = {
        "n1": norm1_g.reshape(DEPTH, 1, D_MODEL),
        "w_main": w_in[:, :, :W_IN_GATE].astype(bf16),
        "w_code": w_code.astype(bf16),
        "w_u": w_in[:, :, W_IN_U:].astype(bf16),
        "w_gate": jnp.pad(w_gate, ((0, 0), (0, LANES - GATE_RANK), (0, 0))).astype(bf16),
        "b_gate": b_gate.reshape(DEPTH, 1, KEY_WIDTH),
        "gla_g": gla_norm_g.reshape(DEPTH, 1, HEAD_V),
        "pool_w": pool_pairs.astype(bf16),
        "pool_scale": pool_scale.reshape(DEPTH, 1, POOL_WIDTH),
        "w_out": w_out,
        "n2": norm2_g.reshape(DEPTH, 1, D_MODEL),
        "w_up": w_up,
        "w_down": w_down,
    }
    final_g2 = final_g.reshape(1, D_MODEL)
    s_state = state_gla.reshape(DEPTH, n_seq, KEY_WIDTH, HEAD_V)
    pool_rows = jnp.swapaxes(state_pool, 1, 2)

    xp = x_prompt
    xs = x_sample.reshape(n_seq, D_MODEL)
    states = None
    for layer in range(DEPTH):
        mix_p, gla_p, hist_p, mix_s, gla_s, pool_s = _mixer(layer, xp, xs, s_state, pool_rows, params, states)
        states = (gla_p, hist_p, gla_s, pool_s)
        xp, xs = _channel_mixer(layer, xp.reshape(B * T, D_MODEL), mix_p.reshape(B * T, D_MODEL), xs, mix_s,
                                params, final_g2)
        xp = xp.reshape(B, T, D_MODEL)
    return (xp, xs.reshape(n_seq, 1, D_MODEL),
            gla_p.reshape(DEPTH, B, N_HEADS, HEAD_K, HEAD_V), hist_p[:, :, POOL_HIST - POOL_BUF:],
            gla_s.reshape(DEPTH, n_seq, N_HEADS, HEAD_K, HEAD_V), jnp.swapaxes(pool_s, 1, 2))
```

```python
import functools

import jax
import jax.numpy as jnp
from jax import lax
from jax.experimental import pallas as pl
from jax.experimental.pallas import tpu as pltpu

D_MODEL = 1024
DEPTH = 4
N_HEADS = 4
HEAD_K = 64
HEAD_V = 128
KEY_WIDTH = N_HEADS * HEAD_K
GLA_WIDTH = N_HEADS * HEAD_V
GATE_RANK = 16
GATE_TEMP = 16.0
CHUNK = 64
POOL_WIDTH = 512
POOL_WINDOWS = (2, 4, 8, 16)
POOL_GROUP = 128
POOL_PAIR = 2 * POOL_GROUP
POOL_BUF = 15
POOL_HIST = 16
D_FF = 4 * D_MODEL
FF_CHUNK = 1024
EPS = 1e-6
LANES = 128
SUBLANES = 8

COL_Q = 0
COL_K = KEY_WIDTH
COL_V = 2 * KEY_WIDTH
COL_G = COL_V + GLA_WIDTH
COL_U = COL_G + GLA_WIDTH
Z_WIDTH = COL_U + POOL_WIDTH
W_IN_GATE = COL_U
W_IN_U = COL_U + GATE_RANK

PROMPT_TILE = 1024
PROMPT_SUB = 256
GLA_PAIR = 2 * CHUNK
PROJ_PIECE = 256
CHANNEL_TILE = 512
VMEM_LIMIT = 56 * 1024 * 1024


def _rmsnorm(x, g):
    return x * lax.rsqrt(jnp.mean(x * x, axis=-1, keepdims=True) + EPS) * g


def _log_sigmoid(x):
    return jnp.minimum(x, 0.0) - jnp.log(1.0 + jnp.exp(-jnp.abs(x)))


def _bf16_round(x):
    return x.astype(jnp.bfloat16).astype(jnp.float32)


def _head_lane_mask(rows):
    lane = lax.broadcasted_iota(jnp.int32, (rows, KEY_WIDTH), 1)
    return [(lane >= h * HEAD_K) & (lane < (h + 1) * HEAD_K) for h in range(N_HEADS)]


def _gate_log_decay(a_low, w_gate_ref, b_gate_ref):
    pre = jnp.dot(a_low.astype(jnp.bfloat16), w_gate_ref[...], preferred_element_type=jnp.float32) + b_gate_ref[...]
    return _log_sigmoid(pre) / GATE_TEMP


def _project(x, n1_ref, w_main_ref, w_code_ref, w_u_ref, w_gate_ref, b_gate_ref):
    h = _rmsnorm(x, n1_ref[...]).astype(jnp.bfloat16)
    qkvg = jnp.dot(h, w_main_ref[...], preferred_element_type=jnp.float32)
    a_low = jnp.dot(h, w_code_ref[...], preferred_element_type=jnp.float32)
    u = jnp.dot(h, w_u_ref[...], preferred_element_type=jnp.float32)
    return qkvg, u, _gate_log_decay(a_low, w_gate_ref, b_gate_ref)


def _gla_finish(o, g, gla_g):
    outs = []
    for h in range(N_HEADS):
        sl = slice(h * HEAD_V, (h + 1) * HEAD_V)
        outs.append(_rmsnorm(o[:, sl], gla_g) * (g[:, sl] * jax.nn.sigmoid(g[:, sl])))
    return jnp.concatenate(outs, axis=-1)


def _trace_interleaved(a, b):
    i = j = 0
    while i < len(a) or j < len(b):
        if j >= len(b) or (i < len(a) and i * len(b) <= j * len(a)):
            a[i]()
            i += 1
        else:
            b[j]()
            j += 1


def _prompt_project_tasks(load_x, slot, n1_ref, w_main_ref, w_code_ref, w_u_ref, w_gate_ref, b_gate_ref,
                          z_ref, b_ref):
    v = {}

    def norm():
        v["h"] = _rmsnorm(load_x(), n1_ref[...]).astype(jnp.bfloat16)

    def piece(lo, hi):
        w_ref, first = (w_main_ref, 0) if hi <= COL_U else (w_u_ref, COL_U)

        def run():
            z_ref[slot, :, lo:hi] = jnp.dot(v["h"], w_ref[:, lo - first:hi - first],
                                            preferred_element_type=jnp.float32)
        return run

    def gate_code():
        v["a_low"] = jnp.dot(v["h"], w_code_ref[...], preferred_element_type=jnp.float32)

    def gate_pre():
        v["pre"] = (jnp.dot(v["a_low"].astype(jnp.bfloat16), w_gate_ref[...], preferred_element_type=jnp.float32)
                    + b_gate_ref[...])

    def log_decay():
        v["b"] = _log_sigmoid(v["pre"]) / GATE_TEMP

    def scan(shifts, last):
        def run():
            b = v["b"]
            row = lax.broadcasted_iota(jnp.int32, b.shape, 0) % CHUNK
            for shift in shifts:
                b = b + jnp.where(row >= shift, pltpu.roll(b, shift, axis=0), 0.0)
            v["b"] = b
            if last:
                b_ref[slot] = b
        return run

    half = PROJ_PIECE
    return [
        norm, gate_code,
        piece(COL_Q, COL_K), gate_pre,
        piece(COL_K, COL_V), log_decay,
        piece(COL_V, COL_V + half), scan((1, 2), False),
        piece(COL_V + half, COL_G), scan((4, 8), False),
        piece(COL_G, COL_G + half), scan((16, 32), True),
        piece(COL_G + half, COL_U), piece(COL_U, COL_U + half), piece(COL_U + half, Z_WIDTH),
    ]


def _prompt_mix_tasks(slot, row0, z_ref, b_ref, carry, first_pos, gla_g_ref, pool_w_ref, pool_scale_ref, mix_ref,
                      base):
    head_mask = _head_lane_mask(CHUNK)
    scale = HEAD_K ** -0.5
    sub = GLA_PAIR
    n_chunks = sub // CHUNK
    part = slice(row0, row0 + sub)
    shared = {}

    def transpose_keys():
        b = b_ref[slot, part, :]
        k = z_ref[slot, part, COL_K:COL_V]
        last = [b[(c + 1) * CHUNK - 1:(c + 1) * CHUNK] for c in range(n_chunks)]
        b_last = jnp.concatenate([jnp.broadcast_to(row, (CHUNK, KEY_WIDTH)) for row in last], axis=0)
        shared["kt_t"] = (k * jnp.exp(-b)).T.astype(jnp.bfloat16)
        shared["ke_t"] = (k * jnp.exp(b_last - b)).T.astype(jnp.bfloat16)
        shared["v"] = z_ref[slot, part, COL_V:COL_G].astype(jnp.bfloat16)
        decay = jnp.exp(jnp.concatenate(last + [jnp.zeros((SUBLANES - n_chunks, KEY_WIDTH), jnp.float32)], axis=0))
        shared["decay"] = jnp.broadcast_to(decay[:, :, None], (SUBLANES, KEY_WIDTH, HEAD_V))

    def chunk_tasks(c):
        rows = slice(row0 + c * CHUNK, row0 + (c + 1) * CHUNK)
        out_rows = slice(base + c * CHUNK, base + (c + 1) * CHUNK)
        token = lax.broadcasted_iota(jnp.int32, (CHUNK, sub), 1) - c * CHUNK
        causal = (token >= 0) & (token <= lax.broadcasted_iota(jnp.int32, (CHUNK, sub), 0))
        in_chunk = (lax.broadcasted_iota(jnp.int32, (KEY_WIDTH, sub), 1) // CHUNK) == c
        v = {}

        def prepare():
            qt = z_ref[slot, rows, COL_Q:COL_K] * jnp.exp(b_ref[slot, rows, :]) * scale
            v["q_stack"] = jnp.concatenate([jnp.where(m, qt, 0.0) for m in head_mask], axis=0).astype(jnp.bfloat16)

        def update_product():
            ke_t = jnp.where(in_chunk, shared["ke_t"], jnp.zeros_like(shared["ke_t"]))
            v["upd"] = jnp.concatenate(
                [jnp.dot(ke_t[h * HEAD_K:(h + 1) * HEAD_K], shared["v"][:, h * HEAD_V:(h + 1) * HEAD_V],
                         preferred_element_type=jnp.float32) for h in range(N_HEADS)], axis=0)

        def query_product():
            rhs = jnp.concatenate([carry["st"].astype(jnp.bfloat16), shared["kt_t"]], axis=1)
            v["r"] = jnp.dot(v["q_stack"], rhs, preferred_element_type=jnp.float32)

        def update_state():
            carry["st"] = shared["decay"][c] * carry["st"] + v["upd"]

        def intra():
            r = v["r"]
            outs = []
            for h in range(N_HEADS):
                hr = slice(h * CHUNK, (h + 1) * CHUNK)
                att = jnp.where(causal, r[hr, HEAD_V:HEAD_V + sub], 0.0).astype(jnp.bfloat16)
                o_intra = jnp.dot(att, shared["v"][:, h * HEAD_V:(h + 1) * HEAD_V], preferred_element_type=jnp.float32)
                outs.append(o_intra + r[hr, 0:HEAD_V])
            v["o"] = jnp.concatenate(outs, axis=-1)

        def finish():
            g_c = z_ref[slot, rows, COL_G:COL_U]
            mix_ref[out_rows, 0:GLA_WIDTH] = _gla_finish(v["o"], g_c, gla_g_ref[...]).astype(mix_ref.dtype)

        return dict(prepare=prepare, update_product=update_product, query_product=query_product,
                    update_state=update_state, intra=intra, finish=finish)

    pooled = {}

    def window_mean_task(gi, w):
        lanes = slice(gi * POOL_GROUP, (gi + 1) * POOL_GROUP)

        def window_mean():
            u = z_ref[slot, part, COL_U + gi * POOL_GROUP:COL_U + (gi + 1) * POOL_GROUP]
            s = jnp.concatenate([carry["hist"][:, lanes], u], axis=0)
            shift = 1
            while shift < w:
                s = s + pltpu.roll(s, shift, axis=0)
                shift *= 2
            seen = first_pos + 1 + lax.broadcasted_iota(jnp.int32, (sub, POOL_GROUP), 0)
            if "inv_seen" not in pooled:
                pooled["inv_seen"] = 1.0 / seen.astype(jnp.float32)
            inv_count = jnp.where(seen < w, pooled["inv_seen"], 1.0 / w)
            pooled[gi] = (s[POOL_HIST:] * inv_count - u).astype(jnp.bfloat16)

        return window_mean

    def group_map_task(pair):
        lanes = slice(pair * POOL_PAIR, (pair + 1) * POOL_PAIR)

        def group_map():
            both = jnp.concatenate([pooled[2 * pair], pooled[2 * pair + 1]], axis=-1)
            pg = jnp.dot(both, pool_w_ref[pair], preferred_element_type=jnp.float32)
            pg = pg * pool_scale_ref[:, lanes]
            mix_ref[base:base + sub, GLA_WIDTH + pair * POOL_PAIR:GLA_WIDTH + (pair + 1) * POOL_PAIR] = (
                pg.astype(mix_ref.dtype))

        return group_map

    def keep_history():
        carry["hist"] = z_ref[slot, row0 + sub - POOL_HIST:row0 + sub, COL_U:Z_WIDTH]

    assert n_chunks == 2
    c0, c1 = chunk_tasks(0), chunk_tasks(1)
    means = [window_mean_task(gi, w) for gi, w in enumerate(POOL_WINDOWS)]
    maps = [group_map_task(pair) for pair in range(len(POOL_WINDOWS) // 2)]
    return [
        transpose_keys, c0["prepare"], c1["prepare"], c0["update_product"], c0["query_product"], c1["update_product"],
        c0["update_state"], c1["query_product"], means[0], c0["intra"], c1["update_state"],
        means[1], c1["intra"], maps[0], c0["finish"], means[2], c1["finish"], means[3], maps[1], keep_history,
    ]


def _sample_project(xs_ref, n1_ref, w_main_ref, w_code_ref, w_u_ref, w_gate_ref, b_gate_ref, pool_w_ref, pool_scale_ref,
                    pool_state_ref, mix_s_ref, pool_new_ref, dec_ref, ke_ref, qt_ref, v_ref, g_ref, o_ref):
    n_seq = xs_ref.shape[0]
    qkvg, u, log_a = _project(xs_ref[...], n1_ref, w_main_ref, w_code_ref, w_u_ref, w_gate_ref, b_gate_ref)
    q = qkvg[:, COL_Q:COL_K]
    k = qkvg[:, COL_K:COL_V]
    v = _bf16_round(qkvg[:, COL_V:COL_G])
    qt = _bf16_round(q * jnp.exp(log_a) * (HEAD_K ** -0.5))
    kt = _bf16_round(k * jnp.exp(-log_a))
    ke = _bf16_round(k * jnp.exp(log_a - log_a))
    prod = qt * kt
    o_intra = []
    for h, m in enumerate(_head_lane_mask(n_seq)):
        att = _bf16_round(jnp.sum(jnp.where(m, prod, 0.0), axis=-1, keepdims=True))
        o_intra.append(att * v[:, h * HEAD_V:(h + 1) * HEAD_V])
    o_ref[...] = jnp.concatenate(o_intra, axis=-1)
    dec_ref[...] = jnp.exp(log_a)
    ke_ref[...] = ke
    qt_ref[...] = qt
    v_ref[...] = v
    g_ref[...] = qkvg[:, COL_G:COL_U]

    pooled = []
    for gi, w in enumerate(POOL_WINDOWS):
        lanes = slice(gi * POOL_GROUP, (gi + 1) * POOL_GROUP)
        s = u[:, lanes]
        for j in range(POOL_BUF - (w - 1), POOL_BUF):
            s = s + pool_state_ref[j, :, lanes]
        pooled.append((s / float(w) - u[:, lanes]).astype(jnp.bfloat16))
    for pair in range(len(POOL_WINDOWS) // 2):
        lanes = slice(pair * POOL_PAIR, (pair + 1) * POOL_PAIR)
        both = jnp.concatenate(pooled[2 * pair:2 * pair + 2], axis=-1)
        pg = jnp.dot(both, pool_w_ref[pair], preferred_element_type=jnp.float32) * pool_scale_ref[:, lanes]
        mix_s_ref[:, GLA_WIDTH + pair * POOL_PAIR:GLA_WIDTH + (pair + 1) * POOL_PAIR] = pg.astype(mix_s_ref.dtype)
    for j in range(POOL_BUF - 1):
        pool_new_ref[j] = pool_state_ref[j + 1]
    pool_new_ref[POOL_BUF - 1] = u


def _sample_state_tasks(first_seq, s_ref, s_new_ref, dec_ref, ke_ref, qt_ref, v_ref, o_ref):
    rows = pl.ds(pl.multiple_of(first_seq, SUBLANES), SUBLANES)

    def update_state():
        v = v_ref[rows, :]
        v_rows = jnp.concatenate(
            [jnp.broadcast_to(v[:, None, h * HEAD_V:(h + 1) * HEAD_V], (SUBLANES, HEAD_K, HEAD_V))
             for h in range(N_HEADS)], axis=1)
        s_new_ref[...] = dec_ref[rows, :][:, :, None] * s_ref[...] + ke_ref[rows, :][:, :, None] * v_rows

    def query_state():
        lane = lax.broadcasted_iota(jnp.int32, (SUBLANES, KEY_WIDTH), 1)
        head = lax.broadcasted_iota(jnp.int32, (SUBLANES, KEY_WIDTH), 0)
        own_head = lane // HEAD_K == head
        q_rows = jnp.where(own_head[None], qt_ref[rows, :][:, None, :], 0.0).astype(jnp.bfloat16)
        o_inter = jnp.einsum("bhk,bkv->bhv", q_rows, s_ref[...].astype(jnp.bfloat16),
                             preferred_element_type=jnp.float32)
        o_ref[rows, :] = o_ref[rows, :] + jnp.concatenate([o_inter[:, h, :] for h in range(N_HEADS)], axis=-1)

    return [update_state, query_state]


def _mixer_kernel(*refs, layer, chained):
    (x_ref, x_next_ref, n1_ref, w_main_ref, w_code_ref, w_u_ref, w_gate_ref, b_gate_ref, gla_g_ref, pool_w_ref,
     pool_scale_ref, xs_ref, s_ref, pool_state_ref) = refs[:14]
    n1_ref, b_gate_ref, gla_g_ref, pool_scale_ref = (
        ref.at[pl.ds(layer, 1), :] for ref in (n1_ref, b_gate_ref, gla_g_ref, pool_scale_ref))
    (mix_ref, s_fin_ref, hist_out_ref, mix_s_ref, s_new_ref, pool_new_ref,
     st_ref, hist_ref, z_ref, b_ref, dec_ref, ke_ref, qt_ref, v_ref, g_ref, o_ref) = refs[-16:]
    tile = x_ref.shape[0]
    n_sub = tile // PROMPT_SUB
    t = pl.program_id(1)
    step = pl.program_id(0) * pl.num_programs(1) + t
    last_step = pl.num_programs(0) * pl.num_programs(1) - 1

    def project_tasks(load_x, slot):
        return _prompt_project_tasks(load_x, slot, n1_ref, w_main_ref, w_code_ref, w_u_ref, w_gate_ref, b_gate_ref,
                                     z_ref, b_ref)

    @pl.when(t == 0)
    def _():
        st_ref[...] = jnp.zeros_like(st_ref)
        hist_ref[...] = jnp.zeros_like(hist_ref)

    @pl.when(step == 0)
    def _():
        for task in project_tasks(lambda: x_ref[0:PROMPT_SUB], 0):
            task()
        _sample_project(xs_ref, n1_ref, w_main_ref, w_code_ref, w_u_ref, w_gate_ref, b_gate_ref, pool_w_ref,
                        pool_scale_ref, pool_state_ref, mix_s_ref, pool_new_ref, dec_ref, ke_ref, qt_ref, v_ref, g_ref,
                        o_ref)

    sample_tasks = _sample_state_tasks(step * SUBLANES, s_ref, s_new_ref, dec_ref, ke_ref, qt_ref, v_ref, o_ref)
    carry = {"st": st_ref[...], "hist": hist_ref[...]}
    for sb in range(n_sub):
        if sb + 1 < n_sub:
            load_next = lambda sb=sb: x_ref[(sb + 1) * PROMPT_SUB:(sb + 2) * PROMPT_SUB]
        else:
            load_next = lambda: x_next_ref[...]
        mix_tasks = []
        for row0 in range(0, PROMPT_SUB, GLA_PAIR):
            mix_tasks += _prompt_mix_tasks(sb % 2, row0, z_ref, b_ref, carry, t * tile + sb * PROMPT_SUB + row0,
                                           gla_g_ref, pool_w_ref, pool_scale_ref, mix_ref, sb * PROMPT_SUB + row0)
        if sample_tasks and sb >= 1:
            mix_tasks.insert(len(mix_tasks) // 2, sample_tasks.pop(0))
        _trace_interleaved(project_tasks(load_next, (sb + 1) % 2), mix_tasks)
    st_ref[...] = carry["st"]
    hist_ref[...] = carry["hist"]

    @pl.when(t == pl.num_programs(1) - 1)
    def _():
        s_fin_ref[...] = carry["st"]
        hist_out_ref[...] = carry["hist"]

    @pl.when(step == last_step)
    def _():
        mix_s_ref[:, 0:GLA_WIDTH] = _gla_finish(o_ref[...], g_ref[...], gla_g_ref[...]).astype(mix_s_ref.dtype)


def _mixer(layer, x, xs, s_state, pool_state, p, stacked):
    B, T, _ = x.shape
    n_seq = xs.shape[0]
    tile = PROMPT_TILE
    n_tiles = T // tile
    n_sub = tile // PROMPT_SUB
    assert n_sub % 2 == 0
    assert n_sub >= 3 and B * n_tiles * SUBLANES == n_seq
    assert PROMPT_SUB % GLA_PAIR == 0
    f32 = jnp.float32
    whole = lambda *shape: pl.BlockSpec((None,) + shape, lambda b, t: (layer,) + (0,) * len(shape))
    fixed = lambda *shape: pl.BlockSpec(shape, lambda b, t: (0,) * len(shape))
    all_layers = lambda width: fixed(DEPTH, width)
    chained = stacked is not None
    operands = [x, x, p["n1"], p["w_main"], p["w_code"], p["w_u"], p["w_gate"], p["b_gate"], p["gla_g"], p["pool_w"],
                p["pool_scale"], xs, s_state, pool_state]

    def next_first_sub_block(b, t):
        n = jnp.minimum(b * n_tiles + t + 1, B * n_tiles - 1)
        return (n // n_tiles, (n % n_tiles) * n_sub, 0)

    sample_state_block = pl.BlockSpec((None, SUBLANES, KEY_WIDTH, HEAD_V), lambda b, t: (layer, b * n_tiles + t, 0, 0))
    in_specs = [
        pl.BlockSpec((None, tile, D_MODEL), lambda b, t: (b, t, 0)),
        pl.BlockSpec((None, PROMPT_SUB, D_MODEL), next_first_sub_block),
        all_layers(D_MODEL),
        whole(D_MODEL, COL_U),
        whole(D_MODEL, LANES),
        whole(D_MODEL, POOL_WIDTH),
        whole(LANES, KEY_WIDTH),
        all_layers(KEY_WIDTH),
        all_layers(HEAD_V),
        whole(len(POOL_WINDOWS) // 2, POOL_PAIR, POOL_PAIR),
        all_layers(POOL_WIDTH),
        fixed(n_seq, D_MODEL),
        sample_state_block,
        whole(POOL_BUF, n_seq, POOL_WIDTH),
    ]
    aliases = {}
    if chained:
        aliases = {len(operands): 1, len(operands) + 1: 2, len(operands) + 2: 4, len(operands) + 3: 5}
        operands += list(stacked)
        in_specs += [pl.BlockSpec(memory_space=pl.ANY)] * 4
    return pl.pallas_call(
        functools.partial(_mixer_kernel, layer=layer, chained=chained),
        grid=(B, n_tiles),
        in_specs=in_specs,
        out_specs=[
            pl.BlockSpec((None, tile, D_MODEL), lambda b, t: (b, t, 0)),
            pl.BlockSpec((None, None, KEY_WIDTH, HEAD_V), lambda b, t: (layer, b, 0, 0)),
            pl.BlockSpec((None, None, POOL_HIST, POOL_WIDTH), lambda b, t: (layer, b, 0, 0)),
            fixed(n_seq, D_MODEL),
            sample_state_block,
            whole(POOL_BUF, n_seq, POOL_WIDTH),
        ],
        out_shape=[
            jax.ShapeDtypeStruct((B, T, D_MODEL), jnp.bfloat16),
            jax.ShapeDtypeStruct((DEPTH, B, KEY_WIDTH, HEAD_V), f32),
            jax.ShapeDtypeStruct((DEPTH, B, POOL_HIST, POOL_WIDTH), f32),
            jax.ShapeDtypeStruct((n_seq, D_MODEL), jnp.bfloat16),
            jax.ShapeDtypeStruct((DEPTH, n_seq, KEY_WIDTH, HEAD_V), f32),
            jax.ShapeDtypeStruct((DEPTH, POOL_BUF, n_seq, POOL_WIDTH), f32),
        ],
        scratch_shapes=[
            pltpu.VMEM((KEY_WIDTH, HEAD_V), f32),
            pltpu.VMEM((POOL_HIST, POOL_WIDTH), f32),
            pltpu.VMEM((2, PROMPT_SUB, Z_WIDTH), f32),
            pltpu.VMEM((2, PROMPT_SUB, KEY_WIDTH), f32),
            pltpu.VMEM((n_seq, KEY_WIDTH), f32),
            pltpu.VMEM((n_seq, KEY_WIDTH), f32),
            pltpu.VMEM((n_seq, KEY_WIDTH), f32),
            pltpu.VMEM((n_seq, GLA_WIDTH), f32),
            pltpu.VMEM((n_seq, GLA_WIDTH), f32),
            pltpu.VMEM((n_seq, GLA_WIDTH), f32),
        ],
        input_output_aliases=aliases,
        compiler_params=pltpu.CompilerParams(
            dimension_semantics=("arbitrary", "arbitrary"), vmem_limit_bytes=VMEM_LIMIT),
        name=f"mixer_l{layer}",
    )(*operands)


N_FF_CHUNKS = D_FF // FF_CHUNK
N_WEIGHT_CHUNKS = 1 + 2 * N_FF_CHUNKS
STAGE_SLOTS = 3
assert FF_CHUNK == D_MODEL


def _channel_kernel(x_ref, mix_ref, xs_ref, mix_s_ref, n2_ref, final_g_ref, w_out_hbm, w_up_hbm, w_down_hbm,
                    y_ref, ys_ref, w_out_ref, w_up_ref, w_down_ref, stage_ref, sem, *, layer, final):
    step = pl.program_id(0)
    n2_ref = n2_ref.at[pl.ds(layer, 1), :]

    def chunk(k):
        if k == 0:
            return w_out_hbm.at[layer], w_out_ref
        j, down = divmod(k - 1, 2)
        span = pl.ds(j * FF_CHUNK, FF_CHUNK)
        if down:
            return w_down_hbm.at[layer, span, :], w_down_ref.at[span, :]
        return w_up_hbm.at[layer, :, span], w_up_ref.at[:, span]

    def chunk_copy(k):
        return pltpu.make_async_copy(chunk(k)[0], stage_ref.at[k % STAGE_SLOTS], sem.at[k % STAGE_SLOTS])

    def fetch(k):
        chunk_copy(k).wait()
        chunk(k)[1][...] = stage_ref[k % STAGE_SLOTS].astype(jnp.bfloat16)
        if k + STAGE_SLOTS < N_WEIGHT_CHUNKS:
            chunk_copy(k + STAGE_SLOTS).start()

    def channel_mix(x, mix, fetch_weights):
        def before_matmul(m):
            if fetch_weights and m + 1 < N_WEIGHT_CHUNKS:
                fetch(m + 1)

        if fetch_weights:
            fetch(0)
        before_matmul(0)
        x1 = x + jnp.dot(mix, w_out_ref[...], preferred_element_type=jnp.float32)
        h2 = _rmsnorm(x1, n2_ref[...]).astype(jnp.bfloat16)
        acc = x1
        for j in range(N_FF_CHUNKS):
            cols = slice(j * FF_CHUNK, (j + 1) * FF_CHUNK)
            before_matmul(1 + 2 * j)
            hid = jnp.dot(h2, w_up_ref[:, cols], preferred_element_type=jnp.float32)
            act = jnp.square(jnp.maximum(hid, 0.0)).astype(jnp.bfloat16)
            before_matmul(2 + 2 * j)
            acc = acc + jnp.dot(act, w_down_ref[cols, :], preferred_element_type=jnp.float32)
        if final:
            acc = _rmsnorm(acc, final_g_ref[...])
        return acc

    @pl.when(step == 0)
    def _():
        for k in range(STAGE_SLOTS):
            chunk_copy(k).start()
        y_ref[...] = channel_mix(x_ref[...], mix_ref[...], True)

    @pl.when(step > 0)
    def _():
        y_ref[...] = channel_mix(x_ref[...], mix_ref[...], False)

    @pl.when(step == pl.num_programs(0) - 1)
    def _():
        ys_ref[...] = channel_mix(xs_ref[...], mix_s_ref[...], False)


def _channel_mixer(layer, x, mix, xs, mix_s, p, final_g):
    rows = x.shape[0]
    n_seq = xs.shape[0]
    tile = CHANNEL_TILE
    assert rows // tile > 1
    bf16 = jnp.bfloat16
    return pl.pallas_call(
        functools.partial(_channel_kernel, layer=layer, final=layer == DEPTH - 1),
        grid=(rows // tile,),
        in_specs=[
            pl.BlockSpec((tile, D_MODEL), lambda i: (i, 0)),
            pl.BlockSpec((tile, D_MODEL), lambda i: (i, 0)),
            pl.BlockSpec((n_seq, D_MODEL), lambda i: (0, 0)),
            pl.BlockSpec((n_seq, D_MODEL), lambda i: (0, 0)),
            pl.BlockSpec((DEPTH, D_MODEL), lambda i: (0, 0)),
            pl.BlockSpec((1, D_MODEL), lambda i: (0, 0)),
            pl.BlockSpec(memory_space=pl.ANY),
            pl.BlockSpec(memory_space=pl.ANY),
            pl.BlockSpec(memory_space=pl.ANY),
        ],
        out_specs=[
            pl.BlockSpec((tile, D_MODEL), lambda i: (i, 0)),
            pl.BlockSpec((n_seq, D_MODEL), lambda i: (0, 0)),
        ],
        out_shape=[
            jax.ShapeDtypeStruct((rows, D_MODEL), jnp.float32),
            jax.ShapeDtypeStruct((n_seq, D_MODEL), jnp.float32),
        ],
        scratch_shapes=[
            pltpu.VMEM((D_MODEL, D_MODEL), bf16),
            pltpu.VMEM((D_MODEL, D_FF), bf16),
            pltpu.VMEM((D_FF, D_MODEL), bf16),
            pltpu.VMEM((STAGE_SLOTS, D_MODEL, D_MODEL), jnp.float32),
            pltpu.SemaphoreType.DMA((STAGE_SLOTS,)),
        ],
        compiler_params=pltpu.CompilerParams(
            dimension_semantics=("arbitrary",), vmem_limit_bytes=VMEM_LIMIT),
        name=f"channel_mixer_l{layer}",
    )(x, mix, xs, mix_s, p["n2"], final_g, p["w_out"], p["w_up"], p["w_down"])


def kernel(x_prompt, x_sample, state_gla, state_pool, norm1_g, w_in, w_gate, b_gate, gla_norm_g, pool_w,
           pool_scale, w_out, norm2_g, w_up, w_down, final_g):
    B, T, _ = x_prompt.shape
    n_seq = x_sample.shape[0]
    bf16 = jnp.bfloat16

    w_code = jnp.pad(w_in[:, :, W_IN_GATE:W_IN_U], ((0, 0), (0, 0), (0, LANES - GATE_RANK)))
    pw = pool_w.reshape(DEPTH, len(POOL_WINDOWS) // 2, 2, POOL_GROUP, POOL_GROUP)
    zero = jnp.zeros_like(pw[:, :, 0])
    pool_pairs = jnp.concatenate([jnp.concatenate([pw[:, :, 0], zero], axis=-1),
                                  jnp.concatenate([zero, pw[:, :, 1]], axis=-1)], axis=-2)
    params = {
        "n1": norm1_g,
        "w_main": w_in[:, :, :W_IN_GATE].astype(bf16),
        "w_code": w_code.astype(bf16),
        "w_u": w_in[:, :, W_IN_U:].astype(bf16),
        "w_gate": jnp.pad(w_gate, ((0, 0), (0, LANES - GATE_RANK), (0, 0))).astype(bf16),
        "b_gate": b_gate,
        "gla_g": gla_norm_g,
        "pool_w": pool_pairs.astype(bf16),
        "pool_scale": pool_scale,
        "w_out": w_out,
        "n2": norm2_g,
        "w_up": w_up,
        "w_down": w_down,
    }
    final_g2 = final_g.reshape(1, D_MODEL)
    s_state = state_gla.reshape(DEPTH, n_seq, KEY_WIDTH, HEAD_V)
    pool_rows = jnp.swapaxes(state_pool, 1, 2)

    xp = x_prompt
    xs = x_sample.reshape(n_seq, D_MODEL)
    states = None
    for layer in range(DEPTH):
        mix_p, gla_p, hist_p, mix_s, gla_s, pool_s = _mixer(layer, xp, xs, s_state, pool_rows, params, states)
        states = (gla_p, hist_p, gla_s, pool_s)
        xp, xs = _channel_mixer(layer, xp.reshape(B * T, D_MODEL), mix_p.reshape(B * T, D_MODEL), xs, mix_s,
                                params, final_g2)
        xp = xp.reshape(B, T, D_MODEL)
    return (xp, xs.reshape(n_seq, 1, D_MODEL),
            gla_p.reshape(DEPTH, B, N_HEADS, HEAD_K, HEAD_V), hist_p[:, :, POOL_HIST - POOL_BUF:],
            gla_s.reshape(DEPTH, n_seq, N_HEADS, HEAD_K, HEAD_V), jnp.swapaxes(pool_s, 1, 2))
```

```python
import functools

import jax
import jax.numpy as jnp
from jax import lax
from jax.experimental import pallas as pl
from jax.experimental.pallas import tpu as pltpu

D_MODEL = 1024
DEPTH = 4
N_HEADS = 4
HEAD_K = 64
HEAD_V = 128
KEY_WIDTH = N_HEADS * HEAD_K
GLA_WIDTH = N_HEADS * HEAD_V
GATE_RANK = 16
GATE_TEMP = 16.0
CHUNK = 64
POOL_WIDTH = 512
POOL_WINDOWS = (2, 4, 8, 16)
POOL_GROUP = 128
POOL_PAIR = 2 * POOL_GROUP
POOL_BUF = 15
POOL_HIST = 16
D_FF = 4 * D_MODEL
FF_CHUNK = 1024
EPS = 1e-6
LANES = 128
SUBLANES = 8

COL_Q = 0
COL_K = KEY_WIDTH
COL_V = 2 * KEY_WIDTH
COL_G = COL_V + GLA_WIDTH
COL_U = COL_G + GLA_WIDTH
Z_WIDTH = COL_U + POOL_WIDTH
W_IN_GATE = COL_U
W_IN_U = COL_U + GATE_RANK

PROMPT_TILE = 1024
PROMPT_SUB = 256
GLA_PAIR = 2 * CHUNK
PROJ_PIECE = 256
OUT_PROJECT_LAG = 6
CHANNEL_TILE = 512
VMEM_LIMIT = 56 * 1024 * 1024


def _rmsnorm(x, g):
    return x * lax.rsqrt(jnp.mean(x * x, axis=-1, keepdims=True) + EPS) * g


def _log_sigmoid(x):
    return jnp.minimum(x, 0.0) - jnp.log(1.0 + jnp.exp(-jnp.abs(x)))


def _bf16_round(x):
    return x.astype(jnp.bfloat16).astype(jnp.float32)


def _head_lane_mask(rows):
    lane = lax.broadcasted_iota(jnp.int32, (rows, KEY_WIDTH), 1)
    return [(lane >= h * HEAD_K) & (lane < (h + 1) * HEAD_K) for h in range(N_HEADS)]


def _gate_log_decay(a_low, w_gate_ref, b_gate_ref):
    pre = jnp.dot(a_low.astype(jnp.bfloat16), w_gate_ref[...], preferred_element_type=jnp.float32) + b_gate_ref[...]
    return _log_sigmoid(pre) / GATE_TEMP


def _project(x, n1_ref, w_main_ref, w_code_ref, w_u_ref, w_gate_ref, b_gate_ref):
    h = _rmsnorm(x, n1_ref[...]).astype(jnp.bfloat16)
    qkvg = jnp.dot(h, w_main_ref[...], preferred_element_type=jnp.float32)
    a_low = jnp.dot(h, w_code_ref[...], preferred_element_type=jnp.float32)
    u = jnp.dot(h, w_u_ref[...], preferred_element_type=jnp.float32)
    return qkvg, u, _gate_log_decay(a_low, w_gate_ref, b_gate_ref)


def _gla_finish(o, g, gla_g):
    outs = []
    for h in range(N_HEADS):
        sl = slice(h * HEAD_V, (h + 1) * HEAD_V)
        outs.append(_rmsnorm(o[:, sl], gla_g) * (g[:, sl] * jax.nn.sigmoid(g[:, sl])))
    return jnp.concatenate(outs, axis=-1)


def _trace_interleaved(a, b):
    i = j = 0
    while i < len(a) or j < len(b):
        if j >= len(b) or (i < len(a) and i * len(b) <= j * len(a)):
            a[i]()
            i += 1
        else:
            b[j]()
            j += 1


def _prompt_project_tasks(load_x, slot, n1_ref, w_main_ref, w_code_ref, w_u_ref, w_gate_ref, b_gate_ref,
                          z_ref, b_ref):
    v = {}

    def norm():
        v["h"] = _rmsnorm(load_x(), n1_ref[...]).astype(jnp.bfloat16)

    def piece(lo, hi):
        w_ref, first = (w_main_ref, 0) if hi <= COL_U else (w_u_ref, COL_U)

        def run():
            z_ref[slot, :, lo:hi] = jnp.dot(v["h"], w_ref[:, lo - first:hi - first],
                                            preferred_element_type=jnp.float32)
        return run

    def gate_code():
        v["a_low"] = jnp.dot(v["h"], w_code_ref[...], preferred_element_type=jnp.float32)

    def gate_pre():
        v["pre"] = (jnp.dot(v["a_low"].astype(jnp.bfloat16), w_gate_ref[...], preferred_element_type=jnp.float32)
                    + b_gate_ref[...])

    def log_decay():
        v["b"] = _log_sigmoid(v["pre"]) / GATE_TEMP

    def scan(shifts, last):
        def run():
            b = v["b"]
            row = lax.broadcasted_iota(jnp.int32, b.shape, 0) % CHUNK
            for shift in shifts:
                b = b + jnp.where(row >= shift, pltpu.roll(b, shift, axis=0), 0.0)
            v["b"] = b
            if last:
                b_ref[slot] = b
        return run

    half = PROJ_PIECE
    return [
        norm, gate_code,
        piece(COL_Q, COL_K), gate_pre,
        piece(COL_K, COL_V), log_decay,
        piece(COL_V, COL_V + half), scan((1, 2), False),
        piece(COL_V + half, COL_G), scan((4, 8), False),
        piece(COL_G, COL_G + half), scan((16, 32), True),
        piece(COL_G + half, COL_U), piece(COL_U, COL_U + half), piece(COL_U + half, Z_WIDTH),
    ]


def _prompt_mix_tasks(slot, row0, z_ref, b_ref, carry, first_pos, gla_g_ref, pool_w_ref, pool_scale_ref, mix_ref,
                      base):
    head_mask = _head_lane_mask(CHUNK)
    scale = HEAD_K ** -0.5
    sub = GLA_PAIR
    n_chunks = sub // CHUNK
    part = slice(row0, row0 + sub)
    shared = {}

    def transpose_keys():
        b = b_ref[slot, part, :]
        k = z_ref[slot, part, COL_K:COL_V]
        last = [b[(c + 1) * CHUNK - 1:(c + 1) * CHUNK] for c in range(n_chunks)]
        b_last = jnp.concatenate([jnp.broadcast_to(row, (CHUNK, KEY_WIDTH)) for row in last], axis=0)
        shared["kt_t"] = (k * jnp.exp(-b)).T.astype(jnp.bfloat16)
        shared["ke_t"] = (k * jnp.exp(b_last - b)).T.astype(jnp.bfloat16)
        shared["v"] = z_ref[slot, part, COL_V:COL_G].astype(jnp.bfloat16)
        decay = jnp.exp(jnp.concatenate(last + [jnp.zeros((SUBLANES - n_chunks, KEY_WIDTH), jnp.float32)], axis=0))
        shared["decay"] = jnp.broadcast_to(decay[:, :, None], (SUBLANES, KEY_WIDTH, HEAD_V))

    def chunk_tasks(c):
        rows = slice(row0 + c * CHUNK, row0 + (c + 1) * CHUNK)
        out_rows = slice(base + c * CHUNK, base + (c + 1) * CHUNK)
        token = lax.broadcasted_iota(jnp.int32, (CHUNK, sub), 1) - c * CHUNK
        causal = (token >= 0) & (token <= lax.broadcasted_iota(jnp.int32, (CHUNK, sub), 0))
        in_chunk = (lax.broadcasted_iota(jnp.int32, (KEY_WIDTH, sub), 1) // CHUNK) == c
        v = {}

        def prepare():
            qt = z_ref[slot, rows, COL_Q:COL_K] * jnp.exp(b_ref[slot, rows, :]) * scale
            v["q_stack"] = jnp.concatenate([jnp.where(m, qt, 0.0) for m in head_mask], axis=0).astype(jnp.bfloat16)

        def update_product():
            ke_t = jnp.where(in_chunk, shared["ke_t"], jnp.zeros_like(shared["ke_t"]))
            v["upd"] = jnp.concatenate(
                [jnp.dot(ke_t[h * HEAD_K:(h + 1) * HEAD_K], shared["v"][:, h * HEAD_V:(h + 1) * HEAD_V],
                         preferred_element_type=jnp.float32) for h in range(N_HEADS)], axis=0)

        def query_product():
            rhs = jnp.concatenate([carry["st"].astype(jnp.bfloat16), shared["kt_t"]], axis=1)
            v["r"] = jnp.dot(v["q_stack"], rhs, preferred_element_type=jnp.float32)

        def update_state():
            carry["st"] = shared["decay"][c] * carry["st"] + v["upd"]

        def intra():
            r = v["r"]
            outs = []
            for h in range(N_HEADS):
                hr = slice(h * CHUNK, (h + 1) * CHUNK)
                att = jnp.where(causal, r[hr, HEAD_V:HEAD_V + sub], 0.0).astype(jnp.bfloat16)
                o_intra = jnp.dot(att, shared["v"][:, h * HEAD_V:(h + 1) * HEAD_V], preferred_element_type=jnp.float32)
                outs.append(o_intra + r[hr, 0:HEAD_V])
            v["o"] = jnp.concatenate(outs, axis=-1)

        def finish():
            g_c = z_ref[slot, rows, COL_G:COL_U]
            mix_ref[out_rows, 0:GLA_WIDTH] = _gla_finish(v["o"], g_c, gla_g_ref[...]).astype(mix_ref.dtype)

        return dict(prepare=prepare, update_product=update_product, query_product=query_product,
                    update_state=update_state, intra=intra, finish=finish)

    pooled = {}

    def window_mean_task(gi, w):
        lanes = slice(gi * POOL_GROUP, (gi + 1) * POOL_GROUP)

        def window_mean():
            u = z_ref[slot, part, COL_U + gi * POOL_GROUP:COL_U + (gi + 1) * POOL_GROUP]
            s = jnp.concatenate([carry["hist"][:, lanes], u], axis=0)
            shift = 1
            while shift < w:
                s = s + pltpu.roll(s, shift, axis=0)
                shift *= 2
            seen = first_pos + 1 + lax.broadcasted_iota(jnp.int32, (sub, POOL_GROUP), 0)
            if "inv_seen" not in pooled:
                pooled["inv_seen"] = 1.0 / seen.astype(jnp.float32)
            inv_count = jnp.where(seen < w, pooled["inv_seen"], 1.0 / w)
            pooled[gi] = (s[POOL_HIST:] * inv_count - u).astype(jnp.bfloat16)

        return window_mean

    def group_map_task(pair):
        lanes = slice(pair * POOL_PAIR, (pair + 1) * POOL_PAIR)

        def group_map():
            both = jnp.concatenate([pooled[2 * pair], pooled[2 * pair + 1]], axis=-1)
            pg = jnp.dot(both, pool_w_ref[pair], preferred_element_type=jnp.float32)
            pg = pg * pool_scale_ref[:, lanes]
            mix_ref[base:base + sub, GLA_WIDTH + pair * POOL_PAIR:GLA_WIDTH + (pair + 1) * POOL_PAIR] = (
                pg.astype(mix_ref.dtype))

        return group_map

    def keep_history():
        carry["hist"] = z_ref[slot, row0 + sub - POOL_HIST:row0 + sub, COL_U:Z_WIDTH]

    assert n_chunks == 2
    c0, c1 = chunk_tasks(0), chunk_tasks(1)
    means = [window_mean_task(gi, w) for gi, w in enumerate(POOL_WINDOWS)]
    maps = [group_map_task(pair) for pair in range(len(POOL_WINDOWS) // 2)]
    return [
        transpose_keys, c0["prepare"], c1["prepare"], c0["update_product"], c0["query_product"], c1["update_product"],
        c0["update_state"], c1["query_product"], means[0], c0["intra"], c1["update_state"],
        means[1], c1["intra"], maps[0], c0["finish"], means[2], c1["finish"], means[3], maps[1], keep_history,
    ]


def _sample_project(xs_ref, n1_ref, w_main_ref, w_code_ref, w_u_ref, w_gate_ref, b_gate_ref, pool_w_ref, pool_scale_ref,
                    pool_state_ref, mix_s_ref, pool_new_ref, dec_ref, ke_ref, qt_ref, v_ref, g_ref, o_ref):
    n_seq = xs_ref.shape[0]
    qkvg, u, log_a = _project(xs_ref[...], n1_ref, w_main_ref, w_code_ref, w_u_ref, w_gate_ref, b_gate_ref)
    q = qkvg[:, COL_Q:COL_K]
    k = qkvg[:, COL_K:COL_V]
    v = _bf16_round(qkvg[:, COL_V:COL_G])
    qt = _bf16_round(q * jnp.exp(log_a) * (HEAD_K ** -0.5))
    kt = _bf16_round(k * jnp.exp(-log_a))
    ke = _bf16_round(k * jnp.exp(log_a - log_a))
    prod = qt * kt
    o_intra = []
    for h, m in enumerate(_head_lane_mask(n_seq)):
        att = _bf16_round(jnp.sum(jnp.where(m, prod, 0.0), axis=-1, keepdims=True))
        o_intra.append(att * v[:, h * HEAD_V:(h + 1) * HEAD_V])
    o_ref[...] = jnp.concatenate(o_intra, axis=-1)
    dec_ref[...] = jnp.exp(log_a)
    ke_ref[...] = ke
    qt_ref[...] = qt
    v_ref[...] = v
    g_ref[...] = qkvg[:, COL_G:COL_U]

    pooled = []
    for gi, w in enumerate(POOL_WINDOWS):
        lanes = slice(gi * POOL_GROUP, (gi + 1) * POOL_GROUP)
        s = u[:, lanes]
        for j in range(POOL_BUF - (w - 1), POOL_BUF):
            s = s + pool_state_ref[j, :, lanes]
        pooled.append((s / float(w) - u[:, lanes]).astype(jnp.bfloat16))
    for pair in range(len(POOL_WINDOWS) // 2):
        lanes = slice(pair * POOL_PAIR, (pair + 1) * POOL_PAIR)
        both = jnp.concatenate(pooled[2 * pair:2 * pair + 2], axis=-1)
        pg = jnp.dot(both, pool_w_ref[pair], preferred_element_type=jnp.float32) * pool_scale_ref[:, lanes]
        mix_s_ref[:, GLA_WIDTH + pair * POOL_PAIR:GLA_WIDTH + (pair + 1) * POOL_PAIR] = pg.astype(mix_s_ref.dtype)
    for j in range(POOL_BUF - 1):
        pool_new_ref[j] = pool_state_ref[j + 1]
    pool_new_ref[POOL_BUF - 1] = u


def _sample_state_tasks(first_seq, s_ref, s_new_ref, dec_ref, ke_ref, qt_ref, v_ref, o_ref):
    rows = pl.ds(pl.multiple_of(first_seq, SUBLANES), SUBLANES)

    def update_state():
        v = v_ref[rows, :]
        v_rows = jnp.concatenate(
            [jnp.broadcast_to(v[:, None, h * HEAD_V:(h + 1) * HEAD_V], (SUBLANES, HEAD_K, HEAD_V))
             for h in range(N_HEADS)], axis=1)
        s_new_ref[...] = dec_ref[rows, :][:, :, None] * s_ref[...] + ke_ref[rows, :][:, :, None] * v_rows

    def query_state():
        lane = lax.broadcasted_iota(jnp.int32, (SUBLANES, KEY_WIDTH), 1)
        head = lax.broadcasted_iota(jnp.int32, (SUBLANES, KEY_WIDTH), 0)
        own_head = lane // HEAD_K == head
        q_rows = jnp.where(own_head[None], qt_ref[rows, :][:, None, :], 0.0).astype(jnp.bfloat16)
        o_inter = jnp.einsum("bhk,bkv->bhv", q_rows, s_ref[...].astype(jnp.bfloat16),
                             preferred_element_type=jnp.float32)
        o_ref[rows, :] = o_ref[rows, :] + jnp.concatenate([o_inter[:, h, :] for h in range(N_HEADS)], axis=-1)

    return [update_state, query_state]


def _mixer_kernel(*refs, layer, chained):
    (x_ref, x_next_ref, n1_ref, w_main_ref, w_code_ref, w_u_ref, w_gate_ref, b_gate_ref, gla_g_ref, pool_w_ref,
     pool_scale_ref, w_out_ref, xs_ref, s_ref, pool_state_ref) = refs[:15]
    n1_ref, b_gate_ref, gla_g_ref, pool_scale_ref = (
        ref.at[pl.ds(layer, 1), :] for ref in (n1_ref, b_gate_ref, gla_g_ref, pool_scale_ref))
    (x1_ref, s_fin_ref, hist_out_ref, x1_s_ref, s_new_ref, pool_new_ref,
     st_ref, hist_ref, z_ref, b_ref, dec_ref, ke_ref, qt_ref, v_ref, g_ref, o_ref, mix_ref, mix_s_ref) = refs[-18:]
    tile = x_ref.shape[0]
    n_sub = tile // PROMPT_SUB
    t = pl.program_id(1)
    step = pl.program_id(0) * pl.num_programs(1) + t
    last_step = pl.num_programs(0) * pl.num_programs(1) - 1

    def project_tasks(load_x, slot):
        return _prompt_project_tasks(load_x, slot, n1_ref, w_main_ref, w_code_ref, w_u_ref, w_gate_ref, b_gate_ref,
                                     z_ref, b_ref)

    @pl.when(t == 0)
    def _():
        st_ref[...] = jnp.zeros_like(st_ref)
        hist_ref[...] = jnp.zeros_like(hist_ref)

    @pl.when(step == 0)
    def _():
        for task in project_tasks(lambda: x_ref[0:PROMPT_SUB], 0):
            task()
        _sample_project(xs_ref, n1_ref, w_main_ref, w_code_ref, w_u_ref, w_gate_ref, b_gate_ref, pool_w_ref,
                        pool_scale_ref, pool_state_ref, mix_s_ref, pool_new_ref, dec_ref, ke_ref, qt_ref, v_ref, g_ref,
                        o_ref)

    sample_tasks = _sample_state_tasks(step * SUBLANES, s_ref, s_new_ref, dec_ref, ke_ref, qt_ref, v_ref, o_ref)
    carry = {"st": st_ref[...], "hist": hist_ref[...]}
    pending = []

    def out_project_task(group):
        rows = slice(group * GLA_PAIR, (group + 1) * GLA_PAIR)

        def out_project():
            x1_ref[rows, :] = x_ref[rows, :] + jnp.dot(mix_ref[group % 2], w_out_ref[...],
                                                       preferred_element_type=jnp.float32)
        return out_project

    for sb in range(n_sub):
        if sb + 1 < n_sub:
            load_next = lambda sb=sb: x_ref[(sb + 1) * PROMPT_SUB:(sb + 2) * PROMPT_SUB]
        else:
            load_next = lambda: x_next_ref[...]
        mix_tasks = []
        for row0 in range(0, PROMPT_SUB, GLA_PAIR):
            group = (sb * PROMPT_SUB + row0) // GLA_PAIR
            tasks = _prompt_mix_tasks(sb % 2, row0, z_ref, b_ref, carry, t * tile + sb * PROMPT_SUB + row0,
                                      gla_g_ref, pool_w_ref, pool_scale_ref, mix_ref.at[group % 2], 0)
            if pending:
                tasks.insert(OUT_PROJECT_LAG, pending.pop())
            pending.append(out_project_task(group))
            mix_tasks += tasks
        if sample_tasks and sb >= 1:
            mix_tasks.insert(len(mix_tasks) // 2, sample_tasks.pop(0))
        _trace_interleaved(project_tasks(load_next, (sb + 1) % 2), mix_tasks)
    pending.pop()()
    st_ref[...] = carry["st"]
    hist_ref[...] = carry["hist"]

    @pl.when(t == pl.num_programs(1) - 1)
    def _():
        s_fin_ref[...] = carry["st"]
        hist_out_ref[...] = carry["hist"]

    @pl.when(step == last_step)
    def _():
        mix_s_ref[:, 0:GLA_WIDTH] = _gla_finish(o_ref[...], g_ref[...], gla_g_ref[...]).astype(mix_s_ref.dtype)
        x1_s_ref[...] = xs_ref[...] + jnp.dot(mix_s_ref[...], w_out_ref[...], preferred_element_type=jnp.float32)


def _mixer(layer, x, xs, s_state, pool_state, p, stacked):
    B, T, _ = x.shape
    n_seq = xs.shape[0]
    tile = PROMPT_TILE
    n_tiles = T // tile
    n_sub = tile // PROMPT_SUB
    assert n_sub % 2 == 0
    assert n_sub >= 3 and B * n_tiles * SUBLANES == n_seq
    assert PROMPT_SUB % GLA_PAIR == 0
    f32 = jnp.float32
    whole = lambda *shape: pl.BlockSpec((None,) + shape, lambda b, t: (layer,) + (0,) * len(shape))
    fixed = lambda *shape: pl.BlockSpec(shape, lambda b, t: (0,) * len(shape))
    all_layers = lambda width: fixed(DEPTH, width)
    chained = stacked is not None
    operands = [x, x, p["n1"], p["w_main"], p["w_code"], p["w_u"], p["w_gate"], p["b_gate"], p["gla_g"], p["pool_w"],
                p["pool_scale"], p["w_out"], xs, s_state, pool_state]

    def next_first_sub_block(b, t):
        n = jnp.minimum(b * n_tiles + t + 1, B * n_tiles - 1)
        return (n // n_tiles, (n % n_tiles) * n_sub, 0)

    sample_state_block = pl.BlockSpec((None, SUBLANES, KEY_WIDTH, HEAD_V), lambda b, t: (layer, b * n_tiles + t, 0, 0))
    in_specs = [
        pl.BlockSpec((None, tile, D_MODEL), lambda b, t: (b, t, 0)),
        pl.BlockSpec((None, PROMPT_SUB, D_MODEL), next_first_sub_block),
        all_layers(D_MODEL),
        whole(D_MODEL, COL_U),
        whole(D_MODEL, LANES),
        whole(D_MODEL, POOL_WIDTH),
        whole(LANES, KEY_WIDTH),
        all_layers(KEY_WIDTH),
        all_layers(HEAD_V),
        whole(len(POOL_WINDOWS) // 2, POOL_PAIR, POOL_PAIR),
        all_layers(POOL_WIDTH),
        whole(D_MODEL, D_MODEL),
        fixed(n_seq, D_MODEL),
        sample_state_block,
        whole(POOL_BUF, n_seq, POOL_WIDTH),
    ]
    aliases = {}
    if chained:
        aliases = {len(operands): 1, len(operands) + 1: 2, len(operands) + 2: 4, len(operands) + 3: 5}
        operands += list(stacked)
        in_specs += [pl.BlockSpec(memory_space=pl.ANY)] * 4
    return pl.pallas_call(
        functools.partial(_mixer_kernel, layer=layer, chained=chained),
        grid=(B, n_tiles),
        in_specs=in_specs,
        out_specs=[
            pl.BlockSpec((None, tile, D_MODEL), lambda b, t: (b, t, 0)),
            pl.BlockSpec((None, None, KEY_WIDTH, HEAD_V), lambda b, t: (layer, b, 0, 0)),
            pl.BlockSpec((None, None, POOL_HIST, POOL_WIDTH), lambda b, t: (layer, b, 0, 0)),
            fixed(n_seq, D_MODEL),
            sample_state_block,
            whole(POOL_BUF, n_seq, POOL_WIDTH),
        ],
        out_shape=[
            jax.ShapeDtypeStruct((B, T, D_MODEL), f32),
            jax.ShapeDtypeStruct((DEPTH, B, KEY_WIDTH, HEAD_V), f32),
            jax.ShapeDtypeStruct((DEPTH, B, POOL_HIST, POOL_WIDTH), f32),
            jax.ShapeDtypeStruct((n_seq, D_MODEL), f32),
            jax.ShapeDtypeStruct((DEPTH, n_seq, KEY_WIDTH, HEAD_V), f32),
            jax.ShapeDtypeStruct((DEPTH, POOL_BUF, n_seq, POOL_WIDTH), f32),
        ],
        scratch_shapes=[
            pltpu.VMEM((KEY_WIDTH, HEAD_V), f32),
            pltpu.VMEM((POOL_HIST, POOL_WIDTH), f32),
            pltpu.VMEM((2, PROMPT_SUB, Z_WIDTH), f32),
            pltpu.VMEM((2, PROMPT_SUB, KEY_WIDTH), f32),
            pltpu.VMEM((n_seq, KEY_WIDTH), f32),
            pltpu.VMEM((n_seq, KEY_WIDTH), f32),
            pltpu.VMEM((n_seq, KEY_WIDTH), f32),
            pltpu.VMEM((n_seq, GLA_WIDTH), f32),
            pltpu.VMEM((n_seq, GLA_WIDTH), f32),
            pltpu.VMEM((n_seq, GLA_WIDTH), f32),
            pltpu.VMEM((2, GLA_PAIR, D_MODEL), jnp.bfloat16),
            pltpu.VMEM((n_seq, D_MODEL), jnp.bfloat16),
        ],
        input_output_aliases=aliases,
        compiler_params=pltpu.CompilerParams(
            dimension_semantics=("arbitrary", "arbitrary"), vmem_limit_bytes=VMEM_LIMIT),
        name=f"mixer_l{layer}",
    )(*operands)


N_FF_CHUNKS = D_FF // FF_CHUNK
N_WEIGHT_CHUNKS = 2 * N_FF_CHUNKS
STAGE_SLOTS = 3
assert FF_CHUNK == D_MODEL


def _channel_kernel(x_ref, xs_ref, n2_ref, final_g_ref, w_up_hbm, w_down_hbm,
                    y_ref, ys_ref, w_up_ref, w_down_ref, stage_ref, sem, *, layer, final):
    step = pl.program_id(0)
    n2_ref = n2_ref.at[pl.ds(layer, 1), :]

    def chunk(k):
        j, down = divmod(k, 2)
        span = pl.ds(j * FF_CHUNK, FF_CHUNK)
        if down:
            return w_down_hbm.at[layer, span, :], w_down_ref.at[span, :]
        return w_up_hbm.at[layer, :, span], w_up_ref.at[:, span]

    def chunk_copy(k):
        return pltpu.make_async_copy(chunk(k)[0], stage_ref.at[k % STAGE_SLOTS], sem.at[k % STAGE_SLOTS])

    def fetch(k):
        chunk_copy(k).wait()
        chunk(k)[1][...] = stage_ref[k % STAGE_SLOTS].astype(jnp.bfloat16)
        if k + STAGE_SLOTS < N_WEIGHT_CHUNKS:
            chunk_copy(k + STAGE_SLOTS).start()

    def channel_mix(x1, fetch_weights):
        def before_matmul(m):
            if fetch_weights and m + 1 < N_WEIGHT_CHUNKS:
                fetch(m + 1)

        if fetch_weights:
            fetch(0)
        h2 = _rmsnorm(x1, n2_ref[...]).astype(jnp.bfloat16)
        acc = x1
        for j in range(N_FF_CHUNKS):
            cols = slice(j * FF_CHUNK, (j + 1) * FF_CHUNK)
            before_matmul(2 * j)
            hid = jnp.dot(h2, w_up_ref[:, cols], preferred_element_type=jnp.float32)
            act = jnp.square(jnp.maximum(hid, 0.0)).astype(jnp.bfloat16)
            before_matmul(2 * j + 1)
            acc = acc + jnp.dot(act, w_down_ref[cols, :], preferred_element_type=jnp.float32)
        if final:
            acc = _rmsnorm(acc, final_g_ref[...])
        return acc

    @pl.when(step == 0)
    def _():
        for k in range(STAGE_SLOTS):
            chunk_copy(k).start()
        y_ref[...] = channel_mix(x_ref[...], True)

    @pl.when(step > 0)
    def _():
        y_ref[...] = channel_mix(x_ref[...], False)

    @pl.when(step == pl.num_programs(0) - 1)
    def _():
        ys_ref[...] = channel_mix(xs_ref[...], False)


def _channel_mixer(layer, x, xs, p, final_g):
    rows = x.shape[0]
    n_seq = xs.shape[0]
    tile = CHANNEL_TILE
    assert rows // tile > 1
    bf16 = jnp.bfloat16
    return pl.pallas_call(
        functools.partial(_channel_kernel, layer=layer, final=layer == DEPTH - 1),
        grid=(rows // tile,),
        in_specs=[
            pl.BlockSpec((tile, D_MODEL), lambda i: (i, 0)),
            pl.BlockSpec((n_seq, D_MODEL), lambda i: (0, 0)),
            pl.BlockSpec((DEPTH, D_MODEL), lambda i: (0, 0)),
            pl.BlockSpec((1, D_MODEL), lambda i: (0, 0)),
            pl.BlockSpec(memory_space=pl.ANY),
            pl.BlockSpec(memory_space=pl.ANY),
        ],
        out_specs=[
            pl.BlockSpec((tile, D_MODEL), lambda i: (i, 0)),
            pl.BlockSpec((n_seq, D_MODEL), lambda i: (0, 0)),
        ],
        out_shape=[
            jax.ShapeDtypeStruct((rows, D_MODEL), jnp.float32),
            jax.ShapeDtypeStruct((n_seq, D_MODEL), jnp.float32),
        ],
        scratch_shapes=[
            pltpu.VMEM((D_MODEL, D_FF), bf16),
            pltpu.VMEM((D_FF, D_MODEL), bf16),
            pltpu.VMEM((STAGE_SLOTS, D_MODEL, D_MODEL), jnp.float32),
            pltpu.SemaphoreType.DMA((STAGE_SLOTS,)),
        ],
        compiler_params=pltpu.CompilerParams(
            dimension_semantics=("arbitrary",), vmem_limit_bytes=VMEM_LIMIT),
        name=f"channel_mixer_l{layer}",
    )(x, xs, p["n2"], final_g, p["w_up"], p["w_down"])


def kernel(x_prompt, x_sample, state_gla, state_pool, norm1_g, w_in, w_gate, b_gate, gla_norm_g, pool_w,
           pool_scale, w_out, norm2_g, w_up, w_down, final_g):
    B, T, _ = x_prompt.shape
    n_seq = x_sample.shape[0]
    bf16 = jnp.bfloat16

    w_code = jnp.pad(w_in[:, :, W_IN_GATE:W_IN_U], ((0, 0), (0, 0), (0, LANES - GATE_RANK)))
    pw = pool_w.reshape(DEPTH, len(POOL_WINDOWS) // 2, 2, POOL_GROUP, POOL_GROUP)
    zero = jnp.zeros_like(pw[:, :, 0])
    pool_pairs = jnp.concatenate([jnp.concatenate([pw[:, :, 0], zero], axis=-1),
                                  jnp.concatenate([zero, pw[:, :, 1]], axis=-1)], axis=-2)
    params = {
        "n1": norm1_g,
        "w_main": w_in[:, :, :W_IN_GATE].astype(bf16),
        "w_code": w_code.astype(bf16),
        "w_u": w_in[:, :, W_IN_U:].astype(bf16),
        "w_gate": jnp.pad(w_gate, ((0, 0), (0, LANES - GATE_RANK), (0, 0))).astype(bf16),
        "b_gate": b_gate,
        "gla_g": gla_norm_g,
        "pool_w": pool_pairs.astype(bf16),
        "pool_scale": pool_scale,
        "w_out": w_out.astype(bf16),
        "n2": norm2_g,
        "w_up": w_up,
        "w_down": w_down,
    }
    final_g2 = final_g.reshape(1, D_MODEL)
    s_state = state_gla.reshape(DEPTH, n_seq, KEY_WIDTH, HEAD_V)
    pool_rows = jnp.swapaxes(state_pool, 1, 2)

    xp = x_prompt
    xs = x_sample.reshape(n_seq, D_MODEL)
    states = None
    for layer in range(DEPTH):
        xp, gla_p, hist_p, xs, gla_s, pool_s = _mixer(layer, xp, xs, s_state, pool_rows, params, states)
        states = (gla_p, hist_p, gla_s, pool_s)
        xp, xs = _channel_mixer(layer, xp.reshape(B * T, D_MODEL), xs, params, final_g2)
        xp = xp.reshape(B, T, D_MODEL)
    return (xp, xs.reshape(n_seq, 1, D_MODEL),
            gla_p.reshape(DEPTH, B, N_HEADS, HEAD_K, HEAD_V), hist_p[:, :, POOL_HIST - POOL_BUF:],
            gla_s.reshape(DEPTH, n_seq, N_HEADS, HEAD_K, HEAD_V), jnp.swapaxes(pool_s, 1, 2))
```

```python
import functools

import jax
import jax.numpy as jnp
from jax import lax
from jax.experimental import pallas as pl
from jax.experimental.pallas import tpu as pltpu

D_MODEL = 1024
DEPTH = 4
N_HEADS = 4
HEAD_K = 64
HEAD_V = 128
KEY_WIDTH = N_HEADS * HEAD_K
GLA_WIDTH = N_HEADS * HEAD_V
GATE_RANK = 16
GATE_TEMP = 16.0
CHUNK = 64
POOL_WIDTH = 512
POOL_WINDOWS = (2, 4, 8, 16)
POOL_GROUP = 128
POOL_PAIR = 2 * POOL_GROUP
POOL_BUF = 15
POOL_HIST = 16
D_FF = 4 * D_MODEL
FF_CHUNK = 1024
EPS = 1e-6
LANES = 128
SUBLANES = 8

COL_Q = 0
COL_K = KEY_WIDTH
COL_V = 2 * KEY_WIDTH
COL_G = COL_V + GLA_WIDTH
COL_U = COL_G + GLA_WIDTH
Z_WIDTH = COL_U + POOL_WIDTH
W_IN_GATE = COL_U
W_IN_U = COL_U + GATE_RANK

PROMPT_TILE = 1024
PROMPT_SUB = 256
GLA_PAIR = 2 * CHUNK
PROJ_PIECE = 256
CHANNEL_TILE = 512
VMEM_LIMIT = 56 * 1024 * 1024


def _rmsnorm(x, g):
    return x * lax.rsqrt(jnp.mean(x * x, axis=-1, keepdims=True) + EPS) * g


def _log_sigmoid(x):
    return jnp.minimum(x, 0.0) - jnp.log(1.0 + jnp.exp(-jnp.abs(x)))


def _bf16_round(x):
    return x.astype(jnp.bfloat16).astype(jnp.float32)


def _head_lane_mask(rows):
    lane = lax.broadcasted_iota(jnp.int32, (rows, KEY_WIDTH), 1)
    return [(lane >= h * HEAD_K) & (lane < (h + 1) * HEAD_K) for h in range(N_HEADS)]


def _gate_log_decay(a_low, w_gate_ref, b_gate_ref):
    pre = jnp.dot(a_low.astype(jnp.bfloat16), w_gate_ref[...], preferred_element_type=jnp.float32) + b_gate_ref[...]
    return _log_sigmoid(pre) / GATE_TEMP


def _project(x, n1_ref, w_main_ref, w_code_ref, w_u_ref, w_gate_ref, b_gate_ref):
    h = _rmsnorm(x, n1_ref[...]).astype(jnp.bfloat16)
    qkvg = jnp.dot(h, w_main_ref[...], preferred_element_type=jnp.float32)
    a_low = jnp.dot(h, w_code_ref[...], preferred_element_type=jnp.float32)
    u = jnp.dot(h, w_u_ref[...], preferred_element_type=jnp.float32)
    return qkvg, u, _gate_log_decay(a_low, w_gate_ref, b_gate_ref)


def _gla_finish(o, g, gla_g):
    outs = []
    for h in range(N_HEADS):
        sl = slice(h * HEAD_V, (h + 1) * HEAD_V)
        outs.append(_rmsnorm(o[:, sl], gla_g) * (g[:, sl] * jax.nn.sigmoid(g[:, sl])))
    return jnp.concatenate(outs, axis=-1)


def _trace_interleaved(a, b):
    i = j = 0
    while i < len(a) or j < len(b):
        if j >= len(b) or (i < len(a) and i * len(b) <= j * len(a)):
            a[i]()
            i += 1
        else:
            b[j]()
            j += 1


def _prompt_project_tasks(load_x, slot, n1_ref, w_main_ref, w_code_ref, w_u_ref, w_gate_ref, b_gate_ref,
                          z_ref, b_ref):
    v = {}

    def norm():
        v["h"] = _rmsnorm(load_x(), n1_ref[...]).astype(jnp.bfloat16)

    def piece(lo, hi):
        w_ref, first = (w_main_ref, 0) if hi <= COL_U else (w_u_ref, COL_U)

        def run():
            z_ref[slot, :, lo:hi] = jnp.dot(v["h"], w_ref[:, lo - first:hi - first],
                                            preferred_element_type=jnp.float32)
        return run

    def gate_code():
        v["a_low"] = jnp.dot(v["h"], w_code_ref[...], preferred_element_type=jnp.float32)

    def gate_pre():
        v["pre"] = (jnp.dot(v["a_low"].astype(jnp.bfloat16), w_gate_ref[...], preferred_element_type=jnp.float32)
                    + b_gate_ref[...])

    def log_decay():
        v["b"] = _log_sigmoid(v["pre"]) / GATE_TEMP

    def scan(shifts, last):
        def run():
            b = v["b"]
            row = lax.broadcasted_iota(jnp.int32, b.shape, 0) % CHUNK
            for shift in shifts:
                b = b + jnp.where(row >= shift, pltpu.roll(b, shift, axis=0), 0.0)
            v["b"] = b
            if last:
                b_ref[slot] = b
        return run

    half = PROJ_PIECE
    return [
        norm, gate_code,
        piece(COL_Q, COL_K), gate_pre,
        piece(COL_K, COL_V), log_decay,
        piece(COL_V, COL_V + half), scan((1, 2), False),
        piece(COL_V + half, COL_G), scan((4, 8), False),
        piece(COL_G, COL_G + half), scan((16, 32), True),
        piece(COL_G + half, COL_U), piece(COL_U, COL_U + half), piece(COL_U + half, Z_WIDTH),
    ]


def _prompt_mix_tasks(slot, row0, z_ref, b_ref, carry, first_pos, gla_g_ref, pool_w_ref, pool_scale_ref, mix_ref,
                      base):
    head_mask = _head_lane_mask(CHUNK)
    scale = HEAD_K ** -0.5
    sub = GLA_PAIR
    n_chunks = sub // CHUNK
    part = slice(row0, row0 + sub)
    shared = {}

    def transpose_keys():
        b = b_ref[slot, part, :]
        k = z_ref[slot, part, COL_K:COL_V]
        last = [b[(c + 1) * CHUNK - 1:(c + 1) * CHUNK] for c in range(n_chunks)]
        b_last = jnp.concatenate([jnp.broadcast_to(row, (CHUNK, KEY_WIDTH)) for row in last], axis=0)
        shared["kt_t"] = (k * jnp.exp(-b)).T.astype(jnp.bfloat16)
        shared["ke_t"] = (k * jnp.exp(b_last - b)).T.astype(jnp.bfloat16)
        shared["v"] = z_ref[slot, part, COL_V:COL_G].astype(jnp.bfloat16)
        decay = jnp.exp(jnp.concatenate(last + [jnp.zeros((SUBLANES - n_chunks, KEY_WIDTH), jnp.float32)], axis=0))
        shared["decay"] = jnp.broadcast_to(decay[:, :, None], (SUBLANES, KEY_WIDTH, HEAD_V))

    def chunk_tasks(c):
        rows = slice(row0 + c * CHUNK, row0 + (c + 1) * CHUNK)
        out_rows = slice(base + c * CHUNK, base + (c + 1) * CHUNK)
        token = lax.broadcasted_iota(jnp.int32, (CHUNK, sub), 1) - c * CHUNK
        causal = (token >= 0) & (token <= lax.broadcasted_iota(jnp.int32, (CHUNK, sub), 0))
        in_chunk = (lax.broadcasted_iota(jnp.int32, (KEY_WIDTH, sub), 1) // CHUNK) == c
        v = {}

        def prepare():
            qt = z_ref[slot, rows, COL_Q:COL_K] * jnp.exp(b_ref[slot, rows, :]) * scale
            v["q_stack"] = jnp.concatenate([jnp.where(m, qt, 0.0) for m in head_mask], axis=0).astype(jnp.bfloat16)

        def query_product():
            rhs = jnp.concatenate([carry["st"].astype(jnp.bfloat16), shared["kt_t"]], axis=1)
            v["r"] = jnp.dot(v["q_stack"], rhs, preferred_element_type=jnp.float32)

        def update_state():
            carry["st"] = shared["decay"][c] * carry["st"] + shared["upd"][c]

        def masked_scores():
            r = v["r"]
            v["att"] = [jnp.where(causal, r[h * CHUNK:(h + 1) * CHUNK, HEAD_V:HEAD_V + sub], 0.0).astype(jnp.bfloat16)
                        for h in range(N_HEADS)]

        def add_inter(o_intra):
            r = v["r"]
            v["o"] = jnp.concatenate([o_intra[h] + r[h * CHUNK:(h + 1) * CHUNK, 0:HEAD_V] for h in range(N_HEADS)],
                                     axis=-1)

        def finish():
            g_c = z_ref[slot, rows, COL_G:COL_U]
            mix_ref[out_rows, 0:GLA_WIDTH] = _gla_finish(v["o"], g_c, gla_g_ref[...]).astype(mix_ref.dtype)

        return dict(prepare=prepare, query_product=query_product, update_state=update_state,
                    masked_scores=masked_scores, add_inter=add_inter, finish=finish, in_chunk=in_chunk, v=v)

    chunks = [chunk_tasks(c) for c in range(n_chunks)]

    def update_products():
        ke_t = shared["ke_t"]
        zero = jnp.zeros_like(ke_t)
        masked = [jnp.where(ch["in_chunk"], ke_t, zero) for ch in chunks]
        per_head = []
        for h in range(N_HEADS):
            lhs = jnp.concatenate([m[h * HEAD_K:(h + 1) * HEAD_K] for m in masked], axis=0)
            per_head.append(jnp.dot(lhs, shared["v"][:, h * HEAD_V:(h + 1) * HEAD_V],
                                    preferred_element_type=jnp.float32))
        shared["upd"] = [jnp.concatenate([p[c * HEAD_K:(c + 1) * HEAD_K] for p in per_head], axis=0)
                         for c in range(n_chunks)]

    def intra_products():
        per_head = []
        for h in range(N_HEADS):
            lhs = jnp.concatenate([ch["v"]["att"][h] for ch in chunks], axis=0)
            per_head.append(jnp.dot(lhs, shared["v"][:, h * HEAD_V:(h + 1) * HEAD_V],
                                    preferred_element_type=jnp.float32))
        for c, ch in enumerate(chunks):
            ch["add_inter"]([p[c * CHUNK:(c + 1) * CHUNK] for p in per_head])

    pooled = {}

    def window_mean_task(gi, w):
        lanes = slice(gi * POOL_GROUP, (gi + 1) * POOL_GROUP)

        def window_mean():
            u = z_ref[slot, part, COL_U + gi * POOL_GROUP:COL_U + (gi + 1) * POOL_GROUP]
            s = jnp.concatenate([carry["hist"][:, lanes], u], axis=0)
            shift = 1
            while shift < w:
                s = s + pltpu.roll(s, shift, axis=0)
                shift *= 2
            seen = first_pos + 1 + lax.broadcasted_iota(jnp.int32, (sub, POOL_GROUP), 0)
            if "inv_seen" not in pooled:
                pooled["inv_seen"] = 1.0 / seen.astype(jnp.float32)
            inv_count = jnp.where(seen < w, pooled["inv_seen"], 1.0 / w)
            pooled[gi] = (s[POOL_HIST:] * inv_count - u).astype(jnp.bfloat16)

        return window_mean

    def group_map_task(pair):
        lanes = slice(pair * POOL_PAIR, (pair + 1) * POOL_PAIR)

        def group_map():
            both = jnp.concatenate([pooled[2 * pair], pooled[2 * pair + 1]], axis=-1)
            pg = jnp.dot(both, pool_w_ref[pair], preferred_element_type=jnp.float32)
            pg = pg * pool_scale_ref[:, lanes]
            mix_ref[base:base + sub, GLA_WIDTH + pair * POOL_PAIR:GLA_WIDTH + (pair + 1) * POOL_PAIR] = (
                pg.astype(mix_ref.dtype))

        return group_map

    def keep_history():
        carry["hist"] = z_ref[slot, row0 + sub - POOL_HIST:row0 + sub, COL_U:Z_WIDTH]

    assert n_chunks == 2
    c0, c1 = chunks
    means = [window_mean_task(gi, w) for gi, w in enumerate(POOL_WINDOWS)]
    maps = [group_map_task(pair) for pair in range(len(POOL_WINDOWS) // 2)]
    return [
        transpose_keys, c0["prepare"], c1["prepare"], update_products, c0["query_product"], means[0],
        c0["update_state"], c1["query_product"], c0["masked_scores"], c1["update_state"], means[1],
        c1["masked_scores"], intra_products, maps[0], means[2], c0["finish"], means[3], c1["finish"], maps[1],
        keep_history,
    ]


def _sample_project(xs_ref, n1_ref, w_main_ref, w_code_ref, w_u_ref, w_gate_ref, b_gate_ref, pool_w_ref, pool_scale_ref,
                    pool_state_ref, mix_s_ref, pool_new_ref, dec_ref, ke_ref, qt_ref, v_ref, g_ref, o_ref):
    n_seq = xs_ref.shape[0]
    qkvg, u, log_a = _project(xs_ref[...], n1_ref, w_main_ref, w_code_ref, w_u_ref, w_gate_ref, b_gate_ref)
    q = qkvg[:, COL_Q:COL_K]
    k = qkvg[:, COL_K:COL_V]
    v = _bf16_round(qkvg[:, COL_V:COL_G])
    qt = _bf16_round(q * jnp.exp(log_a) * (HEAD_K ** -0.5))
    kt = _bf16_round(k * jnp.exp(-log_a))
    ke = _bf16_round(k * jnp.exp(log_a - log_a))
    prod = qt * kt
    o_intra = []
    for h, m in enumerate(_head_lane_mask(n_seq)):
        att = _bf16_round(jnp.sum(jnp.where(m, prod, 0.0), axis=-1, keepdims=True))
        o_intra.append(att * v[:, h * HEAD_V:(h + 1) * HEAD_V])
    o_ref[...] = jnp.concatenate(o_intra, axis=-1)
    dec_ref[...] = jnp.exp(log_a)
    ke_ref[...] = ke
    qt_ref[...] = qt
    v_ref[...] = v
    g_ref[...] = qkvg[:, COL_G:COL_U]

    pooled = []
    for gi, w in enumerate(POOL_WINDOWS):
        lanes = slice(gi * POOL_GROUP, (gi + 1) * POOL_GROUP)
        s = u[:, lanes]
        for j in range(POOL_BUF - (w - 1), POOL_BUF):
            s = s + pool_state_ref[j, :, lanes]
        pooled.append((s / float(w) - u[:, lanes]).astype(jnp.bfloat16))
    for pair in range(len(POOL_WINDOWS) // 2):
        lanes = slice(pair * POOL_PAIR, (pair + 1) * POOL_PAIR)
        both = jnp.concatenate(pooled[2 * pair:2 * pair + 2], axis=-1)
        pg = jnp.dot(both, pool_w_ref[pair], preferred_element_type=jnp.float32) * pool_scale_ref[:, lanes]
        mix_s_ref[:, GLA_WIDTH + pair * POOL_PAIR:GLA_WIDTH + (pair + 1) * POOL_PAIR] = pg.astype(mix_s_ref.dtype)
    for j in range(POOL_BUF - 1):
        pool_new_ref[j] = pool_state_ref[j + 1]
    pool_new_ref[POOL_BUF - 1] = u


def _sample_state_tasks(first_seq, s_ref, s_new_ref, dec_ref, ke_ref, qt_ref, v_ref, o_ref):
    rows = pl.ds(pl.multiple_of(first_seq, SUBLANES), SUBLANES)

    def update_state():
        v = v_ref[rows, :]
        v_rows = jnp.concatenate(
            [jnp.broadcast_to(v[:, None, h * HEAD_V:(h + 1) * HEAD_V], (SUBLANES, HEAD_K, HEAD_V))
             for h in range(N_HEADS)], axis=1)
        s_new_ref[...] = dec_ref[rows, :][:, :, None] * s_ref[...] + ke_ref[rows, :][:, :, None] * v_rows

    def query_state():
        lane = lax.broadcasted_iota(jnp.int32, (SUBLANES, KEY_WIDTH), 1)
        head = lax.broadcasted_iota(jnp.int32, (SUBLANES, KEY_WIDTH), 0)
        own_head = lane // HEAD_K == head
        q_rows = jnp.where(own_head[None], qt_ref[rows, :][:, None, :], 0.0).astype(jnp.bfloat16)
        o_inter = jnp.einsum("bhk,bkv->bhv", q_rows, s_ref[...].astype(jnp.bfloat16),
                             preferred_element_type=jnp.float32)
        o_ref[rows, :] = o_ref[rows, :] + jnp.concatenate([o_inter[:, h, :] for h in range(N_HEADS)], axis=-1)

    return [update_state, query_state]


def _mixer_kernel(*refs, layer, chained):
    (x_ref, x_next_ref, n1_ref, w_main_ref, w_code_ref, w_u_ref, w_gate_ref, b_gate_ref, gla_g_ref, pool_w_ref,
     pool_scale_ref, xs_ref, s_ref, pool_state_ref) = refs[:14]
    n1_ref, b_gate_ref, gla_g_ref, pool_scale_ref = (
        ref.at[pl.ds(layer, 1), :] for ref in (n1_ref, b_gate_ref, gla_g_ref, pool_scale_ref))
    (mix_ref, s_fin_ref, hist_out_ref, mix_s_ref, s_new_ref, pool_new_ref,
     st_ref, hist_ref, z_ref, b_ref, dec_ref, ke_ref, qt_ref, v_ref, g_ref, o_ref) = refs[-16:]
    tile = x_ref.shape[0]
    n_sub = tile // PROMPT_SUB
    t = pl.program_id(1)
    step = pl.program_id(0) * pl.num_programs(1) + t
    last_step = pl.num_programs(0) * pl.num_programs(1) - 1

    def project_tasks(load_x, slot):
        return _prompt_project_tasks(load_x, slot, n1_ref, w_main_ref, w_code_ref, w_u_ref, w_gate_ref, b_gate_ref,
                                     z_ref, b_ref)

    @pl.when(t == 0)
    def _():
        st_ref[...] = jnp.zeros_like(st_ref)
        hist_ref[...] = jnp.zeros_like(hist_ref)

    @pl.when(step == 0)
    def _():
        for task in project_tasks(lambda: x_ref[0:PROMPT_SUB], 0):
            task()
        _sample_project(xs_ref, n1_ref, w_main_ref, w_code_ref, w_u_ref, w_gate_ref, b_gate_ref, pool_w_ref,
                        pool_scale_ref, pool_state_ref, mix_s_ref, pool_new_ref, dec_ref, ke_ref, qt_ref, v_ref, g_ref,
                        o_ref)

    sample_tasks = _sample_state_tasks(step * SUBLANES, s_ref, s_new_ref, dec_ref, ke_ref, qt_ref, v_ref, o_ref)
    carry = {"st": st_ref[...], "hist": hist_ref[...]}
    for sb in range(n_sub):
        if sb + 1 < n_sub:
            load_next = lambda sb=sb: x_ref[(sb + 1) * PROMPT_SUB:(sb + 2) * PROMPT_SUB]
        else:
            load_next = lambda: x_next_ref[...]
        mix_tasks = []
        for row0 in range(0, PROMPT_SUB, GLA_PAIR):
            mix_tasks += _prompt_mix_tasks(sb % 2, row0, z_ref, b_ref, carry, t * tile + sb * PROMPT_SUB + row0,
                                           gla_g_ref, pool_w_ref, pool_scale_ref, mix_ref, sb * PROMPT_SUB + row0)
        if sample_tasks and sb >= 1:
            mix_tasks.insert(len(mix_tasks) // 2, sample_tasks.pop(0))
        _trace_interleaved(project_tasks(load_next, (sb + 1) % 2), mix_tasks)
    st_ref[...] = carry["st"]
    hist_ref[...] = carry["hist"]

    @pl.when(t == pl.num_programs(1) - 1)
    def _():
        s_fin_ref[...] = carry["st"]
        hist_out_ref[...] = carry["hist"]

    @pl.when(step == last_step)
    def _():
        mix_s_ref[:, 0:GLA_WIDTH] = _gla_finish(o_ref[...], g_ref[...], gla_g_ref[...]).astype(mix_s_ref.dtype)


def _mixer(layer, x, xs, s_state, pool_state, p, stacked):
    B, T, _ = x.shape
    n_seq = xs.shape[0]
    tile = PROMPT_TILE
    n_tiles = T // tile
    n_sub = tile // PROMPT_SUB
    assert n_sub % 2 == 0
    assert n_sub >= 3 and B * n_tiles * SUBLANES == n_seq
    assert PROMPT_SUB % GLA_PAIR == 0
    f32 = jnp.float32
    whole = lambda *shape: pl.BlockSpec((None,) + shape, lambda b, t: (layer,) + (0,) * len(shape))
    fixed = lambda *shape: pl.BlockSpec(shape, lambda b, t: (0,) * len(shape))
    all_layers = lambda width: fixed(DEPTH, width)
    chained = stacked is not None
    operands = [x, x, p["n1"], p["w_main"], p["w_code"], p["w_u"], p["w_gate"], p["b_gate"], p["gla_g"], p["pool_w"],
                p["pool_scale"], xs, s_state, pool_state]

    def next_first_sub_block(b, t):
        n = jnp.minimum(b * n_tiles + t + 1, B * n_tiles - 1)
        return (n // n_tiles, (n % n_tiles) * n_sub, 0)

    sample_state_block = pl.BlockSpec((None, SUBLANES, KEY_WIDTH, HEAD_V), lambda b, t: (layer, b * n_tiles + t, 0, 0))
    in_specs = [
        pl.BlockSpec((None, tile, D_MODEL), lambda b, t: (b, t, 0)),
        pl.BlockSpec((None, PROMPT_SUB, D_MODEL), next_first_sub_block),
        all_layers(D_MODEL),
        whole(D_MODEL, COL_U),
        whole(D_MODEL, LANES),
        whole(D_MODEL, POOL_WIDTH),
        whole(LANES, KEY_WIDTH),
        all_layers(KEY_WIDTH),
        all_layers(HEAD_V),
        whole(len(POOL_WINDOWS) // 2, POOL_PAIR, POOL_PAIR),
        all_layers(POOL_WIDTH),
        fixed(n_seq, D_MODEL),
        sample_state_block,
        whole(POOL_BUF, n_seq, POOL_WIDTH),
    ]
    aliases = {}
    if chained:
        aliases = {len(operands): 1, len(operands) + 1: 2, len(operands) + 2: 4, len(operands) + 3: 5}
        operands += list(stacked)
        in_specs += [pl.BlockSpec(memory_space=pl.ANY)] * 4
    return pl.pallas_call(
        functools.partial(_mixer_kernel, layer=layer, chained=chained),
        grid=(B, n_tiles),
        in_specs=in_specs,
        out_specs=[
            pl.BlockSpec((None, tile, D_MODEL), lambda b, t: (b, t, 0)),
            pl.BlockSpec((None, None, KEY_WIDTH, HEAD_V), lambda b, t: (layer, b, 0, 0)),
            pl.BlockSpec((None, None, POOL_HIST, POOL_WIDTH), lambda b, t: (layer, b, 0, 0)),
            fixed(n_seq, D_MODEL),
            sample_state_block,
            whole(POOL_BUF, n_seq, POOL_WIDTH),
        ],
        out_shape=[
            jax.ShapeDtypeStruct((B, T, D_MODEL), jnp.bfloat16),
            jax.ShapeDtypeStruct((DEPTH, B, KEY_WIDTH, HEAD_V), f32),
            jax.ShapeDtypeStruct((DEPTH, B, POOL_HIST, POOL_WIDTH), f32),
            jax.ShapeDtypeStruct((n_seq, D_MODEL), jnp.bfloat16),
            jax.ShapeDtypeStruct((DEPTH, n_seq, KEY_WIDTH, HEAD_V), f32),
            jax.ShapeDtypeStruct((DEPTH, POOL_BUF, n_seq, POOL_WIDTH), f32),
        ],
        scratch_shapes=[
            pltpu.VMEM((KEY_WIDTH, HEAD_V), f32),
            pltpu.VMEM((POOL_HIST, POOL_WIDTH), f32),
            pltpu.VMEM((2, PROMPT_SUB, Z_WIDTH), f32),
            pltpu.VMEM((2, PROMPT_SUB, KEY_WIDTH), f32),
            pltpu.VMEM((n_seq, KEY_WIDTH), f32),
            pltpu.VMEM((n_seq, KEY_WIDTH), f32),
            pltpu.VMEM((n_seq, KEY_WIDTH), f32),
            pltpu.VMEM((n_seq, GLA_WIDTH), f32),
            pltpu.VMEM((n_seq, GLA_WIDTH), f32),
            pltpu.VMEM((n_seq, GLA_WIDTH), f32),
        ],
        input_output_aliases=aliases,
        compiler_params=pltpu.CompilerParams(
            dimension_semantics=("arbitrary", "arbitrary"), vmem_limit_bytes=VMEM_LIMIT),
        name=f"mixer_l{layer}",
    )(*operands)


N_FF_CHUNKS = D_FF // FF_CHUNK
N_WEIGHT_CHUNKS = 1 + 2 * N_FF_CHUNKS
STAGE_SLOTS = 3
assert FF_CHUNK == D_MODEL


def _channel_kernel(x_ref, mix_ref, xs_ref, mix_s_ref, n2_ref, final_g_ref, w_out_hbm, w_up_hbm, w_down_hbm,
                    y_ref, ys_ref, w_out_ref, w_up_ref, w_down_ref, stage_ref, sem, *, layer, final):
    step = pl.program_id(0)
    n2_ref = n2_ref.at[pl.ds(layer, 1), :]

    def chunk(k):
        if k == 0:
            return w_out_hbm.at[layer], w_out_ref
        j, down = divmod(k - 1, 2)
        span = pl.ds(j * FF_CHUNK, FF_CHUNK)
        if down:
            return w_down_hbm.at[layer, span, :], w_down_ref.at[span, :]
        return w_up_hbm.at[layer, :, span], w_up_ref.at[:, span]

    def chunk_copy(k):
        return pltpu.make_async_copy(chunk(k)[0], stage_ref.at[k % STAGE_SLOTS], sem.at[k % STAGE_SLOTS])

    def fetch(k):
        chunk_copy(k).wait()
        chunk(k)[1][...] = stage_ref[k % STAGE_SLOTS].astype(jnp.bfloat16)
        if k + STAGE_SLOTS < N_WEIGHT_CHUNKS:
            chunk_copy(k + STAGE_SLOTS).start()

    def channel_mix(x, mix, fetch_weights):
        def before_matmul(m):
            if fetch_weights and m + 1 < N_WEIGHT_CHUNKS:
                fetch(m + 1)

        if fetch_weights:
            fetch(0)
        before_matmul(0)
        x1 = x + jnp.dot(mix, w_out_ref[...], preferred_element_type=jnp.float32)
        h2 = _rmsnorm(x1, n2_ref[...]).astype(jnp.bfloat16)
        acc = x1
        for j in range(N_FF_CHUNKS):
            cols = slice(j * FF_CHUNK, (j + 1) * FF_CHUNK)
            before_matmul(1 + 2 * j)
            hid = jnp.dot(h2, w_up_ref[:, cols], preferred_element_type=jnp.float32)
            act = jnp.square(jnp.maximum(hid, 0.0)).astype(jnp.bfloat16)
            before_matmul(2 + 2 * j)
            acc = acc + jnp.dot(act, w_down_ref[cols, :], preferred_element_type=jnp.float32)
        if final:
            acc = _rmsnorm(acc, final_g_ref[...])
        return acc

    @pl.when(step == 0)
    def _():
        for k in range(STAGE_SLOTS):
            chunk_copy(k).start()
        y_ref[...] = channel_mix(x_ref[...], mix_ref[...], True)

    @pl.when(step > 0)
    def _():
        y_ref[...] = channel_mix(x_ref[...], mix_ref[...], False)

    @pl.when(step == pl.num_programs(0) - 1)
    def _():
        ys_ref[...] = channel_mix(xs_ref[...], mix_s_ref[...], False)


def _channel_mixer(layer, x, mix, xs, mix_s, p, final_g):
    rows = x.shape[0]
    n_seq = xs.shape[0]
    tile = CHANNEL_TILE
    assert rows // tile > 1
    bf16 = jnp.bfloat16
    return pl.pallas_call(
        functools.partial(_channel_kernel, layer=layer, final=layer == DEPTH - 1),
        grid=(rows // tile,),
        in_specs=[
            pl.BlockSpec((tile, D_MODEL), lambda i: (i, 0)),
            pl.BlockSpec((tile, D_MODEL), lambda i: (i, 0)),
            pl.BlockSpec((n_seq, D_MODEL), lambda i: (0, 0)),
            pl.BlockSpec((n_seq, D_MODEL), lambda i: (0, 0)),
            pl.BlockSpec((DEPTH, D_MODEL), lambda i: (0, 0)),
            pl.BlockSpec((1, D_MODEL), lambda i: (0, 0)),
            pl.BlockSpec(memory_space=pl.ANY),
            pl.BlockSpec(memory_space=pl.ANY),
            pl.BlockSpec(memory_space=pl.ANY),
        ],
        out_specs=[
            pl.BlockSpec((tile, D_MODEL), lambda i: (i, 0)),
            pl.BlockSpec((n_seq, D_MODEL), lambda i: (0, 0)),
        ],
        out_shape=[
            jax.ShapeDtypeStruct((rows, D_MODEL), jnp.float32),
            jax.ShapeDtypeStruct((n_seq, D_MODEL), jnp.float32),
        ],
        scratch_shapes=[
            pltpu.VMEM((D_MODEL, D_MODEL), bf16),
            pltpu.VMEM((D_MODEL, D_FF), bf16),
            pltpu.VMEM((D_FF, D_MODEL), bf16),
            pltpu.VMEM((STAGE_SLOTS, D_MODEL, D_MODEL), jnp.float32),
            pltpu.SemaphoreType.DMA((STAGE_SLOTS,)),
        ],
        compiler_params=pltpu.CompilerParams(
            dimension_semantics=("arbitrary",), vmem_limit_bytes=VMEM_LIMIT),
        name=f"channel_mixer_l{layer}",
    )(x, mix, xs, mix_s, p["n2"], final_g, p["w_out"], p["w_up"], p["w_down"])


def kernel(x_prompt, x_sample, state_gla, state_pool, norm1_g, w_in, w_gate, b_gate, gla_norm_g, pool_w,
           pool_scale, w_out, norm2_g, w_up, w_down, final_g):
    B, T, _ = x_prompt.shape
    n_seq = x_sample.shape[0]
    bf16 = jnp.bfloat16

    w_code = jnp.pad(w_in[:, :, W_IN_GATE:W_IN_U], ((0, 0), (0, 0), (0, LANES - GATE_RANK)))
    pw = pool_w.reshape(DEPTH, len(POOL_WINDOWS) // 2, 2, POOL_GROUP, POOL_GROUP)
    zero = jnp.zeros_like(pw[:, :, 0])
    pool_pairs = jnp.concatenate([jnp.concatenate([pw[:, :, 0], zero], axis=-1),
                                  jnp.concatenate([zero, pw[:, :, 1]], axis=-1)], axis=-2)
    params = {
        "n1": norm1_g,
        "w_main": w_in[:, :, :W_IN_GATE].astype(bf16),
        "w_code": w_code.astype(bf16),
        "w_u": w_in[:, :, W_IN_U:].astype(bf16),
        "w_gate": jnp.pad(w_gate, ((0, 0), (0, LANES - GATE_RANK), (0, 0))).astype(bf16),
        "b_gate": b_gate,
        "gla_g": gla_norm_g,
        "pool_w": pool_pairs.astype(bf16),
        "pool_scale": pool_scale,
        "w_out": w_out,
        "n2": norm2_g,
        "w_up": w_up,
        "w_down": w_down,
    }
    final_g2 = final_g.reshape(1, D_MODEL)
    s_state = state_gla.reshape(DEPTH, n_seq, KEY_WIDTH, HEAD_V)
    pool_rows = jnp.swapaxes(state_pool, 1, 2)

    xp = x_prompt
    xs = x_sample.reshape(n_seq, D_MODEL)
    states = None
    for layer in range(DEPTH):
        mix_p, gla_p, hist_p, mix_s, gla_s, pool_s = _mixer(layer, xp, xs, s_state, pool_rows, params, states)
        states = (gla_p, hist_p, gla_s, pool_s)
        xp, xs = _channel_mixer(layer, xp.reshape(B * T, D_MODEL), mix_p.reshape(B * T, D_MODEL), xs, mix_s,
                                params, final_g2)
        xp = xp.reshape(B, T, D_MODEL)
    return (xp, xs.reshape(n_seq, 1, D_MODEL),
            gla_p.reshape(DEPTH, B, N_HEADS, HEAD_K, HEAD_V), hist_p[:, :, POOL_HIST - POOL_BUF:],
            gla_s.reshape(DEPTH, n_seq, N_HEADS, HEAD_K, HEAD_V), jnp.swapaxes(pool_s, 1, 2))
```

```python
import functools

import jax
import jax.numpy as jnp
from jax import lax
from jax.experimental import pallas as pl
from jax.experimental.pallas import tpu as pltpu

D_MODEL = 1024
DEPTH = 4
N_HEADS = 4
HEAD_K = 64
HEAD_V = 128
KEY_WIDTH = N_HEADS * HEAD_K
GLA_WIDTH = N_HEADS * HEAD_V
GATE_RANK = 16
GATE_TEMP = 16.0
CHUNK = 64
POOL_WIDTH = 512
POOL_WINDOWS = (2, 4, 8, 16)
POOL_GROUP = 128
POOL_PAIR = 2 * POOL_GROUP
POOL_BUF = 15
POOL_HIST = 16
D_FF = 4 * D_MODEL
FF_CHUNK = 1024
EPS = 1e-6
LANES = 128
SUBLANES = 8

COL_Q = 0
COL_K = KEY_WIDTH
COL_V = 2 * KEY_WIDTH
COL_G = COL_V + GLA_WIDTH
COL_U = COL_G + GLA_WIDTH
Z_WIDTH = COL_U + POOL_WIDTH
W_IN_GATE = COL_U
W_IN_U = COL_U + GATE_RANK

PROMPT_TILE = 1024
PROMPT_SUB = 256
GLA_PAIR = 2 * CHUNK
PROJ_PIECE = 256
CHANNEL_TILE = 512
VMEM_LIMIT = 56 * 1024 * 1024


def _rmsnorm(x, g):
    return x * lax.rsqrt(jnp.mean(x * x, axis=-1, keepdims=True) + EPS) * g


def _log_sigmoid(x):
    return jnp.minimum(x, 0.0) - jnp.log(1.0 + jnp.exp(-jnp.abs(x)))


def _bf16_round(x):
    return x.astype(jnp.bfloat16).astype(jnp.float32)


def _head_lane_mask(rows):
    lane = lax.broadcasted_iota(jnp.int32, (rows, KEY_WIDTH), 1)
    return [(lane >= h * HEAD_K) & (lane < (h + 1) * HEAD_K) for h in range(N_HEADS)]


def _gate_log_decay(a_low, w_gate_ref, b_gate_ref):
    pre = jnp.dot(a_low.astype(jnp.bfloat16), w_gate_ref[...], preferred_element_type=jnp.float32) + b_gate_ref[...]
    return _log_sigmoid(pre) / GATE_TEMP


def _project(x, n1_ref, w_main_ref, w_code_ref, w_u_ref, w_gate_ref, b_gate_ref):
    h = _rmsnorm(x, n1_ref[...]).astype(jnp.bfloat16)
    qkvg = jnp.dot(h, w_main_ref[...], preferred_element_type=jnp.float32)
    a_low = jnp.dot(h, w_code_ref[...], preferred_element_type=jnp.float32)
    u = jnp.dot(h, w_u_ref[...], preferred_element_type=jnp.float32)
    return qkvg, u, _gate_log_decay(a_low, w_gate_ref, b_gate_ref)


def _gla_finish(o, g, gla_g):
    outs = []
    for h in range(N_HEADS):
        sl = slice(h * HEAD_V, (h + 1) * HEAD_V)
        outs.append(_rmsnorm(o[:, sl], gla_g) * (g[:, sl] * jax.nn.sigmoid(g[:, sl])))
    return jnp.concatenate(outs, axis=-1)


def _trace_interleaved(a, b):
    i = j = 0
    while i < len(a) or j < len(b):
        if j >= len(b) or (i < len(a) and i * len(b) <= j * len(a)):
            a[i]()
            i += 1
        else:
            b[j]()
            j += 1


def _prompt_project_tasks(load_x, slot, n1_ref, w_main_ref, w_code_ref, w_u_ref, w_gate_ref, b_gate_ref,
                          z_ref, b_ref):
    v = {}

    def norm():
        v["h"] = _rmsnorm(load_x(), n1_ref[...]).astype(jnp.bfloat16)

    def piece(lo, hi):
        w_ref, first = (w_main_ref, 0) if hi <= COL_U else (w_u_ref, COL_U)

        def run():
            z_ref[slot, :, lo:hi] = jnp.dot(v["h"], w_ref[:, lo - first:hi - first],
                                            preferred_element_type=jnp.float32)
        return run

    def gate_code():
        v["a_low"] = jnp.dot(v["h"], w_code_ref[...], preferred_element_type=jnp.float32)

    def gate_pre():
        v["pre"] = (jnp.dot(v["a_low"].astype(jnp.bfloat16), w_gate_ref[...], preferred_element_type=jnp.float32)
                    + b_gate_ref[...])

    def log_decay():
        v["b"] = _log_sigmoid(v["pre"]) / GATE_TEMP

    def scan(shifts, last):
        def run():
            b = v["b"]
            row = lax.broadcasted_iota(jnp.int32, b.shape, 0) % CHUNK
            for shift in shifts:
                b = b + jnp.where(row >= shift, pltpu.roll(b, shift, axis=0), 0.0)
            v["b"] = b
            if last:
                b_ref[slot] = b
        return run

    half = PROJ_PIECE
    return [
        norm, gate_code,
        piece(COL_Q, COL_K), gate_pre,
        piece(COL_K, COL_V), log_decay,
        piece(COL_V, COL_V + half), scan((1, 2), False),
        piece(COL_V + half, COL_G), scan((4, 8), False),
        piece(COL_G, COL_G + half), scan((16, 32), True),
        piece(COL_G + half, COL_U), piece(COL_U, COL_U + half), piece(COL_U + half, Z_WIDTH),
    ]


def _prompt_mix_tasks(slot, row0, z_ref, b_ref, carry, first_pos, gla_g_ref, pool_w_ref, pool_scale_ref, mix_ref,
                      base):
    head_mask = _head_lane_mask(CHUNK)
    scale = HEAD_K ** -0.5
    sub = GLA_PAIR
    n_chunks = sub // CHUNK
    part = slice(row0, row0 + sub)
    shared = {}

    def transpose_keys():
        b = b_ref[slot, part, :]
        k = z_ref[slot, part, COL_K:COL_V]
        last = [b[(c + 1) * CHUNK - 1:(c + 1) * CHUNK] for c in range(n_chunks)]
        b_last = jnp.concatenate([jnp.broadcast_to(row, (CHUNK, KEY_WIDTH)) for row in last], axis=0)
        shared["kt_t"] = (k * jnp.exp(-b)).T.astype(jnp.bfloat16)
        shared["ke_t"] = (k * jnp.exp(b_last - b)).T.astype(jnp.bfloat16)
        shared["v"] = z_ref[slot, part, COL_V:COL_G].astype(jnp.bfloat16)
        decay = jnp.exp(jnp.concatenate(last + [jnp.zeros((SUBLANES - n_chunks, KEY_WIDTH), jnp.float32)], axis=0))
        shared["decay"] = jnp.broadcast_to(decay[:, :, None], (SUBLANES, KEY_WIDTH, HEAD_V))

    def chunk_tasks(c):
        rows = slice(row0 + c * CHUNK, row0 + (c + 1) * CHUNK)
        out_rows = slice(base + c * CHUNK, base + (c + 1) * CHUNK)
        token = lax.broadcasted_iota(jnp.int32, (CHUNK, sub), 1) - c * CHUNK
        causal = (token >= 0) & (token <= lax.broadcasted_iota(jnp.int32, (CHUNK, sub), 0))
        in_chunk = (lax.broadcasted_iota(jnp.int32, (KEY_WIDTH, sub), 1) // CHUNK) == c
        v = {}

        def prepare():
            qt = z_ref[slot, rows, COL_Q:COL_K] * jnp.exp(b_ref[slot, rows, :]) * scale
            v["q_stack"] = jnp.concatenate([jnp.where(m, qt, 0.0) for m in head_mask], axis=0).astype(jnp.bfloat16)

        def query_product():
            rhs = jnp.concatenate([carry["st"].astype(jnp.bfloat16), shared["kt_t"]], axis=1)
            v["r"] = jnp.dot(v["q_stack"], rhs, preferred_element_type=jnp.float32)

        def update_state():
            carry["st"] = shared["decay"][c] * carry["st"] + shared["upd"][c]

        def masked_scores():
            r = v["r"]
            v["att"] = [jnp.where(causal, r[h * CHUNK:(h + 1) * CHUNK, HEAD_V:HEAD_V + sub], 0.0).astype(jnp.bfloat16)
                        for h in range(N_HEADS)]

        def add_inter(o_intra):
            r = v["r"]
            v["o"] = jnp.concatenate([o_intra[h] + r[h * CHUNK:(h + 1) * CHUNK, 0:HEAD_V] for h in range(N_HEADS)],
                                     axis=-1)

        def finish():
            g_c = z_ref[slot, rows, COL_G:COL_U]
            mix_ref[out_rows, 0:GLA_WIDTH] = _gla_finish(v["o"], g_c, gla_g_ref[...]).astype(mix_ref.dtype)

        return dict(prepare=prepare, query_product=query_product, update_state=update_state,
                    masked_scores=masked_scores, add_inter=add_inter, finish=finish, in_chunk=in_chunk, v=v)

    chunks = [chunk_tasks(c) for c in range(n_chunks)]

    def update_products():
        ke_t = shared["ke_t"]
        zero = jnp.zeros_like(ke_t)
        masked = [jnp.where(ch["in_chunk"], ke_t, zero) for ch in chunks]
        per_head = []
        for h in range(N_HEADS):
            lhs = jnp.concatenate([m[h * HEAD_K:(h + 1) * HEAD_K] for m in masked], axis=0)
            per_head.append(jnp.dot(lhs, shared["v"][:, h * HEAD_V:(h + 1) * HEAD_V],
                                    preferred_element_type=jnp.float32))
        shared["upd"] = [jnp.concatenate([p[c * HEAD_K:(c + 1) * HEAD_K] for p in per_head], axis=0)
                         for c in range(n_chunks)]

    def intra_products():
        per_head = []
        for h in range(N_HEADS):
            lhs = jnp.concatenate([ch["v"]["att"][h] for ch in chunks], axis=0)
            per_head.append(jnp.dot(lhs, shared["v"][:, h * HEAD_V:(h + 1) * HEAD_V],
                                    preferred_element_type=jnp.float32))
        for c, ch in enumerate(chunks):
            ch["add_inter"]([p[c * CHUNK:(c + 1) * CHUNK] for p in per_head])

    pooled = {}

    def window_mean_task(gi, w):
        lanes = slice(gi * POOL_GROUP, (gi + 1) * POOL_GROUP)

        def window_mean():
            u = z_ref[slot, part, COL_U + gi * POOL_GROUP:COL_U + (gi + 1) * POOL_GROUP]
            s = jnp.concatenate([carry["hist"][:, lanes], u], axis=0)
            shift = 1
            while shift < w:
                s = s + pltpu.roll(s, shift, axis=0)
                shift *= 2
            seen = first_pos + 1 + lax.broadcasted_iota(jnp.int32, (sub, POOL_GROUP), 0)
            if "inv_seen" not in pooled:
                pooled["inv_seen"] = 1.0 / seen.astype(jnp.float32)
            inv_count = jnp.where(seen < w, pooled["inv_seen"], 1.0 / w)
            pooled[gi] = (s[POOL_HIST:] * inv_count - u).astype(jnp.bfloat16)

        return window_mean

    def group_map_task(pair):
        lanes = slice(pair * POOL_PAIR, (pair + 1) * POOL_PAIR)

        def group_map():
            both = jnp.concatenate([pooled[2 * pair], pooled[2 * pair + 1]], axis=-1)
            pg = jnp.dot(both, pool_w_ref[pair], preferred_element_type=jnp.float32)
            pg = pg * pool_scale_ref[:, lanes]
            mix_ref[base:base + sub, GLA_WIDTH + pair * POOL_PAIR:GLA_WIDTH + (pair + 1) * POOL_PAIR] = (
                pg.astype(mix_ref.dtype))

        return group_map

    def keep_history():
        carry["hist"] = z_ref[slot, row0 + sub - POOL_HIST:row0 + sub, COL_U:Z_WIDTH]

    assert n_chunks == 2
    c0, c1 = chunks
    means = [window_mean_task(gi, w) for gi, w in enumerate(POOL_WINDOWS)]
    maps = [group_map_task(pair) for pair in range(len(POOL_WINDOWS) // 2)]
    return [
        transpose_keys, c0["prepare"], c1["prepare"], update_products, c0["query_product"], means[0],
        c0["update_state"], c1["query_product"], c0["masked_scores"], c1["update_state"], means[1],
        c1["masked_scores"], intra_products, maps[0], means[2], c0["finish"], means[3], c1["finish"], maps[1],
        keep_history,
    ]


def _sample_project(xs_ref, n1_ref, w_main_ref, w_code_ref, w_u_ref, w_gate_ref, b_gate_ref, pool_w_ref, pool_scale_ref,
                    pool_state_ref, mix_s_ref, pool_new_ref, dec_ref, ke_ref, qt_ref, v_ref, g_ref, o_ref):
    n_seq = xs_ref.shape[0]
    qkvg, u, log_a = _project(xs_ref[...], n1_ref, w_main_ref, w_code_ref, w_u_ref, w_gate_ref, b_gate_ref)
    q = qkvg[:, COL_Q:COL_K]
    k = qkvg[:, COL_K:COL_V]
    v = _bf16_round(qkvg[:, COL_V:COL_G])
    qt = _bf16_round(q * jnp.exp(log_a) * (HEAD_K ** -0.5))
    kt = _bf16_round(k * jnp.exp(-log_a))
    ke = _bf16_round(k * jnp.exp(log_a - log_a))
    prod = qt * kt
    o_intra = []
    for h, m in enumerate(_head_lane_mask(n_seq)):
        att = _bf16_round(jnp.sum(jnp.where(m, prod, 0.0), axis=-1, keepdims=True))
        o_intra.append(att * v[:, h * HEAD_V:(h + 1) * HEAD_V])
    o_ref[...] = jnp.concatenate(o_intra, axis=-1)
    dec_ref[...] = jnp.exp(log_a)
    ke_ref[...] = ke
    qt_ref[...] = qt
    v_ref[...] = v
    g_ref[...] = qkvg[:, COL_G:COL_U]

    pooled = []
    for gi, w in enumerate(POOL_WINDOWS):
        lanes = slice(gi * POOL_GROUP, (gi + 1) * POOL_GROUP)
        s = u[:, lanes]
        for j in range(POOL_BUF - (w - 1), POOL_BUF):
            s = s + pool_state_ref[j, :, lanes]
        pooled.append((s / float(w) - u[:, lanes]).astype(jnp.bfloat16))
    for pair in range(len(POOL_WINDOWS) // 2):
        lanes = slice(pair * POOL_PAIR, (pair + 1) * POOL_PAIR)
        both = jnp.concatenate(pooled[2 * pair:2 * pair + 2], axis=-1)
        pg = jnp.dot(both, pool_w_ref[pair], preferred_element_type=jnp.float32) * pool_scale_ref[:, lanes]
        mix_s_ref[:, GLA_WIDTH + pair * POOL_PAIR:GLA_WIDTH + (pair + 1) * POOL_PAIR] = pg.astype(mix_s_ref.dtype)
    for j in range(POOL_BUF - 1):
        pool_new_ref[j] = pool_state_ref[j + 1]
    pool_new_ref[POOL_BUF - 1] = u


def _sample_state_tasks(first_seq, s_ref, s_new_ref, dec_ref, ke_ref, qt_ref, v_ref, o_ref):
    rows = pl.ds(pl.multiple_of(first_seq, SUBLANES), SUBLANES)

    def update_state():
        v = v_ref[rows, :]
        v_rows = jnp.concatenate(
            [jnp.broadcast_to(v[:, None, h * HEAD_V:(h + 1) * HEAD_V], (SUBLANES, HEAD_K, HEAD_V))
             for h in range(N_HEADS)], axis=1)
        s_new_ref[...] = dec_ref[rows, :][:, :, None] * s_ref[...] + ke_ref[rows, :][:, :, None] * v_rows

    def query_state():
        lane = lax.broadcasted_iota(jnp.int32, (SUBLANES, KEY_WIDTH), 1)
        head = lax.broadcasted_iota(jnp.int32, (SUBLANES, KEY_WIDTH), 0)
        own_head = lane // HEAD_K == head
        q_rows = jnp.where(own_head[None], qt_ref[rows, :][:, None, :], 0.0).astype(jnp.bfloat16)
        o_inter = jnp.einsum("bhk,bkv->bhv", q_rows, s_ref[...].astype(jnp.bfloat16),
                             preferred_element_type=jnp.float32)
        o_ref[rows, :] = o_ref[rows, :] + jnp.concatenate([o_inter[:, h, :] for h in range(N_HEADS)], axis=-1)

    return [update_state, query_state]


def _mixer_kernel(*refs, layer, chained):
    (x_ref, x_next_ref, n1_ref, w_main_ref, w_code_ref, w_u_ref, w_gate_ref, b_gate_ref, gla_g_ref, pool_w_ref,
     pool_scale_ref, xs_ref, s_ref, pool_state_ref) = refs[:14]
    n1_ref, b_gate_ref, gla_g_ref, pool_scale_ref = (
        ref.at[pl.ds(layer, 1), :] for ref in (n1_ref, b_gate_ref, gla_g_ref, pool_scale_ref))
    (mix_ref, s_fin_ref, hist_out_ref, mix_s_ref, s_new_ref, pool_new_ref,
     st_ref, hist_ref, z_ref, b_ref, dec_ref, ke_ref, qt_ref, v_ref, g_ref, o_ref) = refs[-16:]
    tile = x_ref.shape[0]
    n_sub = tile // PROMPT_SUB
    t = pl.program_id(1)
    step = pl.program_id(0) * pl.num_programs(1) + t
    last_step = pl.num_programs(0) * pl.num_programs(1) - 1

    def project_tasks(load_x, slot):
        return _prompt_project_tasks(load_x, slot, n1_ref, w_main_ref, w_code_ref, w_u_ref, w_gate_ref, b_gate_ref,
                                     z_ref, b_ref)

    @pl.when(t == 0)
    def _():
        st_ref[...] = jnp.zeros_like(st_ref)
        hist_ref[...] = jnp.zeros_like(hist_ref)

    @pl.when(step == 0)
    def _():
        for task in project_tasks(lambda: x_ref[0:PROMPT_SUB], 0):
            task()
        _sample_project(xs_ref, n1_ref, w_main_ref, w_code_ref, w_u_ref, w_gate_ref, b_gate_ref, pool_w_ref,
                        pool_scale_ref, pool_state_ref, mix_s_ref, pool_new_ref, dec_ref, ke_ref, qt_ref, v_ref, g_ref,
                        o_ref)

    sample_tasks = _sample_state_tasks(step * SUBLANES, s_ref, s_new_ref, dec_ref, ke_ref, qt_ref, v_ref, o_ref)
    carry = {"st": st_ref[...], "hist": hist_ref[...]}
    for sb in range(n_sub):
        if sb + 1 < n_sub:
            load_next = lambda sb=sb: x_ref[(sb + 1) * PROMPT_SUB:(sb + 2) * PROMPT_SUB]
        else:
            load_next = lambda: x_next_ref[...]
        mix_tasks = []
        for row0 in range(0, PROMPT_SUB, GLA_PAIR):
            mix_tasks += _prompt_mix_tasks(sb % 2, row0, z_ref, b_ref, carry, t * tile + sb * PROMPT_SUB + row0,
                                           gla_g_ref, pool_w_ref, pool_scale_ref, mix_ref, sb * PROMPT_SUB + row0)
        if sample_tasks and sb >= 1:
            mix_tasks.insert(len(mix_tasks) // 2, sample_tasks.pop(0))
        _trace_interleaved(project_tasks(load_next, (sb + 1) % 2), mix_tasks)
    st_ref[...] = carry["st"]
    hist_ref[...] = carry["hist"]

    @pl.when(t == pl.num_programs(1) - 1)
    def _():
        s_fin_ref[...] = carry["st"]
        hist_out_ref[...] = carry["hist"]

    @pl.when(step == last_step)
    def _():
        mix_s_ref[:, 0:GLA_WIDTH] = _gla_finish(o_ref[...], g_ref[...], gla_g_ref[...]).astype(mix_s_ref.dtype)


def _mixer(layer, x, xs, s_state, pool_state, p, stacked):
    B, T, _ = x.shape
    n_seq = xs.shape[0]
    tile = PROMPT_TILE
    n_tiles = T // tile
    n_sub = tile // PROMPT_SUB
    assert n_sub % 2 == 0
    assert n_sub >= 3 and B * n_tiles * SUBLANES == n_seq
    assert PROMPT_SUB % GLA_PAIR == 0
    f32 = jnp.float32
    whole = lambda *shape: pl.BlockSpec((None,) + shape, lambda b, t: (layer,) + (0,) * len(shape))
    fixed = lambda *shape: pl.BlockSpec(shape, lambda b, t: (0,) * len(shape))
    all_layers = lambda width: fixed(DEPTH, width)
    chained = stacked is not None
    operands = [x, x, p["n1"], p["w_main"], p["w_code"], p["w_u"], p["w_gate"], p["b_gate"], p["gla_g"], p["pool_w"],
                p["pool_scale"], xs, s_state, pool_state]

    def next_first_sub_block(b, t):
        n = jnp.minimum(b * n_tiles + t + 1, B * n_tiles - 1)
        return (n // n_tiles, (n % n_tiles) * n_sub, 0)

    sample_state_block = pl.BlockSpec((None, SUBLANES, KEY_WIDTH, HEAD_V), lambda b, t: (layer, b * n_tiles + t, 0, 0))
    in_specs = [
        pl.BlockSpec((None, tile, D_MODEL), lambda b, t: (b, t, 0)),
        pl.BlockSpec((None, PROMPT_SUB, D_MODEL), next_first_sub_block),
        all_layers(D_MODEL),
        whole(D_MODEL, COL_U),
        whole(D_MODEL, LANES),
        whole(D_MODEL, POOL_WIDTH),
        whole(LANES, KEY_WIDTH),
        all_layers(KEY_WIDTH),
        all_layers(HEAD_V),
        whole(len(POOL_WINDOWS) // 2, POOL_PAIR, POOL_PAIR),
        all_layers(POOL_WIDTH),
        fixed(n_seq, D_MODEL),
        sample_state_block,
        whole(POOL_BUF, n_seq, POOL_WIDTH),
    ]
    aliases = {}
    if chained:
        aliases = {len(operands): 1, len(operands) + 1: 2, len(operands) + 2: 4, len(operands) + 3: 5}
        operands += list(stacked)
        in_specs += [pl.BlockSpec(memory_space=pl.ANY)] * 4
    return pl.pallas_call(
        functools.partial(_mixer_kernel, layer=layer, chained=chained),
        grid=(B, n_tiles),
        in_specs=in_specs,
        out_specs=[
            pl.BlockSpec((None, tile, D_MODEL), lambda b, t: (b, t, 0)),
            pl.BlockSpec((None, None, KEY_WIDTH, HEAD_V), lambda b, t: (layer, b, 0, 0)),
            pl.BlockSpec((None, None, POOL_HIST, POOL_WIDTH), lambda b, t: (layer, b, 0, 0)),
            fixed(n_seq, D_MODEL),
            sample_state_block,
            whole(POOL_BUF, n_seq, POOL_WIDTH),
        ],
        out_shape=[
            jax.ShapeDtypeStruct((B, T, D_MODEL), jnp.bfloat16),
            jax.ShapeDtypeStruct((DEPTH, B, KEY_WIDTH, HEAD_V), f32),
            jax.ShapeDtypeStruct((DEPTH, B, POOL_HIST, POOL_WIDTH), f32),
            jax.ShapeDtypeStruct((n_seq, D_MODEL), jnp.bfloat16),
            jax.ShapeDtypeStruct((DEPTH, n_seq, KEY_WIDTH, HEAD_V), f32),
            jax.ShapeDtypeStruct((DEPTH, POOL_BUF, n_seq, POOL_WIDTH), f32),
        ],
        scratch_shapes=[
            pltpu.VMEM((KEY_WIDTH, HEAD_V), f32),
            pltpu.VMEM((POOL_HIST, POOL_WIDTH), f32),
            pltpu.VMEM((2, PROMPT_SUB, Z_WIDTH), f32),
            pltpu.VMEM((2, PROMPT_SUB, KEY_WIDTH), f32),
            pltpu.VMEM((n_seq, KEY_WIDTH), f32),
            pltpu.VMEM((n_seq, KEY_WIDTH), f32),
            pltpu.VMEM((n_seq, KEY_WIDTH), f32),
            pltpu.VMEM((n_seq, GLA_WIDTH), f32),
            pltpu.VMEM((n_seq, GLA_WIDTH), f32),
            pltpu.VMEM((n_seq, GLA_WIDTH), f32),
        ],
        input_output_aliases=aliases,
        compiler_params=pltpu.CompilerParams(
            dimension_semantics=("arbitrary", "arbitrary"), vmem_limit_bytes=VMEM_LIMIT),
        name=f"mixer_l{layer}",
    )(*operands)


N_FF_CHUNKS = D_FF // FF_CHUNK
N_WEIGHT_CHUNKS = 1 + 2 * N_FF_CHUNKS
STAGE_SLOTS = 3
ROW_SLOTS = 3
assert FF_CHUNK == D_MODEL


def _channel_kernel(x_hbm, mix_hbm, xs_ref, mix_s_ref, n2_ref, final_g_ref, w_out_hbm, w_up_hbm, w_down_hbm,
                    y_ref, ys_ref, w_out_ref, w_up_ref, w_down_ref, stage_ref, sem, x_ring, mix_ring, ring_sem,
                    *, layer, final):
    step = pl.program_id(0)
    n_steps = pl.num_programs(0)
    tile = y_ref.shape[0]
    n2_ref = n2_ref.at[pl.ds(layer, 1), :]

    def row_copies(s):
        slot = s % ROW_SLOTS
        rows = pl.ds(pl.multiple_of(s * tile, tile), tile)
        return (pltpu.make_async_copy(x_hbm.at[rows, :], x_ring.at[slot], ring_sem.at[0, slot]),
                pltpu.make_async_copy(mix_hbm.at[rows, :], mix_ring.at[slot], ring_sem.at[1, slot]))

    def start_rows(s):
        for copy in row_copies(s):
            copy.start()

    @pl.when(step == 0)
    def _():
        for s in range(ROW_SLOTS - 1):
            start_rows(s)

    @pl.when(step + ROW_SLOTS - 1 < n_steps)
    def _():
        start_rows(step + ROW_SLOTS - 1)

    for copy in row_copies(step):
        copy.wait()
    x_ref = x_ring.at[step % ROW_SLOTS]
    mix_ref = mix_ring.at[step % ROW_SLOTS]

    def chunk(k):
        if k == 0:
            return w_out_hbm.at[layer], w_out_ref
        j, down = divmod(k - 1, 2)
        span = pl.ds(j * FF_CHUNK, FF_CHUNK)
        if down:
            return w_down_hbm.at[layer, span, :], w_down_ref.at[span, :]
        return w_up_hbm.at[layer, :, span], w_up_ref.at[:, span]

    def chunk_copy(k):
        return pltpu.make_async_copy(chunk(k)[0], stage_ref.at[k % STAGE_SLOTS], sem.at[k % STAGE_SLOTS])

    def fetch(k):
        chunk_copy(k).wait()
        chunk(k)[1][...] = stage_ref[k % STAGE_SLOTS].astype(jnp.bfloat16)
        if k + STAGE_SLOTS < N_WEIGHT_CHUNKS:
            chunk_copy(k + STAGE_SLOTS).start()

    def channel_mix(x, mix, fetch_weights):
        def before_matmul(m):
            if fetch_weights and m + 1 < N_WEIGHT_CHUNKS:
                fetch(m + 1)

        if fetch_weights:
            fetch(0)
        before_matmul(0)
        x1 = x + jnp.dot(mix, w_out_ref[...], preferred_element_type=jnp.float32)
        h2 = _rmsnorm(x1, n2_ref[...]).astype(jnp.bfloat16)
        acc = x1
        for j in range(N_FF_CHUNKS):
            cols = slice(j * FF_CHUNK, (j + 1) * FF_CHUNK)
            before_matmul(1 + 2 * j)
            hid = jnp.dot(h2, w_up_ref[:, cols], preferred_element_type=jnp.float32)
            act = jnp.square(jnp.maximum(hid, 0.0)).astype(jnp.bfloat16)
            before_matmul(2 + 2 * j)
            acc = acc + jnp.dot(act, w_down_ref[cols, :], preferred_element_type=jnp.float32)
        if final:
            acc = _rmsnorm(acc, final_g_ref[...])
        return acc

    @pl.when(step == 0)
    def _():
        for k in range(STAGE_SLOTS):
            chunk_copy(k).start()
        y_ref[...] = channel_mix(x_ref[...], mix_ref[...], True)

    @pl.when(step > 0)
    def _():
        y_ref[...] = channel_mix(x_ref[...], mix_ref[...], False)

    @pl.when(step == pl.num_programs(0) - 1)
    def _():
        ys_ref[...] = channel_mix(xs_ref[...], mix_s_ref[...], False)


def _channel_mixer(layer, x, mix, xs, mix_s, p, final_g):
    rows = x.shape[0]
    n_seq = xs.shape[0]
    tile = CHANNEL_TILE
    assert rows // tile >= ROW_SLOTS
    bf16 = jnp.bfloat16
    return pl.pallas_call(
        functools.partial(_channel_kernel, layer=layer, final=layer == DEPTH - 1),
        grid=(rows // tile,),
        in_specs=[
            pl.BlockSpec(memory_space=pl.ANY),
            pl.BlockSpec(memory_space=pl.ANY),
            pl.BlockSpec((n_seq, D_MODEL), lambda i: (0, 0)),
            pl.BlockSpec((n_seq, D_MODEL), lambda i: (0, 0)),
            pl.BlockSpec((DEPTH, D_MODEL), lambda i: (0, 0)),
            pl.BlockSpec((1, D_MODEL), lambda i: (0, 0)),
            pl.BlockSpec(memory_space=pl.ANY),
            pl.BlockSpec(memory_space=pl.ANY),
            pl.BlockSpec(memory_space=pl.ANY),
        ],
        out_specs=[
            pl.BlockSpec((tile, D_MODEL), lambda i: (i, 0)),
            pl.BlockSpec((n_seq, D_MODEL), lambda i: (0, 0)),
        ],
        out_shape=[
            jax.ShapeDtypeStruct((rows, D_MODEL), jnp.float32),
            jax.ShapeDtypeStruct((n_seq, D_MODEL), jnp.float32),
        ],
        scratch_shapes=[
            pltpu.VMEM((D_MODEL, D_MODEL), bf16),
            pltpu.VMEM((D_MODEL, D_FF), bf16),
            pltpu.VMEM((D_FF, D_MODEL), bf16),
            pltpu.VMEM((STAGE_SLOTS, D_MODEL, D_MODEL), jnp.float32),
            pltpu.SemaphoreType.DMA((STAGE_SLOTS,)),
            pltpu.VMEM((ROW_SLOTS, tile, D_MODEL), jnp.float32),
            pltpu.VMEM((ROW_SLOTS, tile, D_MODEL), bf16),
            pltpu.SemaphoreType.DMA((2, ROW_SLOTS)),
        ],
        compiler_params=pltpu.CompilerParams(
            dimension_semantics=("arbitrary",), vmem_limit_bytes=VMEM_LIMIT),
        name=f"channel_mixer_l{layer}",
    )(x, mix, xs, mix_s, p["n2"], final_g, p["w_out"], p["w_up"], p["w_down"])


def kernel(x_prompt, x_sample, state_gla, state_pool, norm1_g, w_in, w_gate, b_gate, gla_norm_g, pool_w,
           pool_scale, w_out, norm2_g, w_up, w_down, final_g):
    B, T, _ = x_prompt.shape
    n_seq = x_sample.shape[0]
    bf16 = jnp.bfloat16

    w_code = jnp.pad(w_in[:, :, W_IN_GATE:W_IN_U], ((0, 0), (0, 0), (0, LANES - GATE_RANK)))
    pw = pool_w.reshape(DEPTH, len(POOL_WINDOWS) // 2, 2, POOL_GROUP, POOL_GROUP)
    zero = jnp.zeros_like(pw[:, :, 0])
    pool_pairs = jnp.concatenate([jnp.concatenate([pw[:, :, 0], zero], axis=-1),
                                  jnp.concatenate([zero, pw[:, :, 1]], axis=-1)], axis=-2)
    params = {
        "n1": norm1_g,
        "w_main": w_in[:, :, :W_IN_GATE].astype(bf16),
        "w_code": w_code.astype(bf16),
        "w_u": w_in[:, :, W_IN_U:].astype(bf16),
        "w_gate": jnp.pad(w_gate, ((0, 0), (0, LANES - GATE_RANK), (0, 0))).astype(bf16),
        "b_gate": b_gate,
        "gla_g": gla_norm_g,
        "pool_w": pool_pairs.astype(bf16),
        "pool_scale": pool_scale,
        "w_out": w_out,
        "n2": norm2_g,
        "w_up": w_up,
        "w_down": w_down,
    }
    final_g2 = final_g.reshape(1, D_MODEL)
    s_state = state_gla.reshape(DEPTH, n_seq, KEY_WIDTH, HEAD_V)
    pool_rows = jnp.swapaxes(state_pool, 1, 2)

    xp = x_prompt
    xs = x_sample.reshape(n_seq, D_MODEL)
    states = None
    for layer in range(DEPTH):
        mix_p, gla_p, hist_p, mix_s, gla_s, pool_s = _mixer(layer, xp, xs, s_state, pool_rows, params, states)
        states = (gla_p, hist_p, gla_s, pool_s)
        xp, xs = _channel_mixer(layer, xp.reshape(B * T, D_MODEL), mix_p.reshape(B * T, D_MODEL), xs, mix_s,
                                params, final_g2)
        xp = xp.reshape(B, T, D_MODEL)
    return (xp, xs.reshape(n_seq, 1, D_MODEL),
            gla_p.reshape(DEPTH, B, N_HEADS, HEAD_K, HEAD_V), hist_p[:, :, POOL_HIST - POOL_BUF:],
            gla_s.reshape(DEPTH, n_seq, N_HEADS, HEAD_K, HEAD_V), jnp.swapaxes(pool_s, 1, 2))
```

```python
import functools

import jax
import jax.numpy as jnp
from jax import lax
from jax.experimental import pallas as pl
from jax.experimental.pallas import tpu as pltpu

D_MODEL = 1024
DEPTH = 4
N_HEADS = 4
HEAD_K = 64
HEAD_V = 128
KEY_WIDTH = N_HEADS * HEAD_K
GLA_WIDTH = N_HEADS * HEAD_V
GATE_RANK = 16
GATE_TEMP = 16.0
CHUNK = 64
POOL_WIDTH = 512
POOL_WINDOWS = (2, 4, 8, 16)
POOL_GROUP = 128
POOL_PAIR = 2 * POOL_GROUP
POOL_BUF = 15
POOL_HIST = 16
D_FF = 4 * D_MODEL
FF_CHUNK = 1024
FF_COMPUTE = 2048
EPS = 1e-6
LANES = 128
SUBLANES = 8

COL_Q = 0
COL_K = KEY_WIDTH
COL_V = 2 * KEY_WIDTH
COL_G = COL_V + GLA_WIDTH
COL_U = COL_G + GLA_WIDTH
Z_WIDTH = COL_U + POOL_WIDTH
W_IN_GATE = COL_U
W_IN_U = COL_U + GATE_RANK

PROMPT_TILE = 1024
PROMPT_SUB = 256
GLA_PAIR = 2 * CHUNK
PROJ_PIECE = 256
CHANNEL_TILE = 512
VMEM_LIMIT = 56 * 1024 * 1024


def _rmsnorm(x, g):
    return x * lax.rsqrt(jnp.mean(x * x, axis=-1, keepdims=True) + EPS) * g


def _log_sigmoid(x):
    return jnp.minimum(x, 0.0) - jnp.log(1.0 + jnp.exp(-jnp.abs(x)))


def _bf16_round(x):
    return x.astype(jnp.bfloat16).astype(jnp.float32)


def _head_lane_mask(rows):
    lane = lax.broadcasted_iota(jnp.int32, (rows, KEY_WIDTH), 1)
    return [(lane >= h * HEAD_K) & (lane < (h + 1) * HEAD_K) for h in range(N_HEADS)]


def _gate_log_decay(a_low, w_gate_ref, b_gate_ref):
    pre = jnp.dot(a_low.astype(jnp.bfloat16), w_gate_ref[...], preferred_element_type=jnp.float32) + b_gate_ref[...]
    return _log_sigmoid(pre) / GATE_TEMP


def _project(x, n1_ref, w_main_ref, w_code_ref, w_u_ref, w_gate_ref, b_gate_ref):
    h = _rmsnorm(x, n1_ref[...]).astype(jnp.bfloat16)
    qkvg = jnp.dot(h, w_main_ref[...], preferred_element_type=jnp.float32)
    a_low = jnp.dot(h, w_code_ref[...], preferred_element_type=jnp.float32)
    u = jnp.dot(h, w_u_ref[...], preferred_element_type=jnp.float32)
    return qkvg, u, _gate_log_decay(a_low, w_gate_ref, b_gate_ref)


def _gla_finish(o, g, gla_g):
    outs = []
    for h in range(N_HEADS):
        sl = slice(h * HEAD_V, (h + 1) * HEAD_V)
        outs.append(_rmsnorm(o[:, sl], gla_g) * (g[:, sl] * jax.nn.sigmoid(g[:, sl])))
    return jnp.concatenate(outs, axis=-1)


def _trace_interleaved(a, b):
    i = j = 0
    while i < len(a) or j < len(b):
        if j >= len(b) or (i < len(a) and i * len(b) <= j * len(a)):
            a[i]()
            i += 1
        else:
            b[j]()
            j += 1


def _prompt_project_tasks(load_x, slot, n1_ref, w_main_ref, w_code_ref, w_u_ref, w_gate_ref, b_gate_ref,
                          z_ref, b_ref):
    v = {}

    def norm():
        v["h"] = _rmsnorm(load_x(), n1_ref[...]).astype(jnp.bfloat16)

    def piece(lo, hi):
        w_ref, first = (w_main_ref, 0) if hi <= COL_U else (w_u_ref, COL_U)

        def run():
            z_ref[slot, :, lo:hi] = jnp.dot(v["h"], w_ref[:, lo - first:hi - first],
                                            preferred_element_type=jnp.float32)
        return run

    def gate_code():
        v["a_low"] = jnp.dot(v["h"], w_code_ref[...], preferred_element_type=jnp.float32)

    def gate_pre():
        v["pre"] = (jnp.dot(v["a_low"].astype(jnp.bfloat16), w_gate_ref[...], preferred_element_type=jnp.float32)
                    + b_gate_ref[...])

    def log_decay():
        v["b"] = _log_sigmoid(v["pre"]) / GATE_TEMP

    def scan(shifts, last):
        def run():
            b = v["b"]
            row = lax.broadcasted_iota(jnp.int32, b.shape, 0) % CHUNK
            for shift in shifts:
                b = b + jnp.where(row >= shift, pltpu.roll(b, shift, axis=0), 0.0)
            v["b"] = b
            if last:
                b_ref[slot] = b
        return run

    half = PROJ_PIECE
    return [
        norm, gate_code,
        piece(COL_Q, COL_K), gate_pre,
        piece(COL_K, COL_V), log_decay,
        piece(COL_V, COL_V + half), scan((1, 2), False),
        piece(COL_V + half, COL_G), scan((4, 8), False),
        piece(COL_G, COL_G + half), scan((16, 32), True),
        piece(COL_G + half, COL_U), piece(COL_U, COL_U + half), piece(COL_U + half, Z_WIDTH),
    ]


def _prompt_mix_tasks(slot, row0, z_ref, b_ref, carry, first_pos, gla_g_ref, pool_w_ref, pool_scale_ref, mix_ref,
                      base):
    head_mask = _head_lane_mask(CHUNK)
    scale = HEAD_K ** -0.5
    sub = GLA_PAIR
    n_chunks = sub // CHUNK
    part = slice(row0, row0 + sub)
    shared = {}

    def transpose_keys():
        b = b_ref[slot, part, :]
        k = z_ref[slot, part, COL_K:COL_V]
        last = [b[(c + 1) * CHUNK - 1:(c + 1) * CHUNK] for c in range(n_chunks)]
        b_last = jnp.concatenate([jnp.broadcast_to(row, (CHUNK, KEY_WIDTH)) for row in last], axis=0)
        shared["kt_t"] = (k * jnp.exp(-b)).T.astype(jnp.bfloat16)
        shared["ke_t"] = (k * jnp.exp(b_last - b)).T.astype(jnp.bfloat16)
        shared["v"] = z_ref[slot, part, COL_V:COL_G].astype(jnp.bfloat16)
        decay = jnp.exp(jnp.concatenate(last + [jnp.zeros((SUBLANES - n_chunks, KEY_WIDTH), jnp.float32)], axis=0))
        shared["decay"] = jnp.broadcast_to(decay[:, :, None], (SUBLANES, KEY_WIDTH, HEAD_V))

    def chunk_tasks(c):
        rows = slice(row0 + c * CHUNK, row0 + (c + 1) * CHUNK)
        out_rows = slice(base + c * CHUNK, base + (c + 1) * CHUNK)
        token = lax.broadcasted_iota(jnp.int32, (CHUNK, sub), 1) - c * CHUNK
        causal = (token >= 0) & (token <= lax.broadcasted_iota(jnp.int32, (CHUNK, sub), 0))
        in_chunk = (lax.broadcasted_iota(jnp.int32, (KEY_WIDTH, sub), 1) // CHUNK) == c
        v = {}

        def prepare():
            qt = z_ref[slot, rows, COL_Q:COL_K] * jnp.exp(b_ref[slot, rows, :]) * scale
            v["q_stack"] = jnp.concatenate([jnp.where(m, qt, 0.0) for m in head_mask], axis=0).astype(jnp.bfloat16)

        def query_product():
            rhs = jnp.concatenate([carry["st"].astype(jnp.bfloat16), shared["kt_t"]], axis=1)
            v["r"] = jnp.dot(v["q_stack"], rhs, preferred_element_type=jnp.float32)

        def update_state():
            carry["st"] = shared["decay"][c] * carry["st"] + shared["upd"][c]

        def masked_scores():
            r = v["r"]
            v["att"] = [jnp.where(causal, r[h * CHUNK:(h + 1) * CHUNK, HEAD_V:HEAD_V + sub], 0.0).astype(jnp.bfloat16)
                        for h in range(N_HEADS)]

        def add_inter(o_intra):
            r = v["r"]
            v["o"] = jnp.concatenate([o_intra[h] + r[h * CHUNK:(h + 1) * CHUNK, 0:HEAD_V] for h in range(N_HEADS)],
                                     axis=-1)

        def finish():
            g_c = z_ref[slot, rows, COL_G:COL_U]
            mix_ref[out_rows, 0:GLA_WIDTH] = _gla_finish(v["o"], g_c, gla_g_ref[...]).astype(mix_ref.dtype)

        return dict(prepare=prepare, query_product=query_product, update_state=update_state,
                    masked_scores=masked_scores, add_inter=add_inter, finish=finish, in_chunk=in_chunk, v=v)

    chunks = [chunk_tasks(c) for c in range(n_chunks)]

    def update_products():
        ke_t = shared["ke_t"]
        zero = jnp.zeros_like(ke_t)
        masked = [jnp.where(ch["in_chunk"], ke_t, zero) for ch in chunks]
        per_head = []
        for h in range(N_HEADS):
            lhs = jnp.concatenate([m[h * HEAD_K:(h + 1) * HEAD_K] for m in masked], axis=0)
            per_head.append(jnp.dot(lhs, shared["v"][:, h * HEAD_V:(h + 1) * HEAD_V],
                                    preferred_element_type=jnp.float32))
        shared["upd"] = [jnp.concatenate([p[c * HEAD_K:(c + 1) * HEAD_K] for p in per_head], axis=0)
                         for c in range(n_chunks)]

    def intra_products():
        per_head = []
        for h in range(N_HEADS):
            lhs = jnp.concatenate([ch["v"]["att"][h] for ch in chunks], axis=0)
            per_head.append(jnp.dot(lhs, shared["v"][:, h * HEAD_V:(h + 1) * HEAD_V],
                                    preferred_element_type=jnp.float32))
        for c, ch in enumerate(chunks):
            ch["add_inter"]([p[c * CHUNK:(c + 1) * CHUNK] for p in per_head])

    pooled = {}

    def window_mean_task(gi, w):
        lanes = slice(gi * POOL_GROUP, (gi + 1) * POOL_GROUP)

        def window_mean():
            u = z_ref[slot, part, COL_U + gi * POOL_GROUP:COL_U + (gi + 1) * POOL_GROUP]
            s = jnp.concatenate([carry["hist"][:, lanes], u], axis=0)
            shift = 1
            while shift < w:
                s = s + pltpu.roll(s, shift, axis=0)
                shift *= 2
            seen = first_pos + 1 + lax.broadcasted_iota(jnp.int32, (sub, POOL_GROUP), 0)
            if "inv_seen" not in pooled:
                pooled["inv_seen"] = 1.0 / seen.astype(jnp.float32)
            inv_count = jnp.where(seen < w, pooled["inv_seen"], 1.0 / w)
            pooled[gi] = (s[POOL_HIST:] * inv_count - u).astype(jnp.bfloat16)

        return window_mean

    def group_map_task(pair):
        lanes = slice(pair * POOL_PAIR, (pair + 1) * POOL_PAIR)

        def group_map():
            both = jnp.concatenate([pooled[2 * pair], pooled[2 * pair + 1]], axis=-1)
            pg = jnp.dot(both, pool_w_ref[pair], preferred_element_type=jnp.float32)
            pg = pg * pool_scale_ref[:, lanes]
            mix_ref[base:base + sub, GLA_WIDTH + pair * POOL_PAIR:GLA_WIDTH + (pair + 1) * POOL_PAIR] = (
                pg.astype(mix_ref.dtype))

        return group_map

    def keep_history():
        carry["hist"] = z_ref[slot, row0 + sub - POOL_HIST:row0 + sub, COL_U:Z_WIDTH]

    assert n_chunks == 2
    c0, c1 = chunks
    means = [window_mean_task(gi, w) for gi, w in enumerate(POOL_WINDOWS)]
    maps = [group_map_task(pair) for pair in range(len(POOL_WINDOWS) // 2)]
    return [
        transpose_keys, c0["prepare"], c1["prepare"], update_products, c0["query_product"], means[0],
        c0["update_state"], c1["query_product"], c0["masked_scores"], c1["update_state"], means[1],
        c1["masked_scores"], intra_products, maps[0], means[2], c0["finish"], means[3], c1["finish"], maps[1],
        keep_history,
    ]


def _sample_project(xs_ref, n1_ref, w_main_ref, w_code_ref, w_u_ref, w_gate_ref, b_gate_ref, pool_w_ref, pool_scale_ref,
                    pool_state_ref, mix_s_ref, pool_new_ref, dec_ref, ke_ref, qt_ref, v_ref, g_ref, o_ref):
    n_seq = xs_ref.shape[0]
    qkvg, u, log_a = _project(xs_ref[...], n1_ref, w_main_ref, w_code_ref, w_u_ref, w_gate_ref, b_gate_ref)
    q = qkvg[:, COL_Q:COL_K]
    k = qkvg[:, COL_K:COL_V]
    v = _bf16_round(qkvg[:, COL_V:COL_G])
    qt = _bf16_round(q * jnp.exp(log_a) * (HEAD_K ** -0.5))
    kt = _bf16_round(k * jnp.exp(-log_a))
    ke = _bf16_round(k * jnp.exp(log_a - log_a))
    prod = qt * kt
    o_intra = []
    for h, m in enumerate(_head_lane_mask(n_seq)):
        att = _bf16_round(jnp.sum(jnp.where(m, prod, 0.0), axis=-1, keepdims=True))
        o_intra.append(att * v[:, h * HEAD_V:(h + 1) * HEAD_V])
    o_ref[...] = jnp.concatenate(o_intra, axis=-1)
    dec_ref[...] = jnp.exp(log_a)
    ke_ref[...] = ke
    qt_ref[...] = qt
    v_ref[...] = v
    g_ref[...] = qkvg[:, COL_G:COL_U]

    pooled = []
    for gi, w in enumerate(POOL_WINDOWS):
        lanes = slice(gi * POOL_GROUP, (gi + 1) * POOL_GROUP)
        s = u[:, lanes]
        for j in range(POOL_BUF - (w - 1), POOL_BUF):
            s = s + pool_state_ref[j, :, lanes]
        pooled.append((s / float(w) - u[:, lanes]).astype(jnp.bfloat16))
    for pair in range(len(POOL_WINDOWS) // 2):
        lanes = slice(pair * POOL_PAIR, (pair + 1) * POOL_PAIR)
        both = jnp.concatenate(pooled[2 * pair:2 * pair + 2], axis=-1)
        pg = jnp.dot(both, pool_w_ref[pair], preferred_element_type=jnp.float32) * pool_scale_ref[:, lanes]
        mix_s_ref[:, GLA_WIDTH + pair * POOL_PAIR:GLA_WIDTH + (pair + 1) * POOL_PAIR] = pg.astype(mix_s_ref.dtype)
    for j in range(POOL_BUF - 1):
        pool_new_ref[j] = pool_state_ref[j + 1]
    pool_new_ref[POOL_BUF - 1] = u


def _sample_state_tasks(first_seq, s_ref, s_new_ref, dec_ref, ke_ref, qt_ref, v_ref, o_ref):
    rows = pl.ds(pl.multiple_of(first_seq, SUBLANES), SUBLANES)

    def update_state():
        v = v_ref[rows, :]
        v_rows = jnp.concatenate(
            [jnp.broadcast_to(v[:, None, h * HEAD_V:(h + 1) * HEAD_V], (SUBLANES, HEAD_K, HEAD_V))
             for h in range(N_HEADS)], axis=1)
        s_new_ref[...] = dec_ref[rows, :][:, :, None] * s_ref[...] + ke_ref[rows, :][:, :, None] * v_rows

    def query_state():
        lane = lax.broadcasted_iota(jnp.int32, (SUBLANES, KEY_WIDTH), 1)
        head = lax.broadcasted_iota(jnp.int32, (SUBLANES, KEY_WIDTH), 0)
        own_head = lane // HEAD_K == head
        q_rows = jnp.where(own_head[None], qt_ref[rows, :][:, None, :], 0.0).astype(jnp.bfloat16)
        o_inter = jnp.einsum("bhk,bkv->bhv", q_rows, s_ref[...].astype(jnp.bfloat16),
                             preferred_element_type=jnp.float32)
        o_ref[rows, :] = o_ref[rows, :] + jnp.concatenate([o_inter[:, h, :] for h in range(N_HEADS)], axis=-1)

    return [update_state, query_state]


def _mixer_kernel(*refs, layer, chained):
    (x_ref, x_next_ref, n1_ref, w_main_ref, w_code_ref, w_u_ref, w_gate_ref, b_gate_ref, gla_g_ref, pool_w_ref,
     pool_scale_ref, xs_ref, s_ref, pool_state_ref) = refs[:14]
    n1_ref, b_gate_ref, gla_g_ref, pool_scale_ref = (
        ref.at[pl.ds(layer, 1), :] for ref in (n1_ref, b_gate_ref, gla_g_ref, pool_scale_ref))
    (mix_ref, s_fin_ref, hist_out_ref, mix_s_ref, s_new_ref, pool_new_ref,
     st_ref, hist_ref, z_ref, b_ref, dec_ref, ke_ref, qt_ref, v_ref, g_ref, o_ref) = refs[-16:]
    tile = x_ref.shape[0]
    n_sub = tile // PROMPT_SUB
    t = pl.program_id(1)
    step = pl.program_id(0) * pl.num_programs(1) + t
    last_step = pl.num_programs(0) * pl.num_programs(1) - 1

    def project_tasks(load_x, slot):
        return _prompt_project_tasks(load_x, slot, n1_ref, w_main_ref, w_code_ref, w_u_ref, w_gate_ref, b_gate_ref,
                                     z_ref, b_ref)

    @pl.when(t == 0)
    def _():
        st_ref[...] = jnp.zeros_like(st_ref)
        hist_ref[...] = jnp.zeros_like(hist_ref)

    @pl.when(step == 0)
    def _():
        for task in project_tasks(lambda: x_ref[0:PROMPT_SUB], 0):
            task()
        _sample_project(xs_ref, n1_ref, w_main_ref, w_code_ref, w_u_ref, w_gate_ref, b_gate_ref, pool_w_ref,
                        pool_scale_ref, pool_state_ref, mix_s_ref, pool_new_ref, dec_ref, ke_ref, qt_ref, v_ref, g_ref,
                        o_ref)

    sample_tasks = _sample_state_tasks(step * SUBLANES, s_ref, s_new_ref, dec_ref, ke_ref, qt_ref, v_ref, o_ref)
    carry = {"st": st_ref[...], "hist": hist_ref[...]}
    for sb in range(n_sub):
        if sb + 1 < n_sub:
            load_next = lambda sb=sb: x_ref[(sb + 1) * PROMPT_SUB:(sb + 2) * PROMPT_SUB]
        else:
            load_next = lambda: x_next_ref[...]
        mix_tasks = []
        for row0 in range(0, PROMPT_SUB, GLA_PAIR):
            mix_tasks += _prompt_mix_tasks(sb % 2, row0, z_ref, b_ref, carry, t * tile + sb * PROMPT_SUB + row0,
                                           gla_g_ref, pool_w_ref, pool_scale_ref, mix_ref, sb * PROMPT_SUB + row0)
        if sample_tasks and sb >= 1:
            mix_tasks.insert(len(mix_tasks) // 2, sample_tasks.pop(0))
        _trace_interleaved(project_tasks(load_next, (sb + 1) % 2), mix_tasks)
    st_ref[...] = carry["st"]
    hist_ref[...] = carry["hist"]

    @pl.when(t == pl.num_programs(1) - 1)
    def _():
        s_fin_ref[...] = carry["st"]
        hist_out_ref[...] = carry["hist"]

    @pl.when(step == last_step)
    def _():
        mix_s_ref[:, 0:GLA_WIDTH] = _gla_finish(o_ref[...], g_ref[...], gla_g_ref[...]).astype(mix_s_ref.dtype)


def _mixer(layer, x, xs, s_state, pool_state, p, stacked):
    B, T, _ = x.shape
    n_seq = xs.shape[0]
    tile = PROMPT_TILE
    n_tiles = T // tile
    n_sub = tile // PROMPT_SUB
    assert n_sub % 2 == 0
    assert n_sub >= 3 and B * n_tiles * SUBLANES == n_seq
    assert PROMPT_SUB % GLA_PAIR == 0
    f32 = jnp.float32
    whole = lambda *shape: pl.BlockSpec((None,) + shape, lambda b, t: (layer,) + (0,) * len(shape))
    fixed = lambda *shape: pl.BlockSpec(shape, lambda b, t: (0,) * len(shape))
    all_layers = lambda width: fixed(DEPTH, width)
    chained = stacked is not None
    operands = [x, x, p["n1"], p["w_main"], p["w_code"], p["w_u"], p["w_gate"], p["b_gate"], p["gla_g"], p["pool_w"],
                p["pool_scale"], xs, s_state, pool_state]

    def next_first_sub_block(b, t):
        n = jnp.minimum(b * n_tiles + t + 1, B * n_tiles - 1)
        return (n // n_tiles, (n % n_tiles) * n_sub, 0)

    sample_state_block = pl.BlockSpec((None, SUBLANES, KEY_WIDTH, HEAD_V), lambda b, t: (layer, b * n_tiles + t, 0, 0))
    in_specs = [
        pl.BlockSpec((None, tile, D_MODEL), lambda b, t: (b, t, 0)),
        pl.BlockSpec((None, PROMPT_SUB, D_MODEL), next_first_sub_block),
        all_layers(D_MODEL),
        whole(D_MODEL, COL_U),
        whole(D_MODEL, LANES),
        whole(D_MODEL, POOL_WIDTH),
        whole(LANES, KEY_WIDTH),
        all_layers(KEY_WIDTH),
        all_layers(HEAD_V),
        whole(len(POOL_WINDOWS) // 2, POOL_PAIR, POOL_PAIR),
        all_layers(POOL_WIDTH),
        fixed(n_seq, D_MODEL),
        sample_state_block,
        whole(POOL_BUF, n_seq, POOL_WIDTH),
    ]
    aliases = {}
    if chained:
        aliases = {len(operands): 1, len(operands) + 1: 2, len(operands) + 2: 4, len(operands) + 3: 5}
        operands += list(stacked)
        in_specs += [pl.BlockSpec(memory_space=pl.ANY)] * 4
    return pl.pallas_call(
        functools.partial(_mixer_kernel, layer=layer, chained=chained),
        grid=(B, n_tiles),
        in_specs=in_specs,
        out_specs=[
            pl.BlockSpec((None, tile, D_MODEL), lambda b, t: (b, t, 0)),
            pl.BlockSpec((None, None, KEY_WIDTH, HEAD_V), lambda b, t: (layer, b, 0, 0)),
            pl.BlockSpec((None, None, POOL_HIST, POOL_WIDTH), lambda b, t: (layer, b, 0, 0)),
            fixed(n_seq, D_MODEL),
            sample_state_block,
            whole(POOL_BUF, n_seq, POOL_WIDTH),
        ],
        out_shape=[
            jax.ShapeDtypeStruct((B, T, D_MODEL), jnp.bfloat16),
            jax.ShapeDtypeStruct((DEPTH, B, KEY_WIDTH, HEAD_V), f32),
            jax.ShapeDtypeStruct((DEPTH, B, POOL_HIST, POOL_WIDTH), f32),
            jax.ShapeDtypeStruct((n_seq, D_MODEL), jnp.bfloat16),
            jax.ShapeDtypeStruct((DEPTH, n_seq, KEY_WIDTH, HEAD_V), f32),
            jax.ShapeDtypeStruct((DEPTH, POOL_BUF, n_seq, POOL_WIDTH), f32),
        ],
        scratch_shapes=[
            pltpu.VMEM((KEY_WIDTH, HEAD_V), f32),
            pltpu.VMEM((POOL_HIST, POOL_WIDTH), f32),
            pltpu.VMEM((2, PROMPT_SUB, Z_WIDTH), f32),
            pltpu.VMEM((2, PROMPT_SUB, KEY_WIDTH), f32),
            pltpu.VMEM((n_seq, KEY_WIDTH), f32),
            pltpu.VMEM((n_seq, KEY_WIDTH), f32),
            pltpu.VMEM((n_seq, KEY_WIDTH), f32),
            pltpu.VMEM((n_seq, GLA_WIDTH), f32),
            pltpu.VMEM((n_seq, GLA_WIDTH), f32),
            pltpu.VMEM((n_seq, GLA_WIDTH), f32),
        ],
        input_output_aliases=aliases,
        compiler_params=pltpu.CompilerParams(
            dimension_semantics=("arbitrary", "arbitrary"), vmem_limit_bytes=VMEM_LIMIT),
        name=f"mixer_l{layer}",
    )(*operands)


N_FF_CHUNKS = D_FF // FF_CHUNK
CHUNKS_PER_SLICE = FF_COMPUTE // FF_CHUNK
assert FF_COMPUTE == CHUNKS_PER_SLICE * FF_CHUNK and D_FF % FF_COMPUTE == 0
WEIGHT_CHUNKS = [("out", 0)]
for _c in range(D_FF // FF_COMPUTE):
    _js = range(_c * CHUNKS_PER_SLICE, (_c + 1) * CHUNKS_PER_SLICE)
    WEIGHT_CHUNKS += [("up", _j) for _j in _js] + [("down", _j) for _j in _js]
N_WEIGHT_CHUNKS = len(WEIGHT_CHUNKS)
STAGE_SLOTS = 3
assert FF_CHUNK == D_MODEL


def _channel_kernel(x_ref, mix_ref, xs_ref, mix_s_ref, n2_ref, final_g_ref, w_out_hbm, w_up_hbm, w_down_hbm,
                    y_ref, ys_ref, w_out_ref, w_up_ref, w_down_ref, stage_ref, sem, *, layer, final):
    step = pl.program_id(0)
    n2_ref = n2_ref.at[pl.ds(layer, 1), :]

    def chunk(k):
        kind, j = WEIGHT_CHUNKS[k]
        span = pl.ds(j * FF_CHUNK, FF_CHUNK)
        if kind == "out":
            return w_out_hbm.at[layer], w_out_ref
        if kind == "down":
            return w_down_hbm.at[layer, span, :], w_down_ref.at[span, :]
        return w_up_hbm.at[layer, :, span], w_up_ref.at[:, span]

    def chunk_copy(k):
        return pltpu.make_async_copy(chunk(k)[0], stage_ref.at[k % STAGE_SLOTS], sem.at[k % STAGE_SLOTS])

    def fetch(k):
        chunk_copy(k).wait()
        chunk(k)[1][...] = stage_ref[k % STAGE_SLOTS].astype(jnp.bfloat16)
        if k + STAGE_SLOTS < N_WEIGHT_CHUNKS:
            chunk_copy(k + STAGE_SLOTS).start()

    def channel_mix(x, mix, fetch_weights):
        fetched = [0]

        def need(k):
            while fetch_weights and fetched[0] <= min(k, N_WEIGHT_CHUNKS - 1):
                fetch(fetched[0])
                fetched[0] += 1

        per_slice = 2 * CHUNKS_PER_SLICE
        need(CHUNKS_PER_SLICE)
        x1 = x + jnp.dot(mix, w_out_ref[...], preferred_element_type=jnp.float32)
        h2 = _rmsnorm(x1, n2_ref[...]).astype(jnp.bfloat16)
        acc = x1
        for c in range(D_FF // FF_COMPUTE):
            cols = slice(c * FF_COMPUTE, (c + 1) * FF_COMPUTE)
            need((c + 1) * per_slice)
            hid = jnp.dot(h2, w_up_ref[:, cols], preferred_element_type=jnp.float32)
            act = jnp.square(jnp.maximum(hid, 0.0)).astype(jnp.bfloat16)
            need((c + 1) * per_slice + CHUNKS_PER_SLICE)
            acc = acc + jnp.dot(act, w_down_ref[cols, :], preferred_element_type=jnp.float32)
        if final:
            acc = _rmsnorm(acc, final_g_ref[...])
        return acc

    @pl.when(step == 0)
    def _():
        for k in range(STAGE_SLOTS):
            chunk_copy(k).start()
        y_ref[...] = channel_mix(x_ref[...], mix_ref[...], True)

    @pl.when(step > 0)
    def _():
        y_ref[...] = channel_mix(x_ref[...], mix_ref[...], False)

    @pl.when(step == pl.num_programs(0) - 1)
    def _():
        ys_ref[...] = channel_mix(xs_ref[...], mix_s_ref[...], False)


def _channel_mixer(layer, x, mix, xs, mix_s, p, final_g):
    rows = x.shape[0]
    n_seq = xs.shape[0]
    tile = CHANNEL_TILE
    assert rows // tile > 1
    bf16 = jnp.bfloat16
    return pl.pallas_call(
        functools.partial(_channel_kernel, layer=layer, final=layer == DEPTH - 1),
        grid=(rows // tile,),
        in_specs=[
            pl.BlockSpec((tile, D_MODEL), lambda i: (i, 0)),
            pl.BlockSpec((tile, D_MODEL), lambda i: (i, 0)),
            pl.BlockSpec((n_seq, D_MODEL), lambda i: (0, 0)),
            pl.BlockSpec((n_seq, D_MODEL), lambda i: (0, 0)),
            pl.BlockSpec((DEPTH, D_MODEL), lambda i: (0, 0)),
            pl.BlockSpec((1, D_MODEL), lambda i: (0, 0)),
            pl.BlockSpec(memory_space=pl.ANY),
            pl.BlockSpec(memory_space=pl.ANY),
            pl.BlockSpec(memory_space=pl.ANY),
        ],
        out_specs=[
            pl.BlockSpec((tile, D_MODEL), lambda i: (i, 0)),
            pl.BlockSpec((n_seq, D_MODEL), lambda i: (0, 0)),
        ],
        out_shape=[
            jax.ShapeDtypeStruct((rows, D_MODEL), jnp.float32),
            jax.ShapeDtypeStruct((n_seq, D_MODEL), jnp.float32),
        ],
        scratch_shapes=[
            pltpu.VMEM((D_MODEL, D_MODEL), bf16),
            pltpu.VMEM((D_MODEL, D_FF), bf16),
            pltpu.VMEM((D_FF, D_MODEL), bf16),
            pltpu.VMEM((STAGE_SLOTS, D_MODEL, D_MODEL), jnp.float32),
            pltpu.SemaphoreType.DMA((STAGE_SLOTS,)),
        ],
        compiler_params=pltpu.CompilerParams(
            dimension_semantics=("arbitrary",), vmem_limit_bytes=VMEM_LIMIT),
        name=f"channel_mixer_l{layer}",
    )(x, mix, xs, mix_s, p["n2"], final_g, p["w_out"], p["w_up"], p["w_down"])


def kernel(x_prompt, x_sample, state_gla, state_pool, norm1_g, w_in, w_gate, b_gate, gla_norm_g, pool_w,
           pool_scale, w_out, norm2_g, w_up, w_down, final_g):
    B, T, _ = x_prompt.shape
    n_seq = x_sample.shape[0]
    bf16 = jnp.bfloat16

    w_code = jnp.pad(w_in[:, :, W_IN_GATE:W_IN_U], ((0, 0), (0, 0), (0, LANES - GATE_RANK)))
    pw = pool_w.reshape(DEPTH, len(POOL_WINDOWS) // 2, 2, POOL_GROUP, POOL_GROUP)
    zero = jnp.zeros_like(pw[:, :, 0])
    pool_pairs = jnp.concatenate([jnp.concatenate([pw[:, :, 0], zero], axis=-1),
                                  jnp.concatenate([zero, pw[:, :, 1]], axis=-1)], axis=-2)
    params = {
        "n1": norm1_g,
        "w_main": w_in[:, :, :W_IN_GATE].astype(bf16),
        "w_code": w_code.astype(bf16),
        "w_u": w_in[:, :, W_IN_U:].astype(bf16),
        "w_gate": jnp.pad(w_gate, ((0, 0), (0, LANES - GATE_RANK), (0, 0))).astype(bf16),
        "b_gate": b_gate,
        "gla_g": gla_norm_g,
        "pool_w": pool_pairs.astype(bf16),
        "pool_scale": pool_scale,
        "w_out": w_out,
        "n2": norm2_g,
        "w_up": w_up,
        "w_down": w_down,
    }
    final_g2 = final_g.reshape(1, D_MODEL)
    s_state = state_gla.reshape(DEPTH, n_seq, KEY_WIDTH, HEAD_V)
    pool_rows = jnp.swapaxes(state_pool, 1, 2)

    xp = x_prompt
    xs = x_sample.reshape(n_seq, D_MODEL)
    states = None
    for layer in range(DEPTH):
        mix_p, gla_p, hist_p, mix_s, gla_s, pool_s = _mixer(layer, xp, xs, s_state, pool_rows, params, states)
        states = (gla_p, hist_p, gla_s, pool_s)
        xp, xs = _channel_mixer(layer, xp.reshape(B * T, D_MODEL), mix_p.reshape(B * T, D_MODEL), xs, mix_s,
                                params, final_g2)
        xp = xp.reshape(B, T, D_MODEL)
    return (xp, xs.reshape(n_seq, 1, D_MODEL),
            gla_p.reshape(DEPTH, B, N_HEADS, HEAD_K, HEAD_V), hist_p[:, :, POOL_HIST - POOL_BUF:],
            gla_s.reshape(DEPTH, n_seq, N_HEADS, HEAD_K, HEAD_V), jnp.swapaxes(pool_s, 1, 2))
```

```python
import functools

import jax
import jax.numpy as jnp
from jax import lax
from jax.experimental import pallas as pl
from jax.experimental.pallas import tpu as pltpu

D_MODEL = 1024
DEPTH = 4
N_HEADS = 4
HEAD_K = 64
HEAD_V = 128
KEY_WIDTH = N_HEADS * HEAD_K
GLA_WIDTH = N_HEADS * HEAD_V
GATE_RANK = 16
GATE_TEMP = 16.0
CHUNK = 64
POOL_WIDTH = 512
POOL_WINDOWS = (2, 4, 8, 16)
POOL_GROUP = 128
POOL_PAIR = 2 * POOL_GROUP
POOL_BUF = 15
POOL_HIST = 16
D_FF = 4 * D_MODEL
FF_CHUNK = 1024
EPS = 1e-6
LANES = 128
SUBLANES = 8

COL_Q = 0
COL_K = KEY_WIDTH
COL_V = 2 * KEY_WIDTH
COL_G = COL_V + GLA_WIDTH
COL_U = COL_G + GLA_WIDTH
Z_WIDTH = COL_U + POOL_WIDTH
W_IN_GATE = COL_U
W_IN_U = COL_U + GATE_RANK

PROMPT_TILE = 1024
PROMPT_SUB = 256
GLA_PAIR = 2 * CHUNK
PROJ_PIECE = 256
CHANNEL_TILE = 512
VMEM_LIMIT = 56 * 1024 * 1024


def _rmsnorm(x, g):
    return x * lax.rsqrt(jnp.mean(x * x, axis=-1, keepdims=True) + EPS) * g


def _log_sigmoid(x):
    return jnp.minimum(x, 0.0) - jnp.log(1.0 + jnp.exp(-jnp.abs(x)))


def _bf16_round(x):
    return x.astype(jnp.bfloat16).astype(jnp.float32)


def _head_lane_mask(rows):
    lane = lax.broadcasted_iota(jnp.int32, (rows, KEY_WIDTH), 1)
    return [(lane >= h * HEAD_K) & (lane < (h + 1) * HEAD_K) for h in range(N_HEADS)]


def _gate_log_decay(a_low, w_gate_ref, b_gate_ref):
    pre = jnp.dot(a_low.astype(jnp.bfloat16), w_gate_ref[...], preferred_element_type=jnp.float32) + b_gate_ref[...]
    return _log_sigmoid(pre) / GATE_TEMP


def _project(x, n1_ref, w_main_ref, w_code_ref, w_u_ref, w_gate_ref, b_gate_ref):
    h = _rmsnorm(x, n1_ref[...]).astype(jnp.bfloat16)
    qkvg = jnp.dot(h, w_main_ref[...], preferred_element_type=jnp.float32)
    a_low = jnp.dot(h, w_code_ref[...], preferred_element_type=jnp.float32)
    u = jnp.dot(h, w_u_ref[...], preferred_element_type=jnp.float32)
    return qkvg, u, _gate_log_decay(a_low, w_gate_ref, b_gate_ref)


def _gla_finish(o, g, gla_g):
    outs = []
    for h in range(N_HEADS):
        sl = slice(h * HEAD_V, (h + 1) * HEAD_V)
        outs.append(_rmsnorm(o[:, sl], gla_g) * (g[:, sl] * jax.nn.sigmoid(g[:, sl])))
    return jnp.concatenate(outs, axis=-1)


def _trace_interleaved(a, b):
    i = j = 0
    while i < len(a) or j < len(b):
        if j >= len(b) or (i < len(a) and i * len(b) <= j * len(a)):
            a[i]()
            i += 1
        else:
            b[j]()
            j += 1


def _prompt_project_tasks(load_x, slot, n1_ref, w_main_ref, w_code_ref, w_u_ref, w_gate_ref, b_gate_ref,
                          z_ref, b_ref):
    v = {}

    def norm():
        v["h"] = _rmsnorm(load_x(), n1_ref[...]).astype(jnp.bfloat16)

    def piece(lo, hi):
        w_ref, first = (w_main_ref, 0) if hi <= COL_U else (w_u_ref, COL_U)

        def run():
            z_ref[slot, :, lo:hi] = jnp.dot(v["h"], w_ref[:, lo - first:hi - first],
                                            preferred_element_type=jnp.float32)
        return run

    def gate_code():
        v["a_low"] = jnp.dot(v["h"], w_code_ref[...], preferred_element_type=jnp.float32)

    def gate_pre():
        v["pre"] = (jnp.dot(v["a_low"].astype(jnp.bfloat16), w_gate_ref[...], preferred_element_type=jnp.float32)
                    + b_gate_ref[...])

    def log_decay():
        v["b"] = _log_sigmoid(v["pre"]) / GATE_TEMP

    def scan(shifts, last):
        def run():
            b = v["b"]
            row = lax.broadcasted_iota(jnp.int32, b.shape, 0) % CHUNK
            for shift in shifts:
                b = b + jnp.where(row >= shift, pltpu.roll(b, shift, axis=0), 0.0)
            v["b"] = b
            if last:
                b_ref[slot] = b
        return run

    half = PROJ_PIECE
    return [
        norm, gate_code,
        piece(COL_Q, COL_K), gate_pre,
        piece(COL_K, COL_V), log_decay,
        piece(COL_V, COL_V + half), scan((1, 2), False),
        piece(COL_V + half, COL_G), scan((4, 8), False),
        piece(COL_G, COL_G + half), scan((16, 32), True),
        piece(COL_G + half, COL_U), piece(COL_U, COL_U + half), piece(COL_U + half, Z_WIDTH),
    ]


def _prompt_mix_tasks(slot, row0, z_ref, b_ref, carry, first_pos, gla_g_ref, pool_w_ref, pool_scale_ref, mix_ref,
                      base):
    head_mask = _head_lane_mask(CHUNK)
    scale = HEAD_K ** -0.5
    sub = GLA_PAIR
    n_chunks = sub // CHUNK
    part = slice(row0, row0 + sub)
    shared = {}

    def transpose_keys():
        b = b_ref[slot, part, :]
        k = z_ref[slot, part, COL_K:COL_V]
        last = [b[(c + 1) * CHUNK - 1:(c + 1) * CHUNK] for c in range(n_chunks)]
        b_last = jnp.concatenate([jnp.broadcast_to(row, (CHUNK, KEY_WIDTH)) for row in last], axis=0)
        shared["kt_t"] = (k * jnp.exp(-b)).T.astype(jnp.bfloat16)
        shared["ke_t"] = (k * jnp.exp(b_last - b)).T.astype(jnp.bfloat16)
        shared["v"] = z_ref[slot, part, COL_V:COL_G].astype(jnp.bfloat16)
        decay = jnp.exp(jnp.concatenate(last + [jnp.zeros((SUBLANES - n_chunks, KEY_WIDTH), jnp.float32)], axis=0))
        shared["decay"] = jnp.broadcast_to(decay[:, :, None], (SUBLANES, KEY_WIDTH, HEAD_V))

    def chunk_tasks(c):
        rows = slice(row0 + c * CHUNK, row0 + (c + 1) * CHUNK)
        out_rows = slice(base + c * CHUNK, base + (c + 1) * CHUNK)
        token = lax.broadcasted_iota(jnp.int32, (CHUNK, sub), 1) - c * CHUNK
        causal = (token >= 0) & (token <= lax.broadcasted_iota(jnp.int32, (CHUNK, sub), 0))
        in_chunk = (lax.broadcasted_iota(jnp.int32, (KEY_WIDTH, sub), 1) // CHUNK) == c
        v = {}

        def prepare():
            qt = z_ref[slot, rows, COL_Q:COL_K] * jnp.exp(b_ref[slot, rows, :]) * scale
            v["q_stack"] = jnp.concatenate([jnp.where(m, qt, 0.0) for m in head_mask], axis=0).astype(jnp.bfloat16)

        def query_product():
            rhs = jnp.concatenate([carry["st"].astype(jnp.bfloat16), shared["kt_t"]], axis=1)
            v["r"] = jnp.dot(v["q_stack"], rhs, preferred_element_type=jnp.float32)

        def update_state():
            carry["st"] = shared["decay"][c] * carry["st"] + shared["upd"][c]

        def masked_scores():
            r = v["r"]
            v["att"] = [jnp.where(causal, r[h * CHUNK:(h + 1) * CHUNK, HEAD_V:HEAD_V + sub], 0.0).astype(jnp.bfloat16)
                        for h in range(N_HEADS)]

        def add_inter(o_intra):
            r = v["r"]
            v["o"] = jnp.concatenate([o_intra[h] + r[h * CHUNK:(h + 1) * CHUNK, 0:HEAD_V] for h in range(N_HEADS)],
                                     axis=-1)

        def finish():
            g_c = z_ref[slot, rows, COL_G:COL_U]
            mix_ref[out_rows, 0:GLA_WIDTH] = _gla_finish(v["o"], g_c, gla_g_ref[...]).astype(mix_ref.dtype)

        return dict(prepare=prepare, query_product=query_product, update_state=update_state,
                    masked_scores=masked_scores, add_inter=add_inter, finish=finish, in_chunk=in_chunk, v=v)

    chunks = [chunk_tasks(c) for c in range(n_chunks)]

    def update_products():
        ke_t = shared["ke_t"]
        zero = jnp.zeros_like(ke_t)
        masked = [jnp.where(ch["in_chunk"], ke_t, zero) for ch in chunks]
        per_head = []
        for h in range(N_HEADS):
            lhs = jnp.concatenate([m[h * HEAD_K:(h + 1) * HEAD_K] for m in masked], axis=0)
            per_head.append(jnp.dot(lhs, shared["v"][:, h * HEAD_V:(h + 1) * HEAD_V],
                                    preferred_element_type=jnp.float32))
        shared["upd"] = [jnp.concatenate([p[c * HEAD_K:(c + 1) * HEAD_K] for p in per_head], axis=0)
                         for c in range(n_chunks)]

    def intra_products():
        per_head = []
        for h in range(N_HEADS):
            lhs = jnp.concatenate([ch["v"]["att"][h] for ch in chunks], axis=0)
            per_head.append(jnp.dot(lhs, shared["v"][:, h * HEAD_V:(h + 1) * HEAD_V],
                                    preferred_element_type=jnp.float32))
        for c, ch in enumerate(chunks):
            ch["add_inter"]([p[c * CHUNK:(c + 1) * CHUNK] for p in per_head])

    pooled = {}

    def window_mean_task(gi, w):
        lanes = slice(gi * POOL_GROUP, (gi + 1) * POOL_GROUP)

        def window_mean():
            u = z_ref[slot, part, COL_U + gi * POOL_GROUP:COL_U + (gi + 1) * POOL_GROUP]
            s = jnp.concatenate([carry["hist"][:, lanes], u], axis=0)
            shift = 1
            while shift < w:
                s = s + pltpu.roll(s, shift, axis=0)
                shift *= 2
            seen = first_pos + 1 + lax.broadcasted_iota(jnp.int32, (sub, POOL_GROUP), 0)
            if "inv_seen" not in pooled:
                pooled["inv_seen"] = 1.0 / seen.astype(jnp.float32)
            inv_count = jnp.where(seen < w, pooled["inv_seen"], 1.0 / w)
            pooled[gi] = (s[POOL_HIST:] * inv_count - u).astype(jnp.bfloat16)

        return window_mean

    def group_map_task(pair):
        lanes = slice(pair * POOL_PAIR, (pair + 1) * POOL_PAIR)

        def group_map():
            both = jnp.concatenate([pooled[2 * pair], pooled[2 * pair + 1]], axis=-1)
            pg = jnp.dot(both, pool_w_ref[pair], preferred_element_type=jnp.float32)
            pg = pg * pool_scale_ref[:, lanes]
            mix_ref[base:base + sub, GLA_WIDTH + pair * POOL_PAIR:GLA_WIDTH + (pair + 1) * POOL_PAIR] = (
                pg.astype(mix_ref.dtype))

        return group_map

    def keep_history():
        carry["hist"] = z_ref[slot, row0 + sub - POOL_HIST:row0 + sub, COL_U:Z_WIDTH]

    assert n_chunks == 2
    c0, c1 = chunks
    means = [window_mean_task(gi, w) for gi, w in enumerate(POOL_WINDOWS)]
    maps = [group_map_task(pair) for pair in range(len(POOL_WINDOWS) // 2)]
    return [
        transpose_keys, c0["prepare"], c1["prepare"], update_products, c0["query_product"], means[0],
        c0["update_state"], c1["query_product"], c0["masked_scores"], c1["update_state"], means[1],
        c1["masked_scores"], intra_products, maps[0], means[2], c0["finish"], means[3], c1["finish"], maps[1],
        keep_history,
    ]


def _sample_project(xs_ref, n1_ref, w_main_ref, w_code_ref, w_u_ref, w_gate_ref, b_gate_ref, pool_w_ref, pool_scale_ref,
                    pool_state_ref, mix_s_ref, pool_new_ref, dec_ref, ke_ref, qt_ref, v_ref, g_ref, o_ref):
    n_seq = xs_ref.shape[0]
    qkvg, u, log_a = _project(xs_ref[...], n1_ref, w_main_ref, w_code_ref, w_u_ref, w_gate_ref, b_gate_ref)
    q = qkvg[:, COL_Q:COL_K]
    k = qkvg[:, COL_K:COL_V]
    v = _bf16_round(qkvg[:, COL_V:COL_G])
    qt = _bf16_round(q * jnp.exp(log_a) * (HEAD_K ** -0.5))
    kt = _bf16_round(k * jnp.exp(-log_a))
    ke = _bf16_round(k * jnp.exp(log_a - log_a))
    prod = qt * kt
    o_intra = []
    for h, m in enumerate(_head_lane_mask(n_seq)):
        att = _bf16_round(jnp.sum(jnp.where(m, prod, 0.0), axis=-1, keepdims=True))
        o_intra.append(att * v[:, h * HEAD_V:(h + 1) * HEAD_V])
    o_ref[...] = jnp.concatenate(o_intra, axis=-1)
    dec_ref[...] = jnp.exp(log_a)
    ke_ref[...] = ke
    qt_ref[...] = qt
    v_ref[...] = v
    g_ref[...] = qkvg[:, COL_G:COL_U]

    pooled = []
    for gi, w in enumerate(POOL_WINDOWS):
        lanes = slice(gi * POOL_GROUP, (gi + 1) * POOL_GROUP)
        s = u[:, lanes]
        for j in range(POOL_BUF - (w - 1), POOL_BUF):
            s = s + pool_state_ref[j, :, lanes]
        pooled.append((s / float(w) - u[:, lanes]).astype(jnp.bfloat16))
    for pair in range(len(POOL_WINDOWS) // 2):
        lanes = slice(pair * POOL_PAIR, (pair + 1) * POOL_PAIR)
        both = jnp.concatenate(pooled[2 * pair:2 * pair + 2], axis=-1)
        pg = jnp.dot(both, pool_w_ref[pair], preferred_element_type=jnp.float32) * pool_scale_ref[:, lanes]
        mix_s_ref[:, GLA_WIDTH + pair * POOL_PAIR:GLA_WIDTH + (pair + 1) * POOL_PAIR] = pg.astype(mix_s_ref.dtype)
    for j in range(POOL_BUF - 1):
        pool_new_ref[j] = pool_state_ref[j + 1]
    pool_new_ref[POOL_BUF - 1] = u


def _sample_state_tasks(first_seq, s_ref, s_new_ref, dec_ref, ke_ref, qt_ref, v_ref, o_ref):
    rows = pl.ds(pl.multiple_of(first_seq, SUBLANES), SUBLANES)

    def update_state():
        v = v_ref[rows, :]
        v_rows = jnp.concatenate(
            [jnp.broadcast_to(v[:, None, h * HEAD_V:(h + 1) * HEAD_V], (SUBLANES, HEAD_K, HEAD_V))
             for h in range(N_HEADS)], axis=1)
        s_new_ref[...] = dec_ref[rows, :][:, :, None] * s_ref[...] + ke_ref[rows, :][:, :, None] * v_rows

    def query_state():
        lane = lax.broadcasted_iota(jnp.int32, (SUBLANES, KEY_WIDTH), 1)
        head = lax.broadcasted_iota(jnp.int32, (SUBLANES, KEY_WIDTH), 0)
        own_head = lane // HEAD_K == head
        q_rows = jnp.where(own_head[None], qt_ref[rows, :][:, None, :], 0.0).astype(jnp.bfloat16)
        o_inter = jnp.einsum("bhk,bkv->bhv", q_rows, s_ref[...].astype(jnp.bfloat16),
                             preferred_element_type=jnp.float32)
        o_ref[rows, :] = o_ref[rows, :] + jnp.concatenate([o_inter[:, h, :] for h in range(N_HEADS)], axis=-1)

    return [update_state, query_state]


def _mixer_kernel(*refs, layer, chained):
    (x_ref, x_next_ref, n1_ref, w_main_ref, w_code_ref, w_u_ref, w_gate_ref, b_gate_ref, gla_g_ref, pool_w_ref,
     pool_scale_ref, xs_ref, s_ref, pool_state_ref) = refs[:14]
    n1_ref, b_gate_ref, gla_g_ref, pool_scale_ref = (
        ref.at[pl.ds(layer, 1), :] for ref in (n1_ref, b_gate_ref, gla_g_ref, pool_scale_ref))
    (mix_ref, s_fin_ref, hist_out_ref, mix_s_ref, s_new_ref, pool_new_ref,
     st_ref, hist_ref, z_ref, b_ref, dec_ref, ke_ref, qt_ref, v_ref, g_ref, o_ref) = refs[-16:]
    tile = x_ref.shape[0]
    n_sub = tile // PROMPT_SUB
    t = pl.program_id(1)
    step = pl.program_id(0) * pl.num_programs(1) + t
    last_step = pl.num_programs(0) * pl.num_programs(1) - 1

    def project_tasks(load_x, slot):
        return _prompt_project_tasks(load_x, slot, n1_ref, w_main_ref, w_code_ref, w_u_ref, w_gate_ref, b_gate_ref,
                                     z_ref, b_ref)

    @pl.when(t == 0)
    def _():
        st_ref[...] = jnp.zeros_like(st_ref)
        hist_ref[...] = jnp.zeros_like(hist_ref)

    @pl.when(step == 0)
    def _():
        for task in project_tasks(lambda: x_ref[0:PROMPT_SUB], 0):
            task()
        _sample_project(xs_ref, n1_ref, w_main_ref, w_code_ref, w_u_ref, w_gate_ref, b_gate_ref, pool_w_ref,
                        pool_scale_ref, pool_state_ref, mix_s_ref, pool_new_ref, dec_ref, ke_ref, qt_ref, v_ref, g_ref,
                        o_ref)

    sample_tasks = _sample_state_tasks(step * SUBLANES, s_ref, s_new_ref, dec_ref, ke_ref, qt_ref, v_ref, o_ref)
    carry = {"st": st_ref[...], "hist": hist_ref[...]}
    for sb in range(n_sub):
        if sb + 1 < n_sub:
            load_next = lambda sb=sb: x_ref[(sb + 1) * PROMPT_SUB:(sb + 2) * PROMPT_SUB]
        else:
            load_next = lambda: x_next_ref[...]
        mix_tasks = []
        for row0 in range(0, PROMPT_SUB, GLA_PAIR):
            mix_tasks += _prompt_mix_tasks(sb % 2, row0, z_ref, b_ref, carry, t * tile + sb * PROMPT_SUB + row0,
                                           gla_g_ref, pool_w_ref, pool_scale_ref, mix_ref, sb * PROMPT_SUB + row0)
        if sample_tasks and sb >= 1:
            mix_tasks.insert(len(mix_tasks) // 2, sample_tasks.pop(0))
        _trace_interleaved(project_tasks(load_next, (sb + 1) % 2), mix_tasks)
    st_ref[...] = carry["st"]
    hist_ref[...] = carry["hist"]

    @pl.when(t == pl.num_programs(1) - 1)
    def _():
        s_fin_ref[...] = carry["st"]
        hist_out_ref[...] = carry["hist"]

    @pl.when(step == last_step)
    def _():
        mix_s_ref[:, 0:GLA_WIDTH] = _gla_finish(o_ref[...], g_ref[...], gla_g_ref[...]).astype(mix_s_ref.dtype)


def _mixer(layer, x, xs, s_state, pool_state, p, stacked):
    B, T, _ = x.shape
    n_seq = xs.shape[0]
    tile = PROMPT_TILE
    n_tiles = T // tile
    n_sub = tile // PROMPT_SUB
    assert n_sub % 2 == 0
    assert n_sub >= 3 and B * n_tiles * SUBLANES == n_seq
    assert PROMPT_SUB % GLA_PAIR == 0
    f32 = jnp.float32
    whole = lambda *shape: pl.BlockSpec((None,) + shape, lambda b, t: (layer,) + (0,) * len(shape))
    fixed = lambda *shape: pl.BlockSpec(shape, lambda b, t: (0,) * len(shape))
    once = pl.Buffered(1)
    whole_in = lambda *shape: pl.BlockSpec((None,) + shape, lambda b, t: (layer,) + (0,) * len(shape),
                                           pipeline_mode=once)
    fixed_in = lambda *shape: pl.BlockSpec(shape, lambda b, t: (0,) * len(shape), pipeline_mode=once)
    all_layers = lambda width: fixed_in(DEPTH, width)
    chained = stacked is not None
    operands = [x, x, p["n1"], p["w_main"], p["w_code"], p["w_u"], p["w_gate"], p["b_gate"], p["gla_g"], p["pool_w"],
                p["pool_scale"], xs, s_state, pool_state]

    def next_first_sub_block(b, t):
        n = jnp.minimum(b * n_tiles + t + 1, B * n_tiles - 1)
        return (n // n_tiles, (n % n_tiles) * n_sub, 0)

    sample_state_block = pl.BlockSpec((None, SUBLANES, KEY_WIDTH, HEAD_V), lambda b, t: (layer, b * n_tiles + t, 0, 0))
    in_specs = [
        pl.BlockSpec((None, tile, D_MODEL), lambda b, t: (b, t, 0)),
        pl.BlockSpec((None, PROMPT_SUB, D_MODEL), next_first_sub_block),
        all_layers(D_MODEL),
        whole_in(D_MODEL, COL_U),
        whole_in(D_MODEL, LANES),
        whole_in(D_MODEL, POOL_WIDTH),
        whole_in(LANES, KEY_WIDTH),
        all_layers(KEY_WIDTH),
        all_layers(HEAD_V),
        whole_in(len(POOL_WINDOWS) // 2, POOL_PAIR, POOL_PAIR),
        all_layers(POOL_WIDTH),
        fixed_in(n_seq, D_MODEL),
        sample_state_block,
        whole_in(POOL_BUF, n_seq, POOL_WIDTH),
    ]
    aliases = {}
    if chained:
        aliases = {len(operands): 1, len(operands) + 1: 2, len(operands) + 2: 4, len(operands) + 3: 5}
        operands += list(stacked)
        in_specs += [pl.BlockSpec(memory_space=pl.ANY)] * 4
    return pl.pallas_call(
        functools.partial(_mixer_kernel, layer=layer, chained=chained),
        grid=(B, n_tiles),
        in_specs=in_specs,
        out_specs=[
            pl.BlockSpec((None, tile, D_MODEL), lambda b, t: (b, t, 0)),
            pl.BlockSpec((None, None, KEY_WIDTH, HEAD_V), lambda b, t: (layer, b, 0, 0)),
            pl.BlockSpec((None, None, POOL_HIST, POOL_WIDTH), lambda b, t: (layer, b, 0, 0)),
            fixed(n_seq, D_MODEL),
            sample_state_block,
            whole(POOL_BUF, n_seq, POOL_WIDTH),
        ],
        out_shape=[
            jax.ShapeDtypeStruct((B, T, D_MODEL), jnp.bfloat16),
            jax.ShapeDtypeStruct((DEPTH, B, KEY_WIDTH, HEAD_V), f32),
            jax.ShapeDtypeStruct((DEPTH, B, POOL_HIST, POOL_WIDTH), f32),
            jax.ShapeDtypeStruct((n_seq, D_MODEL), jnp.bfloat16),
            jax.ShapeDtypeStruct((DEPTH, n_seq, KEY_WIDTH, HEAD_V), f32),
            jax.ShapeDtypeStruct((DEPTH, POOL_BUF, n_seq, POOL_WIDTH), f32),
        ],
        scratch_shapes=[
            pltpu.VMEM((KEY_WIDTH, HEAD_V), f32),
            pltpu.VMEM((POOL_HIST, POOL_WIDTH), f32),
            pltpu.VMEM((2, PROMPT_SUB, Z_WIDTH), f32),
            pltpu.VMEM((2, PROMPT_SUB, KEY_WIDTH), f32),
            pltpu.VMEM((n_seq, KEY_WIDTH), f32),
            pltpu.VMEM((n_seq, KEY_WIDTH), f32),
            pltpu.VMEM((n_seq, KEY_WIDTH), f32),
            pltpu.VMEM((n_seq, GLA_WIDTH), f32),
            pltpu.VMEM((n_seq, GLA_WIDTH), f32),
            pltpu.VMEM((n_seq, GLA_WIDTH), f32),
        ],
        input_output_aliases=aliases,
        compiler_params=pltpu.CompilerParams(
            dimension_semantics=("arbitrary", "arbitrary"), vmem_limit_bytes=VMEM_LIMIT),
        name=f"mixer_l{layer}",
    )(*operands)


N_FF_CHUNKS = D_FF // FF_CHUNK
N_WEIGHT_CHUNKS = 1 + 2 * N_FF_CHUNKS
STAGE_SLOTS = 3
assert FF_CHUNK == D_MODEL


def _channel_kernel(x_ref, mix_ref, xs_ref, mix_s_ref, n2_ref, final_g_ref, w_out_hbm, w_up_hbm, w_down_hbm,
                    y_ref, ys_ref, w_out_ref, w_up_ref, w_down_ref, stage_ref, sem, *, layer, final):
    step = pl.program_id(0)
    n2_ref = n2_ref.at[pl.ds(layer, 1), :]

    def chunk(k):
        if k == 0:
            return w_out_hbm.at[layer], w_out_ref
        j, down = divmod(k - 1, 2)
        span = pl.ds(j * FF_CHUNK, FF_CHUNK)
        if down:
            return w_down_hbm.at[layer, span, :], w_down_ref.at[span, :]
        return w_up_hbm.at[layer, :, span], w_up_ref.at[:, span]

    def chunk_copy(k):
        return pltpu.make_async_copy(chunk(k)[0], stage_ref.at[k % STAGE_SLOTS], sem.at[k % STAGE_SLOTS])

    def fetch(k):
        chunk_copy(k).wait()
        chunk(k)[1][...] = stage_ref[k % STAGE_SLOTS].astype(jnp.bfloat16)
        if k + STAGE_SLOTS < N_WEIGHT_CHUNKS:
            chunk_copy(k + STAGE_SLOTS).start()

    def channel_mix(x, mix, fetch_weights):
        def before_matmul(m):
            if fetch_weights and m + 1 < N_WEIGHT_CHUNKS:
                fetch(m + 1)

        if fetch_weights:
            fetch(0)
        before_matmul(0)
        x1 = x + jnp.dot(mix, w_out_ref[...], preferred_element_type=jnp.float32)
        h2 = _rmsnorm(x1, n2_ref[...]).astype(jnp.bfloat16)
        acc = x1
        for j in range(N_FF_CHUNKS):
            cols = slice(j * FF_CHUNK, (j + 1) * FF_CHUNK)
            before_matmul(1 + 2 * j)
            hid = jnp.dot(h2, w_up_ref[:, cols], preferred_element_type=jnp.float32)
            act = jnp.square(jnp.maximum(hid, 0.0)).astype(jnp.bfloat16)
            before_matmul(2 + 2 * j)
            acc = acc + jnp.dot(act, w_down_ref[cols, :], preferred_element_type=jnp.float32)
        if final:
            acc = _rmsnorm(acc, final_g_ref[...])
        return acc

    @pl.when(step == 0)
    def _():
        for k in range(STAGE_SLOTS):
            chunk_copy(k).start()
        y_ref[...] = channel_mix(x_ref[...], mix_ref[...], True)

    @pl.when(step > 0)
    def _():
        y_ref[...] = channel_mix(x_ref[...], mix_ref[...], False)

    @pl.when(step == pl.num_programs(0) - 1)
    def _():
        ys_ref[...] = channel_mix(xs_ref[...], mix_s_ref[...], False)


def _channel_mixer(layer, x, mix, xs, mix_s, p, final_g):
    rows = x.shape[0]
    n_seq = xs.shape[0]
    tile = CHANNEL_TILE
    assert rows // tile > 1
    bf16 = jnp.bfloat16
    return pl.pallas_call(
        functools.partial(_channel_kernel, layer=layer, final=layer == DEPTH - 1),
        grid=(rows // tile,),
        in_specs=[
            pl.BlockSpec((tile, D_MODEL), lambda i: (i, 0)),
            pl.BlockSpec((tile, D_MODEL), lambda i: (i, 0)),
            pl.BlockSpec((n_seq, D_MODEL), lambda i: (0, 0)),
            pl.BlockSpec((n_seq, D_MODEL), lambda i: (0, 0)),
            pl.BlockSpec((DEPTH, D_MODEL), lambda i: (0, 0)),
            pl.BlockSpec((1, D_MODEL), lambda i: (0, 0)),
            pl.BlockSpec(memory_space=pl.ANY),
            pl.BlockSpec(memory_space=pl.ANY),
            pl.BlockSpec(memory_space=pl.ANY),
        ],
        out_specs=[
            pl.BlockSpec((tile, D_MODEL), lambda i: (i, 0)),
            pl.BlockSpec((n_seq, D_MODEL), lambda i: (0, 0)),
        ],
        out_shape=[
            jax.ShapeDtypeStruct((rows, D_MODEL), jnp.float32),
            jax.ShapeDtypeStruct((n_seq, D_MODEL), jnp.float32),
        ],
        scratch_shapes=[
            pltpu.VMEM((D_MODEL, D_MODEL), bf16),
            pltpu.VMEM((D_MODEL, D_FF), bf16),
            pltpu.VMEM((D_FF, D_MODEL), bf16),
            pltpu.VMEM((STAGE_SLOTS, D_MODEL, D_MODEL), jnp.float32),
            pltpu.SemaphoreType.DMA((STAGE_SLOTS,)),
        ],
        compiler_params=pltpu.CompilerParams(
            dimension_semantics=("arbitrary",), vmem_limit_bytes=VMEM_LIMIT),
        name=f"channel_mixer_l{layer}",
    )(x, mix, xs, mix_s, p["n2"], final_g, p["w_out"], p["w_up"], p["w_down"])


def kernel(x_prompt, x_sample, state_gla, state_pool, norm1_g, w_in, w_gate, b_gate, gla_norm_g, pool_w,
           pool_scale, w_out, norm2_g, w_up, w_down, final_g):
    B, T, _ = x_prompt.shape
    n_seq = x_sample.shape[0]
    bf16 = jnp.bfloat16

    w_code = jnp.pad(w_in[:, :, W_IN_GATE:W_IN_U], ((0, 0), (0, 0), (0, LANES - GATE_RANK)))
    pw = pool_w.reshape(DEPTH, len(POOL_WINDOWS) // 2, 2, POOL_GROUP, POOL_GROUP)
    zero = jnp.zeros_like(pw[:, :, 0])
    pool_pairs = jnp.concatenate([jnp.concatenate([pw[:, :, 0], zero], axis=-1),
                                  jnp.concatenate([zero, pw[:, :, 1]], axis=-1)], axis=-2)
    params = {
        "n1": norm1_g,
        "w_main": w_in[:, :, :W_IN_GATE].astype(bf16),
        "w_code": w_code.astype(bf16),
        "w_u": w_in[:, :, W_IN_U:].astype(bf16),
        "w_gate": jnp.pad(w_gate, ((0, 0), (0, LANES - GATE_RANK), (0, 0))).astype(bf16),
        "b_gate": b_gate,
        "gla_g": gla_norm_g,
        "pool_w": pool_pairs.astype(bf16),
        "pool_scale": pool_scale,
        "w_out": w_out,
        "n2": norm2_g,
        "w_up": w_up,
        "w_down": w_down,
    }
    final_g2 = final_g.reshape(1, D_MODEL)
    s_state = state_gla.reshape(DEPTH, n_seq, KEY_WIDTH, HEAD_V)
    pool_rows = jnp.swapaxes(state_pool, 1, 2)

    xp = x_prompt
    xs = x_sample.reshape(n_seq, D_MODEL)
    states = None
    for layer in range(DEPTH):
        mix_p, gla_p, hist_p, mix_s, gla_s, pool_s = _mixer(layer, xp, xs, s_state, pool_rows, params, states)
        states = (gla_p, hist_p, gla_s, pool_s)
        xp, xs = _channel_mixer(layer, xp.reshape(B * T, D_MODEL), mix_p.reshape(B * T, D_MODEL), xs, mix_s,
                                params, final_g2)
        xp = xp.reshape(B, T, D_MODEL)
    return (xp, xs.reshape(n_seq, 1, D_MODEL),
            gla_p.reshape(DEPTH, B, N_HEADS, HEAD_K, HEAD_V), hist_p[:, :, POOL_HIST - POOL_BUF:],
            gla_s.reshape(DEPTH, n_seq, N_HEADS, HEAD_K, HEAD_V), jnp.swapaxes(pool_s, 1, 2))
```

```python
import functools

import jax
import jax.numpy as jnp
from jax import lax
from jax.experimental import pallas as pl
from jax.experimental.pallas import tpu as pltpu

D_MODEL = 1024
DEPTH = 4
N_HEADS = 4
HEAD_K = 64
HEAD_V = 128
KEY_WIDTH = N_HEADS * HEAD_K
GLA_WIDTH = N_HEADS * HEAD_V
GATE_RANK = 16
GATE_TEMP = 16.0
CHUNK = 64
POOL_WIDTH = 512
POOL_WINDOWS = (2, 4, 8, 16)
POOL_GROUP = 128
POOL_PAIR = 2 * POOL_GROUP
POOL_BUF = 15
POOL_HIST = 16
D_FF = 4 * D_MODEL
FF_CHUNK = 1024
EPS = 1e-6
LANES = 128
SUBLANES = 8

COL_Q = 0
COL_K = KEY_WIDTH
COL_V = 2 * KEY_WIDTH
COL_G = COL_V + GLA_WIDTH
COL_U = COL_G + GLA_WIDTH
Z_WIDTH = COL_U + POOL_WIDTH
W_IN_GATE = COL_U
W_IN_U = COL_U + GATE_RANK

PROMPT_TILE = 1024
PROMPT_SUB = 256
GLA_PAIR = 2 * CHUNK
PROJ_PIECE = 256
CHANNEL_TILE = 512
VMEM_LIMIT = 56 * 1024 * 1024


def _rmsnorm(x, g):
    return x * lax.rsqrt(jnp.mean(x * x, axis=-1, keepdims=True) + EPS) * g


def _log_sigmoid(x):
    return jnp.minimum(x, 0.0) - jnp.log(1.0 + jnp.exp(-jnp.abs(x)))


def _bf16_round(x):
    return x.astype(jnp.bfloat16).astype(jnp.float32)


def _head_lane_mask(rows):
    lane = lax.broadcasted_iota(jnp.int32, (rows, KEY_WIDTH), 1)
    return [(lane >= h * HEAD_K) & (lane < (h + 1) * HEAD_K) for h in range(N_HEADS)]


def _gate_log_decay(a_low, w_gate_ref, b_gate_ref):
    pre = jnp.dot(a_low.astype(jnp.bfloat16), w_gate_ref[...], preferred_element_type=jnp.float32) + b_gate_ref[...]
    return _log_sigmoid(pre) / GATE_TEMP


def _project(x, n1_ref, w_main_ref, w_code_ref, w_u_ref, w_gate_ref, b_gate_ref):
    h = _rmsnorm(x, n1_ref[...]).astype(jnp.bfloat16)
    qkvg = jnp.dot(h, w_main_ref[...], preferred_element_type=jnp.float32)
    a_low = jnp.dot(h, w_code_ref[...], preferred_element_type=jnp.float32)
    u = jnp.dot(h, w_u_ref[...], preferred_element_type=jnp.float32)
    return qkvg, u, _gate_log_decay(a_low, w_gate_ref, b_gate_ref)


def _gla_finish(o, g, gla_g):
    outs = []
    for h in range(N_HEADS):
        sl = slice(h * HEAD_V, (h + 1) * HEAD_V)
        outs.append(_rmsnorm(o[:, sl], gla_g) * (g[:, sl] * jax.nn.sigmoid(g[:, sl])))
    return jnp.concatenate(outs, axis=-1)


def _trace_interleaved(a, b):
    i = j = 0
    while i < len(a) or j < len(b):
        if j >= len(b) or (i < len(a) and i * len(b) <= j * len(a)):
            a[i]()
            i += 1
        else:
            b[j]()
            j += 1


def _prompt_project_tasks(load_x, slot, n1_ref, w_main_ref, w_code_ref, w_u_ref, w_gate_ref, b_gate_ref,
                          z_ref, b_ref):
    v = {}

    def norm():
        v["h"] = _rmsnorm(load_x(), n1_ref[...]).astype(jnp.bfloat16)

    def piece(lo, hi):
        w_ref, first = (w_main_ref, 0) if hi <= COL_U else (w_u_ref, COL_U)

        def run():
            z_ref[slot, :, lo:hi] = jnp.dot(v["h"], w_ref[:, lo - first:hi - first],
                                            preferred_element_type=jnp.float32)
        return run

    def gate_code():
        v["a_low"] = jnp.dot(v["h"], w_code_ref[...], preferred_element_type=jnp.float32)

    def gate_pre():
        v["pre"] = (jnp.dot(v["a_low"].astype(jnp.bfloat16), w_gate_ref[...], preferred_element_type=jnp.float32)
                    + b_gate_ref[...])

    def log_decay():
        v["b"] = _log_sigmoid(v["pre"]) / GATE_TEMP

    def scan(shifts, last):
        def run():
            b = v["b"]
            row = lax.broadcasted_iota(jnp.int32, b.shape, 0) % CHUNK
            for shift in shifts:
                b = b + jnp.where(row >= shift, pltpu.roll(b, shift, axis=0), 0.0)
            v["b"] = b
            if last:
                b_ref[slot] = b
        return run

    half = PROJ_PIECE
    return [
        norm, gate_code,
        piece(COL_Q, COL_K), gate_pre,
        piece(COL_K, COL_V), log_decay,
        piece(COL_V, COL_V + half), scan((1, 2), False),
        piece(COL_V + half, COL_G), scan((4, 8), False),
        piece(COL_G, COL_G + half), scan((16, 32), True),
        piece(COL_G + half, COL_U), piece(COL_U, COL_U + half), piece(COL_U + half, Z_WIDTH),
    ]


def _prompt_mix_tasks(slot, row0, z_ref, b_ref, carry, first_pos, gla_g_ref, pool_w_ref, pool_scale_ref, mix_ref,
                      base):
    head_mask = _head_lane_mask(CHUNK)
    scale = HEAD_K ** -0.5
    sub = GLA_PAIR
    n_chunks = sub // CHUNK
    part = slice(row0, row0 + sub)
    shared = {}

    def transpose_keys():
        b = b_ref[slot, part, :]
        k = z_ref[slot, part, COL_K:COL_V]
        last = [b[(c + 1) * CHUNK - 1:(c + 1) * CHUNK] for c in range(n_chunks)]
        b_last = jnp.concatenate([jnp.broadcast_to(row, (CHUNK, KEY_WIDTH)) for row in last], axis=0)
        shared["kt_t"] = (k * jnp.exp(-b)).T.astype(jnp.bfloat16)
        shared["ke_t"] = (k * jnp.exp(b_last - b)).T.astype(jnp.bfloat16)
        shared["v"] = z_ref[slot, part, COL_V:COL_G].astype(jnp.bfloat16)
        decay = jnp.exp(jnp.concatenate(last + [jnp.zeros((SUBLANES - n_chunks, KEY_WIDTH), jnp.float32)], axis=0))
        shared["decay"] = jnp.broadcast_to(decay[:, :, None], (SUBLANES, KEY_WIDTH, HEAD_V))

    def chunk_tasks(c):
        rows = slice(row0 + c * CHUNK, row0 + (c + 1) * CHUNK)
        out_rows = slice(base + c * CHUNK, base + (c + 1) * CHUNK)
        token = lax.broadcasted_iota(jnp.int32, (CHUNK, sub), 1) - c * CHUNK
        causal = (token >= 0) & (token <= lax.broadcasted_iota(jnp.int32, (CHUNK, sub), 0))
        in_chunk = (lax.broadcasted_iota(jnp.int32, (KEY_WIDTH, sub), 1) // CHUNK) == c
        v = {}

        def prepare():
            qt = z_ref[slot, rows, COL_Q:COL_K] * jnp.exp(b_ref[slot, rows, :]) * scale
            v["q_stack"] = jnp.concatenate([jnp.where(m, qt, 0.0) for m in head_mask], axis=0).astype(jnp.bfloat16)

        def query_product():
            rhs = jnp.concatenate([carry["st"].astype(jnp.bfloat16), shared["kt_t"]], axis=1)
            v["r"] = jnp.dot(v["q_stack"], rhs, preferred_element_type=jnp.float32)

        def update_state():
            carry["st"] = shared["decay"][c] * carry["st"] + shared["upd"][c]

        def masked_scores():
            r = v["r"]
            v["att"] = [jnp.where(causal, r[h * CHUNK:(h + 1) * CHUNK, HEAD_V:HEAD_V + sub], 0.0).astype(jnp.bfloat16)
                        for h in range(N_HEADS)]

        def add_inter(o_intra):
            r = v["r"]
            v["o"] = jnp.concatenate([o_intra[h] + r[h * CHUNK:(h + 1) * CHUNK, 0:HEAD_V] for h in range(N_HEADS)],
                                     axis=-1)

        def finish():
            g_c = z_ref[slot, rows, COL_G:COL_U]
            mix_ref[out_rows, 0:GLA_WIDTH] = _gla_finish(v["o"], g_c, gla_g_ref[...]).astype(mix_ref.dtype)

        return dict(prepare=prepare, query_product=query_product, update_state=update_state,
                    masked_scores=masked_scores, add_inter=add_inter, finish=finish, in_chunk=in_chunk, v=v)

    chunks = [chunk_tasks(c) for c in range(n_chunks)]

    def update_products():
        ke_t = shared["ke_t"]
        zero = jnp.zeros_like(ke_t)
        masked = [jnp.where(ch["in_chunk"], ke_t, zero) for ch in chunks]
        per_head = []
        for h in range(N_HEADS):
            lhs = jnp.concatenate([m[h * HEAD_K:(h + 1) * HEAD_K] for m in masked], axis=0)
            per_head.append(jnp.dot(lhs, shared["v"][:, h * HEAD_V:(h + 1) * HEAD_V],
                                    preferred_element_type=jnp.float32))
        shared["upd"] = [jnp.concatenate([p[c * HEAD_K:(c + 1) * HEAD_K] for p in per_head], axis=0)
                         for c in range(n_chunks)]

    def intra_products():
        per_head = []
        for h in range(N_HEADS):
            lhs = jnp.concatenate([ch["v"]["att"][h] for ch in chunks], axis=0)
            per_head.append(jnp.dot(lhs, shared["v"][:, h * HEAD_V:(h + 1) * HEAD_V],
                                    preferred_element_type=jnp.float32))
        for c, ch in enumerate(chunks):
            ch["add_inter"]([p[c * CHUNK:(c + 1) * CHUNK] for p in per_head])

    pooled = {}

    def window_mean_task(gi, w):
        lanes = slice(gi * POOL_GROUP, (gi + 1) * POOL_GROUP)

        def window_mean():
            u = z_ref[slot, part, COL_U + gi * POOL_GROUP:COL_U + (gi + 1) * POOL_GROUP]
            s = jnp.concatenate([carry["hist"][:, lanes], u], axis=0)
            shift = 1
            while shift < w:
                s = s + pltpu.roll(s, shift, axis=0)
                shift *= 2
            seen = first_pos + 1 + lax.broadcasted_iota(jnp.int32, (sub, POOL_GROUP), 0)
            if "inv_seen" not in pooled:
                pooled["inv_seen"] = 1.0 / seen.astype(jnp.float32)
            inv_count = jnp.where(seen < w, pooled["inv_seen"], 1.0 / w)
            pooled[gi] = (s[POOL_HIST:] * inv_count - u).astype(jnp.bfloat16)

        return window_mean

    def group_map_task(pair):
        lanes = slice(pair * POOL_PAIR, (pair + 1) * POOL_PAIR)

        def group_map():
            both = jnp.concatenate([pooled[2 * pair], pooled[2 * pair + 1]], axis=-1)
            pg = jnp.dot(both, pool_w_ref[pair], preferred_element_type=jnp.float32)
            pg = pg * pool_scale_ref[:, lanes]
            mix_ref[base:base + sub, GLA_WIDTH + pair * POOL_PAIR:GLA_WIDTH + (pair + 1) * POOL_PAIR] = (
                pg.astype(mix_ref.dtype))

        return group_map

    def keep_history():
        carry["hist"] = z_ref[slot, row0 + sub - POOL_HIST:row0 + sub, COL_U:Z_WIDTH]

    assert n_chunks == 2
    c0, c1 = chunks
    means = [window_mean_task(gi, w) for gi, w in enumerate(POOL_WINDOWS)]
    maps = [group_map_task(pair) for pair in range(len(POOL_WINDOWS) // 2)]
    return [
        transpose_keys, c0["prepare"], c1["prepare"], update_products, c0["query_product"], means[0],
        c0["update_state"], c1["query_product"], c0["masked_scores"], c1["update_state"], means[1],
        c1["masked_scores"], intra_products, maps[0], means[2], c0["finish"], means[3], c1["finish"], maps[1],
        keep_history,
    ]


def _sample_project(xs_ref, n1_ref, w_main_ref, w_code_ref, w_u_ref, w_gate_ref, b_gate_ref, pool_w_ref, pool_scale_ref,
                    pool_state_ref, mix_s_ref, pool_new_ref, dec_ref, ke_ref, qt_ref, v_ref, g_ref, o_ref):
    n_seq = xs_ref.shape[0]
    qkvg, u, log_a = _project(xs_ref[...], n1_ref, w_main_ref, w_code_ref, w_u_ref, w_gate_ref, b_gate_ref)
    q = qkvg[:, COL_Q:COL_K]
    k = qkvg[:, COL_K:COL_V]
    v = _bf16_round(qkvg[:, COL_V:COL_G])
    qt = _bf16_round(q * jnp.exp(log_a) * (HEAD_K ** -0.5))
    kt = _bf16_round(k * jnp.exp(-log_a))
    ke = _bf16_round(k * jnp.exp(log_a - log_a))
    prod = qt * kt
    o_intra = []
    for h, m in enumerate(_head_lane_mask(n_seq)):
        att = _bf16_round(jnp.sum(jnp.where(m, prod, 0.0), axis=-1, keepdims=True))
        o_intra.append(att * v[:, h * HEAD_V:(h + 1) * HEAD_V])
    o_ref[...] = jnp.concatenate(o_intra, axis=-1)
    dec_ref[...] = jnp.exp(log_a)
    ke_ref[...] = ke
    qt_ref[...] = qt
    v_ref[...] = v
    g_ref[...] = qkvg[:, COL_G:COL_U]

    pooled = []
    for gi, w in enumerate(POOL_WINDOWS):
        lanes = slice(gi * POOL_GROUP, (gi + 1) * POOL_GROUP)
        s = u[:, lanes]
        for j in range(POOL_BUF - (w - 1), POOL_BUF):
            s = s + pool_state_ref[j, :, lanes]
        pooled.append((s / float(w) - u[:, lanes]).astype(jnp.bfloat16))
    for pair in range(len(POOL_WINDOWS) // 2):
        lanes = slice(pair * POOL_PAIR, (pair + 1) * POOL_PAIR)
        both = jnp.concatenate(pooled[2 * pair:2 * pair + 2], axis=-1)
        pg = jnp.dot(both, pool_w_ref[pair], preferred_element_type=jnp.float32) * pool_scale_ref[:, lanes]
        mix_s_ref[:, GLA_WIDTH + pair * POOL_PAIR:GLA_WIDTH + (pair + 1) * POOL_PAIR] = pg.astype(mix_s_ref.dtype)
    for j in range(POOL_BUF - 1):
        pool_new_ref[j] = pool_state_ref[j + 1]
    pool_new_ref[POOL_BUF - 1] = u


def _sample_state_tasks(first_seq, s_ref, s_new_ref, dec_ref, ke_ref, qt_ref, v_ref, o_ref):
    rows = pl.ds(pl.multiple_of(first_seq, SUBLANES), SUBLANES)

    def update_state():
        v = v_ref[rows, :]
        v_rows = jnp.concatenate(
            [jnp.broadcast_to(v[:, None, h * HEAD_V:(h + 1) * HEAD_V], (SUBLANES, HEAD_K, HEAD_V))
             for h in range(N_HEADS)], axis=1)
        s_new_ref[...] = dec_ref[rows, :][:, :, None] * s_ref[...] + ke_ref[rows, :][:, :, None] * v_rows

    def query_state():
        lane = lax.broadcasted_iota(jnp.int32, (SUBLANES, KEY_WIDTH), 1)
        head = lax.broadcasted_iota(jnp.int32, (SUBLANES, KEY_WIDTH), 0)
        own_head = lane // HEAD_K == head
        q_rows = jnp.where(own_head[None], qt_ref[rows, :][:, None, :], 0.0).astype(jnp.bfloat16)
        o_inter = jnp.einsum("bhk,bkv->bhv", q_rows, s_ref[...].astype(jnp.bfloat16),
                             preferred_element_type=jnp.float32)
        o_ref[rows, :] = o_ref[rows, :] + jnp.concatenate([o_inter[:, h, :] for h in range(N_HEADS)], axis=-1)

    return [update_state, query_state]


def _mixer_kernel(*refs, layer, chained):
    (x_ref, x_next_ref, n1_ref, w_main_ref, w_code_ref, w_u_ref, w_gate_ref, b_gate_ref, gla_g_ref, pool_w_ref,
     pool_scale_ref, xs_ref, s_ref, pool_state_ref) = refs[:14]
    n1_ref, b_gate_ref, gla_g_ref, pool_scale_ref = (
        ref.at[pl.ds(layer, 1), :] for ref in (n1_ref, b_gate_ref, gla_g_ref, pool_scale_ref))
    (mix_ref, s_fin_ref, hist_out_ref, mix_s_ref, s_new_ref, pool_new_ref,
     st_ref, hist_ref, z_ref, b_ref, dec_ref, ke_ref, qt_ref, v_ref, g_ref, o_ref) = refs[-16:]
    tile = x_ref.shape[0]
    n_sub = tile // PROMPT_SUB
    t = pl.program_id(1)
    step = pl.program_id(0) * pl.num_programs(1) + t
    last_step = pl.num_programs(0) * pl.num_programs(1) - 1

    def project_tasks(load_x, slot):
        return _prompt_project_tasks(load_x, slot, n1_ref, w_main_ref, w_code_ref, w_u_ref, w_gate_ref, b_gate_ref,
                                     z_ref, b_ref)

    @pl.when(t == 0)
    def _():
        st_ref[...] = jnp.zeros_like(st_ref)
        hist_ref[...] = jnp.zeros_like(hist_ref)

    @pl.when(step == 0)
    def _():
        for task in project_tasks(lambda: x_ref[0:PROMPT_SUB], 0):
            task()
        _sample_project(xs_ref, n1_ref, w_main_ref, w_code_ref, w_u_ref, w_gate_ref, b_gate_ref, pool_w_ref,
                        pool_scale_ref, pool_state_ref, mix_s_ref, pool_new_ref, dec_ref, ke_ref, qt_ref, v_ref, g_ref,
                        o_ref)

    sample_tasks = _sample_state_tasks(step * SUBLANES, s_ref, s_new_ref, dec_ref, ke_ref, qt_ref, v_ref, o_ref)
    carry = {"st": st_ref[...], "hist": hist_ref[...]}
    for sb in range(n_sub):
        if sb + 1 < n_sub:
            load_next = lambda sb=sb: x_ref[(sb + 1) * PROMPT_SUB:(sb + 2) * PROMPT_SUB]
        else:
            load_next = lambda: x_next_ref[...]
        mix_tasks = []
        for row0 in range(0, PROMPT_SUB, GLA_PAIR):
            mix_tasks += _prompt_mix_tasks(sb % 2, row0, z_ref, b_ref, carry, t * tile + sb * PROMPT_SUB + row0,
                                           gla_g_ref, pool_w_ref, pool_scale_ref, mix_ref, sb * PROMPT_SUB + row0)
        if sample_tasks and sb >= 1:
            mix_tasks.insert(len(mix_tasks) // 2, sample_tasks.pop(0))
        _trace_interleaved(project_tasks(load_next, (sb + 1) % 2), mix_tasks)
    st_ref[...] = carry["st"]
    hist_ref[...] = carry["hist"]

    @pl.when(t == pl.num_programs(1) - 1)
    def _():
        s_fin_ref[...] = carry["st"]
        hist_out_ref[...] = carry["hist"]

    @pl.when(step == last_step)
    def _():
        mix_s_ref[:, 0:GLA_WIDTH] = _gla_finish(o_ref[...], g_ref[...], gla_g_ref[...]).astype(mix_s_ref.dtype)


def _mixer(layer, x, xs, s_state, pool_state, p, stacked):
    B, T, _ = x.shape
    n_seq = xs.shape[0]
    tile = PROMPT_TILE
    n_tiles = T // tile
    n_sub = tile // PROMPT_SUB
    assert n_sub % 2 == 0
    assert n_sub >= 3 and B * n_tiles * SUBLANES == n_seq
    assert PROMPT_SUB % GLA_PAIR == 0
    f32 = jnp.float32
    whole = lambda *shape: pl.BlockSpec((None,) + shape, lambda b, t: (layer,) + (0,) * len(shape))
    fixed = lambda *shape: pl.BlockSpec(shape, lambda b, t: (0,) * len(shape))
    all_layers = lambda width: fixed(DEPTH, width)
    chained = stacked is not None
    operands = [x, x, p["n1"], p["w_main"], p["w_code"], p["w_u"], p["w_gate"], p["b_gate"], p["gla_g"], p["pool_w"],
                p["pool_scale"], xs, s_state, pool_state]

    def next_first_sub_block(b, t):
        n = jnp.minimum(b * n_tiles + t + 1, B * n_tiles - 1)
        return (n // n_tiles, (n % n_tiles) * n_sub, 0)

    sample_state_block = pl.BlockSpec((None, SUBLANES, KEY_WIDTH, HEAD_V), lambda b, t: (layer, b * n_tiles + t, 0, 0))
    in_specs = [
        pl.BlockSpec((None, tile, D_MODEL), lambda b, t: (b, t, 0)),
        pl.BlockSpec((None, PROMPT_SUB, D_MODEL), next_first_sub_block),
        all_layers(D_MODEL),
        whole(D_MODEL, COL_U),
        whole(D_MODEL, LANES),
        whole(D_MODEL, POOL_WIDTH),
        whole(LANES, KEY_WIDTH),
        all_layers(KEY_WIDTH),
        all_layers(HEAD_V),
        whole(len(POOL_WINDOWS) // 2, POOL_PAIR, POOL_PAIR),
        all_layers(POOL_WIDTH),
        fixed(n_seq, D_MODEL),
        sample_state_block,
        whole(POOL_BUF, n_seq, POOL_WIDTH),
    ]
    aliases = {}
    if chained:
        aliases = {len(operands): 1, len(operands) + 1: 2, len(operands) + 2: 4, len(operands) + 3: 5}
        operands += list(stacked)
        in_specs += [pl.BlockSpec(memory_space=pl.ANY)] * 4
    return pl.pallas_call(
        functools.partial(_mixer_kernel, layer=layer, chained=chained),
        grid=(B, n_tiles),
        in_specs=in_specs,
        out_specs=[
            pl.BlockSpec((None, tile, D_MODEL), lambda b, t: (b, t, 0)),
            pl.BlockSpec((None, None, KEY_WIDTH, HEAD_V), lambda b, t: (layer, b, 0, 0)),
            pl.BlockSpec((None, None, POOL_HIST, POOL_WIDTH), lambda b, t: (layer, b, 0, 0)),
            fixed(n_seq, D_MODEL),
            sample_state_block,
            whole(POOL_BUF, n_seq, POOL_WIDTH),
        ],
        out_shape=[
            jax.ShapeDtypeStruct((B, T, D_MODEL), jnp.bfloat16),
            jax.ShapeDtypeStruct((DEPTH, B, KEY_WIDTH, HEAD_V), f32),
            jax.ShapeDtypeStruct((DEPTH, B, POOL_HIST, POOL_WIDTH), f32),
            jax.ShapeDtypeStruct((n_seq, D_MODEL), jnp.bfloat16),
            jax.ShapeDtypeStruct((DEPTH, n_seq, KEY_WIDTH, HEAD_V), f32),
            jax.ShapeDtypeStruct((DEPTH, POOL_BUF, n_seq, POOL_WIDTH), f32),
        ],
        scratch_shapes=[
            pltpu.VMEM((KEY_WIDTH, HEAD_V), f32),
            pltpu.VMEM((POOL_HIST, POOL_WIDTH), f32),
            pltpu.VMEM((2, PROMPT_SUB, Z_WIDTH), f32),
            pltpu.VMEM((2, PROMPT_SUB, KEY_WIDTH), f32),
            pltpu.VMEM((n_seq, KEY_WIDTH), f32),
            pltpu.VMEM((n_seq, KEY_WIDTH), f32),
            pltpu.VMEM((n_seq, KEY_WIDTH), f32),
            pltpu.VMEM((n_seq, GLA_WIDTH), f32),
            pltpu.VMEM((n_seq, GLA_WIDTH), f32),
            pltpu.VMEM((n_seq, GLA_WIDTH), f32),
        ],
        input_output_aliases=aliases,
        compiler_params=pltpu.CompilerParams(
            dimension_semantics=("arbitrary", "arbitrary"), vmem_limit_bytes=VMEM_LIMIT,
            allow_input_fusion=[3 <= i <= 5 for i in range(len(operands))]),
        name=f"mixer_l{layer}",
    )(*operands)


N_FF_CHUNKS = D_FF // FF_CHUNK
N_WEIGHT_CHUNKS = 1 + 2 * N_FF_CHUNKS
STAGE_SLOTS = 3
assert FF_CHUNK == D_MODEL


def _channel_kernel(x_ref, mix_ref, xs_ref, mix_s_ref, n2_ref, final_g_ref, w_out_hbm, w_up_hbm, w_down_hbm,
                    y_ref, ys_ref, w_out_ref, w_up_ref, w_down_ref, stage_ref, sem, *, layer, final):
    step = pl.program_id(0)
    n2_ref = n2_ref.at[pl.ds(layer, 1), :]

    def chunk(k):
        if k == 0:
            return w_out_hbm.at[layer], w_out_ref
        j, down = divmod(k - 1, 2)
        span = pl.ds(j * FF_CHUNK, FF_CHUNK)
        if down:
            return w_down_hbm.at[layer, span, :], w_down_ref.at[span, :]
        return w_up_hbm.at[layer, :, span], w_up_ref.at[:, span]

    def chunk_copy(k):
        return pltpu.make_async_copy(chunk(k)[0], stage_ref.at[k % STAGE_SLOTS], sem.at[k % STAGE_SLOTS])

    def fetch(k):
        chunk_copy(k).wait()
        chunk(k)[1][...] = stage_ref[k % STAGE_SLOTS].astype(jnp.bfloat16)
        if k + STAGE_SLOTS < N_WEIGHT_CHUNKS:
            chunk_copy(k + STAGE_SLOTS).start()

    def channel_mix(x, mix, fetch_weights):
        def before_matmul(m):
            if fetch_weights and m + 1 < N_WEIGHT_CHUNKS:
                fetch(m + 1)

        if fetch_weights:
            fetch(0)
        before_matmul(0)
        x1 = x + jnp.dot(mix, w_out_ref[...], preferred_element_type=jnp.float32)
        h2 = _rmsnorm(x1, n2_ref[...]).astype(jnp.bfloat16)
        acc = x1
        for j in range(N_FF_CHUNKS):
            cols = slice(j * FF_CHUNK, (j + 1) * FF_CHUNK)
            before_matmul(1 + 2 * j)
            hid = jnp.dot(h2, w_up_ref[:, cols], preferred_element_type=jnp.float32)
            act = jnp.square(jnp.maximum(hid, 0.0)).astype(jnp.bfloat16)
            before_matmul(2 + 2 * j)
            acc = acc + jnp.dot(act, w_down_ref[cols, :], preferred_element_type=jnp.float32)
        if final:
            acc = _rmsnorm(acc, final_g_ref[...])
        return acc

    @pl.when(step == 0)
    def _():
        for k in range(STAGE_SLOTS):
            chunk_copy(k).start()
        y_ref[...] = channel_mix(x_ref[...], mix_ref[...], True)

    @pl.when(step > 0)
    def _():
        y_ref[...] = channel_mix(x_ref[...], mix_ref[...], False)

    @pl.when(step == pl.num_programs(0) - 1)
    def _():
        ys_ref[...] = channel_mix(xs_ref[...], mix_s_ref[...], False)


def _channel_mixer(layer, x, mix, xs, mix_s, p, final_g):
    rows = x.shape[0]
    n_seq = xs.shape[0]
    tile = CHANNEL_TILE
    assert rows // tile > 1
    bf16 = jnp.bfloat16
    return pl.pallas_call(
        functools.partial(_channel_kernel, layer=layer, final=layer == DEPTH - 1),
        grid=(rows // tile,),
        in_specs=[
            pl.BlockSpec((tile, D_MODEL), lambda i: (i, 0)),
            pl.BlockSpec((tile, D_MODEL), lambda i: (i, 0)),
            pl.BlockSpec((n_seq, D_MODEL), lambda i: (0, 0)),
            pl.BlockSpec((n_seq, D_MODEL), lambda i: (0, 0)),
            pl.BlockSpec((DEPTH, D_MODEL), lambda i: (0, 0)),
            pl.BlockSpec((1, D_MODEL), lambda i: (0, 0)),
            pl.BlockSpec(memory_space=pl.ANY),
            pl.BlockSpec(memory_space=pl.ANY),
            pl.BlockSpec(memory_space=pl.ANY),
        ],
        out_specs=[
            pl.BlockSpec((tile, D_MODEL), lambda i: (i, 0)),
            pl.BlockSpec((n_seq, D_MODEL), lambda i: (0, 0)),
        ],
        out_shape=[
            jax.ShapeDtypeStruct((rows, D_MODEL), jnp.float32),
            jax.ShapeDtypeStruct((n_seq, D_MODEL), jnp.float32),
        ],
        scratch_shapes=[
            pltpu.VMEM((D_MODEL, D_MODEL), bf16),
            pltpu.VMEM((D_MODEL, D_FF), bf16),
            pltpu.VMEM((D_FF, D_MODEL), bf16),
            pltpu.VMEM((STAGE_SLOTS, D_MODEL, D_MODEL), jnp.float32),
            pltpu.SemaphoreType.DMA((STAGE_SLOTS,)),
        ],
        compiler_params=pltpu.CompilerParams(
            dimension_semantics=("arbitrary",), vmem_limit_bytes=VMEM_LIMIT),
        name=f"channel_mixer_l{layer}",
    )(x, mix, xs, mix_s, p["n2"], final_g, p["w_out"], p["w_up"], p["w_down"])


def kernel(x_prompt, x_sample, state_gla, state_pool, norm1_g, w_in, w_gate, b_gate, gla_norm_g, pool_w,
           pool_scale, w_out, norm2_g, w_up, w_down, final_g):
    B, T, _ = x_prompt.shape
    n_seq = x_sample.shape[0]
    bf16 = jnp.bfloat16

    w_code = jnp.pad(w_in[:, :, W_IN_GATE:W_IN_U], ((0, 0), (0, 0), (0, LANES - GATE_RANK)))
    pw = pool_w.reshape(DEPTH, len(POOL_WINDOWS) // 2, 2, POOL_GROUP, POOL_GROUP)
    zero = jnp.zeros_like(pw[:, :, 0])
    pool_pairs = jnp.concatenate([jnp.concatenate([pw[:, :, 0], zero], axis=-1),
                                  jnp.concatenate([zero, pw[:, :, 1]], axis=-1)], axis=-2)
    params = {
        "n1": norm1_g,
        "w_main": w_in[:, :, :W_IN_GATE].astype(bf16),
        "w_code": w_code.astype(bf16),
        "w_u": w_in[:, :, W_IN_U:].astype(bf16),
        "w_gate": jnp.pad(w_gate, ((0, 0), (0, LANES - GATE_RANK), (0, 0))).astype(bf16),
        "b_gate": b_gate,
        "gla_g": gla_norm_g,
        "pool_w": pool_pairs.astype(bf16),
        "pool_scale": pool_scale,
        "w_out": w_out,
        "n2": norm2_g,
        "w_up": w_up,
        "w_down": w_down,
    }
    final_g2 = final_g.reshape(1, D_MODEL)
    s_state = state_gla.reshape(DEPTH, n_seq, KEY_WIDTH, HEAD_V)
    pool_rows = jnp.swapaxes(state_pool, 1, 2)

    xp = x_prompt
    xs = x_sample.reshape(n_seq, D_MODEL)
    states = None
    for layer in range(DEPTH):
        mix_p, gla_p, hist_p, mix_s, gla_s, pool_s = _mixer(layer, xp, xs, s_state, pool_rows, params, states)
        states = (gla_p, hist_p, gla_s, pool_s)
        xp, xs = _channel_mixer(layer, xp.reshape(B * T, D_MODEL), mix_p.reshape(B * T, D_MODEL), xs, mix_s,
                                params, final_g2)
        xp = xp.reshape(B, T, D_MODEL)
    return (xp, xs.reshape(n_seq, 1, D_MODEL),
            gla_p.reshape(DEPTH, B, N_HEADS, HEAD_K, HEAD_V), hist_p[:, :, POOL_HIST - POOL_BUF:],
            gla_s.reshape(DEPTH, n_seq, N_HEADS, HEAD_K, HEAD_V), jnp.swapaxes(pool_s, 1, 2))
```
